```python
import jax, jax.numpy as jnp
from jax import lax
import numpy as np

D_MODEL = 1024
BATCH = 8
SEQ = 4096
DEPTH = 1

RWKV_HEAD_DIM = 64
RWKV_DIM = D_MODEL // 2
RWKV_HEADS = RWKV_DIM // RWKV_HEAD_DIM
DECAY_RANK = 64
ICLR_RANK = 64
GATE_RANK = 128
GN_EPS = 64e-5
ATTN_HEAD_DIM = 64
ATTN_DIM = D_MODEL // 2
ATTN_HEADS = ATTN_DIM // ATTN_HEAD_DIM
MOBA_BLOCK = 256
MOBA_TOPK = 3
Q_BLOCK = 128
ROPE_THETA = 500000.0
ROPE_DIM = ATTN_HEAD_DIM // 4
NEG_INF = -1e30
N_BRANCHES = 2
N_EXPERTS = 32
TOP_K = 4
D_EXPERT = D_MODEL
SWIGLU_LIMIT = 7.0
SWIGLU_ALPHA = 1.702
EXPERT_ROW_BLOCK = 128
DEEPNORM_ALPHA = (2.0 * DEPTH) ** 0.25
DEEPNORM_BETA = (8.0 * DEPTH) ** -0.25
LN_EPS = 1e-5
RWKV_COLS = 3 * RWKV_DIM + DECAY_RANK + ICLR_RANK + GATE_RANK
ATTN_COLS = 3 * ATTN_DIM
GATE_COLS = N_BRANCHES * D_MODEL
IN_COLS = RWKV_COLS + ATTN_COLS + GATE_COLS
RWKV_SPLITS = (RWKV_DIM, 2 * RWKV_DIM, 3 * RWKV_DIM, 3 * RWKV_DIM + DECAY_RANK, 3 * RWKV_DIM + DECAY_RANK + ICLR_RANK)

kernel_name = 'rwkv7_moba_gated_hybrid_moe_block'


def layer_norm(x, g, b):
    xf = x.astype(jnp.float32)
    mu = jnp.mean(xf, axis=-1, keepdims=True)
    var = jnp.mean(jnp.square(xf - mu), axis=-1, keepdims=True)
    return ((xf - mu) * lax.rsqrt(var + LN_EPS) * g + b).astype(x.dtype)


def rwkv7_time_mix(p, shift_mix, decay_w0, decay_up, iclr_a0, iclr_up, gate_up, k_k, k_a, r_k, gn_g, gn_b):
    B, T, _ = p.shape
    H, N = RWKV_HEADS, RWKV_HEAD_DIM
    prev = jnp.pad(p[:, :-1], ((0, 0), (1, 0), (0, 0)))
    p = p + (prev - p) * shift_mix
    r, k, v, xw, xa, xg = jnp.split(p, RWKV_SPLITS, axis=-1)
    w_raw = (decay_w0 + jnp.tanh(xw) @ decay_up).astype(jnp.float32)
    decay = jnp.exp(-jnp.exp(-jax.nn.softplus(-w_raw) - 0.5))
    a = jax.nn.sigmoid(iclr_a0 + xa @ iclr_up)
    g = jax.nn.sigmoid(xg) @ gate_up
    kk = (k * k_k).reshape(B, T, H, N).astype(jnp.float32)
    kk = kk / jnp.maximum(jnp.linalg.norm(kk, axis=-1, keepdims=True), 1e-12)
    k = k * (1.0 + (a - 1.0) * k_a)

    def heads(t):
        return t.reshape(B, T, H, N).astype(jnp.float32)

    r_h, k_h, v_h, w_h, a_h = heads(r), heads(k), heads(v), heads(decay), heads(a)

    def step(S, inp):
        r_t, w_t, k_t, v_t, kk_t, a_t = inp
        sa = jnp.einsum('bhvk,bhk->bhv', S, -kk_t)
        S = (S * w_t[:, :, None, :]
             + sa[..., None] * (kk_t * a_t)[:, :, None, :]
             + v_t[..., None] * k_t[:, :, None, :])
        return S, jnp.einsum('bhvk,bhk->bhv', S, r_t)

    xs = tuple(jnp.moveaxis(t, 1, 0) for t in (r_h, w_h, k_h, v_h, kk, a_h))
    S0 = jnp.zeros((B, H, N, N), jnp.float32)
    _, ys = lax.scan(step, S0, xs)
    y = jnp.moveaxis(ys, 0, 1)
    mu = jnp.mean(y, axis=-1, keepdims=True)
    var = jnp.mean(jnp.square(y - mu), axis=-1, keepdims=True)
    y = ((y - mu) * lax.rsqrt(var + GN_EPS)).reshape(B, T, RWKV_DIM) * gn_g + gn_b
    bonus = jnp.sum(r_h * k_h * r_k, axis=-1, keepdims=True) * v_h
    y = y + bonus.reshape(B, T, RWKV_DIM)
    return (y * g).astype(p.dtype)


def partial_rope(t, pos):
    half = ROPE_DIM // 2
    inv_freq = jnp.power(ROPE_THETA, -jnp.arange(0, ROPE_DIM, 2, dtype=jnp.float32) / ROPE_DIM)
    ang = pos.astype(jnp.float32)[:, None] * inv_freq[None, :]
    cos = jnp.cos(ang)[None, :, None, :]
    sin = jnp.sin(ang)[None, :, None, :]
    tf = t[..., :ROPE_DIM].astype(jnp.float32)
    x1, x2 = tf[..., :half], tf[..., half:]
    rot = jnp.concatenate([x1 * cos - x2 * sin, x2 * cos + x1 * sin], axis=-1).astype(t.dtype)
    return jnp.concatenate([rot, t[..., ROPE_DIM:]], axis=-1)


def moba_attention(q, k, v):
    B, T, H, Dh = q.shape
    n_kblk = -(-T // MOBA_BLOCK)
    n_qblk = T // Q_BLOCK
    topk = min(MOBA_TOPK, n_kblk)
    pad = n_kblk * MOBA_BLOCK - T

    def to_blocks(t):
        t = jnp.pad(t, ((0, 0), (0, pad), (0, 0), (0, 0)))
        return t.reshape(B, n_kblk, MOBA_BLOCK, H, Dh).transpose(0, 3, 1, 2, 4)

    k_blocks = to_blocks(k)
    v_blocks = to_blocks(v)
    k_mean = jnp.mean(k_blocks.astype(jnp.float32), axis=3)
    q_blocks = q.reshape(B, n_qblk, Q_BLOCK, H, Dh).transpose(0, 1, 3, 2, 4).reshape(B * n_qblk, H, Q_BLOCK, Dh)
    b_ids = jnp.repeat(jnp.arange(B, dtype=jnp.int32), n_qblk)
    qb_ids = jnp.tile(jnp.arange(n_qblk, dtype=jnp.int32), B)
    scale = Dh ** -0.5
    head_ids = jnp.arange(H)[:, None, None]
    blk_ids = jnp.arange(n_kblk)

    def one_block(args):
        q_blk, b, qb = args
        kb, vb, km = k_blocks[b], v_blocks[b], k_mean[b]
        q_start = qb * Q_BLOCK
        own = q_start // MOBA_BLOCK
        gate = jnp.einsum('hqd,hnd->hqn', q_blk.astype(jnp.float32), km)
        gate = jnp.where(blk_ids < own, gate, NEG_INF)
        _, sel = lax.top_k(gate, topk)
        sel_valid = sel < own
        kg = kb[head_ids, sel]
        vg = vb[head_ids, sel]
        s_sel = jnp.einsum('hqd,hqnkd->hqnk', q_blk, kg, preferred_element_type=jnp.float32) * scale
        s_sel = jnp.where(sel_valid[..., None], s_sel, NEG_INF).reshape(H, Q_BLOCK, topk * MOBA_BLOCK)
        ko = lax.dynamic_index_in_dim(kb, own, axis=1, keepdims=False)
        vo = lax.dynamic_index_in_dim(vb, own, axis=1, keepdims=False)
        s_own = jnp.einsum('hqd,hkd->hqk', q_blk, ko, preferred_element_type=jnp.float32) * scale
        q_pos = q_start + jnp.arange(Q_BLOCK)
        k_pos = own * MOBA_BLOCK + jnp.arange(MOBA_BLOCK)
        s_own = jnp.where(k_pos[None, None, :] <= q_pos[None, :, None], s_own, NEG_INF)
        probs = jax.nn.softmax(jnp.concatenate([s_sel, s_own], axis=-1), axis=-1)
        p_sel = probs[..., :topk * MOBA_BLOCK].reshape(H, Q_BLOCK, topk, MOBA_BLOCK).astype(vb.dtype)
        p_own = probs[..., topk * MOBA_BLOCK:].astype(vb.dtype)
        return jnp.einsum('hqnk,hqnkd->hqd', p_sel, vg) + jnp.einsum('hqk,hkd->hqd', p_own, vo)

    out = lax.map(one_block, (q_blocks, b_ids, qb_ids))
    return out.reshape(B, n_qblk, H, Q_BLOCK, Dh).transpose(0, 1, 3, 2, 4).reshape(B, T, H * Dh)


def token_mixer(x, w_in, shift_mix, decay_w0, decay_up, iclr_a0, iclr_up, gate_up, k_k, k_a, r_k,
                gn_g, gn_b, w_branch_rwkv, w_branch_attn, w_out):
    B, T, _ = x.shape
    proj = jnp.einsum('btd,dc->btc', x, w_in)
    p_rwkv = proj[..., :RWKV_COLS]
    p_attn = proj[..., RWKV_COLS:RWKV_COLS + ATTN_COLS]
    p_gate = proj[..., RWKV_COLS + ATTN_COLS:]
    y_rwkv = rwkv7_time_mix(p_rwkv, shift_mix, decay_w0, decay_up, iclr_a0, iclr_up, gate_up,
                            k_k, k_a, r_k, gn_g, gn_b) @ w_branch_rwkv
    q, k, v = jnp.split(p_attn, 3, axis=-1)
    q, k, v = (t.reshape(B, T, ATTN_HEADS, ATTN_HEAD_DIM) for t in (q, k, v))
    pos = jnp.arange(T)
    q, k = partial_rope(q, pos), partial_rope(k, pos)
    y_attn = moba_attention(q, k, v) @ w_branch_attn
    gates = jax.nn.sigmoid(p_gate).reshape(B, T, N_BRANCHES, D_MODEL)
    merged = gates[:, :, 0] * y_rwkv + gates[:, :, 1] * y_attn
    return merged @ w_out


def moe_ffn(x, router_w, router_b, expert_w_in, expert_b_in, expert_w_out, expert_b_out):
    B, T, D = x.shape
    xf = x.reshape(B * T, D)
    n_tok = B * T
    logits = (xf @ router_w + router_b).astype(jnp.float32)
    top_val, top_idx = lax.top_k(logits, TOP_K)
    weights = jax.nn.softmax(top_val, axis=-1)
    n_assign = n_tok * TOP_K
    flat_e = top_idx.reshape(-1)
    flat_tok = jnp.repeat(jnp.arange(n_tok, dtype=jnp.int32), TOP_K)
    flat_w = weights.reshape(-1)
    order = jnp.argsort(flat_e)
    sorted_e, sorted_tok, sorted_w = flat_e[order], flat_tok[order], flat_w[order]
    counts = jnp.bincount(flat_e, length=N_EXPERTS)
    starts = jnp.cumsum(counts) - counts
    padded = (counts + EXPERT_ROW_BLOCK - 1) // EXPERT_ROW_BLOCK * EXPERT_ROW_BLOCK
    p_ends = jnp.cumsum(padded)
    p_starts = p_ends - padded
    dest = p_starts[sorted_e] + jnp.arange(n_assign) - starts[sorted_e]
    n_rblk = -(-n_assign // EXPERT_ROW_BLOCK) + N_EXPERTS
    n_rows = n_rblk * EXPERT_ROW_BLOCK
    row_tok = jnp.zeros((n_rows,), jnp.int32).at[dest].set(sorted_tok)
    row_w = jnp.zeros((n_rows,), jnp.float32).at[dest].set(sorted_w)
    blk_start = jnp.arange(n_rblk) * EXPERT_ROW_BLOCK
    blk_e = jnp.minimum(jnp.sum(blk_start[:, None] >= p_ends[None, :], axis=1), N_EXPERTS - 1)

    def expert_block(args):
        toks, e = args
        h = xf[toks] @ expert_w_in[e] + expert_b_in[e]
        gate_h = jnp.minimum(h[:, :D_EXPERT], SWIGLU_LIMIT)
        lin_h = jnp.clip(h[:, D_EXPERT:], -SWIGLU_LIMIT, SWIGLU_LIMIT)
        act = gate_h * jax.nn.sigmoid(SWIGLU_ALPHA * gate_h) * (lin_h + 1.0)
        return act @ expert_w_out[e] + expert_b_out[e]

    ys = lax.map(expert_block, (row_tok.reshape(n_rblk, EXPERT_ROW_BLOCK), blk_e)).reshape(n_rows, D)
    out = jax.ops.segment_sum(ys * row_w[:, None].astype(ys.dtype), row_tok, num_segments=n_tok)
    return out.reshape(B, T, D).astype(x.dtype)


def setup_inputs(seed: int = 0) -> dict:
    key = jax.random.key(seed)
    ks = jax.random.split(key, 26)
    L, D, E, F = DEPTH, D_MODEL, N_EXPERTS, D_EXPERT
    nrm = lambda k, shape, s: jax.random.normal(k, shape, jnp.float32) * s
    return {
        'x': jax.random.normal(ks[0], (BATCH, SEQ, D), jnp.float32),
        'ln1_g': 1.0 + nrm(ks[1], (L, D), 0.01),
        'ln1_b': nrm(ks[2], (L, D), 0.01),
        'ln2_g': 1.0 + nrm(ks[3], (L, D), 0.01),
        'ln2_b': nrm(ks[4], (L, D), 0.01),
        'w_in': nrm(ks[5], (L, D, IN_COLS), D ** -0.5),
        'shift_mix': jax.random.uniform(ks[6], (L, RWKV_COLS), jnp.float32),
        'decay_w0': jax.random.uniform(ks[7], (L, RWKV_DIM), jnp.float32, -6.5, -1.5),
        'decay_up': nrm(ks[8], (L, DECAY_RANK, RWKV_DIM), 0.1 * DECAY_RANK ** -0.5),
        'iclr_a0': nrm(ks[9], (L, RWKV_DIM), 0.1),
        'iclr_up': nrm(ks[10], (L, ICLR_RANK, RWKV_DIM), 0.1 * ICLR_RANK ** -0.5),
        'gate_up': nrm(ks[11], (L, GATE_RANK, RWKV_DIM), GATE_RANK ** -0.5),
        'k_k': 0.85 + nrm(ks[12], (L, RWKV_DIM), 0.05),
        'k_a': 1.0 + nrm(ks[13], (L, RWKV_DIM), 0.05),
        'r_k': nrm(ks[14], (L, RWKV_HEADS, RWKV_HEAD_DIM), 0.1),
        'gn_g': 1.0 + nrm(ks[15], (L, RWKV_DIM), 0.01),
        'gn_b': nrm(ks[16], (L, RWKV_DIM), 0.01),
        'w_branch_rwkv': nrm(ks[17], (L, RWKV_DIM, D), RWKV_DIM ** -0.5),
        'w_branch_attn': nrm(ks[18], (L, ATTN_DIM, D), ATTN_DIM ** -0.5),
        'w_out': nrm(ks[19], (L, D, D), DEEPNORM_BETA * D ** -0.5),
        'router_w': nrm(ks[20], (L, D, E), D ** -0.5),
        'router_b': nrm(ks[21], (L, E), 0.01),
        'expert_w_in': nrm(ks[22], (L, E, D, 2 * F), D ** -0.5),
        'expert_b_in': nrm(ks[23], (L, E, 2 * F), 0.01),
        'expert_w_out': nrm(ks[24], (L, E, F, D), DEEPNORM_BETA * F ** -0.5),
        'expert_b_out': nrm(ks[25], (L, E, D), 0.01),
    }


def reference(x, ln1_g, ln1_b, ln2_g, ln2_b, w_in, shift_mix, decay_w0, decay_up, iclr_a0, iclr_up,
              gate_up, k_k, k_a, r_k, gn_g, gn_b, w_branch_rwkv, w_branch_attn, w_out,
              router_w, router_b, expert_w_in, expert_b_in, expert_w_out, expert_b_out):
    for i in range(DEPTH):
        mix = token_mixer(x, w_in[i], shift_mix[i], decay_w0[i], decay_up[i], iclr_a0[i], iclr_up[i],
                          gate_up[i], k_k[i], k_a[i], r_k[i], gn_g[i], gn_b[i],
                          w_branch_rwkv[i], w_branch_attn[i], w_out[i])
        x = layer_norm(DEEPNORM_ALPHA * x + mix, ln1_g[i], ln1_b[i])
        ffn = moe_ffn(x, router_w[i], router_b[i], expert_w_in[i], expert_b_in[i],
                      expert_w_out[i], expert_b_out[i])
        x = layer_norm(DEEPNORM_ALPHA * x + ffn, ln2_g[i], ln2_b[i])
    return x
```

```python
import functools
import math

import jax
import jax.numpy as jnp
from jax import lax
from jax.experimental import pallas as pl
from jax.experimental.pallas import tpu as pltpu

F32 = jnp.float32
BF16 = jnp.bfloat16
HI = lax.Precision.HIGHEST

D_MODEL = 1024
DEPTH = 1
RWKV_HEAD_DIM = 64
RWKV_DIM = 512
RWKV_HEADS = 8
DECAY_RANK = 64
ICLR_RANK = 64
GATE_RANK = 128
GN_EPS = 64e-5
ATTN_HEAD_DIM = 64
ATTN_DIM = 512
ATTN_HEADS = 8
MOBA_BLOCK = 256
MOBA_TOPK = 3
ROPE_THETA = 500000.0
ROPE_DIM = 16
NEG_INF = -1e30
N_EXPERTS = 32
TOP_K = 4
D_EXPERT = 1024
SWIGLU_LIMIT = 7.0
SWIGLU_ALPHA = 1.702
DEEPNORM_ALPHA = (2.0 * DEPTH) ** 0.25
LN_EPS = 1e-5
RWKV_COLS = 3 * RWKV_DIM + DECAY_RANK + ICLR_RANK + GATE_RANK
ATTN_COLS = 3 * ATTN_DIM
GATE_COLS = 2 * D_MODEL
IN_COLS = RWKV_COLS + ATTN_COLS + GATE_COLS

VMEM_LIMIT_BYTES = 56 * 1024 * 1024

CHUNK = 64


def _cparams(sem):
    return pltpu.CompilerParams(dimension_semantics=sem, vmem_limit_bytes=VMEM_LIMIT_BYTES)


def _dot(a, b):
    return jnp.dot(a, b, preferred_element_type=F32)


def _dot_hi(a, b):
    return jnp.dot(a, b, preferred_element_type=F32, precision=HI)


def _dot_nt(a, b, precision=None):
    return lax.dot_general(a, b, (((1,), (1,)), ((), ())), preferred_element_type=F32, precision=precision)


def _dot_tn(a, b, precision=None):
    return lax.dot_general(a, b, (((0,), (0,)), ((), ())), preferred_element_type=F32, precision=precision)


PROJ_TM = 256


def _proj_kernel(x_ref, w_ref, cos_ref, sa_ref, sb_ref, prw_ref, q_ref, k_ref, v_ref, g_ref):
    xb = x_ref[...].astype(BF16)
    prw_ref[...] = _dot(xb, w_ref[:, 0:RWKV_COLS])
    c0 = RWKV_COLS
    cos = cos_ref[...]
    sa = sa_ref[...]
    sb = sb_ref[...]

    def rope(t):
        return t * cos + pltpu.roll(t, ATTN_DIM - ROPE_DIM // 2, 1) * sa + pltpu.roll(t, ROPE_DIM // 2, 1) * sb

    q_ref[...] = rope(_dot(xb, w_ref[:, c0:c0 + ATTN_DIM])).astype(BF16)
    k_ref[...] = rope(_dot(xb, w_ref[:, c0 + ATTN_DIM:c0 + 2 * ATTN_DIM])).astype(BF16)
    v_ref[...] = _dot(xb, w_ref[:, c0 + 2 * ATTN_DIM:c0 + 3 * ATTN_DIM]).astype(BF16)
    c1 = RWKV_COLS + ATTN_COLS
    g_ref[...] = jax.nn.sigmoid(_dot(xb, w_ref[:, c1:c1 + GATE_COLS])).astype(BF16)


def _rope_tables(T):
    half = ROPE_DIM // 2
    inv_freq = jnp.power(ROPE_THETA, -jnp.arange(0, ROPE_DIM, 2, dtype=F32) / ROPE_DIM)
    ang = jnp.arange(T).astype(F32)[:, None] * inv_freq[None, :]
    cos, sin = jnp.cos(ang), jnp.sin(ang)
    pad = jnp.zeros((T, ATTN_HEAD_DIM - ROPE_DIM), F32)
    cos_h = jnp.concatenate([cos, cos, pad + 1.0], axis=1)
    sa_h = jnp.concatenate([-sin, jnp.zeros((T, half), F32), pad], axis=1)
    sb_h = jnp.concatenate([jnp.zeros((T, half), F32), sin, pad], axis=1)
    tile = lambda t: jnp.tile(t, (1, ATTN_HEADS))
    return tile(cos_h), tile(sa_h), tile(sb_h)


def _proj(x2, w_in_bf, T):
    n_tok = x2.shape[0]
    tm = PROJ_TM
    t_tiles = T // tm
    cos, sa, sb = _rope_tables(T)
    row = lambda i: (i, 0)
    tab = lambda i: (i % t_tiles, 0)
    return pl.pallas_call(
        _proj_kernel,
        grid=(n_tok // tm,),
        in_specs=[
            pl.BlockSpec((tm, D_MODEL), row),
            pl.BlockSpec((D_MODEL, IN_COLS), lambda i: (0, 0), pipeline_mode=pl.Buffered(1)),
            pl.BlockSpec((tm, ATTN_DIM), tab),
            pl.BlockSpec((tm, ATTN_DIM), tab),
            pl.BlockSpec((tm, ATTN_DIM), tab),
        ],
        out_specs=[
            pl.BlockSpec((tm, RWKV_COLS), row),
            pl.BlockSpec((tm, ATTN_DIM), row),
            pl.BlockSpec((tm, ATTN_DIM), row),
            pl.BlockSpec((tm, ATTN_DIM), row),
            pl.BlockSpec((tm, GATE_COLS), row),
        ],
        out_shape=[
            jax.ShapeDtypeStruct((n_tok, RWKV_COLS), F32),
            jax.ShapeDtypeStruct((n_tok, ATTN_DIM), BF16),
            jax.ShapeDtypeStruct((n_tok, ATTN_DIM), BF16),
            jax.ShapeDtypeStruct((n_tok, ATTN_DIM), BF16),
            jax.ShapeDtypeStruct((n_tok, GATE_COLS), BF16),
        ],
        compiler_params=_cparams(("parallel",)),
        name="proj",
    )(x2, w_in_bf, cos, sa, sb)


RWKV_TT = 128


def _rwkv_kernel(p_ref, mix_ref, w0_ref, dup_ref, a0_ref, iup_ref, gup_ref, kk_ref, ka_ref, rk_ref,
                 gng_ref, gnb_ref, ones_ref, y_ref, s_scr, prev_scr):
    H, N, C = RWKV_HEADS, RWKV_HEAD_DIM, CHUNK

    @pl.when(pl.program_id(1) == 0)
    def _():
        s_scr[...] = jnp.zeros_like(s_scr)
        prev_scr[...] = jnp.zeros_like(prev_scr)

    bones = ones_ref[...]
    row = lax.broadcasted_iota(jnp.int32, (C, C), 0)
    col = lax.broadcasted_iota(jnp.int32, (C, C), 1)
    lower_incl = col <= row
    lower_strict = col < row
    ltri = lower_incl.astype(F32)
    eye = (col == row).astype(F32)
    first_row = lax.broadcasted_iota(jnp.int32, (C, RWKV_COLS), 0) == 0

    for c in range(RWKV_TT // C):
        p = p_ref[c * C:(c + 1) * C, :]
        prev = jnp.where(first_row, prev_scr[...], pltpu.roll(p, 1, 0))
        prev_scr[...] = p[C - 1:C, :]
        ps = p + (prev - p) * mix_ref[...]
        r = ps[:, 0:RWKV_DIM]
        k = ps[:, RWKV_DIM:2 * RWKV_DIM]
        v = ps[:, 2 * RWKV_DIM:3 * RWKV_DIM]
        o = 3 * RWKV_DIM
        xw = ps[:, o:o + DECAY_RANK]
        xa = ps[:, o + DECAY_RANK:o + DECAY_RANK + ICLR_RANK]
        xg = ps[:, o + DECAY_RANK + ICLR_RANK:RWKV_COLS]
        w_raw = w0_ref[...] + _dot_hi(jnp.tanh(xw), dup_ref[...])
        logw = -math.exp(-0.5) * jax.nn.sigmoid(w_raw)
        a = jax.nn.sigmoid(a0_ref[...] + _dot_hi(xa, iup_ref[...]))
        g = _dot_hi(jax.nn.sigmoid(xg), gup_ref[...])
        kk0 = k * kk_ref[...]
        kk = kk0 / jnp.maximum(jnp.sqrt(_dot_hi(kk0 * kk0, bones)), 1e-12)
        kp = k * (1.0 + (a - 1.0) * ka_ref[...])
        kka = kk * a
        cum = _dot_hi(ltri, logw)
        tot = cum[C - 1:C, :]
        e_neg = jnp.exp(-cum)
        at = (-kk * jnp.exp(cum - logw)).astype(BF16)
        rt = (r * jnp.exp(cum)).astype(BF16)
        bt = (kka * e_neg).astype(BF16)
        kt = (kp * e_neg).astype(BF16)
        e_end = jnp.exp(tot - cum)
        bh = (kka * e_end).astype(BF16)
        kh = (kp * e_end).astype(BF16)
        e_tot = jnp.exp(tot)
        vb = v.astype(BF16)

        ys = []
        for h in range(H):
            sl = slice(h * N, (h + 1) * N)
            lhs = jnp.concatenate([at[:, sl], rt[:, sl]], axis=0)
            rhs = jnp.concatenate([bt[:, sl], kt[:, sl]], axis=0)
            aa = _dot_nt(lhs, rhs)
            a_ab = jnp.where(lower_strict, aa[0:C, 0:C], 0.0)
            a_ak = jnp.where(lower_strict, aa[0:C, C:2 * C], 0.0)
            a_rb = jnp.where(lower_incl, aa[C:2 * C, 0:C], 0.0)
            a_rk = jnp.where(lower_incl, aa[C:2 * C, C:2 * C], 0.0)
            tinv = eye + a_ab
            npow = a_ab
            for _ in range(5):
                npow = _dot_hi(npow, npow)
                tinv = tinv + _dot_hi(tinv, npow)
            s_old = s_scr[h]
            ar_s = _dot_nt(lhs, s_old.astype(BF16))
            akv = _dot(jnp.concatenate([a_ak, a_rk], axis=0).astype(BF16), vb[:, sl])
            u = _dot_hi(tinv, ar_s[0:C] + akv[0:C])
            ub = u.astype(BF16)
            ys.append(ar_s[C:2 * C] + akv[C:2 * C] + _dot(a_rb.astype(BF16), ub))
            upd = _dot_tn(jnp.concatenate([ub, vb[:, sl]], axis=0),
                          jnp.concatenate([bh[:, sl], kh[:, sl]], axis=0))
            s_scr[h] = s_old * e_tot[:, sl] + upd
        y = jnp.concatenate(ys, axis=1)
        mu = _dot_hi(y, bones) * (1.0 / N)
        yc = y - mu
        var = _dot_hi(yc * yc, bones) * (1.0 / N)
        yn = yc * lax.rsqrt(var + GN_EPS) * gng_ref[...] + gnb_ref[...]
        bonus = _dot_hi(r * kp * rk_ref[...], bones) * v
        y_ref[c * C:(c + 1) * C, :] = ((yn + bonus) * g).astype(BF16)


def _rwkv(p_rwkv, B, T, shift_mix, decay_w0, decay_up, iclr_a0, iclr_up, gate_up, k_k, k_a, r_k, gn_g, gn_b):
    tt = RWKV_TT
    n_t = T // tt
    head = jnp.arange(RWKV_DIM) // RWKV_HEAD_DIM
    bones = (head[:, None] == head[None, :]).astype(F32)
    vec = lambda a: a.reshape(1, -1)
    full = lambda shape: pl.BlockSpec(shape, lambda b, j: (0, 0))
    return pl.pallas_call(
        _rwkv_kernel,
        grid=(B, n_t),
        in_specs=[
            pl.BlockSpec((tt, RWKV_COLS), lambda b, j: (b * n_t + j, 0)),
            full((1, RWKV_COLS)), full((1, RWKV_DIM)), full((DECAY_RANK, RWKV_DIM)),
            full((1, RWKV_DIM)), full((ICLR_RANK, RWKV_DIM)), full((GATE_RANK, RWKV_DIM)),
            full((1, RWKV_DIM)), full((1, RWKV_DIM)), full((1, RWKV_DIM)),
            full((1, RWKV_DIM)), full((1, RWKV_DIM)), full((RWKV_DIM, RWKV_DIM)),
        ],
        out_specs=pl.BlockSpec((tt, RWKV_DIM), lambda b, j: (b * n_t + j, 0)),
        out_shape=jax.ShapeDtypeStruct((B * T, RWKV_DIM), BF16),
        scratch_shapes=[
            pltpu.VMEM((RWKV_HEADS, RWKV_HEAD_DIM, RWKV_HEAD_DIM), F32),
            pltpu.VMEM((1, RWKV_COLS), F32),
        ],
        compiler_params=_cparams(("parallel", "arbitrary")),
        name="rwkv",
    )(p_rwkv, vec(shift_mix), vec(decay_w0), decay_up, vec(iclr_a0), iclr_up, gate_up,
      vec(k_k), vec(k_a), vec(r_k), vec(gn_g), vec(gn_b), bones)


MOBA_HP = 2


def _moba_kernel(q_ref, k_ref, v_ref, o_ref, kmean_scr, sel_scr):
    blk_sz, dh = MOBA_BLOCK, ATTN_HEAD_DIM
    nb = k_ref.shape[0] // blk_sz
    i = pl.program_id(2)

    @pl.when(i == 0)
    def _():
        for n in range(nb):
            kb = k_ref[n * blk_sz:(n + 1) * blk_sz, :].astype(F32)
            kmean_scr[n:n + 1, :] = jnp.sum(kb, axis=0, keepdims=True) * (1.0 / blk_sz)

    blk = lax.broadcasted_iota(jnp.int32, (nb, blk_sz), 0)
    kpos = lax.broadcasted_iota(jnp.int32, (blk_sz, blk_sz), 0)
    qpos = lax.broadcasted_iota(jnp.int32, (blk_sz, blk_sz), 1)
    own_start = pl.multiple_of(i * blk_sz, blk_sz)
    outs = []
    for hh in range(MOBA_HP):
        hs = slice(hh * dh, (hh + 1) * dh)
        qh = q_ref[:, hs]
        gate = _dot_nt(kmean_scr[:, hs], qh.astype(F32), precision=HI)
        gate = jnp.where(blk < i, gate, NEG_INF)
        rank = jnp.zeros((nb, blk_sz), jnp.int32)
        for m in range(nb):
            gm = gate[m:m + 1, :]
            beats = (gm > gate) | ((gm == gate) & (m < blk))
            rank = rank + beats.astype(jnp.int32)
        sel_scr[hh] = ((rank < MOBA_TOPK) & (blk < i)).astype(F32)

        qs = qh * (dh ** -0.5)
        s = _dot_nt(k_ref[pl.ds(own_start, blk_sz), hs], qs)
        s = jnp.where(kpos <= qpos, s, NEG_INF)
        m0 = jnp.max(s, axis=0, keepdims=True)
        p = jnp.exp(s - m0)
        l0 = jnp.sum(p, axis=0, keepdims=True)
        acc0 = _dot_tn(v_ref[pl.ds(own_start, blk_sz), hs], p.astype(BF16))

        def body(n, carry, hh=hh, hs=hs, qs=qs):
            m_run, l_run, acc = carry
            start = pl.multiple_of(n * blk_sz, blk_sz)
            s = _dot_nt(k_ref[pl.ds(start, blk_sz), hs], qs)
            s = jnp.where(sel_scr[hh, pl.ds(n, 1), :] > 0.0, s, NEG_INF)
            m_new = jnp.maximum(m_run, jnp.max(s, axis=0, keepdims=True))
            alpha = jnp.exp(m_run - m_new)
            p = jnp.exp(s - m_new)
            l_new = alpha * l_run + jnp.sum(p, axis=0, keepdims=True)
            acc = alpha * acc + _dot_tn(v_ref[pl.ds(start, blk_sz), hs], p.astype(BF16))
            return m_new, l_new, acc

        _, l_fin, acc = lax.fori_loop(0, i, body, (m0, l0, acc0))
        outs.append((acc / l_fin).T)
    o_ref[...] = jnp.concatenate(outs, axis=1).astype(BF16)


def _moba(q, k, v, B, T):
    blk_sz = MOBA_BLOCK
    nq = T // blk_sz
    lanes = MOBA_HP * ATTN_HEAD_DIM
    kv_spec = pl.BlockSpec((T, lanes), lambda b, hp, i: (b, hp))
    q_spec = pl.BlockSpec((blk_sz, lanes), lambda b, hp, i: (b * nq + i, hp))
    return pl.pallas_call(
        _moba_kernel,
        grid=(B, ATTN_HEADS // MOBA_HP, nq),
        in_specs=[q_spec, kv_spec, kv_spec],
        out_specs=q_spec,
        out_shape=jax.ShapeDtypeStruct((B * T, ATTN_DIM), BF16),
        scratch_shapes=[
            pltpu.VMEM((T // blk_sz, lanes), F32),
            pltpu.VMEM((MOBA_HP, T // blk_sz, blk_sz), F32),
        ],
        compiler_params=_cparams(("parallel", "parallel", "arbitrary")),
        name="moba",
    )(q, k, v)


MERGE_TM = 256


def _layer_norm(h, g, b):
    mu = jnp.mean(h, axis=-1, keepdims=True)
    hc = h - mu
    var = jnp.mean(hc * hc, axis=-1, keepdims=True)
    return hc * lax.rsqrt(var + LN_EPS) * g + b


def _merge_kernel(x_ref, yr_ref, ya_ref, g_ref, wbr_ref, wba_ref, wo_ref, lng_ref, lnb_ref, rwt_ref, rb_ref,
                  x1_ref, lgt_ref):
    yr = _dot(yr_ref[...], wbr_ref[...])
    ya = _dot(ya_ref[...], wba_ref[...])
    merged = g_ref[:, 0:D_MODEL].astype(F32) * yr + g_ref[:, D_MODEL:2 * D_MODEL].astype(F32) * ya
    mix = _dot(merged.astype(BF16), wo_ref[...])
    x1 = _layer_norm(DEEPNORM_ALPHA * x_ref[...] + mix, lng_ref[...], lnb_ref[...])
    x1_ref[...] = x1
    lgt_ref[...] = _dot_nt(rwt_ref[...], x1, precision=HI) + rb_ref[...]


def _merge(x2, y_rwkv, y_attn, gates, w_br, w_ba, w_o, ln_g, ln_b, router_w, router_b):
    n_tok = x2.shape[0]
    tm = MERGE_TM
    row = lambda i: (i, 0)
    full = lambda shape: pl.BlockSpec(shape, lambda i: (0, 0))
    return pl.pallas_call(
        _merge_kernel,
        grid=(n_tok // tm,),
        in_specs=[
            pl.BlockSpec((tm, D_MODEL), row), pl.BlockSpec((tm, RWKV_DIM), row), pl.BlockSpec((tm, ATTN_DIM), row),
            pl.BlockSpec((tm, GATE_COLS), row),
            full((RWKV_DIM, D_MODEL)), full((ATTN_DIM, D_MODEL)), full((D_MODEL, D_MODEL)),
            full((1, D_MODEL)), full((1, D_MODEL)), full((N_EXPERTS, D_MODEL)), full((N_EXPERTS, 1)),
        ],
        out_specs=[pl.BlockSpec((tm, D_MODEL), row), pl.BlockSpec((N_EXPERTS, tm), lambda i: (0, i))],
        out_shape=[jax.ShapeDtypeStruct((n_tok, D_MODEL), F32), jax.ShapeDtypeStruct((N_EXPERTS, n_tok), F32)],
        compiler_params=_cparams(("parallel",)),
        name="merge",
    )(x2, y_rwkv, y_attn, gates, w_br.astype(BF16), w_ba.astype(BF16), w_o.astype(BF16),
      ln_g.reshape(1, -1), ln_b.reshape(1, -1), router_w.T, router_b.reshape(-1, 1))


ROUTE_TM = 512


def _route_kernel(lg_ref, idx_ref, wt_ref, pos_ref, cnt_ref, run_scr):
    tm = lg_ref.shape[1]

    @pl.when(pl.program_id(0) == 0)
    def _():
        run_scr[...] = jnp.zeros_like(run_scr)

    work = lg_ref[...]
    eio = lax.broadcasted_iota(jnp.int32, work.shape, 0)
    vals, hots = [], []
    for s in range(TOP_K):
        m = jnp.max(work, axis=0, keepdims=True)
        ix = jnp.min(jnp.where(work == m, eio, N_EXPERTS), axis=0, keepdims=True)
        hot = eio == ix
        idx_ref[s:s + 1, :] = ix
        vals.append(m)
        hots.append(hot)
        work = jnp.where(hot, -jnp.inf, work)
    es = [jnp.exp(v - vals[0]) for v in vals]
    denom = es[0] + es[1] + es[2] + es[3]
    for s in range(TOP_K):
        wt_ref[s:s + 1, :] = es[s] / denom
    multi = (hots[0] | hots[1] | hots[2] | hots[3])
    t_from = lax.broadcasted_iota(jnp.int32, (tm, tm), 0)
    t_to = lax.broadcasted_iota(jnp.int32, (tm, tm), 1)
    before = jnp.where(t_from < t_to, 1.0, 0.0).astype(BF16)
    multi_f = jnp.where(multi, 1.0, 0.0)
    count = _dot(multi_f.astype(BF16), before) + run_scr[...]
    for s in range(TOP_K):
        pos_ref[s:s + 1, :] = jnp.sum(jnp.where(hots[s], count, 0.0), axis=0, keepdims=True).astype(jnp.int32)
    run_scr[...] = run_scr[...] + jnp.sum(multi_f, axis=1, keepdims=True)
    cnt_ref[...] = run_scr[...].astype(jnp.int32)


def _route(logits_t):
    n_tok = logits_t.shape[1]
    tm = min(ROUTE_TM, n_tok)
    col = lambda i: (0, i)
    return pl.pallas_call(
        _route_kernel,
        grid=(n_tok // tm,),
        in_specs=[pl.BlockSpec((N_EXPERTS, tm), col)],
        out_specs=[pl.BlockSpec((TOP_K, tm), col), pl.BlockSpec((TOP_K, tm), col), pl.BlockSpec((TOP_K, tm), col),
                   pl.BlockSpec((N_EXPERTS, 1), lambda i: (0, 0))],
        out_shape=[jax.ShapeDtypeStruct((TOP_K, n_tok), jnp.int32), jax.ShapeDtypeStruct((TOP_K, n_tok), F32),
                   jax.ShapeDtypeStruct((TOP_K, n_tok), jnp.int32), jax.ShapeDtypeStruct((N_EXPERTS, 1), jnp.int32)],
        scratch_shapes=[pltpu.VMEM((N_EXPERTS, 1), F32)],
        compiler_params=_cparams(("arbitrary",)),
        name="route",
    )(logits_t)


ROW_BLOCK = 256
DMA_TM = 256
LANE = 128


def _dest_tiles(dest):
    n_tok = dest.shape[1]
    d = dest.reshape(TOP_K, n_tok // DMA_TM, DMA_TM).transpose(1, 0, 2)
    return d.reshape(n_tok // DMA_TM, TOP_K * DMA_TM // LANE, LANE)


def _load_dest(dest_hbm, dest_smem, sem):
    cp = pltpu.make_async_copy(dest_hbm.at[pl.program_id(0)], dest_smem, sem)
    cp.start()
    cp.wait()


def _dest_at(dest_smem, s, j):
    f = s * DMA_TM + j
    return dest_smem[f // LANE, f % LANE]


def _dispatch_kernel(dest_hbm, x1_hbm, xs_in_hbm, xs_hbm, dest_smem, sem_idx, sem_rows):
    del xs_in_hbm
    _load_dest(dest_hbm, dest_smem, sem_idx)
    base = pl.program_id(0) * DMA_TM

    def row_copy(t, d):
        return pltpu.make_async_copy(x1_hbm.at[pl.ds(t, 1)], xs_hbm.at[pl.ds(d, 1)], sem_rows)

    def issue(j, carry):
        for s in range(TOP_K):
            row_copy(base + j, _dest_at(dest_smem, s, j)).start()
        return carry

    lax.fori_loop(0, DMA_TM, issue, 0)

    def drain(j, carry):
        for s in range(TOP_K):
            row_copy(0, 0).wait()
        return carry

    lax.fori_loop(0, DMA_TM, drain, 0)


def _dispatch(dest_t, x1, n_rows):
    n_tok = x1.shape[0]
    any_spec = pl.BlockSpec(memory_space=pl.ANY)
    return pl.pallas_call(
        _dispatch_kernel,
        grid=(n_tok // DMA_TM,),
        in_specs=[any_spec, any_spec, any_spec],
        out_specs=any_spec,
        out_shape=jax.ShapeDtypeStruct((n_rows, D_MODEL), F32),
        scratch_shapes=[pltpu.SMEM((TOP_K * DMA_TM // LANE, LANE), jnp.int32),
                        pltpu.SemaphoreType.DMA(()), pltpu.SemaphoreType.DMA(())],
        input_output_aliases={2: 0},
        compiler_params=_cparams(("arbitrary",)),
        name="dispatch",
    )(dest_t, x1, jnp.zeros((n_rows, D_MODEL), F32))


def _combine_kernel(dest_hbm, ys_hbm, x1_ref, wt_ref, lng_ref, lnb_ref, out_ref, dest_smem, buf, sem_idx, sem_rows):
    _load_dest(dest_hbm, dest_smem, sem_idx)

    def row_copy(d, s, j):
        return pltpu.make_async_copy(ys_hbm.at[pl.ds(d, 1)], buf.at[s, pl.ds(j, 1)], sem_rows)

    def issue(j, carry):
        for s in range(TOP_K):
            row_copy(_dest_at(dest_smem, s, j), s, j).start()
        return carry

    lax.fori_loop(0, DMA_TM, issue, 0)

    def drain(j, carry):
        for s in range(TOP_K):
            row_copy(0, s, j).wait()
        return carry

    lax.fori_loop(0, DMA_TM, drain, 0)
    ffn = wt_ref[:, 0:1] * buf[0]
    for s in range(1, TOP_K):
        ffn = ffn + wt_ref[:, s:s + 1] * buf[s]
    out_ref[...] = _layer_norm(DEEPNORM_ALPHA * x1_ref[...] + ffn, lng_ref[...], lnb_ref[...])


def _combine(dest_t, ys, x1, wts_tok, ln_g, ln_b):
    n_tok = x1.shape[0]
    any_spec = pl.BlockSpec(memory_space=pl.ANY)
    row = lambda i: (i, 0)
    full = lambda shape: pl.BlockSpec(shape, lambda i: (0, 0))
    return pl.pallas_call(
        _combine_kernel,
        grid=(n_tok // DMA_TM,),
        in_specs=[any_spec, any_spec, pl.BlockSpec((DMA_TM, D_MODEL), row), pl.BlockSpec((DMA_TM, TOP_K), row),
                  full((1, D_MODEL)), full((1, D_MODEL))],
        out_specs=pl.BlockSpec((DMA_TM, D_MODEL), row),
        out_shape=jax.ShapeDtypeStruct((n_tok, D_MODEL), F32),
        scratch_shapes=[pltpu.SMEM((TOP_K * DMA_TM // LANE, LANE), jnp.int32),
                        pltpu.VMEM((TOP_K, DMA_TM, D_MODEL), F32),
                        pltpu.SemaphoreType.DMA(()), pltpu.SemaphoreType.DMA(())],
        compiler_params=_cparams(("arbitrary",)),
        name="combine",
    )(dest_t, ys, x1, wts_tok, ln_g.reshape(1, -1), ln_b.reshape(1, -1))


def _expert_kernel(blk_e_ref, n_used_ref, xs_ref, win_ref, bin_ref, wout_ref, bout_ref, ys_ref):
    del blk_e_ref

    @pl.when(pl.program_id(0) < n_used_ref[0])
    def _():
        h = _dot(xs_ref[...].astype(BF16), win_ref[0]) + bin_ref[0]
        gate_h = jnp.minimum(h[:, 0:D_EXPERT], SWIGLU_LIMIT)
        lin_h = jnp.clip(h[:, D_EXPERT:2 * D_EXPERT], -SWIGLU_LIMIT, SWIGLU_LIMIT)
        act = gate_h * jax.nn.sigmoid(SWIGLU_ALPHA * gate_h) * (lin_h + 1.0)
        ys_ref[...] = _dot(act.astype(BF16), wout_ref[0]) + bout_ref[0]

    @pl.when(pl.program_id(0) >= n_used_ref[0])
    def _():
        ys_ref[...] = jnp.zeros_like(ys_ref)


def _experts(blk_e, n_used, xs, w_in_bf, b_in, w_out_bf, b_out):
    n_rows = xs.shape[0]
    n_blk = n_rows // ROW_BLOCK
    used = lambda rb, n_used: jnp.minimum(rb, n_used[0] - 1)
    rows = lambda rb, blk_e, n_used: (used(rb, n_used), 0)
    per_e = lambda rb, blk_e, n_used: (blk_e[used(rb, n_used)], 0, 0)
    return pl.pallas_call(
        _expert_kernel,
        grid_spec=pltpu.PrefetchScalarGridSpec(
            num_scalar_prefetch=2,
            grid=(n_blk,),
            in_specs=[
                pl.BlockSpec((ROW_BLOCK, D_MODEL), rows),
                pl.BlockSpec((1, D_MODEL, 2 * D_EXPERT), per_e),
                pl.BlockSpec((1, 1, 2 * D_EXPERT), per_e),
                pl.BlockSpec((1, D_EXPERT, D_MODEL), per_e),
                pl.BlockSpec((1, 1, D_MODEL), per_e),
            ],
            out_specs=pl.BlockSpec((ROW_BLOCK, D_MODEL), lambda rb, blk_e, n_used: (rb, 0)),
        ),
        out_shape=jax.ShapeDtypeStruct((n_rows, D_MODEL), F32),
        compiler_params=_cparams(("arbitrary",)),
        name="experts",
    )(blk_e, n_used, xs, w_in_bf, b_in.reshape(N_EXPERTS, 1, -1), w_out_bf, b_out.reshape(N_EXPERTS, 1, -1))


def _moe(x1, logits_t, expert_w_in, expert_b_in, expert_w_out, expert_b_out, ln_g, ln_b):
    n_tok = x1.shape[0]
    idx, wts, pos, counts = _route(logits_t)
    padded = (counts[:, 0] + ROW_BLOCK - 1) // ROW_BLOCK * ROW_BLOCK
    p_ends = jnp.cumsum(padded)
    p_starts = p_ends - padded
    dest_t = _dest_tiles(p_starts[idx] + pos)
    n_rows = n_tok * TOP_K + N_EXPERTS * ROW_BLOCK
    blk_start = jnp.arange(n_rows // ROW_BLOCK, dtype=jnp.int32) * ROW_BLOCK
    blk_e = jnp.minimum(jnp.sum(blk_start[:, None] >= p_ends[None, :], axis=1), N_EXPERTS - 1).astype(jnp.int32)
    n_used = (p_ends[-1:] // ROW_BLOCK).astype(jnp.int32)
    xs = _dispatch(dest_t, x1, n_rows)
    ys = _experts(blk_e, n_used, xs, expert_w_in.astype(BF16), expert_b_in, expert_w_out.astype(BF16), expert_b_out)
    return _combine(dest_t, ys, x1, wts.T, ln_g, ln_b)


def kernel(x, ln1_g, ln1_b, ln2_g, ln2_b, w_in, shift_mix, decay_w0, decay_up, iclr_a0, iclr_up, gate_up, k_k, k_a, r_k, gn_g, gn_b, w_branch_rwkv, w_branch_attn, w_out, router_w, router_b, expert_w_in, expert_b_in, expert_w_out, expert_b_out):
    B, T, D = x.shape
    x2 = x.reshape(B * T, D)
    prw, q, k, v, gates = _proj(x2, w_in[0].astype(BF16), T)
    y_rwkv = _rwkv(prw, B, T, shift_mix[0], decay_w0[0], decay_up[0], iclr_a0[0], iclr_up[0], gate_up[0],
                   k_k[0], k_a[0], r_k[0], gn_g[0], gn_b[0])
    y_attn = _moba(q, k, v, B, T)
    x1, logits_t = _merge(x2, y_rwkv, y_attn, gates, w_branch_rwkv[0], w_branch_attn[0], w_out[0],
                          ln1_g[0], ln1_b[0], router_w[0], router_b[0])
    out = _moe(x1, logits_t, expert_w_in[0], expert_b_in[0], expert_w_out[0], expert_b_out[0], ln2_g[0], ln2_b[0])
    return out.reshape(B, T, D)
```

```python
import math

import jax
import jax.numpy as jnp
from jax import lax
from jax.experimental import pallas as pl
from jax.experimental.pallas import tpu as pltpu

F32 = jnp.float32
BF16 = jnp.bfloat16
I32 = jnp.int32
U32 = jnp.uint32
HI = lax.Precision.HIGHEST

D_MODEL = 1024
DEPTH = 1
RWKV_HEAD_DIM = 64
RWKV_DIM = 512
RWKV_HEADS = 8
DECAY_RANK = 64
ICLR_RANK = 64
GATE_RANK = 128
GN_EPS = 64e-5
ATTN_HEAD_DIM = 64
ATTN_DIM = 512
ATTN_HEADS = 8
MOBA_BLOCK = 256
MOBA_TOPK = 3
ROPE_THETA = 500000.0
ROPE_DIM = 16
NEG_INF = -1e30
N_EXPERTS = 32
TOP_K = 4
D_EXPERT = 1024
SWIGLU_LIMIT = 7.0
SWIGLU_ALPHA = 1.702
DEEPNORM_ALPHA = (2.0 * DEPTH) ** 0.25
LN_EPS = 1e-5
RWKV_COLS = 3 * RWKV_DIM + DECAY_RANK + ICLR_RANK + GATE_RANK
ATTN_COLS = 3 * ATTN_DIM
GATE_COLS = 2 * D_MODEL
IN_COLS = RWKV_COLS + ATTN_COLS + GATE_COLS

LANE = 128
SUBLANE = 8
VMEM_LIMIT_BYTES = 56 * 1024 * 1024

CHUNK = 64


def _cparams(sem):
    return pltpu.CompilerParams(dimension_semantics=sem, vmem_limit_bytes=VMEM_LIMIT_BYTES)


def _dot(a, b):
    return jnp.dot(a, b, preferred_element_type=F32)


def _dot_hi(a, b):
    return jnp.dot(a, b, preferred_element_type=F32, precision=HI)


def _dot_nt(a, b, precision=None):
    return lax.dot_general(a, b, (((1,), (1,)), ((), ())), preferred_element_type=F32, precision=precision)


def _dot_tn(a, b, precision=None):
    return lax.dot_general(a, b, (((0,), (0,)), ((), ())), preferred_element_type=F32, precision=precision)


def _bf16_pieces(x, n):
    pieces = []
    for _ in range(n):
        p = x.astype(BF16)
        pieces.append(p)
        x = x - p.astype(F32)
    return pieces


def _dot_exact_rhs(x, b_bf16, n):
    out = None
    for p in _bf16_pieces(x, n):
        d = _dot(p, b_bf16)
        out = d if out is None else out + d
    return out


def _dot_exact_lhs(a_bf16, x, n):
    out = None
    for p in _bf16_pieces(x, n):
        d = _dot(a_bf16, p)
        out = d if out is None else out + d
    return out


PROJ_TM = 256


def _proj_kernel(x_ref, w_ref, cos_ref, sa_ref, sb_ref, prw_ref, q_ref, k_ref, v_ref, g_ref):
    xb = x_ref[...].astype(BF16)
    prw_ref[...] = _dot(xb, w_ref[:, 0:RWKV_COLS])
    c0 = RWKV_COLS
    cos = cos_ref[...]
    sa = sa_ref[...]
    sb = sb_ref[...]

    def rope(t):
        return t * cos + pltpu.roll(t, ATTN_DIM - ROPE_DIM // 2, 1) * sa + pltpu.roll(t, ROPE_DIM // 2, 1) * sb

    q_ref[...] = rope(_dot(xb, w_ref[:, c0:c0 + ATTN_DIM])).astype(BF16)
    k_ref[...] = rope(_dot(xb, w_ref[:, c0 + ATTN_DIM:c0 + 2 * ATTN_DIM])).astype(BF16)
    v_ref[...] = _dot(xb, w_ref[:, c0 + 2 * ATTN_DIM:c0 + 3 * ATTN_DIM]).astype(BF16)
    c1 = RWKV_COLS + ATTN_COLS
    g_ref[...] = jax.nn.sigmoid(_dot(xb, w_ref[:, c1:c1 + GATE_COLS])).astype(BF16)


def _rope_tables(T):
    half = ROPE_DIM // 2
    inv_freq = jnp.power(ROPE_THETA, -jnp.arange(0, ROPE_DIM, 2, dtype=F32) / ROPE_DIM)
    ang = jnp.arange(T).astype(F32)[:, None] * inv_freq[None, :]
    cos, sin = jnp.cos(ang), jnp.sin(ang)
    pad = jnp.zeros((T, ATTN_HEAD_DIM - ROPE_DIM), F32)
    cos_h = jnp.concatenate([cos, cos, pad + 1.0], axis=1)
    sa_h = jnp.concatenate([-sin, jnp.zeros((T, half), F32), pad], axis=1)
    sb_h = jnp.concatenate([jnp.zeros((T, half), F32), sin, pad], axis=1)
    tile = lambda t: jnp.tile(t, (1, ATTN_HEADS))
    return tile(cos_h), tile(sa_h), tile(sb_h)


def _proj(x2, w_in_bf, T):
    n_tok = x2.shape[0]
    tm = PROJ_TM
    t_tiles = T // tm
    cos, sa, sb = _rope_tables(T)
    row = lambda i: (i, 0)
    tab = lambda i: (i % t_tiles, 0)
    return pl.pallas_call(
        _proj_kernel,
        grid=(n_tok // tm,),
        in_specs=[
            pl.BlockSpec((tm, D_MODEL), row),
            pl.BlockSpec((D_MODEL, IN_COLS), lambda i: (0, 0), pipeline_mode=pl.Buffered(1)),
            pl.BlockSpec((tm, ATTN_DIM), tab),
            pl.BlockSpec((tm, ATTN_DIM), tab),
            pl.BlockSpec((tm, ATTN_DIM), tab),
        ],
        out_specs=[
            pl.BlockSpec((tm, RWKV_COLS), row),
            pl.BlockSpec((tm, ATTN_DIM), row),
            pl.BlockSpec((tm, ATTN_DIM), row),
            pl.BlockSpec((tm, ATTN_DIM), row),
            pl.BlockSpec((tm, GATE_COLS), row),
        ],
        out_shape=[
            jax.ShapeDtypeStruct((n_tok, RWKV_COLS), F32),
            jax.ShapeDtypeStruct((n_tok, ATTN_DIM), BF16),
            jax.ShapeDtypeStruct((n_tok, ATTN_DIM), BF16),
            jax.ShapeDtypeStruct((n_tok, ATTN_DIM), BF16),
            jax.ShapeDtypeStruct((n_tok, GATE_COLS), BF16),
        ],
        compiler_params=_cparams(("parallel",)),
        name="proj",
    )(x2, w_in_bf, cos, sa, sb)


RWKV_TT = 128


def _rwkv_kernel(p_ref, mix_ref, w0_ref, dup_ref, a0_ref, iup_ref, gup_ref, kk_ref, ka_ref, rk_ref,
                 gng_ref, gnb_ref, ones_ref, y_ref, s_scr, prev_scr):
    H, N, C = RWKV_HEADS, RWKV_HEAD_DIM, CHUNK

    @pl.when(pl.program_id(1) == 0)
    def _():
        s_scr[...] = jnp.zeros_like(s_scr)
        prev_scr[...] = jnp.zeros_like(prev_scr)

    bones = ones_ref[...]
    row = lax.broadcasted_iota(I32, (C, C), 0)
    col = lax.broadcasted_iota(I32, (C, C), 1)
    lower_incl = col <= row
    lower_strict = col < row
    ltri = jnp.where(lower_incl, 1.0, 0.0).astype(BF16)
    eye = jnp.where(col == row, 1.0, 0.0)
    first_row = lax.broadcasted_iota(I32, (C, RWKV_COLS), 0) == 0

    for c in range(RWKV_TT // C):
        p = p_ref[c * C:(c + 1) * C, :]
        prev = jnp.where(first_row, prev_scr[...], pltpu.roll(p, 1, 0))
        prev_scr[...] = p[C - 1:C, :]
        ps = p + (prev - p) * mix_ref[...]
        r = ps[:, 0:RWKV_DIM]
        k = ps[:, RWKV_DIM:2 * RWKV_DIM]
        v = ps[:, 2 * RWKV_DIM:3 * RWKV_DIM]
        o = 3 * RWKV_DIM
        xw = ps[:, o:o + DECAY_RANK]
        xa = ps[:, o + DECAY_RANK:o + DECAY_RANK + ICLR_RANK]
        xg = ps[:, o + DECAY_RANK + ICLR_RANK:RWKV_COLS]
        w_raw = w0_ref[...] + _dot(jnp.tanh(xw).astype(BF16), dup_ref[...])
        logw = -math.exp(-0.5) * jax.nn.sigmoid(w_raw)
        a = jax.nn.sigmoid(a0_ref[...] + _dot(xa.astype(BF16), iup_ref[...]))
        g = _dot(jax.nn.sigmoid(xg).astype(BF16), gup_ref[...])
        kk0 = k * kk_ref[...]
        kk = kk0 / jnp.maximum(jnp.sqrt(_dot_exact_rhs(kk0 * kk0, bones, 2)), 1e-12)
        kp = k * (1.0 + (a - 1.0) * ka_ref[...])
        kka = kk * a
        cum = _dot_exact_lhs(ltri, logw, 3)
        tot = cum[C - 1:C, :]
        e_neg = jnp.exp(-cum)
        at = (-kk * jnp.exp(cum - logw)).astype(BF16)
        rt = (r * jnp.exp(cum)).astype(BF16)
        bt = (kka * e_neg).astype(BF16)
        kt = (kp * e_neg).astype(BF16)
        e_end = jnp.exp(tot - cum)
        bh = (kka * e_end).astype(BF16)
        kh = (kp * e_end).astype(BF16)
        e_tot = jnp.exp(tot)
        vb = v.astype(BF16)

        hd = range(H)
        sls = [slice(h * N, (h + 1) * N) for h in hd]
        lhs = [jnp.concatenate([at[:, sl], rt[:, sl]], axis=0) for sl in sls]
        rhs = [jnp.concatenate([bt[:, sl], kt[:, sl]], axis=0) for sl in sls]
        s_old = [s_scr[h] for h in hd]
        aa = [_dot_nt(lhs[h], rhs[h]) for h in hd]
        ar_s = [_dot_nt(lhs[h], s_old[h].astype(BF16)) for h in hd]
        a_ab = [jnp.where(lower_strict, aa[h][0:C, 0:C], 0.0) for h in hd]
        a_kv = [jnp.concatenate([jnp.where(lower_strict, aa[h][0:C, C:2 * C], 0.0),
                                 jnp.where(lower_incl, aa[h][C:2 * C, C:2 * C], 0.0)], axis=0).astype(BF16)
                for h in hd]
        a_rb = [jnp.where(lower_incl, aa[h][C:2 * C, 0:C], 0.0).astype(BF16) for h in hd]
        akv = [_dot(a_kv[h], vb[:, sls[h]]) for h in hd]
        tinv = [eye + a_ab[h] for h in hd]
        npow = a_ab
        for _ in range(5):
            npb = [npow[h].astype(BF16) for h in hd]
            npow = [_dot(npb[h], npb[h]) for h in hd]
            tinv = [tinv[h] + _dot(tinv[h].astype(BF16), npow[h].astype(BF16)) for h in hd]
        ub = [_dot(tinv[h].astype(BF16), (ar_s[h][0:C] + akv[h][0:C]).astype(BF16)).astype(BF16) for h in hd]
        ys = [ar_s[h][C:2 * C] + akv[h][C:2 * C] + _dot(a_rb[h], ub[h]) for h in hd]
        upd = [_dot_tn(jnp.concatenate([ub[h], vb[:, sls[h]]], axis=0),
                       jnp.concatenate([bh[:, sls[h]], kh[:, sls[h]]], axis=0)) for h in hd]
        for h in hd:
            s_scr[h] = s_old[h] * e_tot[:, sls[h]] + upd[h]
        y = jnp.concatenate(ys, axis=1)
        mu = _dot_exact_rhs(y, bones, 2) * (1.0 / N)
        yc = y - mu
        var = _dot_exact_rhs(yc * yc, bones, 2) * (1.0 / N)
        yn = yc * lax.rsqrt(var + GN_EPS) * gng_ref[...] + gnb_ref[...]
        bonus = _dot_exact_rhs(r * kp * rk_ref[...], bones, 2) * v
        y_ref[c * C:(c + 1) * C, :] = ((yn + bonus) * g).astype(BF16)


def _rwkv(p_rwkv, B, T, shift_mix, decay_w0, decay_up, iclr_a0, iclr_up, gate_up, k_k, k_a, r_k, gn_g, gn_b):
    tt = RWKV_TT
    n_t = T // tt
    head = jnp.arange(RWKV_DIM) // RWKV_HEAD_DIM
    bones = (head[:, None] == head[None, :]).astype(BF16)
    vec = lambda a: a.reshape(1, -1)
    full = lambda shape: pl.BlockSpec(shape, lambda b, j: (0, 0))
    return pl.pallas_call(
        _rwkv_kernel,
        grid=(B, n_t),
        in_specs=[
            pl.BlockSpec((tt, RWKV_COLS), lambda b, j: (b * n_t + j, 0)),
            full((1, RWKV_COLS)), full((1, RWKV_DIM)), full((DECAY_RANK, RWKV_DIM)),
            full((1, RWKV_DIM)), full((ICLR_RANK, RWKV_DIM)), full((GATE_RANK, RWKV_DIM)),
            full((1, RWKV_DIM)), full((1, RWKV_DIM)), full((1, RWKV_DIM)),
            full((1, RWKV_DIM)), full((1, RWKV_DIM)), full((RWKV_DIM, RWKV_DIM)),
        ],
        out_specs=pl.BlockSpec((tt, RWKV_DIM), lambda b, j: (b * n_t + j, 0)),
        out_shape=jax.ShapeDtypeStruct((B * T, RWKV_DIM), BF16),
        scratch_shapes=[
            pltpu.VMEM((RWKV_HEADS, RWKV_HEAD_DIM, RWKV_HEAD_DIM), F32),
            pltpu.VMEM((1, RWKV_COLS), F32),
        ],
        compiler_params=_cparams(("parallel", "arbitrary")),
        name="rwkv",
    )(p_rwkv, vec(shift_mix), vec(decay_w0), decay_up.astype(BF16), vec(iclr_a0), iclr_up.astype(BF16),
      gate_up.astype(BF16), vec(k_k), vec(k_a), vec(r_k), vec(gn_g), vec(gn_b), bones)


MOBA_HP = 4


def _moba_kernel(q_ref, k_ref, v_ref, o_ref, kmean_scr, sel_scr, acc_scr):
    blk_sz, dh = MOBA_BLOCK, ATTN_HEAD_DIM
    nb = k_ref.shape[0] // blk_sz
    i = pl.program_id(2)

    @pl.when(i == 0)
    def _():
        for n in range(nb):
            kb = k_ref[n * blk_sz:(n + 1) * blk_sz, :].astype(F32)
            kmean_scr[n:n + 1, :] = jnp.sum(kb, axis=0, keepdims=True) * (1.0 / blk_sz)

    blk = lax.broadcasted_iota(I32, (nb, blk_sz), 0)
    kpos = lax.broadcasted_iota(I32, (blk_sz, blk_sz), 0)
    qpos = lax.broadcasted_iota(I32, (blk_sz, blk_sz), 1)
    own_start = pl.multiple_of(i * blk_sz, blk_sz)
    heads = [slice(hh * dh, (hh + 1) * dh) for hh in range(MOBA_HP)]
    qss, stats = [], []
    for hh, hs in enumerate(heads):
        qh = q_ref[:, hs]
        gate = _dot_nt(kmean_scr[:, hs], qh.astype(F32), precision=HI)
        gate = jnp.where(blk < i, gate, NEG_INF)
        rank = jnp.zeros((nb, blk_sz), I32)
        for m in range(nb):
            gm = gate[m:m + 1, :]
            beats = (gm > gate) | ((gm == gate) & (m < blk))
            rank = rank + beats.astype(I32)
        sel_scr[hh] = ((rank < MOBA_TOPK) & (blk < i)).astype(F32)

        qs = qh * (dh ** -0.5)
        s = _dot_nt(k_ref[pl.ds(own_start, blk_sz), hs], qs)
        s = jnp.where(kpos <= qpos, s, NEG_INF)
        m0 = jnp.max(s, axis=0, keepdims=True)
        p = jnp.exp(s - m0)
        l0 = jnp.sum(p, axis=0, keepdims=True)
        acc_scr[hh] = _dot_tn(v_ref[pl.ds(own_start, blk_sz), hs], p.astype(BF16))
        qss.append(qs)
        stats += [m0, l0]

    def body(n, carry):
        start = pl.multiple_of(n * blk_sz, blk_sz)
        ss = [_dot_nt(k_ref[pl.ds(start, blk_sz), hs], qss[hh]) for hh, hs in enumerate(heads)]
        out, ps, alphas = [], [], []
        for hh in range(MOBA_HP):
            m_run, l_run = carry[2 * hh], carry[2 * hh + 1]
            s = jnp.where(sel_scr[hh, pl.ds(n, 1), :] > 0.0, ss[hh], NEG_INF)
            m_new = jnp.maximum(m_run, jnp.max(s, axis=0, keepdims=True))
            alpha = jnp.exp(m_run - m_new)
            p = jnp.exp(s - m_new)
            out += [m_new, alpha * l_run + jnp.sum(p, axis=0, keepdims=True)]
            ps.append(p.astype(BF16))
            alphas.append(alpha)
        pvs = [_dot_tn(v_ref[pl.ds(start, blk_sz), hs], ps[hh]) for hh, hs in enumerate(heads)]
        for hh in range(MOBA_HP):
            acc_scr[hh] = alphas[hh] * acc_scr[hh] + pvs[hh]
        return tuple(out)

    stats = lax.fori_loop(0, i, body, tuple(stats))
    outs = [(acc_scr[hh] / stats[2 * hh + 1]).T for hh in range(MOBA_HP)]
    o_ref[...] = jnp.concatenate(outs, axis=1).astype(BF16)


def _moba(q, k, v, B, T):
    blk_sz = MOBA_BLOCK
    nq = T // blk_sz
    lanes = MOBA_HP * ATTN_HEAD_DIM
    kv_spec = pl.BlockSpec((T, lanes), lambda b, hp, i: (b, hp))
    q_spec = pl.BlockSpec((blk_sz, lanes), lambda b, hp, i: (b * nq + i, hp))
    return pl.pallas_call(
        _moba_kernel,
        grid=(B, ATTN_HEADS // MOBA_HP, nq),
        in_specs=[q_spec, kv_spec, kv_spec],
        out_specs=q_spec,
        out_shape=jax.ShapeDtypeStruct((B * T, ATTN_DIM), BF16),
        scratch_shapes=[
            pltpu.VMEM((T // blk_sz, lanes), F32),
            pltpu.VMEM((MOBA_HP, T // blk_sz, blk_sz), F32),
            pltpu.VMEM((MOBA_HP, ATTN_HEAD_DIM, blk_sz), F32),
        ],
        compiler_params=_cparams(("parallel", "parallel", "arbitrary")),
        name="moba",
    )(q, k, v)


MERGE_TM = 256


def _layer_norm(h, g, b):
    mu = jnp.mean(h, axis=-1, keepdims=True)
    hc = h - mu
    var = jnp.mean(hc * hc, axis=-1, keepdims=True)
    return hc * lax.rsqrt(var + LN_EPS) * g + b


def _merge_kernel(x_ref, yr_ref, ya_ref, g_ref, wbr_ref, wba_ref, wo_ref, lng_ref, lnb_ref, rwt_ref, rb_ref,
                  x1_ref, lgt_ref):
    yr = _dot(yr_ref[...], wbr_ref[...])
    ya = _dot(ya_ref[...], wba_ref[...])
    merged = g_ref[:, 0:D_MODEL].astype(F32) * yr + g_ref[:, D_MODEL:2 * D_MODEL].astype(F32) * ya
    mix = _dot(merged.astype(BF16), wo_ref[...])
    x1 = _layer_norm(DEEPNORM_ALPHA * x_ref[...] + mix, lng_ref[...], lnb_ref[...])
    x1_ref[...] = x1
    lgt_ref[...] = _dot_nt(rwt_ref[...], x1, precision=HI) + rb_ref[...]


def _merge(x2, y_rwkv, y_attn, gates, w_br, w_ba, w_o, ln_g, ln_b, router_w, router_b):
    n_tok = x2.shape[0]
    tm = MERGE_TM
    row = lambda i: (i, 0)
    full = lambda shape: pl.BlockSpec(shape, lambda i: (0, 0))
    return pl.pallas_call(
        _merge_kernel,
        grid=(n_tok // tm,),
        in_specs=[
            pl.BlockSpec((tm, D_MODEL), row), pl.BlockSpec((tm, RWKV_DIM), row), pl.BlockSpec((tm, ATTN_DIM), row),
            pl.BlockSpec((tm, GATE_COLS), row),
            full((RWKV_DIM, D_MODEL)), full((ATTN_DIM, D_MODEL)), full((D_MODEL, D_MODEL)),
            full((1, D_MODEL)), full((1, D_MODEL)), full((N_EXPERTS, D_MODEL)), full((N_EXPERTS, 1)),
        ],
        out_specs=[pl.BlockSpec((tm, D_MODEL), row), pl.BlockSpec((N_EXPERTS, tm), lambda i: (0, i))],
        out_shape=[jax.ShapeDtypeStruct((n_tok, D_MODEL), F32), jax.ShapeDtypeStruct((N_EXPERTS, n_tok), F32)],
        compiler_params=_cparams(("parallel",)),
        name="merge",
    )(x2, y_rwkv, y_attn, gates, w_br.astype(BF16), w_ba.astype(BF16), w_o.astype(BF16),
      ln_g.reshape(1, -1), ln_b.reshape(1, -1), router_w.T, router_b.reshape(-1, 1))


MOE_TM = 512
ROW_BLOCK = 256
HALF = D_MODEL // 2
PIECE_SIZES = (512, 256, 128, 64, 32, 16, 8)
STAGE_CHUNK = 256


def _moe_dims(n_tok):
    n_tiles = n_tok // MOE_TM
    stage_rows = -(-(MOE_TM * TOP_K + N_EXPERTS * (SUBLANE - 1)) // STAGE_CHUNK) * STAGE_CHUNK
    max_rows = n_tok * TOP_K + n_tiles * N_EXPERTS * (SUBLANE - 1) + N_EXPERTS * (ROW_BLOCK - 1)
    n_blk = -(-max_rows // ROW_BLOCK)
    map_lanes = -(-(n_blk + 1) // LANE) * LANE
    return n_tiles, stage_rows, n_blk, map_lanes


def _pack_halves(lo, hi):
    return (lax.bitcast_convert_type(lo, U32) >> 16) | (lax.bitcast_convert_type(hi, U32) & jnp.uint32(0xFFFF0000))


def _unpack_halves(u):
    lo = lax.bitcast_convert_type(u << 16, F32)
    hi = lax.bitcast_convert_type(u & jnp.uint32(0xFFFF0000), F32)
    return lo.astype(BF16), hi.astype(BF16)


def _round_up_f32(x, m):
    return jnp.floor((x + (m - 1)) * (1.0 / m)) * m


def _route_kernel(lg_ref, wt_ref, slot_ref, off_ref, cnt_ref, map_ref, idx_scr, pos_scr, cnt_scr, off_scr):
    tm = MOE_TM
    n_tiles = cnt_scr.shape[1]
    phase = pl.program_id(0)
    i = pl.program_id(1)
    tok0 = pl.multiple_of(i * tm, tm)
    eio = lax.broadcasted_iota(I32, (N_EXPERTS, tm), 0)
    tile_lane = lax.broadcasted_iota(I32, (N_EXPERTS, n_tiles), 1)
    e_from = lax.broadcasted_iota(I32, (N_EXPERTS, N_EXPERTS), 1)
    e_to = lax.broadcasted_iota(I32, (N_EXPERTS, N_EXPERTS), 0)
    earlier_e = jnp.where(e_from < e_to, 1.0, 0.0)

    @pl.when((phase == 0) & (i == 0))
    def _():
        cnt_scr[...] = jnp.zeros_like(cnt_scr)

    @pl.when(phase == 0)
    def _():
        work = lg_ref[...]
        vals, hots = [], []
        for s in range(TOP_K):
            m = jnp.max(work, axis=0, keepdims=True)
            ix = jnp.min(jnp.where(work == m, eio, N_EXPERTS), axis=0, keepdims=True)
            hot = eio == ix
            idx_scr[s:s + 1, pl.ds(tok0, tm)] = ix
            vals.append(m)
            hots.append(hot)
            work = jnp.where(hot, -jnp.inf, work)
        es = [jnp.exp(v - vals[0]) for v in vals]
        denom = es[0] + es[1] + es[2] + es[3]
        for s in range(TOP_K):
            wt_ref[s:s + 1, :] = es[s] / denom
        multi_f = jnp.where(hots[0] | hots[1] | hots[2] | hots[3], 1.0, 0.0)
        t_from = lax.broadcasted_iota(I32, (tm, tm), 0)
        t_to = lax.broadcasted_iota(I32, (tm, tm), 1)
        before = jnp.where(t_from < t_to, 1.0, 0.0).astype(BF16)
        count = _dot(multi_f.astype(BF16), before)
        for s in range(TOP_K):
            pos_scr[s:s + 1, pl.ds(tok0, tm)] = jnp.sum(
                jnp.where(hots[s], count, 0.0), axis=0, keepdims=True).astype(I32)
        cnt8 = _round_up_f32(jnp.sum(multi_f, axis=1, keepdims=True), SUBLANE)
        cnt_scr[...] = cnt_scr[...] + jnp.where(tile_lane == i, cnt8, 0.0)

    @pl.when((phase == 1) & (i == 0))
    def _():
        cnt8 = cnt_scr[...]
        tot = _round_up_f32(jnp.sum(cnt8, axis=1, keepdims=True), ROW_BLOCK)
        p_start = _dot_hi(earlier_e, jnp.broadcast_to(tot, (N_EXPERTS, n_tiles)))
        i_from = lax.broadcasted_iota(I32, (n_tiles, n_tiles), 0)
        i_to = lax.broadcasted_iota(I32, (n_tiles, n_tiles), 1)
        earlier_tiles = _dot_hi(cnt8, jnp.where(i_from < i_to, 1.0, 0.0))
        off_scr[...] = p_start + earlier_tiles
        off_ref[...] = off_scr[...].astype(I32)
        cnt_ref[...] = cnt8.astype(I32)
        p_end = p_start[:, 0:1] + tot
        lanes = map_ref.shape[1]
        blk_start = lax.broadcasted_iota(I32, (N_EXPERTS, lanes), 1).astype(F32) * ROW_BLOCK
        blk_e = jnp.sum(jnp.where(blk_start >= p_end, 1.0, 0.0), axis=0, keepdims=True)
        blk_e = jnp.minimum(blk_e, N_EXPERTS - 1.0)
        n_used = jnp.max(p_end, axis=0, keepdims=True) * (1.0 / ROW_BLOCK)
        last = lax.broadcasted_iota(I32, (1, lanes), 1) == lanes - 1
        map_ref[...] = jnp.where(last, n_used, blk_e).astype(I32)

    @pl.when(phase == 1)
    def _():
        cnt_col = jnp.sum(jnp.where(tile_lane == i, cnt_scr[...], 0.0), axis=1, keepdims=True)
        local_off = _dot_hi(earlier_e, jnp.broadcast_to(cnt_col, (N_EXPERTS, tm)))
        for s in range(TOP_K):
            hot = eio == idx_scr[s:s + 1, pl.ds(tok0, tm)]
            base = jnp.sum(jnp.where(hot, local_off, 0.0), axis=0, keepdims=True).astype(I32)
            slot_ref[s:s + 1, :] = base + pos_scr[s:s + 1, pl.ds(tok0, tm)]


def _route(logits_t):
    n_tok = logits_t.shape[1]
    n_tiles, _, _, map_lanes = _moe_dims(n_tok)
    tm = MOE_TM
    last = n_tiles - 1
    const = lambda p, i: (0, 0)
    return pl.pallas_call(
        _route_kernel,
        grid=(2, n_tiles),
        in_specs=[pl.BlockSpec((N_EXPERTS, tm), lambda p, i: (0, i * (1 - p) + last * p))],
        out_specs=[
            pl.BlockSpec((TOP_K, tm), lambda p, i: (0, i * (1 - p) + last * p)),
            pl.BlockSpec((TOP_K, tm), lambda p, i: (0, i * p)),
            pl.BlockSpec((N_EXPERTS, n_tiles), const),
            pl.BlockSpec((N_EXPERTS, n_tiles), const),
            pl.BlockSpec((1, map_lanes), const),
        ],
        out_shape=[
            jax.ShapeDtypeStruct((TOP_K, n_tok), F32),
            jax.ShapeDtypeStruct((TOP_K, n_tok), I32),
            jax.ShapeDtypeStruct((N_EXPERTS, n_tiles), I32),
            jax.ShapeDtypeStruct((N_EXPERTS, n_tiles), I32),
            jax.ShapeDtypeStruct((1, map_lanes), I32),
        ],
        scratch_shapes=[
            pltpu.VMEM((TOP_K, n_tok), I32),
            pltpu.VMEM((TOP_K, n_tok), I32),
            pltpu.VMEM((N_EXPERTS, n_tiles), F32),
            pltpu.VMEM((N_EXPERTS, n_tiles), F32),
        ],
        compiler_params=_cparams(("arbitrary", "arbitrary")),
        name="route",
    )(logits_t)


def _piece_copies(off_ref, cnt_ref, make_copy, action):
    tile = pl.program_id(0)

    def per_expert(e, stage_row):
        c8 = cnt_ref[tile * N_EXPERTS + e]
        hbm_row = off_ref[tile * N_EXPERTS + e]
        done = 0
        for size in PIECE_SIZES:
            bit = c8 & size

            @pl.when(bit != 0)
            def _(done=done, size=size):
                action(make_copy(pl.multiple_of(stage_row + done, SUBLANE),
                                 pl.multiple_of(hbm_row + done, SUBLANE), size))

            done = done + bit
        return stage_row + c8

    lax.fori_loop(0, N_EXPERTS, per_expert, 0)


def _slot_rows(slot_ref):
    return [slot_ref[s:s + 1, :] for s in range(TOP_K)]


def _dispatch_kernel(off_ref, cnt_ref, slot_ref, x1_ref, xs_in_hbm, xs_hbm, stage, sem):
    del xs_in_hbm
    xb = x1_ref[...].astype(BF16)
    slots = _slot_rows(slot_ref)
    for r0 in range(0, stage.shape[0], STAGE_CHUNK):
        rio = lax.broadcasted_iota(I32, (STAGE_CHUNK, MOE_TM), 0) + r0
        pick = jnp.zeros((STAGE_CHUNK, MOE_TM), F32)
        for s in range(TOP_K):
            pick = jnp.where(rio == slots[s], 1.0, pick)
        rows = _dot(pick.astype(BF16), xb)
        stage[r0:r0 + STAGE_CHUNK, :] = _pack_halves(rows[:, 0:HALF], rows[:, HALF:D_MODEL])

    def make_copy(stage_row, hbm_row, size):
        return pltpu.make_async_copy(stage.at[pl.ds(stage_row, size)], xs_hbm.at[pl.ds(hbm_row, size)], sem)

    _piece_copies(off_ref, cnt_ref, make_copy, lambda cp: cp.start())
    _piece_copies(off_ref, cnt_ref, make_copy, lambda cp: cp.wait())


def _dispatch(off_flat, cnt_flat, slot, x1):
    n_tok = x1.shape[0]
    n_tiles, stage_rows, n_blk, _ = _moe_dims(n_tok)
    n_rows = n_blk * ROW_BLOCK
    any_spec = pl.BlockSpec(memory_space=pl.ANY)
    return pl.pallas_call(
        _dispatch_kernel,
        grid_spec=pltpu.PrefetchScalarGridSpec(
            num_scalar_prefetch=2,
            grid=(n_tiles,),
            in_specs=[pl.BlockSpec((TOP_K, MOE_TM), lambda i, off, cnt: (0, i)),
                      pl.BlockSpec((MOE_TM, D_MODEL), lambda i, off, cnt: (i, 0)),
                      any_spec],
            out_specs=any_spec,
            scratch_shapes=[pltpu.VMEM((stage_rows, HALF), U32), pltpu.SemaphoreType.DMA(())],
        ),
        out_shape=jax.ShapeDtypeStruct((n_rows, HALF), U32),
        input_output_aliases={4: 0},
        compiler_params=_cparams(("arbitrary",)),
        name="dispatch",
    )(off_flat, cnt_flat, slot, x1, jnp.zeros((n_rows, HALF), U32))


def _combine_kernel(off_ref, cnt_ref, slot_ref, wt_ref, x1_ref, lng_ref, lnb_ref, ys_hbm, out_ref, stage, sem):
    @pl.when(pl.program_id(0) == 0)
    def _():
        stage[...] = jnp.zeros_like(stage)

    def make_copy(stage_row, hbm_row, size):
        return pltpu.make_async_copy(ys_hbm.at[pl.ds(hbm_row, size)], stage.at[pl.ds(stage_row, size)], sem)

    _piece_copies(off_ref, cnt_ref, make_copy, lambda cp: cp.start())
    _piece_copies(off_ref, cnt_ref, make_copy, lambda cp: cp.wait())

    slots = _slot_rows(slot_ref)
    acc_lo = jnp.zeros((MOE_TM, HALF), F32)
    acc_hi = jnp.zeros((MOE_TM, HALF), F32)
    for r0 in range(0, stage.shape[0], STAGE_CHUNK):
        rio = lax.broadcasted_iota(I32, (STAGE_CHUNK, MOE_TM), 0) + r0
        w = jnp.zeros((STAGE_CHUNK, MOE_TM), F32)
        for s in range(TOP_K):
            w = jnp.where(rio == slots[s], wt_ref[s:s + 1, :], w)
        w_a, w_b = _bf16_pieces(w, 2)
        y_lo, y_hi = _unpack_halves(stage[r0:r0 + STAGE_CHUNK, :])
        acc_lo = acc_lo + _dot_tn(w_a, y_lo) + _dot_tn(w_b, y_lo)
        acc_hi = acc_hi + _dot_tn(w_a, y_hi) + _dot_tn(w_b, y_hi)
    ffn = jnp.concatenate([acc_lo, acc_hi], axis=1)
    out_ref[...] = _layer_norm(DEEPNORM_ALPHA * x1_ref[...] + ffn, lng_ref[...], lnb_ref[...])


def _combine(off_flat, cnt_flat, slot, wts, x1, ln_g, ln_b, ys):
    n_tok = x1.shape[0]
    n_tiles, stage_rows, _, _ = _moe_dims(n_tok)
    tok = lambda i, off, cnt: (0, i)
    row = lambda i, off, cnt: (i, 0)
    full = lambda shape: pl.BlockSpec(shape, lambda i, off, cnt: (0, 0))
    return pl.pallas_call(
        _combine_kernel,
        grid_spec=pltpu.PrefetchScalarGridSpec(
            num_scalar_prefetch=2,
            grid=(n_tiles,),
            in_specs=[pl.BlockSpec((TOP_K, MOE_TM), tok), pl.BlockSpec((TOP_K, MOE_TM), tok),
                      pl.BlockSpec((MOE_TM, D_MODEL), row), full((1, D_MODEL)), full((1, D_MODEL)),
                      pl.BlockSpec(memory_space=pl.ANY)],
            out_specs=pl.BlockSpec((MOE_TM, D_MODEL), row),
            scratch_shapes=[pltpu.VMEM((stage_rows, HALF), U32), pltpu.SemaphoreType.DMA(())],
        ),
        out_shape=jax.ShapeDtypeStruct((n_tok, D_MODEL), F32),
        compiler_params=_cparams(("arbitrary",)),
        name="combine",
    )(off_flat, cnt_flat, slot, wts, x1, ln_g.reshape(1, -1), ln_b.reshape(1, -1), ys)


def _expert_kernel(blk_e_ref, n_used_ref, xs_ref, win_ref, bin_ref, wout_ref, bout_ref, ys_ref):
    del blk_e_ref

    @pl.when(pl.program_id(0) < n_used_ref[0])
    def _():
        x_lo, x_hi = _unpack_halves(xs_ref[...])
        h = _dot(x_lo, win_ref[0, 0:HALF, :]) + _dot(x_hi, win_ref[0, HALF:D_MODEL, :]) + bin_ref[0]
        gate_h = jnp.minimum(h[:, 0:D_EXPERT], SWIGLU_LIMIT)
        lin_h = jnp.clip(h[:, D_EXPERT:2 * D_EXPERT], -SWIGLU_LIMIT, SWIGLU_LIMIT)
        act = gate_h * jax.nn.sigmoid(SWIGLU_ALPHA * gate_h) * (lin_h + 1.0)
        y = (_dot(act.astype(BF16), wout_ref[0]) + bout_ref[0]).astype(BF16).astype(F32)
        ys_ref[...] = _pack_halves(y[:, 0:HALF], y[:, HALF:D_MODEL])

    @pl.when(pl.program_id(0) >= n_used_ref[0])
    def _():
        ys_ref[...] = jnp.zeros_like(ys_ref)


def _experts(blk_e, n_used, xs, w_in_bf, b_in, w_out_bf, b_out):
    n_rows = xs.shape[0]
    n_blk = n_rows // ROW_BLOCK
    used = lambda rb, n_used: jnp.minimum(rb, n_used[0] - 1)
    rows = lambda rb, blk_e, n_used: (used(rb, n_used), 0)
    per_e = lambda rb, blk_e, n_used: (blk_e[used(rb, n_used)], 0, 0)
    return pl.pallas_call(
        _expert_kernel,
        grid_spec=pltpu.PrefetchScalarGridSpec(
            num_scalar_prefetch=2,
            grid=(n_blk,),
            in_specs=[
                pl.BlockSpec((ROW_BLOCK, HALF), rows),
                pl.BlockSpec((1, D_MODEL, 2 * D_EXPERT), per_e),
                pl.BlockSpec((1, 1, 2 * D_EXPERT), per_e),
                pl.BlockSpec((1, D_EXPERT, D_MODEL), per_e),
                pl.BlockSpec((1, 1, D_MODEL), per_e),
            ],
            out_specs=pl.BlockSpec((ROW_BLOCK, HALF), lambda rb, blk_e, n_used: (rb, 0)),
        ),
        out_shape=jax.ShapeDtypeStruct((n_rows, HALF), U32),
        compiler_params=_cparams(("arbitrary",)),
        name="experts",
    )(blk_e, n_used, xs, w_in_bf, b_in.reshape(N_EXPERTS, 1, -1), w_out_bf, b_out.reshape(N_EXPERTS, 1, -1))


def _moe(x1, logits_t, expert_w_in, expert_b_in, expert_w_out, expert_b_out, ln_g, ln_b):
    n_tok = x1.shape[0]
    _, _, n_blk, map_lanes = _moe_dims(n_tok)
    wts, slot, off, cnt, blk_map = _route(logits_t)
    off_flat = off.T.reshape(-1)
    cnt_flat = cnt.T.reshape(-1)
    blk_e = blk_map[0, 0:n_blk]
    n_used = blk_map[0, map_lanes - 1:map_lanes]
    xs = _dispatch(off_flat, cnt_flat, slot, x1)
    ys = _experts(blk_e, n_used, xs, expert_w_in.astype(BF16), expert_b_in, expert_w_out.astype(BF16), expert_b_out)
    return _combine(off_flat, cnt_flat, slot, wts, x1, ln_g, ln_b, ys)


def kernel(x, ln1_g, ln1_b, ln2_g, ln2_b, w_in, shift_mix, decay_w0, decay_up, iclr_a0, iclr_up, gate_up, k_k, k_a, r_k, gn_g, gn_b, w_branch_rwkv, w_branch_attn, w_out, router_w, router_b, expert_w_in, expert_b_in, expert_w_out, expert_b_out):
    B, T, D = x.shape
    x2 = x.reshape(B * T, D)
    prw, q, k, v, gates = _proj(x2, w_in[0].astype(BF16), T)
    y_rwkv = _rwkv(prw, B, T, shift_mix[0], decay_w0[0], decay_up[0], iclr_a0[0], iclr_up[0], gate_up[0],
                   k_k[0], k_a[0], r_k[0], gn_g[0], gn_b[0])
    y_attn = _moba(q, k, v, B, T)
    x1, logits_t = _merge(x2, y_rwkv, y_attn, gates, w_branch_rwkv[0], w_branch_attn[0], w_out[0],
                          ln1_g[0], ln1_b[0], router_w[0], router_b[0])
    out = _moe(x1, logits_t, expert_w_in[0], expert_b_in[0], expert_w_out[0], expert_b_out[0], ln2_g[0], ln2_b[0])
    return out.reshape(B, T, D)
```

```python
import math

import jax
import jax.numpy as jnp
from jax import lax
from jax.experimental import pallas as pl
from jax.experimental.pallas import tpu as pltpu

F32 = jnp.float32
BF16 = jnp.bfloat16
I32 = jnp.int32
HI = lax.Precision.HIGHEST

D_MODEL = 1024
DEPTH = 1
RWKV_HEAD_DIM = 64
RWKV_DIM = 512
RWKV_HEADS = 8
DECAY_RANK = 64
ICLR_RANK = 64
GATE_RANK = 128
GN_EPS = 64e-5
ATTN_HEAD_DIM = 64
ATTN_DIM = 512
ATTN_HEADS = 8
MOBA_BLOCK = 256
MOBA_TOPK = 3
ROPE_THETA = 500000.0
ROPE_DIM = 16
NEG_INF = -1e30
N_EXPERTS = 32
TOP_K = 4
D_EXPERT = 1024
SWIGLU_LIMIT = 7.0
SWIGLU_ALPHA = 1.702
DEEPNORM_ALPHA = (2.0 * DEPTH) ** 0.25
LN_EPS = 1e-5
RWKV_COLS = 3 * RWKV_DIM + DECAY_RANK + ICLR_RANK + GATE_RANK
ATTN_COLS = 3 * ATTN_DIM
GATE_COLS = 2 * D_MODEL
IN_COLS = RWKV_COLS + ATTN_COLS + GATE_COLS

LANE = 128
SUBLANE = 8
VMEM_LIMIT_BYTES = 56 * 1024 * 1024

CHUNK = 64


def _cparams(sem):
    return pltpu.CompilerParams(dimension_semantics=sem, vmem_limit_bytes=VMEM_LIMIT_BYTES)


def _dot(a, b):
    return jnp.dot(a, b, preferred_element_type=F32)


def _dot_hi(a, b):
    return jnp.dot(a, b, preferred_element_type=F32, precision=HI)


def _dot_nt(a, b, precision=None):
    return lax.dot_general(a, b, (((1,), (1,)), ((), ())), preferred_element_type=F32, precision=precision)


def _dot_tn(a, b, precision=None):
    return lax.dot_general(a, b, (((0,), (0,)), ((), ())), preferred_element_type=F32, precision=precision)


def _bf16_pieces(x, n):
    pieces = []
    for _ in range(n):
        p = x.astype(BF16)
        pieces.append(p)
        x = x - p.astype(F32)
    return pieces


def _dot_exact_rhs(x, b_bf16, n):
    out = None
    for p in _bf16_pieces(x, n):
        d = _dot(p, b_bf16)
        out = d if out is None else out + d
    return out


PROJ_TM = 256


def _proj_kernel(x_ref, w_ref, cos_ref, sa_ref, sb_ref, prw_ref, q_ref, k_ref, v_ref, g_ref):
    xb = x_ref[...].astype(BF16)
    prw_ref[...] = _dot(xb, w_ref[:, 0:RWKV_COLS])
    c0 = RWKV_COLS
    cos = cos_ref[...]
    sa = sa_ref[...]
    sb = sb_ref[...]

    def rope(t):
        return t * cos + pltpu.roll(t, ATTN_DIM - ROPE_DIM // 2, 1) * sa + pltpu.roll(t, ROPE_DIM // 2, 1) * sb

    q_ref[...] = rope(_dot(xb, w_ref[:, c0:c0 + ATTN_DIM])).astype(BF16)
    k_ref[...] = rope(_dot(xb, w_ref[:, c0 + ATTN_DIM:c0 + 2 * ATTN_DIM])).astype(BF16)
    v_ref[...] = _dot(xb, w_ref[:, c0 + 2 * ATTN_DIM:c0 + 3 * ATTN_DIM]).astype(BF16)
    c1 = RWKV_COLS + ATTN_COLS
    g_ref[...] = jax.nn.sigmoid(_dot(xb, w_ref[:, c1:c1 + GATE_COLS])).astype(BF16)


def _rope_tables(T):
    half = ROPE_DIM // 2
    inv_freq = jnp.power(ROPE_THETA, -jnp.arange(0, ROPE_DIM, 2, dtype=F32) / ROPE_DIM)
    ang = jnp.arange(T).astype(F32)[:, None] * inv_freq[None, :]
    cos, sin = jnp.cos(ang), jnp.sin(ang)
    pad = jnp.zeros((T, ATTN_HEAD_DIM - ROPE_DIM), F32)
    cos_h = jnp.concatenate([cos, cos, pad + 1.0], axis=1)
    sa_h = jnp.concatenate([-sin, jnp.zeros((T, half), F32), pad], axis=1)
    sb_h = jnp.concatenate([jnp.zeros((T, half), F32), sin, pad], axis=1)
    tile = lambda t: jnp.tile(t, (1, ATTN_HEADS))
    return tile(cos_h), tile(sa_h), tile(sb_h)


def _proj(x2, w_in_bf, T):
    n_tok = x2.shape[0]
    tm = PROJ_TM
    t_tiles = T // tm
    cos, sa, sb = _rope_tables(T)
    row = lambda i: (i, 0)
    tab = lambda i: (i % t_tiles, 0)
    return pl.pallas_call(
        _proj_kernel,
        grid=(n_tok // tm,),
        in_specs=[
            pl.BlockSpec((tm, D_MODEL), row),
            pl.BlockSpec((D_MODEL, IN_COLS), lambda i: (0, 0), pipeline_mode=pl.Buffered(1)),
            pl.BlockSpec((tm, ATTN_DIM), tab),
            pl.BlockSpec((tm, ATTN_DIM), tab),
            pl.BlockSpec((tm, ATTN_DIM), tab),
        ],
        out_specs=[
            pl.BlockSpec((tm, RWKV_COLS), row),
            pl.BlockSpec((tm, ATTN_DIM), row),
            pl.BlockSpec((tm, ATTN_DIM), row),
            pl.BlockSpec((tm, ATTN_DIM), row),
            pl.BlockSpec((tm, GATE_COLS), row),
        ],
        out_shape=[
            jax.ShapeDtypeStruct((n_tok, RWKV_COLS), F32),
            jax.ShapeDtypeStruct((n_tok, ATTN_DIM), BF16),
            jax.ShapeDtypeStruct((n_tok, ATTN_DIM), BF16),
            jax.ShapeDtypeStruct((n_tok, ATTN_DIM), BF16),
            jax.ShapeDtypeStruct((n_tok, GATE_COLS), BF16),
        ],
        compiler_params=_cparams(("parallel",)),
        name="proj",
    )(x2, w_in_bf, cos, sa, sb)


RWKV_TT = 128


def _rwkv_kernel(p_ref, mix_ref, w0_ref, dup_ref, a0_ref, iup_ref, gup_ref, kk_ref, ka_ref, rk_ref,
                 gng_ref, gnb_ref, ones_ref, y_ref, s_scr, prev_scr):
    H, N, C = RWKV_HEADS, RWKV_HEAD_DIM, CHUNK

    @pl.when(pl.program_id(1) == 0)
    def _():
        s_scr[...] = jnp.zeros_like(s_scr)
        prev_scr[...] = jnp.zeros_like(prev_scr)

    TT = RWKV_TT
    n_chunks = TT // C
    bones = ones_ref[...]
    row = lax.broadcasted_iota(I32, (C, C), 0)
    col = lax.broadcasted_iota(I32, (C, C), 1)
    lower_incl = col <= row
    lower_strict = col < row
    eye = jnp.where(col == row, 1.0, 0.0)
    t_row = lax.broadcasted_iota(I32, (TT, TT), 0)
    t_col = lax.broadcasted_iota(I32, (TT, TT), 1)
    same_chunk = (t_row // C) == (t_col // C)
    chunk_ltri = jnp.where(same_chunk & (t_col <= t_row), 1.0, 0.0).astype(BF16)
    chunk_ones = jnp.where(same_chunk, 1.0, 0.0).astype(BF16)

    p = p_ref[...]
    first_row = lax.broadcasted_iota(I32, (TT, RWKV_COLS), 0) == 0
    prev = jnp.where(first_row, prev_scr[...], pltpu.roll(p, 1, 0))
    prev_scr[...] = p[TT - 1:TT, :]
    ps = p + (prev - p) * mix_ref[...]
    r = ps[:, 0:RWKV_DIM]
    k = ps[:, RWKV_DIM:2 * RWKV_DIM]
    v = ps[:, 2 * RWKV_DIM:3 * RWKV_DIM]
    o = 3 * RWKV_DIM
    xw = ps[:, o:o + DECAY_RANK]
    xa = ps[:, o + DECAY_RANK:o + DECAY_RANK + ICLR_RANK]
    xg = ps[:, o + DECAY_RANK + ICLR_RANK:RWKV_COLS]
    w_raw = w0_ref[...] + _dot(jnp.tanh(xw).astype(BF16), dup_ref[...])
    logw = -math.exp(-0.5) * jax.nn.sigmoid(w_raw)
    a = jax.nn.sigmoid(a0_ref[...] + _dot(xa.astype(BF16), iup_ref[...]))
    g = _dot(jax.nn.sigmoid(xg).astype(BF16), gup_ref[...])
    kk0 = k * kk_ref[...]
    kk = kk0 / jnp.maximum(jnp.sqrt(_dot_exact_rhs(kk0 * kk0, bones, 2)), 1e-12)
    kp = k * (1.0 + (a - 1.0) * ka_ref[...])
    kka = kk * a
    logw_pieces = _bf16_pieces(logw, 3)
    cum = sum(_dot(chunk_ltri, piece) for piece in logw_pieces)
    tot = sum(_dot(chunk_ones, piece) for piece in logw_pieces)
    e_neg = jnp.exp(-cum)
    at = (-kk * jnp.exp(cum - logw)).astype(BF16)
    rt = (r * jnp.exp(cum)).astype(BF16)
    bt = (kka * e_neg).astype(BF16)
    kt = (kp * e_neg).astype(BF16)
    e_end = jnp.exp(tot - cum)
    bh = (kka * e_end).astype(BF16)
    kh = (kp * e_end).astype(BF16)
    e_tot = jnp.exp(tot)
    vb = v.astype(BF16)

    units = [(c, h) for c in range(n_chunks) for h in range(H)]
    blk = lambda t, c, h: t[c * C:(c + 1) * C, h * N:(h + 1) * N]
    lhs = {u: jnp.concatenate([blk(at, *u), blk(rt, *u)], axis=0) for u in units}
    rhs = {u: jnp.concatenate([blk(bt, *u), blk(kt, *u)], axis=0) for u in units}
    aa = {u: _dot_nt(lhs[u], rhs[u]) for u in units}
    a_ab = {u: jnp.where(lower_strict, aa[u][0:C, 0:C], 0.0) for u in units}
    a_kv = {u: jnp.concatenate([jnp.where(lower_strict, aa[u][0:C, C:2 * C], 0.0),
                                jnp.where(lower_incl, aa[u][C:2 * C, C:2 * C], 0.0)], axis=0).astype(BF16)
            for u in units}
    a_rb = {u: jnp.where(lower_incl, aa[u][C:2 * C, 0:C], 0.0).astype(BF16) for u in units}
    akv = {u: _dot(a_kv[u], blk(vb, *u)) for u in units}
    tinv = {u: eye + a_ab[u] for u in units}
    npow = a_ab
    for _ in range(5):
        npb = {u: npow[u].astype(BF16) for u in units}
        npow = {u: _dot(npb[u], npb[u]) for u in units}
        tinv = {u: tinv[u] + _dot(tinv[u].astype(BF16), npow[u].astype(BF16)) for u in units}
    tinv_b = {u: tinv[u].astype(BF16) for u in units}

    state = [s_scr[h] for h in range(H)]
    y_rows = []
    for c in range(n_chunks):
        hs = [(c, h) for h in range(H)]
        ar_s = {u: _dot_nt(lhs[u], state[u[1]].astype(BF16)) for u in hs}
        ub = {u: _dot(tinv_b[u], (ar_s[u][0:C] + akv[u][0:C]).astype(BF16)).astype(BF16) for u in hs}
        ys = [ar_s[u][C:2 * C] + akv[u][C:2 * C] + _dot(a_rb[u], ub[u]) for u in hs]
        upd = {u: _dot_tn(jnp.concatenate([ub[u], blk(vb, *u)], axis=0),
                          jnp.concatenate([blk(bh, *u), blk(kh, *u)], axis=0)) for u in hs}
        state = [state[h] * e_tot[c * C:c * C + 1, h * N:(h + 1) * N] + upd[(c, h)] for h in range(H)]
        y_rows.append(jnp.concatenate(ys, axis=1))
    for h in range(H):
        s_scr[h] = state[h]

    y = jnp.concatenate(y_rows, axis=0)
    mu = _dot_exact_rhs(y, bones, 2) * (1.0 / N)
    yc = y - mu
    var = _dot_exact_rhs(yc * yc, bones, 2) * (1.0 / N)
    yn = yc * lax.rsqrt(var + GN_EPS) * gng_ref[...] + gnb_ref[...]
    bonus = _dot_exact_rhs(r * kp * rk_ref[...], bones, 2) * v
    y_ref[...] = ((yn + bonus) * g).astype(BF16)


def _rwkv(p_rwkv, B, T, shift_mix, decay_w0, decay_up, iclr_a0, iclr_up, gate_up, k_k, k_a, r_k, gn_g, gn_b):
    tt = RWKV_TT
    n_t = T // tt
    head = jnp.arange(RWKV_DIM) // RWKV_HEAD_DIM
    bones = (head[:, None] == head[None, :]).astype(BF16)
    vec = lambda a: a.reshape(1, -1)
    full = lambda shape: pl.BlockSpec(shape, lambda b, j: (0, 0))
    return pl.pallas_call(
        _rwkv_kernel,
        grid=(B, n_t),
        in_specs=[
            pl.BlockSpec((tt, RWKV_COLS), lambda b, j: (b * n_t + j, 0)),
            full((1, RWKV_COLS)), full((1, RWKV_DIM)), full((DECAY_RANK, RWKV_DIM)),
            full((1, RWKV_DIM)), full((ICLR_RANK, RWKV_DIM)), full((GATE_RANK, RWKV_DIM)),
            full((1, RWKV_DIM)), full((1, RWKV_DIM)), full((1, RWKV_DIM)),
            full((1, RWKV_DIM)), full((1, RWKV_DIM)), full((RWKV_DIM, RWKV_DIM)),
        ],
        out_specs=pl.BlockSpec((tt, RWKV_DIM), lambda b, j: (b * n_t + j, 0)),
        out_shape=jax.ShapeDtypeStruct((B * T, RWKV_DIM), BF16),
        scratch_shapes=[
            pltpu.VMEM((RWKV_HEADS, RWKV_HEAD_DIM, RWKV_HEAD_DIM), F32),
            pltpu.VMEM((1, RWKV_COLS), F32),
        ],
        compiler_params=_cparams(("parallel", "arbitrary")),
        name="rwkv",
    )(p_rwkv, vec(shift_mix), vec(decay_w0), decay_up.astype(BF16), vec(iclr_a0), iclr_up.astype(BF16),
      gate_up.astype(BF16), vec(k_k), vec(k_a), vec(r_k), vec(gn_g), vec(gn_b), bones)


MOBA_HP = 4


def _moba_kernel(q_ref, k_ref, v_ref, o_ref, kmean_scr, sel_scr, acc_scr):
    blk_sz, dh = MOBA_BLOCK, ATTN_HEAD_DIM
    nb = k_ref.shape[0] // blk_sz
    i = pl.program_id(2)

    @pl.when(i == 0)
    def _():
        for n in range(nb):
            kb = k_ref[n * blk_sz:(n + 1) * blk_sz, :].astype(F32)
            kmean_scr[n:n + 1, :] = jnp.sum(kb, axis=0, keepdims=True) * (1.0 / blk_sz)

    blk = lax.broadcasted_iota(I32, (nb, blk_sz), 0)
    kpos = lax.broadcasted_iota(I32, (blk_sz, blk_sz), 0)
    qpos = lax.broadcasted_iota(I32, (blk_sz, blk_sz), 1)
    own_start = pl.multiple_of(i * blk_sz, blk_sz)
    heads = [slice(hh * dh, (hh + 1) * dh) for hh in range(MOBA_HP)]
    hds = range(MOBA_HP)
    qhs = [q_ref[:, hs] for hs in heads]
    gates = [_dot_nt(kmean_scr[:, heads[hh]], qhs[hh].astype(F32), precision=HI) for hh in hds]
    qss = [qhs[hh] * (dh ** -0.5) for hh in hds]
    own = [_dot_nt(k_ref[pl.ds(own_start, blk_sz), heads[hh]], qss[hh]) for hh in hds]
    for hh in hds:
        gate = jnp.where(blk < i, gates[hh], NEG_INF)
        rank = jnp.zeros((nb, blk_sz), I32)
        for m in range(nb):
            gm = gate[m:m + 1, :]
            beats = (gm > gate) | ((gm == gate) & (m < blk))
            rank = rank + beats.astype(I32)
        sel_scr[hh] = jnp.where((rank < MOBA_TOPK) & (blk < i), 0.0, NEG_INF)
    stats, ps = [], []
    for hh in hds:
        s = jnp.where(kpos <= qpos, own[hh], NEG_INF)
        m0 = jnp.max(s, axis=0, keepdims=True)
        p = jnp.exp(s - m0)
        stats += [m0, jnp.sum(p, axis=0, keepdims=True)]
        ps.append(p.astype(BF16))
    pvs = [_dot_tn(v_ref[pl.ds(own_start, blk_sz), heads[hh]], ps[hh]) for hh in hds]
    for hh in hds:
        acc_scr[hh] = pvs[hh]

    def body(n, carry):
        start = pl.multiple_of(n * blk_sz, blk_sz)
        ss = [_dot_nt(k_ref[pl.ds(start, blk_sz), hs], qss[hh]) for hh, hs in enumerate(heads)]
        out, ps, alphas = [], [], []
        for hh in range(MOBA_HP):
            m_run, l_run = carry[2 * hh], carry[2 * hh + 1]
            s = ss[hh] + sel_scr[hh, pl.ds(n, 1), :]
            m_new = jnp.maximum(m_run, jnp.max(s, axis=0, keepdims=True))
            alpha = jnp.exp(m_run - m_new)
            p = jnp.exp(s - m_new)
            out += [m_new, alpha * l_run + jnp.sum(p, axis=0, keepdims=True)]
            ps.append(p.astype(BF16))
            alphas.append(alpha)
        pvs = [_dot_tn(v_ref[pl.ds(start, blk_sz), hs], ps[hh]) for hh, hs in enumerate(heads)]
        for hh in range(MOBA_HP):
            acc_scr[hh] = alphas[hh] * acc_scr[hh] + pvs[hh]
        return tuple(out)

    stats = lax.fori_loop(0, i, body, tuple(stats))
    outs = [(acc_scr[hh] / stats[2 * hh + 1]).T for hh in range(MOBA_HP)]
    o_ref[...] = jnp.concatenate(outs, axis=1).astype(BF16)


def _moba(q, k, v, B, T):
    blk_sz = MOBA_BLOCK
    nq = T // blk_sz
    lanes = MOBA_HP * ATTN_HEAD_DIM
    kv_spec = pl.BlockSpec((T, lanes), lambda b, hp, i: (b, hp))
    q_spec = pl.BlockSpec((blk_sz, lanes), lambda b, hp, i: (b * nq + i, hp))
    return pl.pallas_call(
        _moba_kernel,
        grid=(B, ATTN_HEADS // MOBA_HP, nq),
        in_specs=[q_spec, kv_spec, kv_spec],
        out_specs=q_spec,
        out_shape=jax.ShapeDtypeStruct((B * T, ATTN_DIM), BF16),
        scratch_shapes=[
            pltpu.VMEM((T // blk_sz, lanes), F32),
            pltpu.VMEM((MOBA_HP, T // blk_sz, blk_sz), F32),
            pltpu.VMEM((MOBA_HP, ATTN_HEAD_DIM, blk_sz), F32),
        ],
        compiler_params=_cparams(("parallel", "parallel", "arbitrary")),
        name="moba",
    )(q, k, v)


MERGE_TM = 256


def _layer_norm(h, g, b):
    mu = jnp.mean(h, axis=-1, keepdims=True)
    hc = h - mu
    var = jnp.mean(hc * hc, axis=-1, keepdims=True)
    return hc * lax.rsqrt(var + LN_EPS) * g + b


def _merge_kernel(x_ref, yr_ref, ya_ref, g_ref, wbr_ref, wba_ref, wo_ref, lng_ref, lnb_ref, rwt_ref, rb_ref,
                  x1_ref, lgt_ref):
    yr = _dot(yr_ref[...], wbr_ref[...])
    ya = _dot(ya_ref[...], wba_ref[...])
    merged = g_ref[:, 0:D_MODEL].astype(F32) * yr + g_ref[:, D_MODEL:2 * D_MODEL].astype(F32) * ya
    mix = _dot(merged.astype(BF16), wo_ref[...])
    x1 = _layer_norm(DEEPNORM_ALPHA * x_ref[...] + mix, lng_ref[...], lnb_ref[...])
    x1_ref[...] = x1
    lgt_ref[...] = _dot_nt(rwt_ref[...], x1, precision=HI) + rb_ref[...]


def _merge(x2, y_rwkv, y_attn, gates, w_br, w_ba, w_o, ln_g, ln_b, router_w, router_b):
    n_tok = x2.shape[0]
    tm = MERGE_TM
    row = lambda i: (i, 0)
    full = lambda shape: pl.BlockSpec(shape, lambda i: (0, 0))
    return pl.pallas_call(
        _merge_kernel,
        grid=(n_tok // tm,),
        in_specs=[
            pl.BlockSpec((tm, D_MODEL), row), pl.BlockSpec((tm, RWKV_DIM), row), pl.BlockSpec((tm, ATTN_DIM), row),
            pl.BlockSpec((tm, GATE_COLS), row),
            full((RWKV_DIM, D_MODEL)), full((ATTN_DIM, D_MODEL)), full((D_MODEL, D_MODEL)),
            full((1, D_MODEL)), full((1, D_MODEL)), full((N_EXPERTS, D_MODEL)), full((N_EXPERTS, 1)),
        ],
        out_specs=[pl.BlockSpec((tm, D_MODEL), row), pl.BlockSpec((N_EXPERTS, tm), lambda i: (0, i))],
        out_shape=[jax.ShapeDtypeStruct((n_tok, D_MODEL), F32), jax.ShapeDtypeStruct((N_EXPERTS, n_tok), F32)],
        compiler_params=_cparams(("parallel",)),
        name="merge",
    )(x2, y_rwkv, y_attn, gates, w_br.astype(BF16), w_ba.astype(BF16), w_o.astype(BF16),
      ln_g.reshape(1, -1), ln_b.reshape(1, -1), router_w.T, router_b.reshape(-1, 1))


MOE_TM = 512
ROW_BLOCK = 256
PIECE_ALIGN = 2 * SUBLANE
PIECE_SIZES = (512, 256, 128, 64, 32, 16)
STAGE_CHUNK = 256


def _moe_dims(n_tok):
    n_tiles = n_tok // MOE_TM
    stage_rows = -(-(MOE_TM * TOP_K + N_EXPERTS * (PIECE_ALIGN - 1)) // STAGE_CHUNK) * STAGE_CHUNK
    max_rows = n_tok * TOP_K + n_tiles * N_EXPERTS * (PIECE_ALIGN - 1) + N_EXPERTS * (ROW_BLOCK - 1)
    n_blk = -(-max_rows // ROW_BLOCK)
    map_lanes = -(-(n_blk + 1) // LANE) * LANE
    return n_tiles, stage_rows, n_blk, map_lanes


def _round_up_f32(x, m):
    return jnp.floor((x + (m - 1)) * (1.0 / m)) * m


def _route_kernel(lg_ref, wt_ref, slot_ref, off_ref, cnt_ref, map_ref, idx_scr, pos_scr, cnt_scr, off_scr):
    tm = MOE_TM
    n_tiles = cnt_scr.shape[1]
    phase = pl.program_id(0)
    i = pl.program_id(1)
    tok0 = pl.multiple_of(i * tm, tm)
    eio = lax.broadcasted_iota(I32, (N_EXPERTS, tm), 0)
    tile_lane = lax.broadcasted_iota(I32, (N_EXPERTS, n_tiles), 1)
    e_from = lax.broadcasted_iota(I32, (N_EXPERTS, N_EXPERTS), 1)
    e_to = lax.broadcasted_iota(I32, (N_EXPERTS, N_EXPERTS), 0)
    earlier_e = jnp.where(e_from < e_to, 1.0, 0.0)

    @pl.when((phase == 0) & (i == 0))
    def _():
        cnt_scr[...] = jnp.zeros_like(cnt_scr)

    @pl.when(phase == 0)
    def _():
        work = lg_ref[...]
        vals, hots = [], []
        for s in range(TOP_K):
            m = jnp.max(work, axis=0, keepdims=True)
            ix = jnp.min(jnp.where(work == m, eio, N_EXPERTS), axis=0, keepdims=True)
            hot = eio == ix
            idx_scr[s:s + 1, pl.ds(tok0, tm)] = ix
            vals.append(m)
            hots.append(hot)
            work = jnp.where(hot, -jnp.inf, work)
        es = [jnp.exp(v - vals[0]) for v in vals]
        denom = es[0] + es[1] + es[2] + es[3]
        for s in range(TOP_K):
            wt_ref[s:s + 1, :] = es[s] / denom
        multi_f = jnp.where(hots[0] | hots[1] | hots[2] | hots[3], 1.0, 0.0)
        t_from = lax.broadcasted_iota(I32, (tm, tm), 0)
        t_to = lax.broadcasted_iota(I32, (tm, tm), 1)
        before = jnp.where(t_from < t_to, 1.0, 0.0).astype(BF16)
        count = _dot(multi_f.astype(BF16), before)
        for s in range(TOP_K):
            pos_scr[s:s + 1, pl.ds(tok0, tm)] = jnp.sum(
                jnp.where(hots[s], count, 0.0), axis=0, keepdims=True).astype(I32)
        cnt8 = _round_up_f32(jnp.sum(multi_f, axis=1, keepdims=True), PIECE_ALIGN)
        cnt_scr[...] = cnt_scr[...] + jnp.where(tile_lane == i, cnt8, 0.0)

    @pl.when((phase == 1) & (i == 0))
    def _():
        cnt8 = cnt_scr[...]
        tot = _round_up_f32(jnp.sum(cnt8, axis=1, keepdims=True), ROW_BLOCK)
        p_start = _dot_hi(earlier_e, jnp.broadcast_to(tot, (N_EXPERTS, n_tiles)))
        i_from = lax.broadcasted_iota(I32, (n_tiles, n_tiles), 0)
        i_to = lax.broadcasted_iota(I32, (n_tiles, n_tiles), 1)
        earlier_tiles = _dot_hi(cnt8, jnp.where(i_from < i_to, 1.0, 0.0))
        off_scr[...] = p_start + earlier_tiles
        off_ref[...] = off_scr[...].astype(I32)
        cnt_ref[...] = cnt8.astype(I32)
        p_end = p_start[:, 0:1] + tot
        lanes = map_ref.shape[1]
        blk_start = lax.broadcasted_iota(I32, (N_EXPERTS, lanes), 1).astype(F32) * ROW_BLOCK
        blk_e = jnp.sum(jnp.where(blk_start >= p_end, 1.0, 0.0), axis=0, keepdims=True)
        blk_e = jnp.minimum(blk_e, N_EXPERTS - 1.0)
        n_used = jnp.max(p_end, axis=0, keepdims=True) * (1.0 / ROW_BLOCK)
        last = lax.broadcasted_iota(I32, (1, lanes), 1) == lanes - 1
        map_ref[...] = jnp.where(last, n_used, blk_e).astype(I32)

    @pl.when(phase == 1)
    def _():
        cnt_col = jnp.sum(jnp.where(tile_lane == i, cnt_scr[...], 0.0), axis=1, keepdims=True)
        local_off = _dot_hi(earlier_e, jnp.broadcast_to(cnt_col, (N_EXPERTS, tm)))
        for s in range(TOP_K):
            hot = eio == idx_scr[s:s + 1, pl.ds(tok0, tm)]
            base = jnp.sum(jnp.where(hot, local_off, 0.0), axis=0, keepdims=True).astype(I32)
            slot_ref[s:s + 1, :] = base + pos_scr[s:s + 1, pl.ds(tok0, tm)]


def _route(logits_t):
    n_tok = logits_t.shape[1]
    n_tiles, _, _, map_lanes = _moe_dims(n_tok)
    tm = MOE_TM
    last = n_tiles - 1
    const = lambda p, i: (0, 0)
    return pl.pallas_call(
        _route_kernel,
        grid=(2, n_tiles),
        in_specs=[pl.BlockSpec((N_EXPERTS, tm), lambda p, i: (0, i * (1 - p) + last * p))],
        out_specs=[
            pl.BlockSpec((TOP_K, tm), lambda p, i: (0, i * (1 - p) + last * p)),
            pl.BlockSpec((TOP_K, tm), lambda p, i: (0, i * p)),
            pl.BlockSpec((N_EXPERTS, n_tiles), const),
            pl.BlockSpec((N_EXPERTS, n_tiles), const),
            pl.BlockSpec((1, map_lanes), const),
        ],
        out_shape=[
            jax.ShapeDtypeStruct((TOP_K, n_tok), F32),
            jax.ShapeDtypeStruct((TOP_K, n_tok), I32),
            jax.ShapeDtypeStruct((N_EXPERTS, n_tiles), I32),
            jax.ShapeDtypeStruct((N_EXPERTS, n_tiles), I32),
            jax.ShapeDtypeStruct((1, map_lanes), I32),
        ],
        scratch_shapes=[
            pltpu.VMEM((TOP_K, n_tok), I32),
            pltpu.VMEM((TOP_K, n_tok), I32),
            pltpu.VMEM((N_EXPERTS, n_tiles), F32),
            pltpu.VMEM((N_EXPERTS, n_tiles), F32),
        ],
        compiler_params=_cparams(("arbitrary", "arbitrary")),
        name="route",
    )(logits_t)


def _piece_copies(off_ref, cnt_ref, make_copy, action):
    tile = pl.program_id(0)

    def per_expert(e, stage_row):
        c8 = cnt_ref[tile * N_EXPERTS + e]
        hbm_row = off_ref[tile * N_EXPERTS + e]
        done = 0
        for size in PIECE_SIZES:
            bit = c8 & size

            @pl.when(bit != 0)
            def _(done=done, size=size):
                action(make_copy(pl.multiple_of(stage_row + done, PIECE_ALIGN),
                                 pl.multiple_of(hbm_row + done, PIECE_ALIGN), size))

            done = done + bit
        return stage_row + c8

    lax.fori_loop(0, N_EXPERTS, per_expert, 0)


def _slot_rows(slot_ref):
    return [slot_ref[s:s + 1, :] for s in range(TOP_K)]


def _dispatch_kernel(off_ref, cnt_ref, slot_ref, x1_ref, xs_in_hbm, xs_hbm, stage, sem):
    del xs_in_hbm
    xb = x1_ref[...].astype(BF16)
    slots = _slot_rows(slot_ref)
    for r0 in range(0, stage.shape[0], STAGE_CHUNK):
        rio = lax.broadcasted_iota(I32, (STAGE_CHUNK, MOE_TM), 0) + r0
        pick = jnp.zeros((STAGE_CHUNK, MOE_TM), F32)
        for s in range(TOP_K):
            pick = jnp.where(rio == slots[s], 1.0, pick)
        stage[r0:r0 + STAGE_CHUNK, :] = _dot(pick.astype(BF16), xb).astype(BF16)

    def make_copy(stage_row, hbm_row, size):
        return pltpu.make_async_copy(stage.at[pl.ds(stage_row, size)], xs_hbm.at[pl.ds(hbm_row, size)], sem)

    _piece_copies(off_ref, cnt_ref, make_copy, lambda cp: cp.start())
    _piece_copies(off_ref, cnt_ref, make_copy, lambda cp: cp.wait())


def _dispatch(off_flat, cnt_flat, slot, x1):
    n_tok = x1.shape[0]
    n_tiles, stage_rows, n_blk, _ = _moe_dims(n_tok)
    n_rows = n_blk * ROW_BLOCK
    any_spec = pl.BlockSpec(memory_space=pl.ANY)
    return pl.pallas_call(
        _dispatch_kernel,
        grid_spec=pltpu.PrefetchScalarGridSpec(
            num_scalar_prefetch=2,
            grid=(n_tiles,),
            in_specs=[pl.BlockSpec((TOP_K, MOE_TM), lambda i, off, cnt: (0, i)),
                      pl.BlockSpec((MOE_TM, D_MODEL), lambda i, off, cnt: (i, 0)),
                      any_spec],
            out_specs=any_spec,
            scratch_shapes=[pltpu.VMEM((stage_rows, D_MODEL), BF16), pltpu.SemaphoreType.DMA(())],
        ),
        out_shape=jax.ShapeDtypeStruct((n_rows, D_MODEL), BF16),
        input_output_aliases={4: 0},
        compiler_params=_cparams(("arbitrary",)),
        name="dispatch",
    )(off_flat, cnt_flat, slot, x1, jnp.zeros((n_rows, D_MODEL), BF16))


def _combine_kernel(off_ref, cnt_ref, slot_ref, wt_ref, x1_ref, lng_ref, lnb_ref, ys_hbm, out_ref, stage, sem):
    @pl.when(pl.program_id(0) == 0)
    def _():
        stage[...] = jnp.zeros_like(stage)

    def make_copy(stage_row, hbm_row, size):
        return pltpu.make_async_copy(ys_hbm.at[pl.ds(hbm_row, size)], stage.at[pl.ds(stage_row, size)], sem)

    _piece_copies(off_ref, cnt_ref, make_copy, lambda cp: cp.start())
    _piece_copies(off_ref, cnt_ref, make_copy, lambda cp: cp.wait())

    slots = _slot_rows(slot_ref)
    ffn = jnp.zeros((MOE_TM, D_MODEL), F32)
    for r0 in range(0, stage.shape[0], STAGE_CHUNK):
        rio = lax.broadcasted_iota(I32, (STAGE_CHUNK, MOE_TM), 0) + r0
        w = jnp.zeros((STAGE_CHUNK, MOE_TM), F32)
        for s in range(TOP_K):
            w = jnp.where(rio == slots[s], wt_ref[s:s + 1, :], w)
        w_a, w_b = _bf16_pieces(w, 2)
        y = stage[r0:r0 + STAGE_CHUNK, :]
        ffn = ffn + _dot_tn(w_a, y) + _dot_tn(w_b, y)
    out_ref[...] = _layer_norm(DEEPNORM_ALPHA * x1_ref[...] + ffn, lng_ref[...], lnb_ref[...])


def _combine(off_flat, cnt_flat, slot, wts, x1, ln_g, ln_b, ys):
    n_tok = x1.shape[0]
    n_tiles, stage_rows, _, _ = _moe_dims(n_tok)
    tok = lambda i, off, cnt: (0, i)
    row = lambda i, off, cnt: (i, 0)
    full = lambda shape: pl.BlockSpec(shape, lambda i, off, cnt: (0, 0))
    return pl.pallas_call(
        _combine_kernel,
        grid_spec=pltpu.PrefetchScalarGridSpec(
            num_scalar_prefetch=2,
            grid=(n_tiles,),
            in_specs=[pl.BlockSpec((TOP_K, MOE_TM), tok), pl.BlockSpec((TOP_K, MOE_TM), tok),
                      pl.BlockSpec((MOE_TM, D_MODEL), row), full((1, D_MODEL)), full((1, D_MODEL)),
                      pl.BlockSpec(memory_space=pl.ANY)],
            out_specs=pl.BlockSpec((MOE_TM, D_MODEL), row),
            scratch_shapes=[pltpu.VMEM((stage_rows, D_MODEL), BF16), pltpu.SemaphoreType.DMA(())],
        ),
        out_shape=jax.ShapeDtypeStruct((n_tok, D_MODEL), F32),
        compiler_params=_cparams(("arbitrary",)),
        name="combine",
    )(off_flat, cnt_flat, slot, wts, x1, ln_g.reshape(1, -1), ln_b.reshape(1, -1), ys)


CAST_ROWS = 128


def _expert_kernel(blk_e_ref, n_used_ref, xs_ref, win_ref, bin_ref, wout_ref, bout_ref, ys_ref, win_bf, wout_bf):
    rb = pl.program_id(0)
    new_expert = (rb == 0) | (blk_e_ref[rb] != blk_e_ref[jnp.maximum(rb - 1, 0)])

    @pl.when((rb < n_used_ref[0]) & new_expert)
    def _():
        for r in range(0, D_MODEL, CAST_ROWS):
            win_bf[r:r + CAST_ROWS, :] = win_ref[0, r:r + CAST_ROWS, :].astype(BF16)
        for r in range(0, D_EXPERT, CAST_ROWS):
            wout_bf[r:r + CAST_ROWS, :] = wout_ref[0, r:r + CAST_ROWS, :].astype(BF16)

    @pl.when(rb < n_used_ref[0])
    def _():
        h = _dot(xs_ref[...], win_bf[...]) + bin_ref[0]
        gate_h = jnp.minimum(h[:, 0:D_EXPERT], SWIGLU_LIMIT)
        lin_h = jnp.clip(h[:, D_EXPERT:2 * D_EXPERT], -SWIGLU_LIMIT, SWIGLU_LIMIT)
        act = gate_h * jax.nn.sigmoid(SWIGLU_ALPHA * gate_h) * (lin_h + 1.0)
        ys_ref[...] = (_dot(act.astype(BF16), wout_bf[...]) + bout_ref[0]).astype(BF16)

    @pl.when(rb >= n_used_ref[0])
    def _():
        ys_ref[...] = jnp.zeros_like(ys_ref)


def _experts(blk_e, n_used, xs, w_in, b_in, w_out, b_out):
    n_rows = xs.shape[0]
    n_blk = n_rows // ROW_BLOCK
    used = lambda rb, n_used: jnp.maximum(jnp.minimum(rb, n_used[0] - 1), 0)
    rows = lambda rb, blk_e, n_used: (used(rb, n_used), 0)
    per_e = lambda rb, blk_e, n_used: (blk_e[used(rb, n_used)], 0, 0)
    return pl.pallas_call(
        _expert_kernel,
        grid_spec=pltpu.PrefetchScalarGridSpec(
            num_scalar_prefetch=2,
            grid=(n_blk,),
            in_specs=[
                pl.BlockSpec((ROW_BLOCK, D_MODEL), rows),
                pl.BlockSpec((1, D_MODEL, 2 * D_EXPERT), per_e),
                pl.BlockSpec((1, 1, 2 * D_EXPERT), per_e),
                pl.BlockSpec((1, D_EXPERT, D_MODEL), per_e),
                pl.BlockSpec((1, 1, D_MODEL), per_e),
            ],
            out_specs=pl.BlockSpec((ROW_BLOCK, D_MODEL), lambda rb, blk_e, n_used: (rb, 0)),
            scratch_shapes=[pltpu.VMEM((D_MODEL, 2 * D_EXPERT), BF16), pltpu.VMEM((D_EXPERT, D_MODEL), BF16)],
        ),
        out_shape=jax.ShapeDtypeStruct((n_rows, D_MODEL), BF16),
        compiler_params=_cparams(("arbitrary",)),
        name="experts",
    )(blk_e, n_used, xs, w_in, b_in.reshape(N_EXPERTS, 1, -1), w_out, b_out.reshape(N_EXPERTS, 1, -1))


def _moe(x1, logits_t, expert_w_in, expert_b_in, expert_w_out, expert_b_out, ln_g, ln_b):
    n_tok = x1.shape[0]
    _, _, n_blk, map_lanes = _moe_dims(n_tok)
    wts, slot, off, cnt, blk_map = _route(logits_t)
    off_flat = off.T.reshape(-1)
    cnt_flat = cnt.T.reshape(-1)
    blk_e = blk_map[0, 0:n_blk]
    n_used = blk_map[0, map_lanes - 1:map_lanes]
    xs = _dispatch(off_flat, cnt_flat, slot, x1)
    ys = _experts(blk_e, n_used, xs, expert_w_in, expert_b_in, expert_w_out, expert_b_out)
    return _combine(off_flat, cnt_flat, slot, wts, x1, ln_g, ln_b, ys)


def kernel(x, ln1_g, ln1_b, ln2_g, ln2_b, w_in, shift_mix, decay_w0, decay_up, iclr_a0, iclr_up, gate_up, k_k, k_a, r_k, gn_g, gn_b, w_branch_rwkv, w_branch_attn, w_out, router_w, router_b, expert_w_in, expert_b_in, expert_w_out, expert_b_out):
    B, T, D = x.shape
    x2 = x.reshape(B * T, D)
    prw, q, k, v, gates = _proj(x2, w_in[0].astype(BF16), T)
    y_rwkv = _rwkv(prw, B, T, shift_mix[0], decay_w0[0], decay_up[0], iclr_a0[0], iclr_up[0], gate_up[0],
                   k_k[0], k_a[0], r_k[0], gn_g[0], gn_b[0])
    y_attn = _moba(q, k, v, B, T)
    x1, logits_t = _merge(x2, y_rwkv, y_attn, gates, w_branch_rwkv[0], w_branch_attn[0], w_out[0],
                          ln1_g[0], ln1_b[0], router_w[0], router_b[0])
    out = _moe(x1, logits_t, expert_w_in[0], expert_b_in[0], expert_w_out[0], expert_b_out[0], ln2_g[0], ln2_b[0])
    return out.reshape(B, T, D)
```

```python
import math

import jax
import jax.numpy as jnp
from jax import lax
from jax.experimental import pallas as pl
from jax.experimental.pallas import tpu as pltpu

F32 = jnp.float32
BF16 = jnp.bfloat16
I32 = jnp.int32
HI = lax.Precision.HIGHEST

D_MODEL = 1024
DEPTH = 1
RWKV_HEAD_DIM = 64
RWKV_DIM = 512
RWKV_HEADS = 8
DECAY_RANK = 64
ICLR_RANK = 64
GATE_RANK = 128
GN_EPS = 64e-5
ATTN_HEAD_DIM = 64
ATTN_DIM = 512
ATTN_HEADS = 8
MOBA_BLOCK = 256
MOBA_TOPK = 3
ROPE_THETA = 500000.0
ROPE_DIM = 16
NEG_INF = -1e30
N_EXPERTS = 32
TOP_K = 4
D_EXPERT = 1024
SWIGLU_LIMIT = 7.0
SWIGLU_ALPHA = 1.702
DEEPNORM_ALPHA = (2.0 * DEPTH) ** 0.25
LN_EPS = 1e-5
RWKV_COLS = 3 * RWKV_DIM + DECAY_RANK + ICLR_RANK + GATE_RANK
ATTN_COLS = 3 * ATTN_DIM
GATE_COLS = 2 * D_MODEL
IN_COLS = RWKV_COLS + ATTN_COLS + GATE_COLS

LANE = 128
SUBLANE = 8
VMEM_LIMIT_BYTES = 56 * 1024 * 1024

CHUNK = 64


def _cparams(sem):
    return pltpu.CompilerParams(dimension_semantics=sem, vmem_limit_bytes=VMEM_LIMIT_BYTES)


def _dot(a, b):
    return jnp.dot(a, b, preferred_element_type=F32)


def _dot_hi(a, b):
    return jnp.dot(a, b, preferred_element_type=F32, precision=HI)


def _dot_nt(a, b, precision=None):
    return lax.dot_general(a, b, (((1,), (1,)), ((), ())), preferred_element_type=F32, precision=precision)


def _dot_tn(a, b, precision=None):
    return lax.dot_general(a, b, (((0,), (0,)), ((), ())), preferred_element_type=F32, precision=precision)


def _bf16_pieces(x, n):
    pieces = []
    for _ in range(n):
        p = x.astype(BF16)
        pieces.append(p)
        x = x - p.astype(F32)
    return pieces


def _dot_exact_rhs(x, b_bf16, n):
    out = None
    for p in _bf16_pieces(x, n):
        d = _dot(p, b_bf16)
        out = d if out is None else out + d
    return out


PROJ_TM = 256


def _proj_kernel(x_ref, w_ref, cos_ref, sa_ref, sb_ref, prw_ref, q_ref, k_ref, v_ref, g_ref):
    xb = x_ref[...].astype(BF16)
    prw_ref[...] = _dot(xb, w_ref[:, 0:RWKV_COLS])
    c0 = RWKV_COLS
    cos = cos_ref[...]
    sa = sa_ref[...]
    sb = sb_ref[...]

    def rope(t):
        return t * cos + pltpu.roll(t, ATTN_DIM - ROPE_DIM // 2, 1) * sa + pltpu.roll(t, ROPE_DIM // 2, 1) * sb

    q_ref[...] = rope(_dot(xb, w_ref[:, c0:c0 + ATTN_DIM])).astype(BF16)
    k_ref[...] = rope(_dot(xb, w_ref[:, c0 + ATTN_DIM:c0 + 2 * ATTN_DIM])).astype(BF16)
    v_ref[...] = _dot(xb, w_ref[:, c0 + 2 * ATTN_DIM:c0 + 3 * ATTN_DIM]).astype(BF16)
    c1 = RWKV_COLS + ATTN_COLS
    g_ref[...] = jax.nn.sigmoid(_dot(xb, w_ref[:, c1:c1 + GATE_COLS])).astype(BF16)


def _rope_tables(T):
    half = ROPE_DIM // 2
    inv_freq = jnp.power(ROPE_THETA, -jnp.arange(0, ROPE_DIM, 2, dtype=F32) / ROPE_DIM)
    ang = jnp.arange(T).astype(F32)[:, None] * inv_freq[None, :]
    cos, sin = jnp.cos(ang), jnp.sin(ang)
    pad = jnp.zeros((T, ATTN_HEAD_DIM - ROPE_DIM), F32)
    cos_h = jnp.concatenate([cos, cos, pad + 1.0], axis=1)
    sa_h = jnp.concatenate([-sin, jnp.zeros((T, half), F32), pad], axis=1)
    sb_h = jnp.concatenate([jnp.zeros((T, half), F32), sin, pad], axis=1)
    tile = lambda t: jnp.tile(t, (1, ATTN_HEADS))
    return tile(cos_h), tile(sa_h), tile(sb_h)


def _proj(x2, w_in_bf, T):
    n_tok = x2.shape[0]
    tm = PROJ_TM
    t_tiles = T // tm
    cos, sa, sb = _rope_tables(T)
    row = lambda i: (i, 0)
    tab = lambda i: (i % t_tiles, 0)
    return pl.pallas_call(
        _proj_kernel,
        grid=(n_tok // tm,),
        in_specs=[
            pl.BlockSpec((tm, D_MODEL), row),
            pl.BlockSpec((D_MODEL, IN_COLS), lambda i: (0, 0), pipeline_mode=pl.Buffered(1)),
            pl.BlockSpec((tm, ATTN_DIM), tab),
            pl.BlockSpec((tm, ATTN_DIM), tab),
            pl.BlockSpec((tm, ATTN_DIM), tab),
        ],
        out_specs=[
            pl.BlockSpec((tm, RWKV_COLS), row),
            pl.BlockSpec((tm, ATTN_DIM), row),
            pl.BlockSpec((tm, ATTN_DIM), row),
            pl.BlockSpec((tm, ATTN_DIM), row),
            pl.BlockSpec((tm, GATE_COLS), row),
        ],
        out_shape=[
            jax.ShapeDtypeStruct((n_tok, RWKV_COLS), F32),
            jax.ShapeDtypeStruct((n_tok, ATTN_DIM), BF16),
            jax.ShapeDtypeStruct((n_tok, ATTN_DIM), BF16),
            jax.ShapeDtypeStruct((n_tok, ATTN_DIM), BF16),
            jax.ShapeDtypeStruct((n_tok, GATE_COLS), BF16),
        ],
        compiler_params=_cparams(("parallel",)),
        name="proj",
    )(x2, w_in_bf, cos, sa, sb)


RWKV_TT = 256


def _rwkv_kernel(p_ref, mix_ref, w0_ref, dup_ref, a0_ref, iup_ref, gup_ref, kk_ref, ka_ref, rk_ref,
                 gng_ref, gnb_ref, ones_ref, y_ref, s_scr, prev_scr):
    H, N, C = RWKV_HEADS, RWKV_HEAD_DIM, CHUNK

    @pl.when(pl.program_id(1) == 0)
    def _():
        s_scr[...] = jnp.zeros_like(s_scr)
        prev_scr[...] = jnp.zeros_like(prev_scr)

    TT = RWKV_TT
    n_chunks = TT // C
    bones = ones_ref[...]
    row = lax.broadcasted_iota(I32, (C, C), 0)
    col = lax.broadcasted_iota(I32, (C, C), 1)
    lower_incl = col <= row
    lower_strict = col < row
    eye = jnp.where(col == row, 1.0, 0.0)
    t_row = lax.broadcasted_iota(I32, (TT, TT), 0)
    t_col = lax.broadcasted_iota(I32, (TT, TT), 1)
    same_chunk = (t_row // C) == (t_col // C)
    chunk_ltri = jnp.where(same_chunk & (t_col <= t_row), 1.0, 0.0).astype(BF16)
    chunk_ones = jnp.where(same_chunk, 1.0, 0.0).astype(BF16)

    p = p_ref[...]
    first_row = lax.broadcasted_iota(I32, (TT, RWKV_COLS), 0) == 0
    prev = jnp.where(first_row, prev_scr[...], pltpu.roll(p, 1, 0))
    prev_scr[...] = p[TT - 1:TT, :]
    ps = p + (prev - p) * mix_ref[...]
    r = ps[:, 0:RWKV_DIM]
    k = ps[:, RWKV_DIM:2 * RWKV_DIM]
    v = ps[:, 2 * RWKV_DIM:3 * RWKV_DIM]
    o = 3 * RWKV_DIM
    xw = ps[:, o:o + DECAY_RANK]
    xa = ps[:, o + DECAY_RANK:o + DECAY_RANK + ICLR_RANK]
    xg = ps[:, o + DECAY_RANK + ICLR_RANK:RWKV_COLS]
    w_raw = w0_ref[...] + _dot(jnp.tanh(xw).astype(BF16), dup_ref[...])
    logw = -math.exp(-0.5) * jax.nn.sigmoid(w_raw)
    a = jax.nn.sigmoid(a0_ref[...] + _dot(xa.astype(BF16), iup_ref[...]))
    g = _dot(jax.nn.sigmoid(xg).astype(BF16), gup_ref[...])
    kk0 = k * kk_ref[...]
    kk = kk0 / jnp.maximum(jnp.sqrt(_dot_exact_rhs(kk0 * kk0, bones, 2)), 1e-12)
    kp = k * (1.0 + (a - 1.0) * ka_ref[...])
    kka = kk * a
    logw_pieces = _bf16_pieces(logw, 3)
    cum = sum(_dot(chunk_ltri, piece) for piece in logw_pieces)
    tot = sum(_dot(chunk_ones, piece) for piece in logw_pieces)
    e_neg = jnp.exp(-cum)
    at = (-kk * jnp.exp(cum - logw)).astype(BF16)
    rt = (r * jnp.exp(cum)).astype(BF16)
    bt = (kka * e_neg).astype(BF16)
    kt = (kp * e_neg).astype(BF16)
    e_end = jnp.exp(tot - cum)
    bh = (kka * e_end).astype(BF16)
    kh = (kp * e_end).astype(BF16)
    e_tot = jnp.exp(tot)
    vb = v.astype(BF16)

    units = [(c, h) for c in range(n_chunks) for h in range(H)]
    blk = lambda t, c, h: t[c * C:(c + 1) * C, h * N:(h + 1) * N]
    lhs = {u: jnp.concatenate([blk(at, *u), blk(rt, *u)], axis=0) for u in units}
    rhs = {u: jnp.concatenate([blk(bt, *u), blk(kt, *u)], axis=0) for u in units}
    aa = {u: _dot_nt(lhs[u], rhs[u]) for u in units}
    a_ab = {u: jnp.where(lower_strict, aa[u][0:C, 0:C], 0.0) for u in units}
    a_kv = {u: jnp.concatenate([jnp.where(lower_strict, aa[u][0:C, C:2 * C], 0.0),
                                jnp.where(lower_incl, aa[u][C:2 * C, C:2 * C], 0.0)], axis=0).astype(BF16)
            for u in units}
    a_rb = {u: jnp.where(lower_incl, aa[u][C:2 * C, 0:C], 0.0).astype(BF16) for u in units}
    akv = {u: _dot(a_kv[u], blk(vb, *u)) for u in units}
    tinv = {u: eye + a_ab[u] for u in units}
    npow = a_ab
    for _ in range(5):
        npb = {u: npow[u].astype(BF16) for u in units}
        npow = {u: _dot(npb[u], npb[u]) for u in units}
        tinv = {u: tinv[u] + _dot(tinv[u].astype(BF16), npow[u].astype(BF16)) for u in units}
    tinv_b = {u: tinv[u].astype(BF16) for u in units}

    state = [s_scr[h] for h in range(H)]
    y_rows = []
    for c in range(n_chunks):
        hs = [(c, h) for h in range(H)]
        ar_s = {u: _dot_nt(lhs[u], state[u[1]].astype(BF16)) for u in hs}
        ub = {u: _dot(tinv_b[u], (ar_s[u][0:C] + akv[u][0:C]).astype(BF16)).astype(BF16) for u in hs}
        ys = [ar_s[u][C:2 * C] + akv[u][C:2 * C] + _dot(a_rb[u], ub[u]) for u in hs]
        upd = {u: _dot_tn(jnp.concatenate([ub[u], blk(vb, *u)], axis=0),
                          jnp.concatenate([blk(bh, *u), blk(kh, *u)], axis=0)) for u in hs}
        state = [state[h] * e_tot[c * C:c * C + 1, h * N:(h + 1) * N] + upd[(c, h)] for h in range(H)]
        y_rows.append(jnp.concatenate(ys, axis=1))
    for h in range(H):
        s_scr[h] = state[h]

    y = jnp.concatenate(y_rows, axis=0)
    mu = _dot_exact_rhs(y, bones, 2) * (1.0 / N)
    yc = y - mu
    var = _dot_exact_rhs(yc * yc, bones, 2) * (1.0 / N)
    yn = yc * lax.rsqrt(var + GN_EPS) * gng_ref[...] + gnb_ref[...]
    bonus = _dot_exact_rhs(r * kp * rk_ref[...], bones, 2) * v
    y_ref[...] = ((yn + bonus) * g).astype(BF16)


def _rwkv(p_rwkv, B, T, shift_mix, decay_w0, decay_up, iclr_a0, iclr_up, gate_up, k_k, k_a, r_k, gn_g, gn_b):
    tt = RWKV_TT
    n_t = T // tt
    head = jnp.arange(RWKV_DIM) // RWKV_HEAD_DIM
    bones = (head[:, None] == head[None, :]).astype(BF16)
    vec = lambda a: a.reshape(1, -1)
    full = lambda shape: pl.BlockSpec(shape, lambda b, j: (0, 0))
    return pl.pallas_call(
        _rwkv_kernel,
        grid=(B, n_t),
        in_specs=[
            pl.BlockSpec((tt, RWKV_COLS), lambda b, j: (b * n_t + j, 0)),
            full((1, RWKV_COLS)), full((1, RWKV_DIM)), full((DECAY_RANK, RWKV_DIM)),
            full((1, RWKV_DIM)), full((ICLR_RANK, RWKV_DIM)), full((GATE_RANK, RWKV_DIM)),
            full((1, RWKV_DIM)), full((1, RWKV_DIM)), full((1, RWKV_DIM)),
            full((1, RWKV_DIM)), full((1, RWKV_DIM)), full((RWKV_DIM, RWKV_DIM)),
        ],
        out_specs=pl.BlockSpec((tt, RWKV_DIM), lambda b, j: (b * n_t + j, 0)),
        out_shape=jax.ShapeDtypeStruct((B * T, RWKV_DIM), BF16),
        scratch_shapes=[
            pltpu.VMEM((RWKV_HEADS, RWKV_HEAD_DIM, RWKV_HEAD_DIM), F32),
            pltpu.VMEM((1, RWKV_COLS), F32),
        ],
        compiler_params=_cparams(("parallel", "arbitrary")),
        name="rwkv",
    )(p_rwkv, vec(shift_mix), vec(decay_w0), decay_up.astype(BF16), vec(iclr_a0), iclr_up.astype(BF16),
      gate_up.astype(BF16), vec(k_k), vec(k_a), vec(r_k), vec(gn_g), vec(gn_b), bones)


MOBA_HP = 4


def _moba_kernel(q_ref, k_ref, v_ref, o_ref, kmean_scr, sel_scr, acc_scr, score_scr):
    blk_sz, dh = MOBA_BLOCK, ATTN_HEAD_DIM
    nb = k_ref.shape[0] // blk_sz
    i = pl.program_id(2)

    @pl.when(i == 0)
    def _():
        for n in range(nb):
            kb = k_ref[n * blk_sz:(n + 1) * blk_sz, :].astype(F32)
            kmean_scr[n:n + 1, :] = jnp.sum(kb, axis=0, keepdims=True) * (1.0 / blk_sz)

    blk = lax.broadcasted_iota(I32, (nb, blk_sz), 0)
    kpos = lax.broadcasted_iota(I32, (blk_sz, blk_sz), 0)
    qpos = lax.broadcasted_iota(I32, (blk_sz, blk_sz), 1)
    own_start = pl.multiple_of(i * blk_sz, blk_sz)
    heads = [slice(hh * dh, (hh + 1) * dh) for hh in range(MOBA_HP)]
    hds = range(MOBA_HP)
    qhs = [q_ref[:, hs] for hs in heads]
    kmean_pieces = _bf16_pieces(kmean_scr[...], 3)
    gates = [sum(_dot_nt(piece[:, heads[hh]], qhs[hh]) for piece in kmean_pieces) for hh in hds]
    qss = [qhs[hh] * (dh ** -0.5) for hh in hds]
    own = [_dot_nt(k_ref[pl.ds(own_start, blk_sz), heads[hh]], qss[hh]) for hh in hds]
    for hh in hds:
        gate = jnp.where(blk < i, gates[hh], NEG_INF)
        rank = jnp.zeros((nb, blk_sz), I32)
        for m in range(nb):
            gm = gate[m:m + 1, :]
            beats = (gm > gate) | ((gm == gate) & (m < blk))
            rank = rank + beats.astype(I32)
        sel_scr[hh] = ((rank < MOBA_TOPK) & (blk < i)).astype(F32)
    stats, ps = [], []
    for hh in hds:
        s = jnp.where(kpos <= qpos, own[hh], NEG_INF)
        m0 = jnp.max(s, axis=0, keepdims=True)
        p = jnp.exp(s - m0)
        stats += [m0, jnp.sum(p, axis=0, keepdims=True)]
        ps.append(p.astype(BF16))
    pvs = [_dot_tn(v_ref[pl.ds(own_start, blk_sz), heads[hh]], ps[hh]) for hh in hds]
    for hh in hds:
        acc_scr[hh] = pvs[hh]

    def scores(n):
        start = pl.multiple_of(n * blk_sz, blk_sz)
        return [_dot_nt(k_ref[pl.ds(start, blk_sz), heads[hh]], qss[hh]) for hh in hds]

    first = scores(0)
    for hh in hds:
        score_scr[hh] = first[hh]

    def body(n, carry):
        ahead = scores(jnp.minimum(n + 1, i - 1))
        start = pl.multiple_of(n * blk_sz, blk_sz)
        out, ps, alphas = [], [], []
        for hh in hds:
            m_run, l_run = carry[2 * hh], carry[2 * hh + 1]
            s = jnp.where(sel_scr[hh, pl.ds(n, 1), :] > 0.0, score_scr[hh], NEG_INF)
            m_new = jnp.maximum(m_run, jnp.max(s, axis=0, keepdims=True))
            alpha = jnp.exp(m_run - m_new)
            p = jnp.exp(s - m_new)
            out += [m_new, alpha * l_run + jnp.sum(p, axis=0, keepdims=True)]
            ps.append(p.astype(BF16))
            alphas.append(alpha)
        pvs = [_dot_tn(v_ref[pl.ds(start, blk_sz), heads[hh]], ps[hh]) for hh in hds]
        for hh in hds:
            acc_scr[hh] = alphas[hh] * acc_scr[hh] + pvs[hh]
            score_scr[hh] = ahead[hh]
        return tuple(out)

    stats = lax.fori_loop(0, i, body, tuple(stats))
    outs = [(acc_scr[hh] / stats[2 * hh + 1]).T for hh in range(MOBA_HP)]
    o_ref[...] = jnp.concatenate(outs, axis=1).astype(BF16)


def _moba(q, k, v, B, T):
    blk_sz = MOBA_BLOCK
    nq = T // blk_sz
    lanes = MOBA_HP * ATTN_HEAD_DIM
    kv_spec = pl.BlockSpec((T, lanes), lambda b, hp, i: (b, hp))
    q_spec = pl.BlockSpec((blk_sz, lanes), lambda b, hp, i: (b * nq + i, hp))
    return pl.pallas_call(
        _moba_kernel,
        grid=(B, ATTN_HEADS // MOBA_HP, nq),
        in_specs=[q_spec, kv_spec, kv_spec],
        out_specs=q_spec,
        out_shape=jax.ShapeDtypeStruct((B * T, ATTN_DIM), BF16),
        scratch_shapes=[
            pltpu.VMEM((T // blk_sz, lanes), F32),
            pltpu.VMEM((MOBA_HP, T // blk_sz, blk_sz), F32),
            pltpu.VMEM((MOBA_HP, ATTN_HEAD_DIM, blk_sz), F32),
            pltpu.VMEM((MOBA_HP, blk_sz, blk_sz), F32),
        ],
        compiler_params=_cparams(("parallel", "parallel", "arbitrary")),
        name="moba",
    )(q, k, v)


MERGE_TM = 256


def _layer_norm(h, g, b):
    mu = jnp.mean(h, axis=-1, keepdims=True)
    hc = h - mu
    var = jnp.mean(hc * hc, axis=-1, keepdims=True)
    return hc * lax.rsqrt(var + LN_EPS) * g + b


def _merge_kernel(x_ref, yr_ref, ya_ref, g_ref, wbr_ref, wba_ref, wo_ref, lng_ref, lnb_ref, rwt_ref, rb_ref,
                  x1_ref, lgt_ref):
    yr = _dot(yr_ref[...], wbr_ref[...])
    ya = _dot(ya_ref[...], wba_ref[...])
    merged = g_ref[:, 0:D_MODEL].astype(F32) * yr + g_ref[:, D_MODEL:2 * D_MODEL].astype(F32) * ya
    mix = _dot(merged.astype(BF16), wo_ref[...])
    x1 = _layer_norm(DEEPNORM_ALPHA * x_ref[...] + mix, lng_ref[...], lnb_ref[...])
    x1_ref[...] = x1
    lgt_ref[...] = _dot_nt(rwt_ref[...], x1, precision=HI) + rb_ref[...]


def _merge(x2, y_rwkv, y_attn, gates, w_br, w_ba, w_o, ln_g, ln_b, router_w, router_b):
    n_tok = x2.shape[0]
    tm = MERGE_TM
    row = lambda i: (i, 0)
    full = lambda shape: pl.BlockSpec(shape, lambda i: (0, 0))
    return pl.pallas_call(
        _merge_kernel,
        grid=(n_tok // tm,),
        in_specs=[
            pl.BlockSpec((tm, D_MODEL), row), pl.BlockSpec((tm, RWKV_DIM), row), pl.BlockSpec((tm, ATTN_DIM), row),
            pl.BlockSpec((tm, GATE_COLS), row),
            full((RWKV_DIM, D_MODEL)), full((ATTN_DIM, D_MODEL)), full((D_MODEL, D_MODEL)),
            full((1, D_MODEL)), full((1, D_MODEL)), full((N_EXPERTS, D_MODEL)), full((N_EXPERTS, 1)),
        ],
        out_specs=[pl.BlockSpec((tm, D_MODEL), row), pl.BlockSpec((N_EXPERTS, tm), lambda i: (0, i))],
        out_shape=[jax.ShapeDtypeStruct((n_tok, D_MODEL), F32), jax.ShapeDtypeStruct((N_EXPERTS, n_tok), F32)],
        compiler_params=_cparams(("parallel",)),
        name="merge",
    )(x2, y_rwkv, y_attn, gates, w_br.astype(BF16), w_ba.astype(BF16), w_o.astype(BF16),
      ln_g.reshape(1, -1), ln_b.reshape(1, -1), router_w.T, router_b.reshape(-1, 1))


MOE_TM = 512
ROW_BLOCK = 256
PIECE_ALIGN = 2 * SUBLANE
PIECE_SIZES = (512, 256, 128, 64, 32, 16)
STAGE_CHUNK = 256


def _moe_dims(n_tok):
    n_tiles = n_tok // MOE_TM
    stage_rows = -(-(MOE_TM * TOP_K + N_EXPERTS * (PIECE_ALIGN - 1)) // STAGE_CHUNK) * STAGE_CHUNK
    max_rows = n_tok * TOP_K + n_tiles * N_EXPERTS * (PIECE_ALIGN - 1) + N_EXPERTS * (ROW_BLOCK - 1)
    n_blk = -(-max_rows // ROW_BLOCK)
    map_lanes = -(-(n_blk + 1) // LANE) * LANE
    return n_tiles, stage_rows, n_blk, map_lanes


def _round_up_f32(x, m):
    return jnp.floor((x + (m - 1)) * (1.0 / m)) * m


def _route_kernel(lg_ref, wt_ref, slot_ref, off_ref, cnt_ref, map_ref, idx_scr, pos_scr, cnt_scr, off_scr):
    tm = MOE_TM
    n_tiles = cnt_scr.shape[1]
    phase = pl.program_id(0)
    i = pl.program_id(1)
    tok0 = pl.multiple_of(i * tm, tm)
    eio = lax.broadcasted_iota(I32, (N_EXPERTS, tm), 0)
    tile_lane = lax.broadcasted_iota(I32, (N_EXPERTS, n_tiles), 1)
    e_from = lax.broadcasted_iota(I32, (N_EXPERTS, N_EXPERTS), 1)
    e_to = lax.broadcasted_iota(I32, (N_EXPERTS, N_EXPERTS), 0)
    earlier_e = jnp.where(e_from < e_to, 1.0, 0.0)

    @pl.when((phase == 0) & (i == 0))
    def _():
        cnt_scr[...] = jnp.zeros_like(cnt_scr)

    @pl.when(phase == 0)
    def _():
        work = lg_ref[...]
        vals, hots = [], []
        for s in range(TOP_K):
            m = jnp.max(work, axis=0, keepdims=True)
            ix = jnp.min(jnp.where(work == m, eio, N_EXPERTS), axis=0, keepdims=True)
            hot = eio == ix
            idx_scr[s:s + 1, pl.ds(tok0, tm)] = ix
            vals.append(m)
            hots.append(hot)
            work = jnp.where(hot, -jnp.inf, work)
        es = [jnp.exp(v - vals[0]) for v in vals]
        denom = es[0] + es[1] + es[2] + es[3]
        for s in range(TOP_K):
            wt_ref[s:s + 1, :] = es[s] / denom
        multi_f = jnp.where(hots[0] | hots[1] | hots[2] | hots[3], 1.0, 0.0)
        t_from = lax.broadcasted_iota(I32, (tm, tm), 0)
        t_to = lax.broadcasted_iota(I32, (tm, tm), 1)
        before = jnp.where(t_from < t_to, 1.0, 0.0).astype(BF16)
        count = _dot(multi_f.astype(BF16), before)
        for s in range(TOP_K):
            pos_scr[s:s + 1, pl.ds(tok0, tm)] = jnp.sum(
                jnp.where(hots[s], count, 0.0), axis=0, keepdims=True).astype(I32)
        cnt8 = _round_up_f32(jnp.sum(multi_f, axis=1, keepdims=True), PIECE_ALIGN)
        cnt_scr[...] = cnt_scr[...] + jnp.where(tile_lane == i, cnt8, 0.0)

    @pl.when((phase == 1) & (i == 0))
    def _():
        cnt8 = cnt_scr[...]
        tot = _round_up_f32(jnp.sum(cnt8, axis=1, keepdims=True), ROW_BLOCK)
        p_start = _dot_hi(earlier_e, jnp.broadcast_to(tot, (N_EXPERTS, n_tiles)))
        i_from = lax.broadcasted_iota(I32, (n_tiles, n_tiles), 0)
        i_to = lax.broadcasted_iota(I32, (n_tiles, n_tiles), 1)
        earlier_tiles = _dot_hi(cnt8, jnp.where(i_from < i_to, 1.0, 0.0))
        off_scr[...] = p_start + earlier_tiles
        off_ref[...] = off_scr[...].astype(I32)
        cnt_ref[...] = cnt8.astype(I32)
        p_end = p_start[:, 0:1] + tot
        lanes = map_ref.shape[1]
        blk_start = lax.broadcasted_iota(I32, (N_EXPERTS, lanes), 1).astype(F32) * ROW_BLOCK
        blk_e = jnp.sum(jnp.where(blk_start >= p_end, 1.0, 0.0), axis=0, keepdims=True)
        blk_e = jnp.minimum(blk_e, N_EXPERTS - 1.0)
        n_used = jnp.max(p_end, axis=0, keepdims=True) * (1.0 / ROW_BLOCK)
        last = lax.broadcasted_iota(I32, (1, lanes), 1) == lanes - 1
        map_ref[...] = jnp.where(last, n_used, blk_e).astype(I32)

    @pl.when(phase == 1)
    def _():
        cnt_col = jnp.sum(jnp.where(tile_lane == i, cnt_scr[...], 0.0), axis=1, keepdims=True)
        local_off = _dot_hi(earlier_e, jnp.broadcast_to(cnt_col, (N_EXPERTS, tm)))
        for s in range(TOP_K):
            hot = eio == idx_scr[s:s + 1, pl.ds(tok0, tm)]
            base = jnp.sum(jnp.where(hot, local_off, 0.0), axis=0, keepdims=True).astype(I32)
            slot_ref[s:s + 1, :] = base + pos_scr[s:s + 1, pl.ds(tok0, tm)]


def _route(logits_t):
    n_tok = logits_t.shape[1]
    n_tiles, _, _, map_lanes = _moe_dims(n_tok)
    tm = MOE_TM
    last = n_tiles - 1
    const = lambda p, i: (0, 0)
    return pl.pallas_call(
        _route_kernel,
        grid=(2, n_tiles),
        in_specs=[pl.BlockSpec((N_EXPERTS, tm), lambda p, i: (0, i * (1 - p) + last * p))],
        out_specs=[
            pl.BlockSpec((TOP_K, tm), lambda p, i: (0, i * (1 - p) + last * p)),
            pl.BlockSpec((TOP_K, tm), lambda p, i: (0, i * p)),
            pl.BlockSpec((N_EXPERTS, n_tiles), const),
            pl.BlockSpec((N_EXPERTS, n_tiles), const),
            pl.BlockSpec((1, map_lanes), const),
        ],
        out_shape=[
            jax.ShapeDtypeStruct((TOP_K, n_tok), F32),
            jax.ShapeDtypeStruct((TOP_K, n_tok), I32),
            jax.ShapeDtypeStruct((N_EXPERTS, n_tiles), I32),
            jax.ShapeDtypeStruct((N_EXPERTS, n_tiles), I32),
            jax.ShapeDtypeStruct((1, map_lanes), I32),
        ],
        scratch_shapes=[
            pltpu.VMEM((TOP_K, n_tok), I32),
            pltpu.VMEM((TOP_K, n_tok), I32),
            pltpu.VMEM((N_EXPERTS, n_tiles), F32),
            pltpu.VMEM((N_EXPERTS, n_tiles), F32),
        ],
        compiler_params=_cparams(("arbitrary", "arbitrary")),
        name="route",
    )(logits_t)


def _piece_copies(off_ref, cnt_ref, tile, make_copy, action):
    def per_expert(e, stage_row):
        c8 = cnt_ref[tile * N_EXPERTS + e]
        hbm_row = off_ref[tile * N_EXPERTS + e]
        done = 0
        for size in PIECE_SIZES:
            bit = c8 & size

            @pl.when(bit != 0)
            def _(done=done, size=size):
                action(make_copy(pl.multiple_of(stage_row + done, PIECE_ALIGN),
                                 pl.multiple_of(hbm_row + done, PIECE_ALIGN), size))

            done = done + bit
        return stage_row + c8

    lax.fori_loop(0, N_EXPERTS, per_expert, 0)


def _slot_rows(slot_ref):
    return [slot_ref[s:s + 1, :] for s in range(TOP_K)]


def _dispatch_kernel(off_ref, cnt_ref, slot_ref, x1_ref, xs_in_hbm, xs_hbm, stage, sem):
    del xs_in_hbm
    i = pl.program_id(0)
    n_tiles = pl.num_programs(0)
    buf = i % 2

    def out_copies(tile, b, action):
        def make_copy(stage_row, hbm_row, size):
            return pltpu.make_async_copy(stage.at[b, pl.ds(stage_row, size)], xs_hbm.at[pl.ds(hbm_row, size)],
                                         sem.at[b])
        _piece_copies(off_ref, cnt_ref, tile, make_copy, action)

    @pl.when(i >= 2)
    def _():
        out_copies(i - 2, buf, lambda cp: cp.wait())

    xb = x1_ref[...].astype(BF16)
    slots = _slot_rows(slot_ref)
    for r0 in range(0, stage.shape[1], STAGE_CHUNK):
        rio = lax.broadcasted_iota(I32, (STAGE_CHUNK, MOE_TM), 0) + r0
        pick = jnp.zeros((STAGE_CHUNK, MOE_TM), F32)
        for s in range(TOP_K):
            pick = jnp.where(rio == slots[s], 1.0, pick)
        stage[buf, r0:r0 + STAGE_CHUNK, :] = _dot(pick.astype(BF16), xb).astype(BF16)
    out_copies(i, buf, lambda cp: cp.start())

    @pl.when(i == n_tiles - 1)
    def _():
        @pl.when(i >= 1)
        def _():
            out_copies(i - 1, 1 - buf, lambda cp: cp.wait())

        out_copies(i, buf, lambda cp: cp.wait())


def _dispatch(off_flat, cnt_flat, slot, x1):
    n_tok = x1.shape[0]
    n_tiles, stage_rows, n_blk, _ = _moe_dims(n_tok)
    n_rows = n_blk * ROW_BLOCK
    any_spec = pl.BlockSpec(memory_space=pl.ANY)
    return pl.pallas_call(
        _dispatch_kernel,
        grid_spec=pltpu.PrefetchScalarGridSpec(
            num_scalar_prefetch=2,
            grid=(n_tiles,),
            in_specs=[pl.BlockSpec((TOP_K, MOE_TM), lambda i, off, cnt: (0, i)),
                      pl.BlockSpec((MOE_TM, D_MODEL), lambda i, off, cnt: (i, 0)),
                      any_spec],
            out_specs=any_spec,
            scratch_shapes=[pltpu.VMEM((2, stage_rows, D_MODEL), BF16), pltpu.SemaphoreType.DMA((2,))],
        ),
        out_shape=jax.ShapeDtypeStruct((n_rows, D_MODEL), BF16),
        input_output_aliases={4: 0},
        compiler_params=_cparams(("arbitrary",)),
        name="dispatch",
    )(off_flat, cnt_flat, slot, x1, jnp.zeros((n_rows, D_MODEL), BF16))


def _combine_kernel(off_ref, cnt_ref, slot_ref, wt_ref, x1_ref, lng_ref, lnb_ref, ys_hbm, out_ref, stage, sem):
    i = pl.program_id(0)
    n_tiles = pl.num_programs(0)
    buf = i % 2

    def in_copies(tile, b, action):
        def make_copy(stage_row, hbm_row, size):
            return pltpu.make_async_copy(ys_hbm.at[pl.ds(hbm_row, size)], stage.at[b, pl.ds(stage_row, size)],
                                         sem.at[b])
        _piece_copies(off_ref, cnt_ref, tile, make_copy, action)

    @pl.when(i == 0)
    def _():
        stage[...] = jnp.zeros_like(stage)
        in_copies(0, 0, lambda cp: cp.start())

    @pl.when(i + 1 < n_tiles)
    def _():
        in_copies(i + 1, 1 - buf, lambda cp: cp.start())

    in_copies(i, buf, lambda cp: cp.wait())

    slots = _slot_rows(slot_ref)
    ffn = jnp.zeros((MOE_TM, D_MODEL), F32)
    for r0 in range(0, stage.shape[1], STAGE_CHUNK):
        rio = lax.broadcasted_iota(I32, (STAGE_CHUNK, MOE_TM), 0) + r0
        w = jnp.zeros((STAGE_CHUNK, MOE_TM), F32)
        for s in range(TOP_K):
            w = jnp.where(rio == slots[s], wt_ref[s:s + 1, :], w)
        w_a, w_b = _bf16_pieces(w, 2)
        y = stage[buf, r0:r0 + STAGE_CHUNK, :]
        ffn = ffn + _dot_tn(w_a, y) + _dot_tn(w_b, y)
    out_ref[...] = _layer_norm(DEEPNORM_ALPHA * x1_ref[...] + ffn, lng_ref[...], lnb_ref[...])


def _combine(off_flat, cnt_flat, slot, wts, x1, ln_g, ln_b, ys):
    n_tok = x1.shape[0]
    n_tiles, stage_rows, _, _ = _moe_dims(n_tok)
    tok = lambda i, off, cnt: (0, i)
    row = lambda i, off, cnt: (i, 0)
    full = lambda shape: pl.BlockSpec(shape, lambda i, off, cnt: (0, 0))
    return pl.pallas_call(
        _combine_kernel,
        grid_spec=pltpu.PrefetchScalarGridSpec(
            num_scalar_prefetch=2,
            grid=(n_tiles,),
            in_specs=[pl.BlockSpec((TOP_K, MOE_TM), tok), pl.BlockSpec((TOP_K, MOE_TM), tok),
                      pl.BlockSpec((MOE_TM, D_MODEL), row), full((1, D_MODEL)), full((1, D_MODEL)),
                      pl.BlockSpec(memory_space=pl.ANY)],
            out_specs=pl.BlockSpec((MOE_TM, D_MODEL), row),
            scratch_shapes=[pltpu.VMEM((2, stage_rows, D_MODEL), BF16), pltpu.SemaphoreType.DMA((2,))],
        ),
        out_shape=jax.ShapeDtypeStruct((n_tok, D_MODEL), F32),
        compiler_params=_cparams(("arbitrary",)),
        name="combine",
    )(off_flat, cnt_flat, slot, wts, x1, ln_g.reshape(1, -1), ln_b.reshape(1, -1), ys)


CAST_ROWS = 128


def _expert_kernel(blk_e_ref, n_used_ref, xs_ref, win_ref, bin_ref, wout_ref, bout_ref, ys_ref, win_bf, wout_bf):
    rb = pl.program_id(0)
    new_expert = (rb == 0) | (blk_e_ref[rb] != blk_e_ref[jnp.maximum(rb - 1, 0)])

    @pl.when((rb < n_used_ref[0]) & new_expert)
    def _():
        for r in range(0, D_MODEL, CAST_ROWS):
            win_bf[r:r + CAST_ROWS, :] = win_ref[0, r:r + CAST_ROWS, :].astype(BF16)
        for r in range(0, D_EXPERT, CAST_ROWS):
            wout_bf[r:r + CAST_ROWS, :] = wout_ref[0, r:r + CAST_ROWS, :].astype(BF16)

    @pl.when(rb < n_used_ref[0])
    def _():
        h = _dot(xs_ref[...], win_bf[...]) + bin_ref[0]
        gate_h = jnp.minimum(h[:, 0:D_EXPERT], SWIGLU_LIMIT)
        lin_h = jnp.clip(h[:, D_EXPERT:2 * D_EXPERT], -SWIGLU_LIMIT, SWIGLU_LIMIT)
        act = gate_h * jax.nn.sigmoid(SWIGLU_ALPHA * gate_h) * (lin_h + 1.0)
        ys_ref[...] = (_dot(act.astype(BF16), wout_bf[...]) + bout_ref[0]).astype(BF16)

    @pl.when(rb >= n_used_ref[0])
    def _():
        ys_ref[...] = jnp.zeros_like(ys_ref)


def _experts(blk_e, n_used, xs, w_in, b_in, w_out, b_out):
    n_rows = xs.shape[0]
    n_blk = n_rows // ROW_BLOCK
    used = lambda rb, n_used: jnp.maximum(jnp.minimum(rb, n_used[0] - 1), 0)
    rows = lambda rb, blk_e, n_used: (used(rb, n_used), 0)
    per_e = lambda rb, blk_e, n_used: (blk_e[used(rb, n_used)], 0, 0)
    return pl.pallas_call(
        _expert_kernel,
        grid_spec=pltpu.PrefetchScalarGridSpec(
            num_scalar_prefetch=2,
            grid=(n_blk,),
            in_specs=[
                pl.BlockSpec((ROW_BLOCK, D_MODEL), rows),
                pl.BlockSpec((1, D_MODEL, 2 * D_EXPERT), per_e),
                pl.BlockSpec((1, 1, 2 * D_EXPERT), per_e),
                pl.BlockSpec((1, D_EXPERT, D_MODEL), per_e),
                pl.BlockSpec((1, 1, D_MODEL), per_e),
            ],
            out_specs=pl.BlockSpec((ROW_BLOCK, D_MODEL), lambda rb, blk_e, n_used: (rb, 0)),
            scratch_shapes=[pltpu.VMEM((D_MODEL, 2 * D_EXPERT), BF16), pltpu.VMEM((D_EXPERT, D_MODEL), BF16)],
        ),
        out_shape=jax.ShapeDtypeStruct((n_rows, D_MODEL), BF16),
        compiler_params=_cparams(("arbitrary",)),
        name="experts",
    )(blk_e, n_used, xs, w_in, b_in.reshape(N_EXPERTS, 1, -1), w_out, b_out.reshape(N_EXPERTS, 1, -1))


def _moe(x1, logits_t, expert_w_in, expert_b_in, expert_w_out, expert_b_out, ln_g, ln_b):
    n_tok = x1.shape[0]
    _, _, n_blk, map_lanes = _moe_dims(n_tok)
    wts, slot, off, cnt, blk_map = _route(logits_t)
    off_flat = off.T.reshape(-1)
    cnt_flat = cnt.T.reshape(-1)
    blk_e = blk_map[0, 0:n_blk]
    n_used = blk_map[0, map_lanes - 1:map_lanes]
    xs = _dispatch(off_flat, cnt_flat, slot, x1)
    ys = _experts(blk_e, n_used, xs, expert_w_in, expert_b_in, expert_w_out, expert_b_out)
    return _combine(off_flat, cnt_flat, slot, wts, x1, ln_g, ln_b, ys)


def kernel(x, ln1_g, ln1_b, ln2_g, ln2_b, w_in, shift_mix, decay_w0, decay_up, iclr_a0, iclr_up, gate_up, k_k, k_a, r_k, gn_g, gn_b, w_branch_rwkv, w_branch_attn, w_out, router_w, router_b, expert_w_in, expert_b_in, expert_w_out, expert_b_out):
    B, T, D = x.shape
    x2 = x.reshape(B * T, D)
    prw, q, k, v, gates = _proj(x2, w_in[0].astype(BF16), T)
    y_rwkv = _rwkv(prw, B, T, shift_mix[0], decay_w0[0], decay_up[0], iclr_a0[0], iclr_up[0], gate_up[0],
                   k_k[0], k_a[0], r_k[0], gn_g[0], gn_b[0])
    y_attn = _moba(q, k, v, B, T)
    x1, logits_t = _merge(x2, y_rwkv, y_attn, gates, w_branch_rwkv[0], w_branch_attn[0], w_out[0],
                          ln1_g[0], ln1_b[0], router_w[0], router_b[0])
    out = _moe(x1, logits_t, expert_w_in[0], expert_b_in[0], expert_w_out[0], expert_b_out[0], ln2_g[0], ln2_b[0])
    return out.reshape(B, T, D)
```

```python
import math

import jax
import jax.numpy as jnp
from jax import lax
from jax.experimental import pallas as pl
from jax.experimental.pallas import tpu as pltpu

F32 = jnp.float32
BF16 = jnp.bfloat16
I32 = jnp.int32
HI = lax.Precision.HIGHEST

D_MODEL = 1024
DEPTH = 1
RWKV_HEAD_DIM = 64
RWKV_DIM = 512
RWKV_HEADS = 8
DECAY_RANK = 64
ICLR_RANK = 64
GATE_RANK = 128
GN_EPS = 64e-5
ATTN_HEAD_DIM = 64
ATTN_DIM = 512
ATTN_HEADS = 8
MOBA_BLOCK = 256
MOBA_TOPK = 3
ROPE_THETA = 500000.0
ROPE_DIM = 16
NEG_INF = -1e30
N_EXPERTS = 32
TOP_K = 4
D_EXPERT = 1024
SWIGLU_LIMIT = 7.0
SWIGLU_ALPHA = 1.702
DEEPNORM_ALPHA = (2.0 * DEPTH) ** 0.25
LN_EPS = 1e-5
RWKV_COLS = 3 * RWKV_DIM + DECAY_RANK + ICLR_RANK + GATE_RANK
ATTN_COLS = 3 * ATTN_DIM
GATE_COLS = 2 * D_MODEL
IN_COLS = RWKV_COLS + ATTN_COLS + GATE_COLS

LANE = 128
SUBLANE = 8
VMEM_LIMIT_BYTES = 56 * 1024 * 1024

CHUNK = 64


def _cparams(sem):
    return pltpu.CompilerParams(dimension_semantics=sem, vmem_limit_bytes=VMEM_LIMIT_BYTES)


def _dot(a, b):
    return jnp.dot(a, b, preferred_element_type=F32)


def _dot_hi(a, b):
    return jnp.dot(a, b, preferred_element_type=F32, precision=HI)


def _dot_nt(a, b, precision=None):
    return lax.dot_general(a, b, (((1,), (1,)), ((), ())), preferred_element_type=F32, precision=precision)


def _dot_tn(a, b, precision=None):
    return lax.dot_general(a, b, (((0,), (0,)), ((), ())), preferred_element_type=F32, precision=precision)


def _bf16_pieces(x, n):
    pieces = []
    for _ in range(n):
        p = x.astype(BF16)
        pieces.append(p)
        x = x - p.astype(F32)
    return pieces


def _dot_exact_rhs(x, b_bf16, n):
    out = None
    for p in _bf16_pieces(x, n):
        d = _dot(p, b_bf16)
        out = d if out is None else out + d
    return out


PROJ_TM = 256


def _proj_kernel(x_ref, w_ref, cos_ref, sa_ref, sb_ref, prw_ref, q_ref, k_ref, v_ref, g_ref):
    xb = x_ref[...].astype(BF16)
    prw_ref[...] = _dot(xb, w_ref[:, 0:RWKV_COLS])
    c0 = RWKV_COLS
    cos = cos_ref[...]
    sa = sa_ref[...]
    sb = sb_ref[...]

    def rope(t):
        return t * cos + pltpu.roll(t, ATTN_DIM - ROPE_DIM // 2, 1) * sa + pltpu.roll(t, ROPE_DIM // 2, 1) * sb

    q_ref[...] = rope(_dot(xb, w_ref[:, c0:c0 + ATTN_DIM])).astype(BF16)
    k_ref[...] = rope(_dot(xb, w_ref[:, c0 + ATTN_DIM:c0 + 2 * ATTN_DIM])).astype(BF16)
    v_ref[...] = _dot(xb, w_ref[:, c0 + 2 * ATTN_DIM:c0 + 3 * ATTN_DIM]).astype(BF16)
    c1 = RWKV_COLS + ATTN_COLS
    g_ref[...] = jax.nn.sigmoid(_dot(xb, w_ref[:, c1:c1 + GATE_COLS])).astype(BF16)


def _rope_tables(T):
    half = ROPE_DIM // 2
    inv_freq = jnp.power(ROPE_THETA, -jnp.arange(0, ROPE_DIM, 2, dtype=F32) / ROPE_DIM)
    ang = jnp.arange(T).astype(F32)[:, None] * inv_freq[None, :]
    cos, sin = jnp.cos(ang), jnp.sin(ang)
    pad = jnp.zeros((T, ATTN_HEAD_DIM - ROPE_DIM), F32)
    cos_h = jnp.concatenate([cos, cos, pad + 1.0], axis=1)
    sa_h = jnp.concatenate([-sin, jnp.zeros((T, half), F32), pad], axis=1)
    sb_h = jnp.concatenate([jnp.zeros((T, half), F32), sin, pad], axis=1)
    tile = lambda t: jnp.tile(t, (1, ATTN_HEADS))
    return tile(cos_h), tile(sa_h), tile(sb_h)


def _proj(x2, w_in_bf, T):
    n_tok = x2.shape[0]
    tm = PROJ_TM
    t_tiles = T // tm
    cos, sa, sb = _rope_tables(T)
    row = lambda i: (i, 0)
    tab = lambda i: (i % t_tiles, 0)
    return pl.pallas_call(
        _proj_kernel,
        grid=(n_tok // tm,),
        in_specs=[
            pl.BlockSpec((tm, D_MODEL), row),
            pl.BlockSpec((D_MODEL, IN_COLS), lambda i: (0, 0), pipeline_mode=pl.Buffered(1)),
            pl.BlockSpec((tm, ATTN_DIM), tab),
            pl.BlockSpec((tm, ATTN_DIM), tab),
            pl.BlockSpec((tm, ATTN_DIM), tab),
        ],
        out_specs=[
            pl.BlockSpec((tm, RWKV_COLS), row),
            pl.BlockSpec((tm, ATTN_DIM), row),
            pl.BlockSpec((tm, ATTN_DIM), row),
            pl.BlockSpec((tm, ATTN_DIM), row),
            pl.BlockSpec((tm, GATE_COLS), row),
        ],
        out_shape=[
            jax.ShapeDtypeStruct((n_tok, RWKV_COLS), F32),
            jax.ShapeDtypeStruct((n_tok, ATTN_DIM), BF16),
            jax.ShapeDtypeStruct((n_tok, ATTN_DIM), BF16),
            jax.ShapeDtypeStruct((n_tok, ATTN_DIM), BF16),
            jax.ShapeDtypeStruct((n_tok, GATE_COLS), BF16),
        ],
        compiler_params=_cparams(("parallel",)),
        name="proj",
    )(x2, w_in_bf, cos, sa, sb)


RWKV_TT = 256


def _rwkv_kernel(p_ref, mix_ref, w0_ref, dup_ref, a0_ref, iup_ref, gup_ref, kk_ref, ka_ref, rk_ref,
                 gng_ref, gnb_ref, ones_ref, y_ref, s_scr, prev_scr):
    H, N, C = RWKV_HEADS, RWKV_HEAD_DIM, CHUNK

    @pl.when(pl.program_id(1) == 0)
    def _():
        s_scr[...] = jnp.zeros_like(s_scr)
        prev_scr[...] = jnp.zeros_like(prev_scr)

    TT = RWKV_TT
    n_chunks = TT // C
    bones = ones_ref[...]
    row = lax.broadcasted_iota(I32, (C, C), 0)
    col = lax.broadcasted_iota(I32, (C, C), 1)
    lower_incl = col <= row
    lower_strict = col < row
    eye = jnp.where(col == row, 1.0, 0.0)
    t_row = lax.broadcasted_iota(I32, (TT, TT), 0)
    t_col = lax.broadcasted_iota(I32, (TT, TT), 1)
    same_chunk = (t_row // C) == (t_col // C)
    chunk_ltri = jnp.where(same_chunk & (t_col <= t_row), 1.0, 0.0).astype(BF16)
    chunk_ones = jnp.where(same_chunk, 1.0, 0.0).astype(BF16)

    p = p_ref[...]
    first_row = lax.broadcasted_iota(I32, (TT, RWKV_COLS), 0) == 0
    prev = jnp.where(first_row, prev_scr[...], pltpu.roll(p, 1, 0))
    prev_scr[...] = p[TT - 1:TT, :]
    ps = p + (prev - p) * mix_ref[...]
    r = ps[:, 0:RWKV_DIM]
    k = ps[:, RWKV_DIM:2 * RWKV_DIM]
    v = ps[:, 2 * RWKV_DIM:3 * RWKV_DIM]
    o = 3 * RWKV_DIM
    xw = ps[:, o:o + DECAY_RANK]
    xa = ps[:, o + DECAY_RANK:o + DECAY_RANK + ICLR_RANK]
    xg = ps[:, o + DECAY_RANK + ICLR_RANK:RWKV_COLS]
    w_raw = w0_ref[...] + _dot(jnp.tanh(xw).astype(BF16), dup_ref[...])
    logw = -math.exp(-0.5) * jax.nn.sigmoid(w_raw)
    a = jax.nn.sigmoid(a0_ref[...] + _dot(xa.astype(BF16), iup_ref[...]))
    g = _dot(jax.nn.sigmoid(xg).astype(BF16), gup_ref[...])
    kk0 = k * kk_ref[...]
    kk = kk0 / jnp.maximum(jnp.sqrt(_dot_exact_rhs(kk0 * kk0, bones, 2)), 1e-12)
    kp = k * (1.0 + (a - 1.0) * ka_ref[...])
    kka = kk * a
    logw_pieces = _bf16_pieces(logw, 3)
    cum = sum(_dot(chunk_ltri, piece) for piece in logw_pieces)
    tot = sum(_dot(chunk_ones, piece) for piece in logw_pieces)
    e_neg = jnp.exp(-cum)
    at = (-kk * jnp.exp(cum - logw)).astype(BF16)
    rt = (r * jnp.exp(cum)).astype(BF16)
    bt = (kka * e_neg).astype(BF16)
    kt = (kp * e_neg).astype(BF16)
    e_end = jnp.exp(tot - cum)
    bh = (kka * e_end).astype(BF16)
    kh = (kp * e_end).astype(BF16)
    e_tot = jnp.exp(tot)
    vb = v.astype(BF16)

    units = [(c, h) for c in range(n_chunks) for h in range(H)]
    blk = lambda t, c, h: t[c * C:(c + 1) * C, h * N:(h + 1) * N]
    lhs = {u: jnp.concatenate([blk(at, *u), blk(rt, *u)], axis=0) for u in units}
    rhs = {u: jnp.concatenate([blk(bt, *u), blk(kt, *u)], axis=0) for u in units}
    aa = {u: _dot_nt(lhs[u], rhs[u]) for u in units}
    a_ab = {u: jnp.where(lower_strict, aa[u][0:C, 0:C], 0.0) for u in units}
    a_kv = {u: jnp.concatenate([jnp.where(lower_strict, aa[u][0:C, C:2 * C], 0.0),
                                jnp.where(lower_incl, aa[u][C:2 * C, C:2 * C], 0.0)], axis=0).astype(BF16)
            for u in units}
    a_rb = {u: jnp.where(lower_incl, aa[u][C:2 * C, 0:C], 0.0).astype(BF16) for u in units}
    akv = {u: _dot(a_kv[u], blk(vb, *u)) for u in units}
    tinv = {u: eye + a_ab[u] for u in units}
    npow = a_ab
    for _ in range(5):
        npb = {u: npow[u].astype(BF16) for u in units}
        npow = {u: _dot(npb[u], npb[u]) for u in units}
        tinv = {u: tinv[u] + _dot(tinv[u].astype(BF16), npow[u].astype(BF16)) for u in units}
    tinv_b = {u: tinv[u].astype(BF16) for u in units}

    state = [s_scr[h] for h in range(H)]
    y_rows = []
    for c in range(n_chunks):
        hs = [(c, h) for h in range(H)]
        ar_s = {u: _dot_nt(lhs[u], state[u[1]].astype(BF16)) for u in hs}
        ub = {u: _dot(tinv_b[u], (ar_s[u][0:C] + akv[u][0:C]).astype(BF16)).astype(BF16) for u in hs}
        ys = [ar_s[u][C:2 * C] + akv[u][C:2 * C] + _dot(a_rb[u], ub[u]) for u in hs]
        upd = {u: _dot_tn(jnp.concatenate([ub[u], blk(vb, *u)], axis=0),
                          jnp.concatenate([blk(bh, *u), blk(kh, *u)], axis=0)) for u in hs}
        state = [state[h] * e_tot[c * C:c * C + 1, h * N:(h + 1) * N] + upd[(c, h)] for h in range(H)]
        y_rows.append(jnp.concatenate(ys, axis=1))
    for h in range(H):
        s_scr[h] = state[h]

    y = jnp.concatenate(y_rows, axis=0)
    mu = _dot_exact_rhs(y, bones, 2) * (1.0 / N)
    yc = y - mu
    var = _dot_exact_rhs(yc * yc, bones, 2) * (1.0 / N)
    yn = yc * lax.rsqrt(var + GN_EPS) * gng_ref[...] + gnb_ref[...]
    bonus = _dot_exact_rhs(r * kp * rk_ref[...], bones, 2) * v
    y_ref[...] = ((yn + bonus) * g).astype(BF16)


def _rwkv(p_rwkv, B, T, shift_mix, decay_w0, decay_up, iclr_a0, iclr_up, gate_up, k_k, k_a, r_k, gn_g, gn_b):
    tt = RWKV_TT
    n_t = T // tt
    head = jnp.arange(RWKV_DIM) // RWKV_HEAD_DIM
    bones = (head[:, None] == head[None, :]).astype(BF16)
    vec = lambda a: a.reshape(1, -1)
    full = lambda shape: pl.BlockSpec(shape, lambda b, j: (0, 0))
    return pl.pallas_call(
        _rwkv_kernel,
        grid=(B, n_t),
        in_specs=[
            pl.BlockSpec((tt, RWKV_COLS), lambda b, j: (b * n_t + j, 0)),
            full((1, RWKV_COLS)), full((1, RWKV_DIM)), full((DECAY_RANK, RWKV_DIM)),
            full((1, RWKV_DIM)), full((ICLR_RANK, RWKV_DIM)), full((GATE_RANK, RWKV_DIM)),
            full((1, RWKV_DIM)), full((1, RWKV_DIM)), full((1, RWKV_DIM)),
            full((1, RWKV_DIM)), full((1, RWKV_DIM)), full((RWKV_DIM, RWKV_DIM)),
        ],
        out_specs=pl.BlockSpec((tt, RWKV_DIM), lambda b, j: (b * n_t + j, 0)),
        out_shape=jax.ShapeDtypeStruct((B * T, RWKV_DIM), BF16),
        scratch_shapes=[
            pltpu.VMEM((RWKV_HEADS, RWKV_HEAD_DIM, RWKV_HEAD_DIM), F32),
            pltpu.VMEM((1, RWKV_COLS), F32),
        ],
        compiler_params=_cparams(("parallel", "arbitrary")),
        name="rwkv",
    )(p_rwkv, vec(shift_mix), vec(decay_w0), decay_up.astype(BF16), vec(iclr_a0), iclr_up.astype(BF16),
      gate_up.astype(BF16), vec(k_k), vec(k_a), vec(r_k), vec(gn_g), vec(gn_b), bones)


MOBA_HP = 4


def _moba_kernel(q_ref, k_ref, v_ref, o_ref, kmean_scr, sel_scr, acc_scr, score_scr):
    blk_sz, dh = MOBA_BLOCK, ATTN_HEAD_DIM
    nb = k_ref.shape[0] // blk_sz
    i = pl.program_id(2)

    @pl.when(i == 0)
    def _():
        for n in range(nb):
            kb = k_ref[n * blk_sz:(n + 1) * blk_sz, :].astype(F32)
            kmean_scr[n:n + 1, :] = jnp.sum(kb, axis=0, keepdims=True) * (1.0 / blk_sz)

    blk = lax.broadcasted_iota(I32, (nb, blk_sz), 0)
    kpos = lax.broadcasted_iota(I32, (blk_sz, blk_sz), 0)
    qpos = lax.broadcasted_iota(I32, (blk_sz, blk_sz), 1)
    own_start = pl.multiple_of(i * blk_sz, blk_sz)
    heads = [slice(hh * dh, (hh + 1) * dh) for hh in range(MOBA_HP)]
    hds = range(MOBA_HP)
    qhs = [q_ref[:, hs] for hs in heads]
    kmean_pieces = _bf16_pieces(kmean_scr[...], 3)
    gates = [sum(_dot_nt(piece[:, heads[hh]], qhs[hh]) for piece in kmean_pieces) for hh in hds]
    qss = [qhs[hh] * (dh ** -0.5) for hh in hds]
    own = [_dot_nt(k_ref[pl.ds(own_start, blk_sz), heads[hh]], qss[hh]) for hh in hds]
    for hh in hds:
        gate = jnp.where(blk < i, gates[hh], NEG_INF)
        rank = jnp.zeros((nb, blk_sz), I32)
        for m in range(nb):
            gm = gate[m:m + 1, :]
            beats = (gm > gate) | ((gm == gate) & (m < blk))
            rank = rank + beats.astype(I32)
        sel_scr[hh] = ((rank < MOBA_TOPK) & (blk < i)).astype(F32)
    stats, ps = [], []
    for hh in hds:
        s = jnp.where(kpos <= qpos, own[hh], NEG_INF)
        m0 = jnp.max(s, axis=0, keepdims=True)
        p = jnp.exp(s - m0)
        stats += [m0, jnp.sum(p, axis=0, keepdims=True)]
        ps.append(p.astype(BF16))
    pvs = [_dot_tn(v_ref[pl.ds(own_start, blk_sz), heads[hh]], ps[hh]) for hh in hds]
    for hh in hds:
        acc_scr[hh] = pvs[hh]

    def scores(n):
        start = pl.multiple_of(n * blk_sz, blk_sz)
        return [_dot_nt(k_ref[pl.ds(start, blk_sz), heads[hh]], qss[hh]) for hh in hds]

    first = scores(0)
    for hh in hds:
        score_scr[hh] = first[hh]

    def body(n, carry):
        ahead = scores(jnp.minimum(n + 1, i - 1))
        start = pl.multiple_of(n * blk_sz, blk_sz)
        out, ps, alphas = [], [], []
        for hh in hds:
            m_run, l_run = carry[2 * hh], carry[2 * hh + 1]
            s = jnp.where(sel_scr[hh, pl.ds(n, 1), :] > 0.0, score_scr[hh], NEG_INF)
            m_new = jnp.maximum(m_run, jnp.max(s, axis=0, keepdims=True))
            alpha = jnp.exp(m_run - m_new)
            p = jnp.exp(s - m_new)
            out += [m_new, alpha * l_run + jnp.sum(p, axis=0, keepdims=True)]
            ps.append(p.astype(BF16))
            alphas.append(alpha)
        pvs = [_dot_tn(v_ref[pl.ds(start, blk_sz), heads[hh]], ps[hh]) for hh in hds]
        for hh in hds:
            acc_scr[hh] = alphas[hh] * acc_scr[hh] + pvs[hh]
            score_scr[hh] = ahead[hh]
        return tuple(out)

    stats = lax.fori_loop(0, i, body, tuple(stats))
    outs = [(acc_scr[hh] / stats[2 * hh + 1]).T for hh in range(MOBA_HP)]
    o_ref[...] = jnp.concatenate(outs, axis=1).astype(BF16)


def _moba(q, k, v, B, T):
    blk_sz = MOBA_BLOCK
    nq = T // blk_sz
    lanes = MOBA_HP * ATTN_HEAD_DIM
    kv_spec = pl.BlockSpec((T, lanes), lambda b, hp, i: (b, hp))
    q_spec = pl.BlockSpec((blk_sz, lanes), lambda b, hp, i: (b * nq + i, hp))
    return pl.pallas_call(
        _moba_kernel,
        grid=(B, ATTN_HEADS // MOBA_HP, nq),
        in_specs=[q_spec, kv_spec, kv_spec],
        out_specs=q_spec,
        out_shape=jax.ShapeDtypeStruct((B * T, ATTN_DIM), BF16),
        scratch_shapes=[
            pltpu.VMEM((T // blk_sz, lanes), F32),
            pltpu.VMEM((MOBA_HP, T // blk_sz, blk_sz), F32),
            pltpu.VMEM((MOBA_HP, ATTN_HEAD_DIM, blk_sz), F32),
            pltpu.VMEM((MOBA_HP, blk_sz, blk_sz), F32),
        ],
        compiler_params=_cparams(("parallel", "parallel", "arbitrary")),
        name="moba",
    )(q, k, v)


MERGE_TM = 512
MERGE_SUB = 128


def _layer_norm(h, g, b):
    mu = jnp.mean(h, axis=-1, keepdims=True)
    hc = h - mu
    var = jnp.mean(hc * hc, axis=-1, keepdims=True)
    return hc * lax.rsqrt(var + LN_EPS) * g + b


def _merge_kernel(x_ref, yr_ref, ya_ref, g_ref, wbr_ref, wba_ref, wo_ref, lng_ref, lnb_ref, rwt_ref, rb_ref,
                  x1_ref, lgt_ref):
    subs = [slice(s * MERGE_SUB, (s + 1) * MERGE_SUB) for s in range(MERGE_TM // MERGE_SUB)]
    yr = [_dot(yr_ref[sl, :], wbr_ref[...]) for sl in subs]
    ya = [_dot(ya_ref[sl, :], wba_ref[...]) for sl in subs]
    merged = [(g_ref[sl, 0:D_MODEL].astype(F32) * yr[s] + g_ref[sl, D_MODEL:2 * D_MODEL].astype(F32) * ya[s])
              .astype(BF16) for s, sl in enumerate(subs)]
    mix = [_dot(m, wo_ref[...]) for m in merged]
    x1 = [_layer_norm(DEEPNORM_ALPHA * x_ref[sl, :] + mix[s], lng_ref[...], lnb_ref[...])
          for s, sl in enumerate(subs)]
    for s, sl in enumerate(subs):
        x1_ref[sl, :] = x1[s]
    for s, sl in enumerate(subs):
        lgt_ref[:, sl] = _dot_nt(rwt_ref[...], x1[s], precision=HI) + rb_ref[...]


def _merge(x2, y_rwkv, y_attn, gates, w_br, w_ba, w_o, ln_g, ln_b, router_w, router_b):
    n_tok = x2.shape[0]
    tm = MERGE_TM
    row = lambda i: (i, 0)
    full = lambda shape: pl.BlockSpec(shape, lambda i: (0, 0))
    return pl.pallas_call(
        _merge_kernel,
        grid=(n_tok // tm,),
        in_specs=[
            pl.BlockSpec((tm, D_MODEL), row), pl.BlockSpec((tm, RWKV_DIM), row), pl.BlockSpec((tm, ATTN_DIM), row),
            pl.BlockSpec((tm, GATE_COLS), row),
            full((RWKV_DIM, D_MODEL)), full((ATTN_DIM, D_MODEL)), full((D_MODEL, D_MODEL)),
            full((1, D_MODEL)), full((1, D_MODEL)), full((N_EXPERTS, D_MODEL)), full((N_EXPERTS, 1)),
        ],
        out_specs=[pl.BlockSpec((tm, D_MODEL), row), pl.BlockSpec((N_EXPERTS, tm), lambda i: (0, i))],
        out_shape=[jax.ShapeDtypeStruct((n_tok, D_MODEL), F32), jax.ShapeDtypeStruct((N_EXPERTS, n_tok), F32)],
        compiler_params=_cparams(("parallel",)),
        name="merge",
    )(x2, y_rwkv, y_attn, gates, w_br.astype(BF16), w_ba.astype(BF16), w_o.astype(BF16),
      ln_g.reshape(1, -1), ln_b.reshape(1, -1), router_w.T, router_b.reshape(-1, 1))


MOE_TM = 512
ROW_BLOCK = 256
PIECE_ALIGN = 2 * SUBLANE
PIECE_SIZES = (512, 256, 128, 64, 32, 16)
STAGE_CHUNK = 256
ROW_W = D_MODEL + LANE


def _moe_dims(n_tok):
    n_tiles = n_tok // MOE_TM
    stage_rows = -(-(MOE_TM * TOP_K + N_EXPERTS * (PIECE_ALIGN - 1)) // STAGE_CHUNK) * STAGE_CHUNK
    max_rows = n_tok * TOP_K + n_tiles * N_EXPERTS * (PIECE_ALIGN - 1) + N_EXPERTS * (ROW_BLOCK - 1)
    n_blk = -(-max_rows // ROW_BLOCK)
    map_lanes = -(-(n_blk + 1) // LANE) * LANE
    return n_tiles, stage_rows, n_blk, map_lanes


def _round_up_f32(x, m):
    return jnp.floor((x + (m - 1)) * (1.0 / m)) * m


def _route_kernel(lg_ref, wt_ref, slot_ref, off_ref, cnt_ref, map_ref, idx_scr, pos_scr, cnt_scr, off_scr):
    tm = MOE_TM
    n_tiles = cnt_scr.shape[1]
    phase = pl.program_id(0)
    i = pl.program_id(1)
    tok0 = pl.multiple_of(i * tm, tm)
    eio = lax.broadcasted_iota(I32, (N_EXPERTS, tm), 0)
    tile_lane = lax.broadcasted_iota(I32, (N_EXPERTS, n_tiles), 1)
    e_from = lax.broadcasted_iota(I32, (N_EXPERTS, N_EXPERTS), 1)
    e_to = lax.broadcasted_iota(I32, (N_EXPERTS, N_EXPERTS), 0)
    earlier_e = jnp.where(e_from < e_to, 1.0, 0.0)

    @pl.when((phase == 0) & (i == 0))
    def _():
        cnt_scr[...] = jnp.zeros_like(cnt_scr)

    @pl.when(phase == 0)
    def _():
        work = lg_ref[...]
        vals, hots = [], []
        for s in range(TOP_K):
            m = jnp.max(work, axis=0, keepdims=True)
            ix = jnp.min(jnp.where(work == m, eio, N_EXPERTS), axis=0, keepdims=True)
            hot = eio == ix
            idx_scr[s:s + 1, pl.ds(tok0, tm)] = ix
            vals.append(m)
            hots.append(hot)
            work = jnp.where(hot, -jnp.inf, work)
        es = [jnp.exp(v - vals[0]) for v in vals]
        denom = es[0] + es[1] + es[2] + es[3]
        for s in range(TOP_K):
            wt_ref[s:s + 1, :] = es[s] / denom
        multi_f = jnp.where(hots[0] | hots[1] | hots[2] | hots[3], 1.0, 0.0)
        t_from = lax.broadcasted_iota(I32, (tm, tm), 0)
        t_to = lax.broadcasted_iota(I32, (tm, tm), 1)
        before = jnp.where(t_from < t_to, 1.0, 0.0).astype(BF16)
        count = _dot(multi_f.astype(BF16), before)
        for s in range(TOP_K):
            pos_scr[s:s + 1, pl.ds(tok0, tm)] = jnp.sum(
                jnp.where(hots[s], count, 0.0), axis=0, keepdims=True).astype(I32)
        cnt8 = _round_up_f32(jnp.sum(multi_f, axis=1, keepdims=True), PIECE_ALIGN)
        cnt_scr[...] = cnt_scr[...] + jnp.where(tile_lane == i, cnt8, 0.0)

    @pl.when((phase == 1) & (i == 0))
    def _():
        cnt8 = cnt_scr[...]
        tot = _round_up_f32(jnp.sum(cnt8, axis=1, keepdims=True), ROW_BLOCK)
        p_start = _dot_hi(earlier_e, jnp.broadcast_to(tot, (N_EXPERTS, n_tiles)))
        i_from = lax.broadcasted_iota(I32, (n_tiles, n_tiles), 0)
        i_to = lax.broadcasted_iota(I32, (n_tiles, n_tiles), 1)
        earlier_tiles = _dot_hi(cnt8, jnp.where(i_from < i_to, 1.0, 0.0))
        off_scr[...] = p_start + earlier_tiles
        off_ref[...] = off_scr[...].astype(I32)
        cnt_ref[...] = cnt8.astype(I32)
        p_end = p_start[:, 0:1] + tot
        lanes = map_ref.shape[1]
        blk_start = lax.broadcasted_iota(I32, (N_EXPERTS, lanes), 1).astype(F32) * ROW_BLOCK
        blk_e = jnp.sum(jnp.where(blk_start >= p_end, 1.0, 0.0), axis=0, keepdims=True)
        blk_e = jnp.minimum(blk_e, N_EXPERTS - 1.0)
        n_used = jnp.max(p_end, axis=0, keepdims=True) * (1.0 / ROW_BLOCK)
        last = lax.broadcasted_iota(I32, (1, lanes), 1) == lanes - 1
        map_ref[...] = jnp.where(last, n_used, blk_e).astype(I32)

    @pl.when(phase == 1)
    def _():
        cnt_col = jnp.sum(jnp.where(tile_lane == i, cnt_scr[...], 0.0), axis=1, keepdims=True)
        local_off = _dot_hi(earlier_e, jnp.broadcast_to(cnt_col, (N_EXPERTS, tm)))
        for s in range(TOP_K):
            hot = eio == idx_scr[s:s + 1, pl.ds(tok0, tm)]
            base = jnp.sum(jnp.where(hot, local_off, 0.0), axis=0, keepdims=True).astype(I32)
            slot_ref[s:s + 1, :] = base + pos_scr[s:s + 1, pl.ds(tok0, tm)]


def _route(logits_t):
    n_tok = logits_t.shape[1]
    n_tiles, _, _, map_lanes = _moe_dims(n_tok)
    tm = MOE_TM
    last = n_tiles - 1
    const = lambda p, i: (0, 0)
    return pl.pallas_call(
        _route_kernel,
        grid=(2, n_tiles),
        in_specs=[pl.BlockSpec((N_EXPERTS, tm), lambda p, i: (0, i * (1 - p) + last * p))],
        out_specs=[
            pl.BlockSpec((TOP_K, tm), lambda p, i: (0, i * (1 - p) + last * p)),
            pl.BlockSpec((TOP_K, tm), lambda p, i: (0, i * p)),
            pl.BlockSpec((N_EXPERTS, n_tiles), const),
            pl.BlockSpec((N_EXPERTS, n_tiles), const),
            pl.BlockSpec((1, map_lanes), const),
        ],
        out_shape=[
            jax.ShapeDtypeStruct((TOP_K, n_tok), F32),
            jax.ShapeDtypeStruct((TOP_K, n_tok), I32),
            jax.ShapeDtypeStruct((N_EXPERTS, n_tiles), I32),
            jax.ShapeDtypeStruct((N_EXPERTS, n_tiles), I32),
            jax.ShapeDtypeStruct((1, map_lanes), I32),
        ],
        scratch_shapes=[
            pltpu.VMEM((TOP_K, n_tok), I32),
            pltpu.VMEM((TOP_K, n_tok), I32),
            pltpu.VMEM((N_EXPERTS, n_tiles), F32),
            pltpu.VMEM((N_EXPERTS, n_tiles), F32),
        ],
        compiler_params=_cparams(("arbitrary", "arbitrary")),
        name="route",
    )(logits_t)


def _piece_copies(off_ref, cnt_ref, tile, make_copy, action):
    def per_expert(e, stage_row):
        c8 = cnt_ref[tile * N_EXPERTS + e]
        hbm_row = off_ref[tile * N_EXPERTS + e]
        done = 0
        for size in PIECE_SIZES:
            bit = c8 & size

            @pl.when(bit != 0)
            def _(done=done, size=size):
                action(make_copy(pl.multiple_of(stage_row + done, PIECE_ALIGN),
                                 pl.multiple_of(hbm_row + done, PIECE_ALIGN), size))

            done = done + bit
        return stage_row + c8

    lax.fori_loop(0, N_EXPERTS, per_expert, 0)


def _slot_rows(slot_ref):
    return [slot_ref[s:s + 1, :] for s in range(TOP_K)]


def _one_hot_rows(slots, r0):
    rio = lax.broadcasted_iota(I32, (STAGE_CHUNK, MOE_TM), 0) + r0
    pick = jnp.zeros((STAGE_CHUNK, MOE_TM), F32)
    for s in range(TOP_K):
        pick = jnp.where(rio == slots[s], 1.0, pick)
    return pick.astype(BF16)


def _dispatch_kernel(off_ref, cnt_ref, slot_ref, wt_ref, x1_ref, xs_in_hbm, xs_hbm, stage, sem):
    del xs_in_hbm
    i = pl.program_id(0)
    n_tiles = pl.num_programs(0)
    buf = i % 2

    def out_copies(tile, b, action):
        def make_copy(stage_row, hbm_row, size):
            return pltpu.make_async_copy(stage.at[b, pl.ds(stage_row, size)], xs_hbm.at[pl.ds(hbm_row, size)],
                                         sem.at[b])
        _piece_copies(off_ref, cnt_ref, tile, make_copy, action)

    @pl.when(i >= 2)
    def _():
        out_copies(i - 2, buf, lambda cp: cp.wait())

    xb = x1_ref[...].astype(BF16)
    slots = _slot_rows(slot_ref)
    tail_lane = lax.broadcasted_iota(I32, (STAGE_CHUNK, LANE), 1)
    for r0 in range(0, stage.shape[1], STAGE_CHUNK):
        stage[buf, r0:r0 + STAGE_CHUNK, 0:D_MODEL] = _dot(_one_hot_rows(slots, r0), xb).astype(BF16)
        rio = lax.broadcasted_iota(I32, (STAGE_CHUNK, MOE_TM), 0) + r0
        w_sel = jnp.zeros((STAGE_CHUNK, MOE_TM), F32)
        for s in range(TOP_K):
            w_sel = jnp.where(rio == slots[s], wt_ref[s:s + 1, :], w_sel)
        w_row = jnp.sum(w_sel, axis=1, keepdims=True)
        w_a = w_row.astype(BF16).astype(F32)
        tail = jnp.where(tail_lane == 0, w_a, jnp.where(tail_lane == 1, w_row - w_a, 0.0))
        stage[buf, r0:r0 + STAGE_CHUNK, D_MODEL:ROW_W] = tail.astype(BF16)
    out_copies(i, buf, lambda cp: cp.start())

    @pl.when(i == n_tiles - 1)
    def _():
        @pl.when(i >= 1)
        def _():
            out_copies(i - 1, 1 - buf, lambda cp: cp.wait())

        out_copies(i, buf, lambda cp: cp.wait())


def _dispatch(off_flat, cnt_flat, slot, wts, x1):
    n_tok = x1.shape[0]
    n_tiles, stage_rows, n_blk, _ = _moe_dims(n_tok)
    n_rows = n_blk * ROW_BLOCK
    any_spec = pl.BlockSpec(memory_space=pl.ANY)
    return pl.pallas_call(
        _dispatch_kernel,
        grid_spec=pltpu.PrefetchScalarGridSpec(
            num_scalar_prefetch=2,
            grid=(n_tiles,),
            in_specs=[pl.BlockSpec((TOP_K, MOE_TM), lambda i, off, cnt: (0, i)),
                      pl.BlockSpec((TOP_K, MOE_TM), lambda i, off, cnt: (0, i)),
                      pl.BlockSpec((MOE_TM, D_MODEL), lambda i, off, cnt: (i, 0)),
                      any_spec],
            out_specs=any_spec,
            scratch_shapes=[pltpu.VMEM((2, stage_rows, ROW_W), BF16), pltpu.SemaphoreType.DMA((2,))],
        ),
        out_shape=jax.ShapeDtypeStruct((n_rows, ROW_W), BF16),
        input_output_aliases={5: 0},
        compiler_params=_cparams(("arbitrary",)),
        name="dispatch",
    )(off_flat, cnt_flat, slot, wts, x1, jnp.zeros((n_rows, ROW_W), BF16))


def _combine_kernel(off_ref, cnt_ref, slot_ref, x1_ref, lng_ref, lnb_ref, ys_hbm, out_ref, stage, sem):
    i = pl.program_id(0)
    n_tiles = pl.num_programs(0)
    buf = i % 2

    def in_copies(tile, b, action):
        def make_copy(stage_row, hbm_row, size):
            return pltpu.make_async_copy(ys_hbm.at[pl.ds(hbm_row, size)], stage.at[b, pl.ds(stage_row, size)],
                                         sem.at[b])
        _piece_copies(off_ref, cnt_ref, tile, make_copy, action)

    @pl.when(i == 0)
    def _():
        stage[...] = jnp.zeros_like(stage)
        in_copies(0, 0, lambda cp: cp.start())

    @pl.when(i + 1 < n_tiles)
    def _():
        in_copies(i + 1, 1 - buf, lambda cp: cp.start())

    in_copies(i, buf, lambda cp: cp.wait())

    slots = _slot_rows(slot_ref)
    ffn = jnp.zeros((MOE_TM, D_MODEL), F32)
    for r0 in range(0, stage.shape[1], STAGE_CHUNK):
        ffn = ffn + _dot_tn(_one_hot_rows(slots, r0), stage[buf, r0:r0 + STAGE_CHUNK, :])
    out_ref[...] = _layer_norm(DEEPNORM_ALPHA * x1_ref[...] + ffn, lng_ref[...], lnb_ref[...])


def _combine(off_flat, cnt_flat, slot, x1, ln_g, ln_b, ys):
    n_tok = x1.shape[0]
    n_tiles, stage_rows, _, _ = _moe_dims(n_tok)
    tok = lambda i, off, cnt: (0, i)
    row = lambda i, off, cnt: (i, 0)
    full = lambda shape: pl.BlockSpec(shape, lambda i, off, cnt: (0, 0))
    return pl.pallas_call(
        _combine_kernel,
        grid_spec=pltpu.PrefetchScalarGridSpec(
            num_scalar_prefetch=2,
            grid=(n_tiles,),
            in_specs=[pl.BlockSpec((TOP_K, MOE_TM), tok),
                      pl.BlockSpec((MOE_TM, D_MODEL), row), full((1, D_MODEL)), full((1, D_MODEL)),
                      pl.BlockSpec(memory_space=pl.ANY)],
            out_specs=pl.BlockSpec((MOE_TM, D_MODEL), row),
            scratch_shapes=[pltpu.VMEM((2, stage_rows, D_MODEL), BF16), pltpu.SemaphoreType.DMA((2,))],
        ),
        out_shape=jax.ShapeDtypeStruct((n_tok, D_MODEL), F32),
        compiler_params=_cparams(("arbitrary",)),
        name="combine",
    )(off_flat, cnt_flat, slot, x1, ln_g.reshape(1, -1), ln_b.reshape(1, -1), ys)


CAST_ROWS = 128
FFN_CHUNKS = 4


def _expert_kernel(blk_e_ref, n_used_ref, xs_ref, win_ref, bin_ref, wout_ref, bout_ref, ys_ref, win_bf, wout_bf):
    rb = pl.program_id(0)
    new_expert = (rb == 0) | (blk_e_ref[rb] != blk_e_ref[jnp.maximum(rb - 1, 0)])

    @pl.when((rb < n_used_ref[0]) & new_expert)
    def _():
        for r in range(0, D_MODEL, CAST_ROWS):
            win_bf[r:r + CAST_ROWS, :] = win_ref[0, r:r + CAST_ROWS, :].astype(BF16)
        for r in range(0, D_EXPERT, CAST_ROWS):
            wout_bf[r:r + CAST_ROWS, :] = wout_ref[0, r:r + CAST_ROWS, :].astype(BF16)

    @pl.when(rb < n_used_ref[0])
    def _():
        x = xs_ref[:, 0:D_MODEL]
        w_tail = xs_ref[:, D_MODEL:ROW_W].astype(F32)
        w_row = w_tail[:, 0:1] + w_tail[:, 1:2]
        cw = D_EXPERT // FFN_CHUNKS

        def hidden(j):
            gs = slice(j * cw, (j + 1) * cw)
            ls = slice(D_EXPERT + j * cw, D_EXPERT + (j + 1) * cw)
            return _dot(x, win_bf[:, gs]) + bin_ref[0, :, gs], _dot(x, win_bf[:, ls]) + bin_ref[0, :, ls]

        ahead = hidden(0)
        y = bout_ref[0]
        for j in range(FFN_CHUNKS):
            g, lin = ahead
            if j + 1 < FFN_CHUNKS:
                ahead = hidden(j + 1)
            gate_h = jnp.minimum(g, SWIGLU_LIMIT)
            lin_h = jnp.clip(lin, -SWIGLU_LIMIT, SWIGLU_LIMIT)
            act = gate_h * jax.nn.sigmoid(SWIGLU_ALPHA * gate_h) * (lin_h + 1.0)
            y = y + _dot(act.astype(BF16), wout_bf[j * cw:(j + 1) * cw, :])
        ys_ref[...] = (y * w_row).astype(BF16)

    @pl.when(rb >= n_used_ref[0])
    def _():
        ys_ref[...] = jnp.zeros_like(ys_ref)


def _experts(blk_e, n_used, xs, w_in, b_in, w_out, b_out):
    n_rows = xs.shape[0]
    n_blk = n_rows // ROW_BLOCK
    used = lambda rb, n_used: jnp.maximum(jnp.minimum(rb, n_used[0] - 1), 0)
    rows = lambda rb, blk_e, n_used: (used(rb, n_used), 0)
    per_e = lambda rb, blk_e, n_used: (blk_e[used(rb, n_used)], 0, 0)
    return pl.pallas_call(
        _expert_kernel,
        grid_spec=pltpu.PrefetchScalarGridSpec(
            num_scalar_prefetch=2,
            grid=(n_blk,),
            in_specs=[
                pl.BlockSpec((ROW_BLOCK, ROW_W), rows),
                pl.BlockSpec((1, D_MODEL, 2 * D_EXPERT), per_e),
                pl.BlockSpec((1, 1, 2 * D_EXPERT), per_e),
                pl.BlockSpec((1, D_EXPERT, D_MODEL), per_e),
                pl.BlockSpec((1, 1, D_MODEL), per_e),
            ],
            out_specs=pl.BlockSpec((ROW_BLOCK, D_MODEL), lambda rb, blk_e, n_used: (rb, 0)),
            scratch_shapes=[pltpu.VMEM((D_MODEL, 2 * D_EXPERT), BF16), pltpu.VMEM((D_EXPERT, D_MODEL), BF16)],
        ),
        out_shape=jax.ShapeDtypeStruct((n_rows, D_MODEL), BF16),
        compiler_params=_cparams(("arbitrary",)),
        name="experts",
    )(blk_e, n_used, xs, w_in, b_in.reshape(N_EXPERTS, 1, -1), w_out, b_out.reshape(N_EXPERTS, 1, -1))


def _moe(x1, logits_t, expert_w_in, expert_b_in, expert_w_out, expert_b_out, ln_g, ln_b):
    n_tok = x1.shape[0]
    _, _, n_blk, map_lanes = _moe_dims(n_tok)
    wts, slot, off, cnt, blk_map = _route(logits_t)
    off_flat = off.T.reshape(-1)
    cnt_flat = cnt.T.reshape(-1)
    blk_e = blk_map[0, 0:n_blk]
    n_used = blk_map[0, map_lanes - 1:map_lanes]
    xs = _dispatch(off_flat, cnt_flat, slot, wts, x1)
    ys = _experts(blk_e, n_used, xs, expert_w_in, expert_b_in, expert_w_out, expert_b_out)
    return _combine(off_flat, cnt_flat, slot, x1, ln_g, ln_b, ys)


def kernel(x, ln1_g, ln1_b, ln2_g, ln2_b, w_in, shift_mix, decay_w0, decay_up, iclr_a0, iclr_up, gate_up, k_k, k_a, r_k, gn_g, gn_b, w_branch_rwkv, w_branch_attn, w_out, router_w, router_b, expert_w_in, expert_b_in, expert_w_out, expert_b_out):
    B, T, D = x.shape
    x2 = x.reshape(B * T, D)
    prw, q, k, v, gates = _proj(x2, w_in[0].astype(BF16), T)
    y_rwkv = _rwkv(prw, B, T, shift_mix[0], decay_w0[0], decay_up[0], iclr_a0[0], iclr_up[0], gate_up[0],
                   k_k[0], k_a[0], r_k[0], gn_g[0], gn_b[0])
    y_attn = _moba(q, k, v, B, T)
    x1, logits_t = _merge(x2, y_rwkv, y_attn, gates, w_branch_rwkv[0], w_branch_attn[0], w_out[0],
                          ln1_g[0], ln1_b[0], router_w[0], router_b[0])
    out = _moe(x1, logits_t, expert_w_in[0], expert_b_in[0], expert_w_out[0], expert_b_out[0], ln2_g[0], ln2_b[0])
    return out.reshape(B, T, D)
```

```python
import math

import jax
import jax.numpy as jnp
from jax import lax
from jax.experimental import pallas as pl
from jax.experimental.pallas import tpu as pltpu

F32 = jnp.float32
BF16 = jnp.bfloat16
I32 = jnp.int32
HI = lax.Precision.HIGHEST

D_MODEL = 1024
DEPTH = 1
RWKV_HEAD_DIM = 64
RWKV_DIM = 512
RWKV_HEADS = 8
DECAY_RANK = 64
ICLR_RANK = 64
GATE_RANK = 128
GN_EPS = 64e-5
ATTN_HEAD_DIM = 64
ATTN_DIM = 512
ATTN_HEADS = 8
MOBA_BLOCK = 256
MOBA_TOPK = 3
ROPE_THETA = 500000.0
ROPE_DIM = 16
NEG_INF = -1e30
N_EXPERTS = 32
TOP_K = 4
D_EXPERT = 1024
SWIGLU_LIMIT = 7.0
SWIGLU_ALPHA = 1.702
DEEPNORM_ALPHA = (2.0 * DEPTH) ** 0.25
LN_EPS = 1e-5
RWKV_COLS = 3 * RWKV_DIM + DECAY_RANK + ICLR_RANK + GATE_RANK
ATTN_COLS = 3 * ATTN_DIM
GATE_COLS = 2 * D_MODEL
IN_COLS = RWKV_COLS + ATTN_COLS + GATE_COLS

LANE = 128
SUBLANE = 8
VMEM_LIMIT_BYTES = 56 * 1024 * 1024

CHUNK = 64


def _cparams(sem):
    return pltpu.CompilerParams(dimension_semantics=sem, vmem_limit_bytes=VMEM_LIMIT_BYTES)


def _dot(a, b):
    return jnp.dot(a, b, preferred_element_type=F32)


def _dot_hi(a, b):
    return jnp.dot(a, b, preferred_element_type=F32, precision=HI)


def _dot_nt(a, b, precision=None):
    return lax.dot_general(a, b, (((1,), (1,)), ((), ())), preferred_element_type=F32, precision=precision)


def _dot_tn(a, b, precision=None):
    return lax.dot_general(a, b, (((0,), (0,)), ((), ())), preferred_element_type=F32, precision=precision)


def _bf16_pieces(x, n):
    pieces = []
    for _ in range(n):
        p = x.astype(BF16)
        pieces.append(p)
        x = x - p.astype(F32)
    return pieces


def _dot_exact_rhs(x, b_bf16, n):
    out = None
    for p in _bf16_pieces(x, n):
        d = _dot(p, b_bf16)
        out = d if out is None else out + d
    return out


PROJ_TM = 256
Q_SCALE = math.log2(math.e) * ATTN_HEAD_DIM ** -0.5


def _proj_kernel(x_ref, w_ref, cos_ref, sa_ref, sb_ref, prw_ref, q_ref, k_ref, v_ref, g_ref):
    xb = x_ref[...].astype(BF16)
    prw_ref[...] = _dot(xb, w_ref[:, 0:RWKV_COLS])
    c0 = RWKV_COLS
    cos = cos_ref[...]
    sa = sa_ref[...]
    sb = sb_ref[...]

    def rope(t):
        return t * cos + pltpu.roll(t, ATTN_DIM - ROPE_DIM // 2, 1) * sa + pltpu.roll(t, ROPE_DIM // 2, 1) * sb

    q_ref[...] = (rope(_dot(xb, w_ref[:, c0:c0 + ATTN_DIM])) * Q_SCALE).astype(BF16)
    k_ref[...] = rope(_dot(xb, w_ref[:, c0 + ATTN_DIM:c0 + 2 * ATTN_DIM])).astype(BF16)
    v_ref[...] = _dot(xb, w_ref[:, c0 + 2 * ATTN_DIM:c0 + 3 * ATTN_DIM]).astype(BF16)
    c1 = RWKV_COLS + ATTN_COLS
    g_ref[...] = jax.nn.sigmoid(_dot(xb, w_ref[:, c1:c1 + GATE_COLS])).astype(BF16)


def _rope_tables(T):
    half = ROPE_DIM // 2
    inv_freq = jnp.power(ROPE_THETA, -jnp.arange(0, ROPE_DIM, 2, dtype=F32) / ROPE_DIM)
    ang = jnp.arange(T).astype(F32)[:, None] * inv_freq[None, :]
    cos, sin = jnp.cos(ang), jnp.sin(ang)
    pad = jnp.zeros((T, ATTN_HEAD_DIM - ROPE_DIM), F32)
    cos_h = jnp.concatenate([cos, cos, pad + 1.0], axis=1)
    sa_h = jnp.concatenate([-sin, jnp.zeros((T, half), F32), pad], axis=1)
    sb_h = jnp.concatenate([jnp.zeros((T, half), F32), sin, pad], axis=1)
    tile = lambda t: jnp.tile(t, (1, ATTN_HEADS))
    return tile(cos_h), tile(sa_h), tile(sb_h)


def _proj(x2, w_in_bf, T):
    n_tok = x2.shape[0]
    tm = PROJ_TM
    t_tiles = T // tm
    cos, sa, sb = _rope_tables(T)
    row = lambda i: (i, 0)
    tab = lambda i: (i % t_tiles, 0)
    return pl.pallas_call(
        _proj_kernel,
        grid=(n_tok // tm,),
        in_specs=[
            pl.BlockSpec((tm, D_MODEL), row),
            pl.BlockSpec((D_MODEL, IN_COLS), lambda i: (0, 0), pipeline_mode=pl.Buffered(1)),
            pl.BlockSpec((tm, ATTN_DIM), tab),
            pl.BlockSpec((tm, ATTN_DIM), tab),
            pl.BlockSpec((tm, ATTN_DIM), tab),
        ],
        out_specs=[
            pl.BlockSpec((tm, RWKV_COLS), row),
            pl.BlockSpec((tm, ATTN_DIM), row),
            pl.BlockSpec((tm, ATTN_DIM), row),
            pl.BlockSpec((tm, ATTN_DIM), row),
            pl.BlockSpec((tm, GATE_COLS), row),
        ],
        out_shape=[
            jax.ShapeDtypeStruct((n_tok, RWKV_COLS), F32),
            jax.ShapeDtypeStruct((n_tok, ATTN_DIM), BF16),
            jax.ShapeDtypeStruct((n_tok, ATTN_DIM), BF16),
            jax.ShapeDtypeStruct((n_tok, ATTN_DIM), BF16),
            jax.ShapeDtypeStruct((n_tok, GATE_COLS), BF16),
        ],
        compiler_params=_cparams(("parallel",)),
        name="proj",
    )(x2, w_in_bf, cos, sa, sb)


RWKV_TT = 256


def _rwkv_kernel(p_ref, mix_ref, w0_ref, dup_ref, a0_ref, iup_ref, gup_ref, kk_ref, ka_ref, rk_ref,
                 gng_ref, gnb_ref, ones_ref, y_ref, s_scr, prev_scr):
    H, N, C = RWKV_HEADS, RWKV_HEAD_DIM, CHUNK

    @pl.when(pl.program_id(1) == 0)
    def _():
        s_scr[...] = jnp.zeros_like(s_scr)
        prev_scr[...] = jnp.zeros_like(prev_scr)

    TT = RWKV_TT
    n_chunks = TT // C
    bones = ones_ref[...]
    row = lax.broadcasted_iota(I32, (C, C), 0)
    col = lax.broadcasted_iota(I32, (C, C), 1)
    lower_incl = col <= row
    lower_strict = col < row
    eye = jnp.where(col == row, 1.0, 0.0)
    t_row = lax.broadcasted_iota(I32, (TT, TT), 0)
    t_col = lax.broadcasted_iota(I32, (TT, TT), 1)
    same_chunk = (t_row // C) == (t_col // C)
    chunk_ltri = jnp.where(same_chunk & (t_col <= t_row), 1.0, 0.0).astype(BF16)
    chunk_ones = jnp.where(same_chunk, 1.0, 0.0).astype(BF16)

    p = p_ref[...]
    first_row = lax.broadcasted_iota(I32, (TT, RWKV_COLS), 0) == 0
    prev = jnp.where(first_row, prev_scr[...], pltpu.roll(p, 1, 0))
    prev_scr[...] = p[TT - 1:TT, :]
    ps = p + (prev - p) * mix_ref[...]
    r = ps[:, 0:RWKV_DIM]
    k = ps[:, RWKV_DIM:2 * RWKV_DIM]
    v = ps[:, 2 * RWKV_DIM:3 * RWKV_DIM]
    o = 3 * RWKV_DIM
    xw = ps[:, o:o + DECAY_RANK]
    xa = ps[:, o + DECAY_RANK:o + DECAY_RANK + ICLR_RANK]
    xg = ps[:, o + DECAY_RANK + ICLR_RANK:RWKV_COLS]
    w_raw = w0_ref[...] + _dot(jnp.tanh(xw).astype(BF16), dup_ref[...])
    logw = -math.exp(-0.5) * jax.nn.sigmoid(w_raw)
    a = jax.nn.sigmoid(a0_ref[...] + _dot(xa.astype(BF16), iup_ref[...]))
    g = _dot(jax.nn.sigmoid(xg).astype(BF16), gup_ref[...])
    kk0 = k * kk_ref[...]
    kk = kk0 / jnp.maximum(jnp.sqrt(_dot_exact_rhs(kk0 * kk0, bones, 2)), 1e-12)
    kp = k * (1.0 + (a - 1.0) * ka_ref[...])
    kka = kk * a
    logw_pieces = _bf16_pieces(logw, 3)
    cum = sum(_dot(chunk_ltri, piece) for piece in logw_pieces)
    tot = sum(_dot(chunk_ones, piece) for piece in logw_pieces)
    e_neg = jnp.exp(-cum)
    at = (-kk * jnp.exp(cum - logw)).astype(BF16)
    rt = (r * jnp.exp(cum)).astype(BF16)
    bt = (kka * e_neg).astype(BF16)
    kt = (kp * e_neg).astype(BF16)
    e_end = jnp.exp(tot - cum)
    bh = (kka * e_end).astype(BF16)
    kh = (kp * e_end).astype(BF16)
    e_tot = jnp.exp(tot)
    vb = v.astype(BF16)

    units = [(c, h) for c in range(n_chunks) for h in range(H)]
    blk = lambda t, c, h: t[c * C:(c + 1) * C, h * N:(h + 1) * N]
    lhs = {u: jnp.concatenate([blk(at, *u), blk(rt, *u)], axis=0) for u in units}
    rhs = {u: jnp.concatenate([blk(bt, *u), blk(kt, *u)], axis=0) for u in units}
    aa = {u: _dot_nt(lhs[u], rhs[u]) for u in units}
    a_ab = {u: jnp.where(lower_strict, aa[u][0:C, 0:C], 0.0) for u in units}
    a_kv = {u: jnp.concatenate([jnp.where(lower_strict, aa[u][0:C, C:2 * C], 0.0),
                                jnp.where(lower_incl, aa[u][C:2 * C, C:2 * C], 0.0)], axis=0).astype(BF16)
            for u in units}
    a_rb = {u: jnp.where(lower_incl, aa[u][C:2 * C, 0:C], 0.0).astype(BF16) for u in units}
    akv = {u: _dot(a_kv[u], blk(vb, *u)) for u in units}
    tinv = {u: eye + a_ab[u] for u in units}
    npow = a_ab
    for _ in range(5):
        npb = {u: npow[u].astype(BF16) for u in units}
        npow = {u: _dot(npb[u], npb[u]) for u in units}
        tinv = {u: tinv[u] + _dot(tinv[u].astype(BF16), npow[u].astype(BF16)) for u in units}
    tinv_b = {u: tinv[u].astype(BF16) for u in units}

    state = [s_scr[h] for h in range(H)]
    y_rows = []
    for c in range(n_chunks):
        hs = [(c, h) for h in range(H)]
        ar_s = {u: _dot_nt(lhs[u], state[u[1]].astype(BF16)) for u in hs}
        ub = {u: _dot(tinv_b[u], (ar_s[u][0:C] + akv[u][0:C]).astype(BF16)).astype(BF16) for u in hs}
        ys = [ar_s[u][C:2 * C] + akv[u][C:2 * C] + _dot(a_rb[u], ub[u]) for u in hs]
        upd = {u: _dot_tn(jnp.concatenate([ub[u], blk(vb, *u)], axis=0),
                          jnp.concatenate([blk(bh, *u), blk(kh, *u)], axis=0)) for u in hs}
        state = [state[h] * e_tot[c * C:c * C + 1, h * N:(h + 1) * N] + upd[(c, h)] for h in range(H)]
        y_rows.append(jnp.concatenate(ys, axis=1))
    for h in range(H):
        s_scr[h] = state[h]

    y = jnp.concatenate(y_rows, axis=0)
    mu = _dot_exact_rhs(y, bones, 1) * (1.0 / N)
    yc = y - mu
    var = _dot_exact_rhs(yc * yc, bones, 1) * (1.0 / N)
    yn = yc * lax.rsqrt(var + GN_EPS) * gng_ref[...] + gnb_ref[...]
    bonus = _dot_exact_rhs(r * kp * rk_ref[...], bones, 1) * v
    y_ref[...] = ((yn + bonus) * g).astype(BF16)


def _rwkv(p_rwkv, B, T, shift_mix, decay_w0, decay_up, iclr_a0, iclr_up, gate_up, k_k, k_a, r_k, gn_g, gn_b):
    tt = RWKV_TT
    n_t = T // tt
    head = jnp.arange(RWKV_DIM) // RWKV_HEAD_DIM
    bones = (head[:, None] == head[None, :]).astype(BF16)
    vec = lambda a: a.reshape(1, -1)
    full = lambda shape: pl.BlockSpec(shape, lambda b, j: (0, 0))
    return pl.pallas_call(
        _rwkv_kernel,
        grid=(B, n_t),
        in_specs=[
            pl.BlockSpec((tt, RWKV_COLS), lambda b, j: (b * n_t + j, 0)),
            full((1, RWKV_COLS)), full((1, RWKV_DIM)), full((DECAY_RANK, RWKV_DIM)),
            full((1, RWKV_DIM)), full((ICLR_RANK, RWKV_DIM)), full((GATE_RANK, RWKV_DIM)),
            full((1, RWKV_DIM)), full((1, RWKV_DIM)), full((1, RWKV_DIM)),
            full((1, RWKV_DIM)), full((1, RWKV_DIM)), full((RWKV_DIM, RWKV_DIM)),
        ],
        out_specs=pl.BlockSpec((tt, RWKV_DIM), lambda b, j: (b * n_t + j, 0)),
        out_shape=jax.ShapeDtypeStruct((B * T, RWKV_DIM), BF16),
        scratch_shapes=[
            pltpu.VMEM((RWKV_HEADS, RWKV_HEAD_DIM, RWKV_HEAD_DIM), F32),
            pltpu.VMEM((1, RWKV_COLS), F32),
        ],
        compiler_params=_cparams(("parallel", "arbitrary")),
        name="rwkv",
    )(p_rwkv, vec(shift_mix), vec(decay_w0), decay_up.astype(BF16), vec(iclr_a0), iclr_up.astype(BF16),
      gate_up.astype(BF16), vec(k_k), vec(k_a), vec(r_k), vec(gn_g), vec(gn_b), bones)


MOBA_HP = 4


def _moba_kernel(q_ref, k_ref, v_ref, o_ref, kmean_scr, sel_scr, acc_scr, score_scr):
    blk_sz, dh = MOBA_BLOCK, ATTN_HEAD_DIM
    nb = k_ref.shape[0] // blk_sz
    i = pl.program_id(2)

    @pl.when(i == 0)
    def _():
        for n in range(nb):
            kb = k_ref[n * blk_sz:(n + 1) * blk_sz, :].astype(F32)
            kmean_scr[n:n + 1, :] = jnp.sum(kb, axis=0, keepdims=True) * (1.0 / blk_sz)

    blk = lax.broadcasted_iota(I32, (nb, blk_sz), 0)
    kpos = lax.broadcasted_iota(I32, (blk_sz, blk_sz), 0)
    qpos = lax.broadcasted_iota(I32, (blk_sz, blk_sz), 1)
    own_start = pl.multiple_of(i * blk_sz, blk_sz)
    heads = [slice(hh * dh, (hh + 1) * dh) for hh in range(MOBA_HP)]
    hds = range(MOBA_HP)
    qss = [q_ref[:, hs] for hs in heads]
    kmean_pieces = _bf16_pieces(kmean_scr[...], 3)
    gates = [sum(_dot_nt(piece[:, heads[hh]], qss[hh]) for piece in kmean_pieces) for hh in hds]
    own = [_dot_nt(k_ref[pl.ds(own_start, blk_sz), heads[hh]], qss[hh]) for hh in hds]
    for hh in hds:
        gate = jnp.where(blk < i, gates[hh], NEG_INF)
        rank = jnp.zeros((nb, blk_sz), I32)
        for m in range(nb):
            gm = gate[m:m + 1, :]
            beats = (gm > gate) | ((gm == gate) & (m < blk))
            rank = rank + beats.astype(I32)
        sel_scr[hh] = ((rank < MOBA_TOPK) & (blk < i)).astype(F32)
    stats, ps = [], []
    for hh in hds:
        s = jnp.where(kpos <= qpos, own[hh], NEG_INF)
        m0 = jnp.max(s, axis=0, keepdims=True)
        p = jnp.exp2(s - m0)
        stats += [m0, jnp.sum(p, axis=0, keepdims=True)]
        ps.append(p.astype(BF16))
    pvs = [_dot_tn(v_ref[pl.ds(own_start, blk_sz), heads[hh]], ps[hh]) for hh in hds]
    for hh in hds:
        acc_scr[hh] = pvs[hh]

    def scores(n):
        start = pl.multiple_of(n * blk_sz, blk_sz)
        return [_dot_nt(k_ref[pl.ds(start, blk_sz), heads[hh]], qss[hh]) for hh in hds]

    first = scores(0)
    for hh in hds:
        score_scr[hh] = first[hh]

    def body(n, carry):
        ahead = scores(jnp.minimum(n + 1, i - 1))
        start = pl.multiple_of(n * blk_sz, blk_sz)
        out, ps, alphas = [], [], []
        for hh in hds:
            m_run, l_run = carry[2 * hh], carry[2 * hh + 1]
            s = jnp.where(sel_scr[hh, pl.ds(n, 1), :] > 0.0, score_scr[hh], NEG_INF)
            m_new = jnp.maximum(m_run, jnp.max(s, axis=0, keepdims=True))
            alpha = jnp.exp2(m_run - m_new)
            p = jnp.exp2(s - m_new)
            out += [m_new, alpha * l_run + jnp.sum(p, axis=0, keepdims=True)]
            ps.append(p.astype(BF16))
            alphas.append(alpha)
        pvs = [_dot_tn(v_ref[pl.ds(start, blk_sz), heads[hh]], ps[hh]) for hh in hds]
        for hh in hds:
            acc_scr[hh] = alphas[hh] * acc_scr[hh] + pvs[hh]
            score_scr[hh] = ahead[hh]
        return tuple(out)

    stats = lax.fori_loop(0, i, body, tuple(stats))
    outs = [(acc_scr[hh] / stats[2 * hh + 1]).T for hh in range(MOBA_HP)]
    o_ref[...] = jnp.concatenate(outs, axis=1).astype(BF16)


def _moba(q, k, v, B, T):
    blk_sz = MOBA_BLOCK
    nq = T // blk_sz
    lanes = MOBA_HP * ATTN_HEAD_DIM
    kv_spec = pl.BlockSpec((T, lanes), lambda b, hp, i: (b, hp))
    q_spec = pl.BlockSpec((blk_sz, lanes), lambda b, hp, i: (b * nq + i, hp))
    return pl.pallas_call(
        _moba_kernel,
        grid=(B, ATTN_HEADS // MOBA_HP, nq),
        in_specs=[q_spec, kv_spec, kv_spec],
        out_specs=q_spec,
        out_shape=jax.ShapeDtypeStruct((B * T, ATTN_DIM), BF16),
        scratch_shapes=[
            pltpu.VMEM((T // blk_sz, lanes), F32),
            pltpu.VMEM((MOBA_HP, T // blk_sz, blk_sz), F32),
            pltpu.VMEM((MOBA_HP, ATTN_HEAD_DIM, blk_sz), F32),
            pltpu.VMEM((MOBA_HP, blk_sz, blk_sz), F32),
        ],
        compiler_params=_cparams(("parallel", "parallel", "arbitrary")),
        name="moba",
    )(q, k, v)


MERGE_TM = 512
MERGE_SUB = 128


def _layer_norm(h, g, b):
    mu = jnp.mean(h, axis=-1, keepdims=True)
    hc = h - mu
    var = jnp.mean(hc * hc, axis=-1, keepdims=True)
    return hc * lax.rsqrt(var + LN_EPS) * g + b


def _merge_kernel(x_ref, yr_ref, ya_ref, g_ref, wbr_ref, wba_ref, wo_ref, lng_ref, lnb_ref, rwt_ref, rb_ref,
                  x1_ref, lgt_ref):
    subs = [slice(s * MERGE_SUB, (s + 1) * MERGE_SUB) for s in range(MERGE_TM // MERGE_SUB)]
    yr = [_dot(yr_ref[sl, :], wbr_ref[...]) for sl in subs]
    ya = [_dot(ya_ref[sl, :], wba_ref[...]) for sl in subs]
    merged = [(g_ref[sl, 0:D_MODEL].astype(F32) * yr[s] + g_ref[sl, D_MODEL:2 * D_MODEL].astype(F32) * ya[s])
              .astype(BF16) for s, sl in enumerate(subs)]
    mix = [_dot(m, wo_ref[...]) for m in merged]
    x1 = [_layer_norm(DEEPNORM_ALPHA * x_ref[sl, :] + mix[s], lng_ref[...], lnb_ref[...])
          for s, sl in enumerate(subs)]
    for s, sl in enumerate(subs):
        x1_ref[sl, :] = x1[s]
    for s, sl in enumerate(subs):
        lgt_ref[:, sl] = _dot_nt(rwt_ref[...], x1[s], precision=HI) + rb_ref[...]


def _merge(x2, y_rwkv, y_attn, gates, w_br, w_ba, w_o, ln_g, ln_b, router_w, router_b):
    n_tok = x2.shape[0]
    tm = MERGE_TM
    row = lambda i: (i, 0)
    full = lambda shape: pl.BlockSpec(shape, lambda i: (0, 0))
    return pl.pallas_call(
        _merge_kernel,
        grid=(n_tok // tm,),
        in_specs=[
            pl.BlockSpec((tm, D_MODEL), row), pl.BlockSpec((tm, RWKV_DIM), row), pl.BlockSpec((tm, ATTN_DIM), row),
            pl.BlockSpec((tm, GATE_COLS), row),
            full((RWKV_DIM, D_MODEL)), full((ATTN_DIM, D_MODEL)), full((D_MODEL, D_MODEL)),
            full((1, D_MODEL)), full((1, D_MODEL)), full((N_EXPERTS, D_MODEL)), full((N_EXPERTS, 1)),
        ],
        out_specs=[pl.BlockSpec((tm, D_MODEL), row), pl.BlockSpec((N_EXPERTS, tm), lambda i: (0, i))],
        out_shape=[jax.ShapeDtypeStruct((n_tok, D_MODEL), F32), jax.ShapeDtypeStruct((N_EXPERTS, n_tok), F32)],
        compiler_params=_cparams(("parallel",)),
        name="merge",
    )(x2, y_rwkv, y_attn, gates, w_br.astype(BF16), w_ba.astype(BF16), w_o.astype(BF16),
      ln_g.reshape(1, -1), ln_b.reshape(1, -1), router_w.T, router_b.reshape(-1, 1))


MOE_TM = 512
ROW_BLOCK = 256
PIECE_ALIGN = 2 * SUBLANE
PIECE_SIZES = (512, 256, 128, 64, 32, 16)
TAIL_SIZES = (128, 64, 32, 16)
STAGE_CHUNK = 256
ROW_W = D_MODEL + LANE


def _moe_dims(n_tok):
    n_tiles = n_tok // MOE_TM
    stage_rows = -(-(MOE_TM * TOP_K + N_EXPERTS * (PIECE_ALIGN - 1)) // STAGE_CHUNK) * STAGE_CHUNK
    max_rows = n_tok * TOP_K + n_tiles * N_EXPERTS * (PIECE_ALIGN - 1) + N_EXPERTS * (ROW_BLOCK - 1)
    n_blk = -(-max_rows // ROW_BLOCK)
    map_lanes = -(-(n_blk + 1) // LANE) * LANE
    return n_tiles, stage_rows, n_blk, map_lanes


def _round_up_f32(x, m):
    return jnp.floor((x + (m - 1)) * (1.0 / m)) * m


def _route_kernel(lg_ref, wt_ref, slot_ref, off_ref, cnt_ref, end_ref, map_ref, idx_scr, pos_scr, cnt_scr, off_scr):
    tm = MOE_TM
    n_tiles = cnt_scr.shape[1]
    phase = pl.program_id(0)
    i = pl.program_id(1)
    tok0 = pl.multiple_of(i * tm, tm)
    eio = lax.broadcasted_iota(I32, (N_EXPERTS, tm), 0)
    tile_lane = lax.broadcasted_iota(I32, (N_EXPERTS, n_tiles), 1)
    e_from = lax.broadcasted_iota(I32, (N_EXPERTS, N_EXPERTS), 1)
    e_to = lax.broadcasted_iota(I32, (N_EXPERTS, N_EXPERTS), 0)
    earlier_e = jnp.where(e_from < e_to, 1.0, 0.0)

    @pl.when((phase == 0) & (i == 0))
    def _():
        cnt_scr[...] = jnp.zeros_like(cnt_scr)

    @pl.when(phase == 0)
    def _():
        work = lg_ref[...]
        vals, hots = [], []
        for s in range(TOP_K):
            m = jnp.max(work, axis=0, keepdims=True)
            ix = jnp.min(jnp.where(work == m, eio, N_EXPERTS), axis=0, keepdims=True)
            hot = eio == ix
            idx_scr[s:s + 1, pl.ds(tok0, tm)] = ix
            vals.append(m)
            hots.append(hot)
            work = jnp.where(hot, -jnp.inf, work)
        es = [jnp.exp(v - vals[0]) for v in vals]
        denom = es[0] + es[1] + es[2] + es[3]
        for s in range(TOP_K):
            wt_ref[s:s + 1, :] = es[s] / denom
        multi_f = jnp.where(hots[0] | hots[1] | hots[2] | hots[3], 1.0, 0.0)
        t_from = lax.broadcasted_iota(I32, (tm, tm), 0)
        t_to = lax.broadcasted_iota(I32, (tm, tm), 1)
        before = jnp.where(t_from < t_to, 1.0, 0.0).astype(BF16)
        count = _dot(multi_f.astype(BF16), before)
        for s in range(TOP_K):
            pos_scr[s:s + 1, pl.ds(tok0, tm)] = jnp.sum(
                jnp.where(hots[s], count, 0.0), axis=0, keepdims=True).astype(I32)
        cnt8 = _round_up_f32(jnp.sum(multi_f, axis=1, keepdims=True), PIECE_ALIGN)
        cnt_scr[...] = cnt_scr[...] + jnp.where(tile_lane == i, cnt8, 0.0)

    @pl.when((phase == 1) & (i == 0))
    def _():
        cnt8 = cnt_scr[...]
        tot = _round_up_f32(jnp.sum(cnt8, axis=1, keepdims=True), ROW_BLOCK)
        p_start = _dot_hi(earlier_e, jnp.broadcast_to(tot, (N_EXPERTS, n_tiles)))
        i_from = lax.broadcasted_iota(I32, (n_tiles, n_tiles), 0)
        i_to = lax.broadcasted_iota(I32, (n_tiles, n_tiles), 1)
        earlier_tiles = _dot_hi(cnt8, jnp.where(i_from < i_to, 1.0, 0.0))
        off_scr[...] = p_start + earlier_tiles
        off_ref[...] = off_scr[...].astype(I32)
        cnt_ref[...] = cnt8.astype(I32)
        p_end = p_start[:, 0:1] + tot
        end_ref[...] = p_end.astype(I32)
        lanes = map_ref.shape[1]
        blk_start = lax.broadcasted_iota(I32, (N_EXPERTS, lanes), 1).astype(F32) * ROW_BLOCK
        blk_e = jnp.sum(jnp.where(blk_start >= p_end, 1.0, 0.0), axis=0, keepdims=True)
        blk_e = jnp.minimum(blk_e, N_EXPERTS - 1.0)
        n_used = jnp.max(p_end, axis=0, keepdims=True) * (1.0 / ROW_BLOCK)
        last = lax.broadcasted_iota(I32, (1, lanes), 1) == lanes - 1
        map_ref[...] = jnp.where(last, n_used, blk_e).astype(I32)

    @pl.when(phase == 1)
    def _():
        cnt_col = jnp.sum(jnp.where(tile_lane == i, cnt_scr[...], 0.0), axis=1, keepdims=True)
        local_off = _dot_hi(earlier_e, jnp.broadcast_to(cnt_col, (N_EXPERTS, tm)))
        for s in range(TOP_K):
            hot = eio == idx_scr[s:s + 1, pl.ds(tok0, tm)]
            base = jnp.sum(jnp.where(hot, local_off, 0.0), axis=0, keepdims=True).astype(I32)
            slot_ref[s:s + 1, :] = base + pos_scr[s:s + 1, pl.ds(tok0, tm)]


def _route(logits_t):
    n_tok = logits_t.shape[1]
    n_tiles, _, _, map_lanes = _moe_dims(n_tok)
    tm = MOE_TM
    last = n_tiles - 1
    const = lambda p, i: (0, 0)
    return pl.pallas_call(
        _route_kernel,
        grid=(2, n_tiles),
        in_specs=[pl.BlockSpec((N_EXPERTS, tm), lambda p, i: (0, i * (1 - p) + last * p))],
        out_specs=[
            pl.BlockSpec((TOP_K, tm), lambda p, i: (0, i * (1 - p) + last * p)),
            pl.BlockSpec((TOP_K, tm), lambda p, i: (0, i * p)),
            pl.BlockSpec((N_EXPERTS, n_tiles), const),
            pl.BlockSpec((N_EXPERTS, n_tiles), const),
            pl.BlockSpec((N_EXPERTS, 1), const),
            pl.BlockSpec((1, map_lanes), const),
        ],
        out_shape=[
            jax.ShapeDtypeStruct((TOP_K, n_tok), F32),
            jax.ShapeDtypeStruct((TOP_K, n_tok), I32),
            jax.ShapeDtypeStruct((N_EXPERTS, n_tiles), I32),
            jax.ShapeDtypeStruct((N_EXPERTS, n_tiles), I32),
            jax.ShapeDtypeStruct((N_EXPERTS, 1), I32),
            jax.ShapeDtypeStruct((1, map_lanes), I32),
        ],
        scratch_shapes=[
            pltpu.VMEM((TOP_K, n_tok), I32),
            pltpu.VMEM((TOP_K, n_tok), I32),
            pltpu.VMEM((N_EXPERTS, n_tiles), F32),
            pltpu.VMEM((N_EXPERTS, n_tiles), F32),
        ],
        compiler_params=_cparams(("arbitrary", "arbitrary")),
        name="route",
    )(logits_t)


def _piece_copies(off_ref, cnt_ref, tile, make_copy, action):
    def per_expert(e, stage_row):
        c8 = cnt_ref[tile * N_EXPERTS + e]
        hbm_row = off_ref[tile * N_EXPERTS + e]
        done = 0
        for size in PIECE_SIZES:
            bit = c8 & size

            @pl.when(bit != 0)
            def _(done=done, size=size):
                action(make_copy(pl.multiple_of(stage_row + done, PIECE_ALIGN),
                                 pl.multiple_of(hbm_row + done, PIECE_ALIGN), size))

            done = done + bit
        return stage_row + c8

    lax.fori_loop(0, N_EXPERTS, per_expert, 0)


def _slot_rows(slot_ref):
    return [slot_ref[s:s + 1, :] for s in range(TOP_K)]


def _one_hot_rows(slots, r0):
    rio = lax.broadcasted_iota(I32, (STAGE_CHUNK, MOE_TM), 0) + r0
    pick = jnp.zeros((STAGE_CHUNK, MOE_TM), F32)
    for s in range(TOP_K):
        pick = jnp.where(rio == slots[s], 1.0, pick)
    return pick.astype(BF16)


def _dispatch_kernel(off_ref, cnt_ref, end_ref, slot_ref, wt_ref, x1_ref, xs_hbm, stage, zeros, sem, zero_sem):
    i = pl.program_id(0)
    n_tiles = pl.num_programs(0)
    buf = i % 2

    @pl.when(i == 0)
    def _():
        zeros[...] = jnp.zeros_like(zeros)

        def tail_copies(action):
            def per_expert(e, carry):
                last_piece = (n_tiles - 1) * N_EXPERTS + e
                start = off_ref[last_piece] + cnt_ref[last_piece]
                tail = end_ref[e] - start
                done = 0
                for size in TAIL_SIZES:
                    bit = tail & size

                    @pl.when(bit != 0)
                    def _(done=done, size=size):
                        action(pltpu.make_async_copy(
                            zeros.at[pl.ds(0, size)],
                            xs_hbm.at[pl.ds(pl.multiple_of(start + done, PIECE_ALIGN), size)], zero_sem))

                    done = done + bit
                return carry

            lax.fori_loop(0, N_EXPERTS, per_expert, 0)

            def per_block(b, carry):
                action(pltpu.make_async_copy(
                    zeros, xs_hbm.at[pl.ds(pl.multiple_of(b * ROW_BLOCK, ROW_BLOCK), ROW_BLOCK)], zero_sem))
                return carry

            lax.fori_loop(end_ref[N_EXPERTS - 1] // ROW_BLOCK, xs_hbm.shape[0] // ROW_BLOCK, per_block, 0)

        tail_copies(lambda cp: cp.start())
        tail_copies(lambda cp: cp.wait())

    def out_copies(tile, b, action):
        def make_copy(stage_row, hbm_row, size):
            return pltpu.make_async_copy(stage.at[b, pl.ds(stage_row, size)], xs_hbm.at[pl.ds(hbm_row, size)],
                                         sem.at[b])
        _piece_copies(off_ref, cnt_ref, tile, make_copy, action)

    @pl.when(i >= 2)
    def _():
        out_copies(i - 2, buf, lambda cp: cp.wait())

    xb = x1_ref[...].astype(BF16)
    slots = _slot_rows(slot_ref)
    tail_lane = lax.broadcasted_iota(I32, (STAGE_CHUNK, LANE), 1)
    for r0 in range(0, stage.shape[1], STAGE_CHUNK):
        stage[buf, r0:r0 + STAGE_CHUNK, 0:D_MODEL] = _dot(_one_hot_rows(slots, r0), xb).astype(BF16)
        rio = lax.broadcasted_iota(I32, (STAGE_CHUNK, MOE_TM), 0) + r0
        w_sel = jnp.zeros((STAGE_CHUNK, MOE_TM), F32)
        for s in range(TOP_K):
            w_sel = jnp.where(rio == slots[s], wt_ref[s:s + 1, :], w_sel)
        w_row = jnp.sum(w_sel, axis=1, keepdims=True)
        w_a = w_row.astype(BF16).astype(F32)
        tail = jnp.where(tail_lane == 0, w_a, jnp.where(tail_lane == 1, w_row - w_a, 0.0))
        stage[buf, r0:r0 + STAGE_CHUNK, D_MODEL:ROW_W] = tail.astype(BF16)
    out_copies(i, buf, lambda cp: cp.start())

    @pl.when(i == n_tiles - 1)
    def _():
        @pl.when(i >= 1)
        def _():
            out_copies(i - 1, 1 - buf, lambda cp: cp.wait())

        out_copies(i, buf, lambda cp: cp.wait())


def _dispatch(off_flat, cnt_flat, end_flat, slot, wts, x1):
    n_tok = x1.shape[0]
    n_tiles, stage_rows, n_blk, _ = _moe_dims(n_tok)
    n_rows = n_blk * ROW_BLOCK
    return pl.pallas_call(
        _dispatch_kernel,
        grid_spec=pltpu.PrefetchScalarGridSpec(
            num_scalar_prefetch=3,
            grid=(n_tiles,),
            in_specs=[pl.BlockSpec((TOP_K, MOE_TM), lambda i, off, cnt, end: (0, i)),
                      pl.BlockSpec((TOP_K, MOE_TM), lambda i, off, cnt, end: (0, i)),
                      pl.BlockSpec((MOE_TM, D_MODEL), lambda i, off, cnt, end: (i, 0))],
            out_specs=pl.BlockSpec(memory_space=pl.ANY),
            scratch_shapes=[pltpu.VMEM((2, stage_rows, ROW_W), BF16),
                            pltpu.VMEM((ROW_BLOCK, ROW_W), BF16),
                            pltpu.SemaphoreType.DMA((2,)), pltpu.SemaphoreType.DMA(())],
        ),
        out_shape=jax.ShapeDtypeStruct((n_rows, ROW_W), BF16),
        compiler_params=_cparams(("arbitrary",)),
        name="dispatch",
    )(off_flat, cnt_flat, end_flat, slot, wts, x1)


def _combine_kernel(off_ref, cnt_ref, slot_ref, x1_ref, lng_ref, lnb_ref, ys_hbm, out_ref, stage, sem):
    i = pl.program_id(0)
    n_tiles = pl.num_programs(0)
    buf = i % 2

    def in_copies(tile, b, action):
        def make_copy(stage_row, hbm_row, size):
            return pltpu.make_async_copy(ys_hbm.at[pl.ds(hbm_row, size)], stage.at[b, pl.ds(stage_row, size)],
                                         sem.at[b])
        _piece_copies(off_ref, cnt_ref, tile, make_copy, action)

    @pl.when(i == 0)
    def _():
        stage[...] = jnp.zeros_like(stage)
        in_copies(0, 0, lambda cp: cp.start())

    @pl.when(i + 1 < n_tiles)
    def _():
        in_copies(i + 1, 1 - buf, lambda cp: cp.start())

    in_copies(i, buf, lambda cp: cp.wait())

    slots = _slot_rows(slot_ref)
    ffn = jnp.zeros((MOE_TM, D_MODEL), F32)
    for r0 in range(0, stage.shape[1], STAGE_CHUNK):
        ffn = ffn + _dot_tn(_one_hot_rows(slots, r0), stage[buf, r0:r0 + STAGE_CHUNK, :])
    out_ref[...] = _layer_norm(DEEPNORM_ALPHA * x1_ref[...] + ffn, lng_ref[...], lnb_ref[...])


def _combine(off_flat, cnt_flat, slot, x1, ln_g, ln_b, ys):
    n_tok = x1.shape[0]
    n_tiles, stage_rows, _, _ = _moe_dims(n_tok)
    tok = lambda i, off, cnt: (0, i)
    row = lambda i, off, cnt: (i, 0)
    full = lambda shape: pl.BlockSpec(shape, lambda i, off, cnt: (0, 0))
    return pl.pallas_call(
        _combine_kernel,
        grid_spec=pltpu.PrefetchScalarGridSpec(
            num_scalar_prefetch=2,
            grid=(n_tiles,),
            in_specs=[pl.BlockSpec((TOP_K, MOE_TM), tok),
                      pl.BlockSpec((MOE_TM, D_MODEL), row), full((1, D_MODEL)), full((1, D_MODEL)),
                      pl.BlockSpec(memory_space=pl.ANY)],
            out_specs=pl.BlockSpec((MOE_TM, D_MODEL), row),
            scratch_shapes=[pltpu.VMEM((2, stage_rows, D_MODEL), BF16), pltpu.SemaphoreType.DMA((2,))],
        ),
        out_shape=jax.ShapeDtypeStruct((n_tok, D_MODEL), F32),
        compiler_params=_cparams(("arbitrary",)),
        name="combine",
    )(off_flat, cnt_flat, slot, x1, ln_g.reshape(1, -1), ln_b.reshape(1, -1), ys)


CAST_ROWS = 128
FFN_CHUNKS = 4


def _expert_kernel(blk_e_ref, n_used_ref, xs_ref, win_ref, bin_ref, wout_ref, bout_ref, ys_ref, win_bf, wout_bf):
    rb = pl.program_id(0)
    new_expert = (rb == 0) | (blk_e_ref[rb] != blk_e_ref[jnp.maximum(rb - 1, 0)])

    @pl.when((rb < n_used_ref[0]) & new_expert)
    def _():
        for r in range(0, D_MODEL, CAST_ROWS):
            win_bf[r:r + CAST_ROWS, :] = win_ref[0, r:r + CAST_ROWS, :].astype(BF16)
        for r in range(0, D_EXPERT, CAST_ROWS):
            wout_bf[r:r + CAST_ROWS, :] = wout_ref[0, r:r + CAST_ROWS, :].astype(BF16)

    @pl.when(rb < n_used_ref[0])
    def _():
        x = xs_ref[:, 0:D_MODEL]
        w_tail = xs_ref[:, D_MODEL:ROW_W].astype(F32)
        w_row = w_tail[:, 0:1] + w_tail[:, 1:2]
        cw = D_EXPERT // FFN_CHUNKS

        def hidden(j):
            gs = slice(j * cw, (j + 1) * cw)
            ls = slice(D_EXPERT + j * cw, D_EXPERT + (j + 1) * cw)
            return _dot(x, win_bf[:, gs]) + bin_ref[0, :, gs], _dot(x, win_bf[:, ls]) + bin_ref[0, :, ls]

        ahead = hidden(0)
        y = bout_ref[0]
        for j in range(FFN_CHUNKS):
            g, lin = ahead
            if j + 1 < FFN_CHUNKS:
                ahead = hidden(j + 1)
            gate_h = jnp.minimum(g, SWIGLU_LIMIT)
            lin_h = jnp.clip(lin, -SWIGLU_LIMIT, SWIGLU_LIMIT)
            act = gate_h * jax.nn.sigmoid(SWIGLU_ALPHA * gate_h) * (lin_h + 1.0)
            y = y + _dot(act.astype(BF16), wout_bf[j * cw:(j + 1) * cw, :])
        ys_ref[...] = (y * w_row).astype(BF16)

    @pl.when(rb >= n_used_ref[0])
    def _():
        ys_ref[...] = jnp.zeros_like(ys_ref)


def _experts(blk_e, n_used, xs, w_in, b_in, w_out, b_out):
    n_rows = xs.shape[0]
    n_blk = n_rows // ROW_BLOCK
    used = lambda rb, n_used: jnp.maximum(jnp.minimum(rb, n_used[0] - 1), 0)
    rows = lambda rb, blk_e, n_used: (used(rb, n_used), 0)
    per_e = lambda rb, blk_e, n_used: (blk_e[used(rb, n_used)], 0, 0)
    return pl.pallas_call(
        _expert_kernel,
        grid_spec=pltpu.PrefetchScalarGridSpec(
            num_scalar_prefetch=2,
            grid=(n_blk,),
            in_specs=[
                pl.BlockSpec((ROW_BLOCK, ROW_W), rows),
                pl.BlockSpec((1, D_MODEL, 2 * D_EXPERT), per_e),
                pl.BlockSpec((1, 1, 2 * D_EXPERT), per_e),
                pl.BlockSpec((1, D_EXPERT, D_MODEL), per_e),
                pl.BlockSpec((1, 1, D_MODEL), per_e),
            ],
            out_specs=pl.BlockSpec((ROW_BLOCK, D_MODEL), lambda rb, blk_e, n_used: (rb, 0)),
            scratch_shapes=[pltpu.VMEM((D_MODEL, 2 * D_EXPERT), BF16), pltpu.VMEM((D_EXPERT, D_MODEL), BF16)],
        ),
        out_shape=jax.ShapeDtypeStruct((n_rows, D_MODEL), BF16),
        compiler_params=_cparams(("arbitrary",)),
        name="experts",
    )(blk_e, n_used, xs, w_in, b_in.reshape(N_EXPERTS, 1, -1), w_out, b_out.reshape(N_EXPERTS, 1, -1))


def _moe(x1, logits_t, expert_w_in, expert_b_in, expert_w_out, expert_b_out, ln_g, ln_b):
    n_tok = x1.shape[0]
    _, _, n_blk, map_lanes = _moe_dims(n_tok)
    wts, slot, off, cnt, ends, blk_map = _route(logits_t)
    off_flat = off.T.reshape(-1)
    cnt_flat = cnt.T.reshape(-1)
    blk_e = blk_map[0, 0:n_blk]
    n_used = blk_map[0, map_lanes - 1:map_lanes]
    xs = _dispatch(off_flat, cnt_flat, ends.reshape(-1), slot, wts, x1)
    ys = _experts(blk_e, n_used, xs, expert_w_in, expert_b_in, expert_w_out, expert_b_out)
    return _combine(off_flat, cnt_flat, slot, x1, ln_g, ln_b, ys)


def kernel(x, ln1_g, ln1_b, ln2_g, ln2_b, w_in, shift_mix, decay_w0, decay_up, iclr_a0, iclr_up, gate_up, k_k, k_a, r_k, gn_g, gn_b, w_branch_rwkv, w_branch_attn, w_out, router_w, router_b, expert_w_in, expert_b_in, expert_w_out, expert_b_out):
    B, T, D = x.shape
    x2 = x.reshape(B * T, D)
    prw, q, k, v, gates = _proj(x2, w_in[0].astype(BF16), T)
    y_rwkv = _rwkv(prw, B, T, shift_mix[0], decay_w0[0], decay_up[0], iclr_a0[0], iclr_up[0], gate_up[0],
                   k_k[0], k_a[0], r_k[0], gn_g[0], gn_b[0])
    y_attn = _moba(q, k, v, B, T)
    x1, logits_t = _merge(x2, y_rwkv, y_attn, gates, w_branch_rwkv[0], w_branch_attn[0], w_out[0],
                          ln1_g[0], ln1_b[0], router_w[0], router_b[0])
    out = _moe(x1, logits_t, expert_w_in[0], expert_b_in[0], expert_w_out[0], expert_b_out[0], ln2_g[0], ln2_b[0])
    return out.reshape(B, T, D)
```

```python
import math

import jax
import jax.numpy as jnp
from jax import lax
from jax.experimental import pallas as pl
from jax.experimental.pallas import tpu as pltpu

F32 = jnp.float32
BF16 = jnp.bfloat16
I32 = jnp.int32
HI = lax.Precision.HIGHEST

D_MODEL = 1024
DEPTH = 1
RWKV_HEAD_DIM = 64
RWKV_DIM = 512
RWKV_HEADS = 8
DECAY_RANK = 64
ICLR_RANK = 64
GATE_RANK = 128
GN_EPS = 64e-5
ATTN_HEAD_DIM = 64
ATTN_DIM = 512
ATTN_HEADS = 8
MOBA_BLOCK = 256
MOBA_TOPK = 3
ROPE_THETA = 500000.0
ROPE_DIM = 16
NEG_INF = -1e30
N_EXPERTS = 32
TOP_K = 4
D_EXPERT = 1024
SWIGLU_LIMIT = 7.0
SWIGLU_ALPHA = 1.702
DEEPNORM_ALPHA = (2.0 * DEPTH) ** 0.25
LN_EPS = 1e-5
RWKV_COLS = 3 * RWKV_DIM + DECAY_RANK + ICLR_RANK + GATE_RANK
ATTN_COLS = 3 * ATTN_DIM
GATE_COLS = 2 * D_MODEL
IN_COLS = RWKV_COLS + ATTN_COLS + GATE_COLS

LANE = 128
SUBLANE = 8
VMEM_LIMIT_BYTES = 56 * 1024 * 1024

CHUNK = 64


def _cparams(sem):
    return pltpu.CompilerParams(dimension_semantics=sem, vmem_limit_bytes=VMEM_LIMIT_BYTES)


def _dot(a, b):
    return jnp.dot(a, b, preferred_element_type=F32)


def _dot_hi(a, b):
    return jnp.dot(a, b, preferred_element_type=F32, precision=HI)


def _dot_nt(a, b, precision=None):
    return lax.dot_general(a, b, (((1,), (1,)), ((), ())), preferred_element_type=F32, precision=precision)


def _dot_tn(a, b, precision=None):
    return lax.dot_general(a, b, (((0,), (0,)), ((), ())), preferred_element_type=F32, precision=precision)


def _bf16_pieces(x, n):
    pieces = []
    for _ in range(n):
        p = x.astype(BF16)
        pieces.append(p)
        x = x - p.astype(F32)
    return pieces


def _dot_exact_rhs(x, b_bf16, n):
    out = None
    for p in _bf16_pieces(x, n):
        d = _dot(p, b_bf16)
        out = d if out is None else out + d
    return out


PROJ_TM = 256
Q_SCALE = math.log2(math.e) * ATTN_HEAD_DIM ** -0.5


def _proj_kernel(x_ref, w_ref, cos_ref, sa_ref, sb_ref, prw_ref, q_ref, k_ref, vt_ref, g_ref):
    xb = x_ref[...].astype(BF16)
    prw_ref[...] = _dot(xb, w_ref[:, 0:RWKV_COLS])
    c0 = RWKV_COLS
    cos = cos_ref[...]
    sa = sa_ref[...]
    sb = sb_ref[...]

    def rope(t):
        return t * cos + pltpu.roll(t, ATTN_DIM - ROPE_DIM // 2, 1) * sa + pltpu.roll(t, ROPE_DIM // 2, 1) * sb

    q_ref[...] = (rope(_dot(xb, w_ref[:, c0:c0 + ATTN_DIM])) * Q_SCALE).astype(BF16)
    k_ref[...] = rope(_dot(xb, w_ref[:, c0 + ATTN_DIM:c0 + 2 * ATTN_DIM])).astype(BF16)
    vt_ref[0] = _dot(xb, w_ref[:, c0 + 2 * ATTN_DIM:c0 + 3 * ATTN_DIM]).T.astype(BF16)
    c1 = RWKV_COLS + ATTN_COLS
    g_ref[...] = jax.nn.sigmoid(_dot(xb, w_ref[:, c1:c1 + GATE_COLS])).astype(BF16)


def _rope_tables(T):
    half = ROPE_DIM // 2
    inv_freq = jnp.power(ROPE_THETA, -jnp.arange(0, ROPE_DIM, 2, dtype=F32) / ROPE_DIM)
    ang = jnp.arange(T).astype(F32)[:, None] * inv_freq[None, :]
    cos, sin = jnp.cos(ang), jnp.sin(ang)
    pad = jnp.zeros((T, ATTN_HEAD_DIM - ROPE_DIM), F32)
    cos_h = jnp.concatenate([cos, cos, pad + 1.0], axis=1)
    sa_h = jnp.concatenate([-sin, jnp.zeros((T, half), F32), pad], axis=1)
    sb_h = jnp.concatenate([jnp.zeros((T, half), F32), sin, pad], axis=1)
    tile = lambda t: jnp.tile(t, (1, ATTN_HEADS))
    return tile(cos_h), tile(sa_h), tile(sb_h)


def _proj(x2, w_in_bf, T):
    n_tok = x2.shape[0]
    tm = PROJ_TM
    t_tiles = T // tm
    cos, sa, sb = _rope_tables(T)
    row = lambda i: (i, 0)
    tab = lambda i: (i % t_tiles, 0)
    return pl.pallas_call(
        _proj_kernel,
        grid=(n_tok // tm,),
        in_specs=[
            pl.BlockSpec((tm, D_MODEL), row),
            pl.BlockSpec((D_MODEL, IN_COLS), lambda i: (0, 0), pipeline_mode=pl.Buffered(1)),
            pl.BlockSpec((tm, ATTN_DIM), tab),
            pl.BlockSpec((tm, ATTN_DIM), tab),
            pl.BlockSpec((tm, ATTN_DIM), tab),
        ],
        out_specs=[
            pl.BlockSpec((tm, RWKV_COLS), row),
            pl.BlockSpec((tm, ATTN_DIM), row),
            pl.BlockSpec((tm, ATTN_DIM), row),
            pl.BlockSpec((1, ATTN_DIM, tm), lambda i: (i // t_tiles, 0, i % t_tiles)),
            pl.BlockSpec((tm, GATE_COLS), row),
        ],
        out_shape=[
            jax.ShapeDtypeStruct((n_tok, RWKV_COLS), F32),
            jax.ShapeDtypeStruct((n_tok, ATTN_DIM), BF16),
            jax.ShapeDtypeStruct((n_tok, ATTN_DIM), BF16),
            jax.ShapeDtypeStruct((n_tok // T, ATTN_DIM, T), BF16),
            jax.ShapeDtypeStruct((n_tok, GATE_COLS), BF16),
        ],
        compiler_params=_cparams(("parallel",)),
        name="proj",
    )(x2, w_in_bf, cos, sa, sb)


RWKV_TT = 256


def _rwkv_kernel(p_ref, mix_ref, w0_ref, dup_ref, a0_ref, iup_ref, gup_ref, kk_ref, ka_ref, rk_ref,
                 gng_ref, gnb_ref, ones_ref, y_ref, s_scr, prev_scr):
    H, N, C = RWKV_HEADS, RWKV_HEAD_DIM, CHUNK

    @pl.when(pl.program_id(1) == 0)
    def _():
        s_scr[...] = jnp.zeros_like(s_scr)
        prev_scr[...] = jnp.zeros_like(prev_scr)

    TT = RWKV_TT
    n_chunks = TT // C
    bones = ones_ref[...]
    row = lax.broadcasted_iota(I32, (C, C), 0)
    col = lax.broadcasted_iota(I32, (C, C), 1)
    lower_incl = col <= row
    lower_strict = col < row
    eye = jnp.where(col == row, 1.0, 0.0)
    t_row = lax.broadcasted_iota(I32, (TT, TT), 0)
    t_col = lax.broadcasted_iota(I32, (TT, TT), 1)
    same_chunk = (t_row // C) == (t_col // C)
    chunk_ltri = jnp.where(same_chunk & (t_col <= t_row), 1.0, 0.0).astype(BF16)
    chunk_ones = jnp.where(same_chunk, 1.0, 0.0).astype(BF16)

    p = p_ref[...]
    first_row = lax.broadcasted_iota(I32, (TT, RWKV_COLS), 0) == 0
    prev = jnp.where(first_row, prev_scr[...], pltpu.roll(p, 1, 0))
    prev_scr[...] = p[TT - 1:TT, :]
    ps = p + (prev - p) * mix_ref[...]
    r = ps[:, 0:RWKV_DIM]
    k = ps[:, RWKV_DIM:2 * RWKV_DIM]
    v = ps[:, 2 * RWKV_DIM:3 * RWKV_DIM]
    o = 3 * RWKV_DIM
    xw = ps[:, o:o + DECAY_RANK]
    xa = ps[:, o + DECAY_RANK:o + DECAY_RANK + ICLR_RANK]
    xg = ps[:, o + DECAY_RANK + ICLR_RANK:RWKV_COLS]
    w_raw = w0_ref[...] + _dot(jnp.tanh(xw).astype(BF16), dup_ref[...])
    logw = -math.exp(-0.5) * jax.nn.sigmoid(w_raw)
    a = jax.nn.sigmoid(a0_ref[...] + _dot(xa.astype(BF16), iup_ref[...]))
    g = _dot(jax.nn.sigmoid(xg).astype(BF16), gup_ref[...])
    kk0 = k * kk_ref[...]
    kk = kk0 / jnp.maximum(jnp.sqrt(_dot_exact_rhs(kk0 * kk0, bones, 2)), 1e-12)
    kp = k * (1.0 + (a - 1.0) * ka_ref[...])
    kka = kk * a
    logw_pieces = _bf16_pieces(logw, 3)
    cum = sum(_dot(chunk_ltri, piece) for piece in logw_pieces)
    tot = sum(_dot(chunk_ones, piece) for piece in logw_pieces)
    e_neg = jnp.exp(-cum)
    at = (-kk * jnp.exp(cum - logw)).astype(BF16)
    rt = (r * jnp.exp(cum)).astype(BF16)
    bt = (kka * e_neg).astype(BF16)
    kt = (kp * e_neg).astype(BF16)
    e_end = jnp.exp(tot - cum)
    bh = (kka * e_end).astype(BF16)
    kh = (kp * e_end).astype(BF16)
    e_tot = jnp.exp(tot)
    vb = v.astype(BF16)

    units = [(c, h) for c in range(n_chunks) for h in range(H)]
    blk = lambda t, c, h: t[c * C:(c + 1) * C, h * N:(h + 1) * N]
    lhs = {u: jnp.concatenate([blk(at, *u), blk(rt, *u)], axis=0) for u in units}
    rhs = {u: jnp.concatenate([blk(bt, *u), blk(kt, *u)], axis=0) for u in units}
    aa = {u: _dot_nt(lhs[u], rhs[u]) for u in units}
    a_ab = {u: jnp.where(lower_strict, aa[u][0:C, 0:C], 0.0) for u in units}
    a_kv = {u: jnp.concatenate([jnp.where(lower_strict, aa[u][0:C, C:2 * C], 0.0),
                                jnp.where(lower_incl, aa[u][C:2 * C, C:2 * C], 0.0)], axis=0).astype(BF16)
            for u in units}
    a_rb = {u: jnp.where(lower_incl, aa[u][C:2 * C, 0:C], 0.0).astype(BF16) for u in units}
    akv = {u: _dot(a_kv[u], blk(vb, *u)) for u in units}
    tinv = {u: eye + a_ab[u] for u in units}
    npow = a_ab
    for _ in range(5):
        npb = {u: npow[u].astype(BF16) for u in units}
        npow = {u: _dot(npb[u], npb[u]) for u in units}
        tinv = {u: tinv[u] + _dot(tinv[u].astype(BF16), npow[u].astype(BF16)) for u in units}
    tinv_b = {u: tinv[u].astype(BF16) for u in units}

    state = [s_scr[h] for h in range(H)]
    y_rows = []
    for c in range(n_chunks):
        hs = [(c, h) for h in range(H)]
        ar_s = {u: _dot_nt(lhs[u], state[u[1]].astype(BF16)) for u in hs}
        ub = {u: _dot(tinv_b[u], (ar_s[u][0:C] + akv[u][0:C]).astype(BF16)).astype(BF16) for u in hs}
        ys = [ar_s[u][C:2 * C] + akv[u][C:2 * C] + _dot(a_rb[u], ub[u]) for u in hs]
        upd = {u: _dot_tn(jnp.concatenate([ub[u], blk(vb, *u)], axis=0),
                          jnp.concatenate([blk(bh, *u), blk(kh, *u)], axis=0)) for u in hs}
        state = [state[h] * e_tot[c * C:c * C + 1, h * N:(h + 1) * N] + upd[(c, h)] for h in range(H)]
        y_rows.append(jnp.concatenate(ys, axis=1))
    for h in range(H):
        s_scr[h] = state[h]

    y = jnp.concatenate(y_rows, axis=0)
    mu = _dot_exact_rhs(y, bones, 1) * (1.0 / N)
    yc = y - mu
    var = _dot_exact_rhs(yc * yc, bones, 1) * (1.0 / N)
    yn = yc * lax.rsqrt(var + GN_EPS) * gng_ref[...] + gnb_ref[...]
    bonus = _dot_exact_rhs(r * kp * rk_ref[...], bones, 1) * v
    y_ref[...] = ((yn + bonus) * g).astype(BF16)


def _rwkv(p_rwkv, B, T, shift_mix, decay_w0, decay_up, iclr_a0, iclr_up, gate_up, k_k, k_a, r_k, gn_g, gn_b):
    tt = RWKV_TT
    n_t = T // tt
    head = jnp.arange(RWKV_DIM) // RWKV_HEAD_DIM
    bones = (head[:, None] == head[None, :]).astype(BF16)
    vec = lambda a: a.reshape(1, -1)
    full = lambda shape: pl.BlockSpec(shape, lambda b, j: (0, 0))
    return pl.pallas_call(
        _rwkv_kernel,
        grid=(B, n_t),
        in_specs=[
            pl.BlockSpec((tt, RWKV_COLS), lambda b, j: (b * n_t + j, 0)),
            full((1, RWKV_COLS)), full((1, RWKV_DIM)), full((DECAY_RANK, RWKV_DIM)),
            full((1, RWKV_DIM)), full((ICLR_RANK, RWKV_DIM)), full((GATE_RANK, RWKV_DIM)),
            full((1, RWKV_DIM)), full((1, RWKV_DIM)), full((1, RWKV_DIM)),
            full((1, RWKV_DIM)), full((1, RWKV_DIM)), full((RWKV_DIM, RWKV_DIM)),
        ],
        out_specs=pl.BlockSpec((tt, RWKV_DIM), lambda b, j: (b * n_t + j, 0)),
        out_shape=jax.ShapeDtypeStruct((B * T, RWKV_DIM), BF16),
        scratch_shapes=[
            pltpu.VMEM((RWKV_HEADS, RWKV_HEAD_DIM, RWKV_HEAD_DIM), F32),
            pltpu.VMEM((1, RWKV_COLS), F32),
        ],
        compiler_params=_cparams(("parallel", "arbitrary")),
        name="rwkv",
    )(p_rwkv, vec(shift_mix), vec(decay_w0), decay_up.astype(BF16), vec(iclr_a0), iclr_up.astype(BF16),
      gate_up.astype(BF16), vec(k_k), vec(k_a), vec(r_k), vec(gn_g), vec(gn_b), bones)


MOBA_HP = 4


def _moba_kernel(q_ref, k_ref, vt_ref, o_ref, kmean_scr, sel_scr, acc_scr, score_scr, prob_scr):
    blk_sz, dh = MOBA_BLOCK, ATTN_HEAD_DIM
    pair = 2 * blk_sz
    nb = k_ref.shape[0] // blk_sz
    i = pl.program_id(2)

    @pl.when(i == 0)
    def _():
        for n in range(nb):
            kb = k_ref[n * blk_sz:(n + 1) * blk_sz, :].astype(F32)
            kmean_scr[n:n + 1, :] = jnp.sum(kb, axis=0, keepdims=True) * (1.0 / blk_sz)

    blk = lax.broadcasted_iota(I32, (nb, blk_sz), 0)
    kpos = lax.broadcasted_iota(I32, (blk_sz, blk_sz), 0)
    qpos = lax.broadcasted_iota(I32, (blk_sz, blk_sz), 1)
    own_start = pl.multiple_of(i * blk_sz, blk_sz)
    heads = [slice(hh * dh, (hh + 1) * dh) for hh in range(MOBA_HP)]
    hds = range(MOBA_HP)
    q_t = q_ref[...].astype(F32).T
    row_head = lax.broadcasted_iota(I32, q_t.shape, 0) // dh
    q_cols = jnp.concatenate([jnp.where(row_head == hh, q_t, 0.0) for hh in hds], axis=1).astype(BF16)

    def per_head(x):
        return [x[:, hh * blk_sz:(hh + 1) * blk_sz] for hh in hds]

    def scores(first_row, rows):
        return per_head(_dot(k_ref[pl.ds(first_row, rows), :], q_cols))

    def values(first_key, keys, ps):
        return [_dot(vt_ref[0, heads[hh], pl.ds(first_key, keys)], ps[hh]) for hh in hds]

    gates = per_head(sum(_dot(piece, q_cols) for piece in _bf16_pieces(kmean_scr[...], 3)))
    own = scores(own_start, blk_sz)
    first = scores(0, pair)
    for hh in hds:
        gate = jnp.where(blk < i, gates[hh], NEG_INF)
        rank = jnp.zeros((nb, blk_sz), I32)
        for m in range(nb):
            gm = gate[m:m + 1, :]
            beats = (gm > gate) | ((gm == gate) & (m < blk))
            rank = rank + beats.astype(I32)
        sel_scr[hh] = ((rank < MOBA_TOPK) & (blk < i)).astype(F32)
    carry0, ps = [], []
    for hh in hds:
        s = jnp.where(kpos <= qpos, own[hh], NEG_INF)
        m0 = jnp.max(s, axis=0, keepdims=True)
        p = jnp.exp2(s - m0)
        carry0 += [m0, jnp.sum(p, axis=0, keepdims=True), jnp.ones_like(m0)]
        ps.append(p.astype(BF16))
    own_pv = values(own_start, blk_sz, ps)

    def keep_chosen(hh, t, pair_scores):
        for j in range(2):
            score_scr[hh, j * blk_sz:(j + 1) * blk_sz, :] = jnp.where(
                sel_scr[hh, pl.ds(2 * t + j, 1), :] > 0.0, pair_scores[j * blk_sz:(j + 1) * blk_sz, :], NEG_INF)

    for hh in hds:
        acc_scr[hh] = own_pv[hh]
        prob_scr[hh] = jnp.zeros_like(prob_scr[hh])
        keep_chosen(hh, 0, first[hh])

    n_pairs = (i + 1) // 2

    def body(t, carry):
        t_next = jnp.minimum(t + 1, n_pairs - 1)
        ahead = scores(pl.multiple_of(t_next * pair, pair), pair)
        pend = values(pl.multiple_of(jnp.maximum(t - 1, 0) * pair, pair), pair, [prob_scr[hh] for hh in hds])
        out = []
        for hh in hds:
            m_run, l_run, rescale = carry[3 * hh], carry[3 * hh + 1], carry[3 * hh + 2]
            m_new = jnp.maximum(m_run, jnp.max(score_scr[hh], axis=0, keepdims=True))
            alpha = jnp.exp2(m_run - m_new)
            p = jnp.exp2(score_scr[hh] - m_new)
            out += [m_new, alpha * l_run + jnp.sum(p, axis=0, keepdims=True), alpha]
            prob_scr[hh] = p.astype(BF16)
            acc_scr[hh] = rescale * acc_scr[hh] + pend[hh]
            keep_chosen(hh, t_next, ahead[hh])
        return tuple(out)

    fin = lax.fori_loop(0, n_pairs, body, tuple(carry0))
    last = values(pl.multiple_of(jnp.maximum(n_pairs - 1, 0) * pair, pair), pair, [prob_scr[hh] for hh in hds])
    outs = [((fin[3 * hh + 2] * acc_scr[hh] + last[hh]) / fin[3 * hh + 1]).T for hh in hds]
    o_ref[...] = jnp.concatenate(outs, axis=1).astype(BF16)


def _moba(q, k, vt, B, T):
    blk_sz = MOBA_BLOCK
    nq = T // blk_sz
    lanes = MOBA_HP * ATTN_HEAD_DIM
    k_spec = pl.BlockSpec((T, lanes), lambda b, hp, i: (b, hp))
    vt_spec = pl.BlockSpec((1, lanes, T), lambda b, hp, i: (b, hp, 0))
    q_spec = pl.BlockSpec((blk_sz, lanes), lambda b, hp, i: (b * nq + i, hp))
    return pl.pallas_call(
        _moba_kernel,
        grid=(B, ATTN_HEADS // MOBA_HP, nq),
        in_specs=[q_spec, k_spec, vt_spec],
        out_specs=q_spec,
        out_shape=jax.ShapeDtypeStruct((B * T, ATTN_DIM), BF16),
        scratch_shapes=[
            pltpu.VMEM((T // blk_sz, lanes), F32),
            pltpu.VMEM((MOBA_HP, T // blk_sz, blk_sz), F32),
            pltpu.VMEM((MOBA_HP, ATTN_HEAD_DIM, blk_sz), F32),
            pltpu.VMEM((MOBA_HP, 2 * blk_sz, blk_sz), F32),
            pltpu.VMEM((MOBA_HP, 2 * blk_sz, blk_sz), BF16),
        ],
        compiler_params=_cparams(("parallel", "parallel", "arbitrary")),
        name="moba",
    )(q, k, vt)


MERGE_TM = 512
MERGE_SUB = 128


def _layer_norm(h, g, b):
    mu = jnp.mean(h, axis=-1, keepdims=True)
    hc = h - mu
    var = jnp.mean(hc * hc, axis=-1, keepdims=True)
    return hc * lax.rsqrt(var + LN_EPS) * g + b


def _merge_kernel(x_ref, yr_ref, ya_ref, g_ref, wbr_ref, wba_ref, wo_ref, lng_ref, lnb_ref, rwt_ref, rb_ref,
                  x1_ref, lgt_ref):
    subs = [slice(s * MERGE_SUB, (s + 1) * MERGE_SUB) for s in range(MERGE_TM // MERGE_SUB)]
    yr = [_dot(yr_ref[sl, :], wbr_ref[...]) for sl in subs]
    ya = [_dot(ya_ref[sl, :], wba_ref[...]) for sl in subs]
    merged = [(g_ref[sl, 0:D_MODEL].astype(F32) * yr[s] + g_ref[sl, D_MODEL:2 * D_MODEL].astype(F32) * ya[s])
              .astype(BF16) for s, sl in enumerate(subs)]
    mix = [_dot(m, wo_ref[...]) for m in merged]
    x1 = [_layer_norm(DEEPNORM_ALPHA * x_ref[sl, :] + mix[s], lng_ref[...], lnb_ref[...])
          for s, sl in enumerate(subs)]
    for s, sl in enumerate(subs):
        x1_ref[sl, :] = x1[s]
    for s, sl in enumerate(subs):
        lgt_ref[:, sl] = _dot_nt(rwt_ref[...], x1[s], precision=HI) + rb_ref[...]


def _merge(x2, y_rwkv, y_attn, gates, w_br, w_ba, w_o, ln_g, ln_b, router_w, router_b):
    n_tok = x2.shape[0]
    tm = MERGE_TM
    row = lambda i: (i, 0)
    full = lambda shape: pl.BlockSpec(shape, lambda i: (0, 0))
    return pl.pallas_call(
        _merge_kernel,
        grid=(n_tok // tm,),
        in_specs=[
            pl.BlockSpec((tm, D_MODEL), row), pl.BlockSpec((tm, RWKV_DIM), row), pl.BlockSpec((tm, ATTN_DIM), row),
            pl.BlockSpec((tm, GATE_COLS), row),
            full((RWKV_DIM, D_MODEL)), full((ATTN_DIM, D_MODEL)), full((D_MODEL, D_MODEL)),
            full((1, D_MODEL)), full((1, D_MODEL)), full((N_EXPERTS, D_MODEL)), full((N_EXPERTS, 1)),
        ],
        out_specs=[pl.BlockSpec((tm, D_MODEL), row), pl.BlockSpec((N_EXPERTS, tm), lambda i: (0, i))],
        out_shape=[jax.ShapeDtypeStruct((n_tok, D_MODEL), F32), jax.ShapeDtypeStruct((N_EXPERTS, n_tok), F32)],
        compiler_params=_cparams(("parallel",)),
        name="merge",
    )(x2, y_rwkv, y_attn, gates, w_br.astype(BF16), w_ba.astype(BF16), w_o.astype(BF16),
      ln_g.reshape(1, -1), ln_b.reshape(1, -1), router_w.T, router_b.reshape(-1, 1))


MOE_TM = 512
ROW_BLOCK = 256
PIECE_ALIGN = 2 * SUBLANE
PIECE_SIZES = (512, 256, 128, 64, 32, 16)
TAIL_SIZES = (128, 64, 32, 16)
STAGE_CHUNK = 256
ROW_W = D_MODEL + LANE


def _moe_dims(n_tok):
    n_tiles = n_tok // MOE_TM
    stage_rows = -(-(MOE_TM * TOP_K + N_EXPERTS * (PIECE_ALIGN - 1)) // STAGE_CHUNK) * STAGE_CHUNK
    max_rows = n_tok * TOP_K + n_tiles * N_EXPERTS * (PIECE_ALIGN - 1) + N_EXPERTS * (ROW_BLOCK - 1)
    n_blk = -(-max_rows // ROW_BLOCK)
    map_lanes = -(-(n_blk + 1) // LANE) * LANE
    return n_tiles, stage_rows, n_blk, map_lanes


def _round_up_f32(x, m):
    return jnp.floor((x + (m - 1)) * (1.0 / m)) * m


def _route_kernel(lg_ref, wt_ref, slot_ref, off_ref, cnt_ref, end_ref, map_ref, idx_scr, pos_scr, cnt_scr, off_scr):
    tm = MOE_TM
    n_tiles = cnt_scr.shape[1]
    phase = pl.program_id(0)
    i = pl.program_id(1)
    tok0 = pl.multiple_of(i * tm, tm)
    eio = lax.broadcasted_iota(I32, (N_EXPERTS, tm), 0)
    tile_lane = lax.broadcasted_iota(I32, (N_EXPERTS, n_tiles), 1)
    e_from = lax.broadcasted_iota(I32, (N_EXPERTS, N_EXPERTS), 1)
    e_to = lax.broadcasted_iota(I32, (N_EXPERTS, N_EXPERTS), 0)
    earlier_e = jnp.where(e_from < e_to, 1.0, 0.0)

    @pl.when((phase == 0) & (i == 0))
    def _():
        cnt_scr[...] = jnp.zeros_like(cnt_scr)

    @pl.when(phase == 0)
    def _():
        work = lg_ref[...]
        vals, hots = [], []
        for s in range(TOP_K):
            m = jnp.max(work, axis=0, keepdims=True)
            ix = jnp.min(jnp.where(work == m, eio, N_EXPERTS), axis=0, keepdims=True)
            hot = eio == ix
            idx_scr[s:s + 1, pl.ds(tok0, tm)] = ix
            vals.append(m)
            hots.append(hot)
            work = jnp.where(hot, -jnp.inf, work)
        es = [jnp.exp(v - vals[0]) for v in vals]
        denom = es[0] + es[1] + es[2] + es[3]
        for s in range(TOP_K):
            wt_ref[s:s + 1, :] = es[s] / denom
        multi_f = jnp.where(hots[0] | hots[1] | hots[2] | hots[3], 1.0, 0.0)
        t_from = lax.broadcasted_iota(I32, (tm, tm), 0)
        t_to = lax.broadcasted_iota(I32, (tm, tm), 1)
        before = jnp.where(t_from < t_to, 1.0, 0.0).astype(BF16)
        count = _dot(multi_f.astype(BF16), before)
        for s in range(TOP_K):
            pos_scr[s:s + 1, pl.ds(tok0, tm)] = jnp.sum(
                jnp.where(hots[s], count, 0.0), axis=0, keepdims=True).astype(I32)
        cnt8 = _round_up_f32(jnp.sum(multi_f, axis=1, keepdims=True), PIECE_ALIGN)
        cnt_scr[...] = cnt_scr[...] + jnp.where(tile_lane == i, cnt8, 0.0)

    @pl.when((phase == 1) & (i == 0))
    def _():
        cnt8 = cnt_scr[...]
        tot = _round_up_f32(jnp.sum(cnt8, axis=1, keepdims=True), ROW_BLOCK)
        p_start = _dot_hi(earlier_e, jnp.broadcast_to(tot, (N_EXPERTS, n_tiles)))
        i_from = lax.broadcasted_iota(I32, (n_tiles, n_tiles), 0)
        i_to = lax.broadcasted_iota(I32, (n_tiles, n_tiles), 1)
        earlier_tiles = _dot_hi(cnt8, jnp.where(i_from < i_to, 1.0, 0.0))
        off_scr[...] = p_start + earlier_tiles
        off_ref[...] = off_scr[...].astype(I32)
        cnt_ref[...] = cnt8.astype(I32)
        p_end = p_start[:, 0:1] + tot
        end_ref[...] = p_end.astype(I32)
        lanes = map_ref.shape[1]
        blk_start = lax.broadcasted_iota(I32, (N_EXPERTS, lanes), 1).astype(F32) * ROW_BLOCK
        blk_e = jnp.sum(jnp.where(blk_start >= p_end, 1.0, 0.0), axis=0, keepdims=True)
        blk_e = jnp.minimum(blk_e, N_EXPERTS - 1.0)
        n_used = jnp.max(p_end, axis=0, keepdims=True) * (1.0 / ROW_BLOCK)
        last = lax.broadcasted_iota(I32, (1, lanes), 1) == lanes - 1
        map_ref[...] = jnp.where(last, n_used, blk_e).astype(I32)

    @pl.when(phase == 1)
    def _():
        cnt_col = jnp.sum(jnp.where(tile_lane == i, cnt_scr[...], 0.0), axis=1, keepdims=True)
        local_off = _dot_hi(earlier_e, jnp.broadcast_to(cnt_col, (N_EXPERTS, tm)))
        for s in range(TOP_K):
            hot = eio == idx_scr[s:s + 1, pl.ds(tok0, tm)]
            base = jnp.sum(jnp.where(hot, local_off, 0.0), axis=0, keepdims=True).astype(I32)
            slot_ref[s:s + 1, :] = base + pos_scr[s:s + 1, pl.ds(tok0, tm)]


def _route(logits_t):
    n_tok = logits_t.shape[1]
    n_tiles, _, _, map_lanes = _moe_dims(n_tok)
    tm = MOE_TM
    last = n_tiles - 1
    const = lambda p, i: (0, 0)
    return pl.pallas_call(
        _route_kernel,
        grid=(2, n_tiles),
        in_specs=[pl.BlockSpec((N_EXPERTS, tm), lambda p, i: (0, i * (1 - p) + last * p))],
        out_specs=[
            pl.BlockSpec((TOP_K, tm), lambda p, i: (0, i * (1 - p) + last * p)),
            pl.BlockSpec((TOP_K, tm), lambda p, i: (0, i * p)),
            pl.BlockSpec((N_EXPERTS, n_tiles), const),
            pl.BlockSpec((N_EXPERTS, n_tiles), const),
            pl.BlockSpec((N_EXPERTS, 1), const),
            pl.BlockSpec((1, map_lanes), const),
        ],
        out_shape=[
            jax.ShapeDtypeStruct((TOP_K, n_tok), F32),
            jax.ShapeDtypeStruct((TOP_K, n_tok), I32),
            jax.ShapeDtypeStruct((N_EXPERTS, n_tiles), I32),
            jax.ShapeDtypeStruct((N_EXPERTS, n_tiles), I32),
            jax.ShapeDtypeStruct((N_EXPERTS, 1), I32),
            jax.ShapeDtypeStruct((1, map_lanes), I32),
        ],
        scratch_shapes=[
            pltpu.VMEM((TOP_K, n_tok), I32),
            pltpu.VMEM((TOP_K, n_tok), I32),
            pltpu.VMEM((N_EXPERTS, n_tiles), F32),
            pltpu.VMEM((N_EXPERTS, n_tiles), F32),
        ],
        compiler_params=_cparams(("arbitrary", "arbitrary")),
        name="route",
    )(logits_t)


def _piece_copies(off_ref, cnt_ref, tile, make_copy, action):
    def per_expert(e, stage_row):
        c8 = cnt_ref[tile * N_EXPERTS + e]
        hbm_row = off_ref[tile * N_EXPERTS + e]
        done = 0
        for size in PIECE_SIZES:
            bit = c8 & size

            @pl.when(bit != 0)
            def _(done=done, size=size):
                action(make_copy(pl.multiple_of(stage_row + done, PIECE_ALIGN),
                                 pl.multiple_of(hbm_row + done, PIECE_ALIGN), size))

            done = done + bit
        return stage_row + c8

    lax.fori_loop(0, N_EXPERTS, per_expert, 0)


def _slot_rows(slot_ref):
    return [slot_ref[s:s + 1, :] for s in range(TOP_K)]


def _one_hot_rows(slots, r0):
    rio = lax.broadcasted_iota(I32, (STAGE_CHUNK, MOE_TM), 0) + r0
    pick = jnp.zeros((STAGE_CHUNK, MOE_TM), F32)
    for s in range(TOP_K):
        pick = jnp.where(rio == slots[s], 1.0, pick)
    return pick.astype(BF16)


def _dispatch_kernel(off_ref, cnt_ref, end_ref, slot_ref, wt_ref, x1_ref, xs_hbm, stage, zeros, sem, zero_sem):
    i = pl.program_id(0)
    n_tiles = pl.num_programs(0)
    buf = i % 2

    @pl.when(i == 0)
    def _():
        zeros[...] = jnp.zeros_like(zeros)

        def tail_copies(action):
            def per_expert(e, carry):
                last_piece = (n_tiles - 1) * N_EXPERTS + e
                start = off_ref[last_piece] + cnt_ref[last_piece]
                tail = end_ref[e] - start
                done = 0
                for size in TAIL_SIZES:
                    bit = tail & size

                    @pl.when(bit != 0)
                    def _(done=done, size=size):
                        action(pltpu.make_async_copy(
                            zeros.at[pl.ds(0, size)],
                            xs_hbm.at[pl.ds(pl.multiple_of(start + done, PIECE_ALIGN), size)], zero_sem))

                    done = done + bit
                return carry

            lax.fori_loop(0, N_EXPERTS, per_expert, 0)

            def per_block(b, carry):
                action(pltpu.make_async_copy(
                    zeros, xs_hbm.at[pl.ds(pl.multiple_of(b * ROW_BLOCK, ROW_BLOCK), ROW_BLOCK)], zero_sem))
                return carry

            lax.fori_loop(end_ref[N_EXPERTS - 1] // ROW_BLOCK, xs_hbm.shape[0] // ROW_BLOCK, per_block, 0)

        tail_copies(lambda cp: cp.start())
        tail_copies(lambda cp: cp.wait())

    def out_copies(tile, b, action):
        def make_copy(stage_row, hbm_row, size):
            return pltpu.make_async_copy(stage.at[b, pl.ds(stage_row, size)], xs_hbm.at[pl.ds(hbm_row, size)],
                                         sem.at[b])
        _piece_copies(off_ref, cnt_ref, tile, make_copy, action)

    @pl.when(i >= 2)
    def _():
        out_copies(i - 2, buf, lambda cp: cp.wait())

    xb = x1_ref[...].astype(BF16)
    slots = _slot_rows(slot_ref)
    tail_lane = lax.broadcasted_iota(I32, (STAGE_CHUNK, LANE), 1)
    for r0 in range(0, stage.shape[1], STAGE_CHUNK):
        stage[buf, r0:r0 + STAGE_CHUNK, 0:D_MODEL] = _dot(_one_hot_rows(slots, r0), xb).astype(BF16)
        rio = lax.broadcasted_iota(I32, (STAGE_CHUNK, MOE_TM), 0) + r0
        w_sel = jnp.zeros((STAGE_CHUNK, MOE_TM), F32)
        for s in range(TOP_K):
            w_sel = jnp.where(rio == slots[s], wt_ref[s:s + 1, :], w_sel)
        w_row = jnp.sum(w_sel, axis=1, keepdims=True)
        w_a = w_row.astype(BF16).astype(F32)
        tail = jnp.where(tail_lane == 0, w_a, jnp.where(tail_lane == 1, w_row - w_a, 0.0))
        stage[buf, r0:r0 + STAGE_CHUNK, D_MODEL:ROW_W] = tail.astype(BF16)
    out_copies(i, buf, lambda cp: cp.start())

    @pl.when(i == n_tiles - 1)
    def _():
        @pl.when(i >= 1)
        def _():
            out_copies(i - 1, 1 - buf, lambda cp: cp.wait())

        out_copies(i, buf, lambda cp: cp.wait())


def _dispatch(off_flat, cnt_flat, end_flat, slot, wts, x1):
    n_tok = x1.shape[0]
    n_tiles, stage_rows, n_blk, _ = _moe_dims(n_tok)
    n_rows = n_blk * ROW_BLOCK
    return pl.pallas_call(
        _dispatch_kernel,
        grid_spec=pltpu.PrefetchScalarGridSpec(
            num_scalar_prefetch=3,
            grid=(n_tiles,),
            in_specs=[pl.BlockSpec((TOP_K, MOE_TM), lambda i, off, cnt, end: (0, i)),
                      pl.BlockSpec((TOP_K, MOE_TM), lambda i, off, cnt, end: (0, i)),
                      pl.BlockSpec((MOE_TM, D_MODEL), lambda i, off, cnt, end: (i, 0))],
            out_specs=pl.BlockSpec(memory_space=pl.ANY),
            scratch_shapes=[pltpu.VMEM((2, stage_rows, ROW_W), BF16),
                            pltpu.VMEM((ROW_BLOCK, ROW_W), BF16),
                            pltpu.SemaphoreType.DMA((2,)), pltpu.SemaphoreType.DMA(())],
        ),
        out_shape=jax.ShapeDtypeStruct((n_rows, ROW_W), BF16),
        compiler_params=_cparams(("arbitrary",)),
        name="dispatch",
    )(off_flat, cnt_flat, end_flat, slot, wts, x1)


def _combine_kernel(off_ref, cnt_ref, slot_ref, x1_ref, lng_ref, lnb_ref, ys_hbm, out_ref, stage, sem):
    i = pl.program_id(0)
    n_tiles = pl.num_programs(0)
    buf = i % 2

    def in_copies(tile, b, action):
        def make_copy(stage_row, hbm_row, size):
            return pltpu.make_async_copy(ys_hbm.at[pl.ds(hbm_row, size)], stage.at[b, pl.ds(stage_row, size)],
                                         sem.at[b])
        _piece_copies(off_ref, cnt_ref, tile, make_copy, action)

    @pl.when(i == 0)
    def _():
        stage[...] = jnp.zeros_like(stage)
        in_copies(0, 0, lambda cp: cp.start())

    @pl.when(i + 1 < n_tiles)
    def _():
        in_copies(i + 1, 1 - buf, lambda cp: cp.start())

    in_copies(i, buf, lambda cp: cp.wait())

    slots = _slot_rows(slot_ref)
    ffn = jnp.zeros((MOE_TM, D_MODEL), F32)
    for r0 in range(0, stage.shape[1], STAGE_CHUNK):
        ffn = ffn + _dot_tn(_one_hot_rows(slots, r0), stage[buf, r0:r0 + STAGE_CHUNK, :])
    out_ref[...] = _layer_norm(DEEPNORM_ALPHA * x1_ref[...] + ffn, lng_ref[...], lnb_ref[...])


def _combine(off_flat, cnt_flat, slot, x1, ln_g, ln_b, ys):
    n_tok = x1.shape[0]
    n_tiles, stage_rows, _, _ = _moe_dims(n_tok)
    tok = lambda i, off, cnt: (0, i)
    row = lambda i, off, cnt: (i, 0)
    full = lambda shape: pl.BlockSpec(shape, lambda i, off, cnt: (0, 0))
    return pl.pallas_call(
        _combine_kernel,
        grid_spec=pltpu.PrefetchScalarGridSpec(
            num_scalar_prefetch=2,
            grid=(n_tiles,),
            in_specs=[pl.BlockSpec((TOP_K, MOE_TM), tok),
                      pl.BlockSpec((MOE_TM, D_MODEL), row), full((1, D_MODEL)), full((1, D_MODEL)),
                      pl.BlockSpec(memory_space=pl.ANY)],
            out_specs=pl.BlockSpec((MOE_TM, D_MODEL), row),
            scratch_shapes=[pltpu.VMEM((2, stage_rows, D_MODEL), BF16), pltpu.SemaphoreType.DMA((2,))],
        ),
        out_shape=jax.ShapeDtypeStruct((n_tok, D_MODEL), F32),
        compiler_params=_cparams(("arbitrary",)),
        name="combine",
    )(off_flat, cnt_flat, slot, x1, ln_g.reshape(1, -1), ln_b.reshape(1, -1), ys)


CAST_ROWS = 128
FFN_CHUNKS = 4


def _expert_kernel(blk_e_ref, n_used_ref, xs_ref, win_ref, bin_ref, wout_ref, bout_ref, ys_ref, win_bf, wout_bf):
    rb = pl.program_id(0)
    new_expert = (rb == 0) | (blk_e_ref[rb] != blk_e_ref[jnp.maximum(rb - 1, 0)])

    @pl.when((rb < n_used_ref[0]) & new_expert)
    def _():
        for r in range(0, D_MODEL, CAST_ROWS):
            win_bf[r:r + CAST_ROWS, :] = win_ref[0, r:r + CAST_ROWS, :].astype(BF16)
        for r in range(0, D_EXPERT, CAST_ROWS):
            wout_bf[r:r + CAST_ROWS, :] = wout_ref[0, r:r + CAST_ROWS, :].astype(BF16)

    @pl.when(rb < n_used_ref[0])
    def _():
        x = xs_ref[:, 0:D_MODEL]
        w_tail = xs_ref[:, D_MODEL:ROW_W].astype(F32)
        w_row = w_tail[:, 0:1] + w_tail[:, 1:2]
        cw = D_EXPERT // FFN_CHUNKS

        def hidden(j):
            gs = slice(j * cw, (j + 1) * cw)
            ls = slice(D_EXPERT + j * cw, D_EXPERT + (j + 1) * cw)
            return _dot(x, win_bf[:, gs]) + bin_ref[0, :, gs], _dot(x, win_bf[:, ls]) + bin_ref[0, :, ls]

        ahead = hidden(0)
        y = bout_ref[0]
        for j in range(FFN_CHUNKS):
            g, lin = ahead
            if j + 1 < FFN_CHUNKS:
                ahead = hidden(j + 1)
            gate_h = jnp.minimum(g, SWIGLU_LIMIT)
            lin_h = jnp.clip(lin, -SWIGLU_LIMIT, SWIGLU_LIMIT)
            act = gate_h * jax.nn.sigmoid(SWIGLU_ALPHA * gate_h) * (lin_h + 1.0)
            y = y + _dot(act.astype(BF16), wout_bf[j * cw:(j + 1) * cw, :])
        ys_ref[...] = (y * w_row).astype(BF16)

    @pl.when(rb >= n_used_ref[0])
    def _():
        ys_ref[...] = jnp.zeros_like(ys_ref)


def _experts(blk_e, n_used, xs, w_in, b_in, w_out, b_out):
    n_rows = xs.shape[0]
    n_blk = n_rows // ROW_BLOCK
    used = lambda rb, n_used: jnp.maximum(jnp.minimum(rb, n_used[0] - 1), 0)
    rows = lambda rb, blk_e, n_used: (used(rb, n_used), 0)
    per_e = lambda rb, blk_e, n_used: (blk_e[used(rb, n_used)], 0, 0)
    return pl.pallas_call(
        _expert_kernel,
        grid_spec=pltpu.PrefetchScalarGridSpec(
            num_scalar_prefetch=2,
            grid=(n_blk,),
            in_specs=[
                pl.BlockSpec((ROW_BLOCK, ROW_W), rows),
                pl.BlockSpec((1, D_MODEL, 2 * D_EXPERT), per_e),
                pl.BlockSpec((1, 1, 2 * D_EXPERT), per_e),
                pl.BlockSpec((1, D_EXPERT, D_MODEL), per_e),
                pl.BlockSpec((1, 1, D_MODEL), per_e),
            ],
            out_specs=pl.BlockSpec((ROW_BLOCK, D_MODEL), lambda rb, blk_e, n_used: (rb, 0)),
            scratch_shapes=[pltpu.VMEM((D_MODEL, 2 * D_EXPERT), BF16), pltpu.VMEM((D_EXPERT, D_MODEL), BF16)],
        ),
        out_shape=jax.ShapeDtypeStruct((n_rows, D_MODEL), BF16),
        compiler_params=_cparams(("arbitrary",)),
        name="experts",
    )(blk_e, n_used, xs, w_in, b_in.reshape(N_EXPERTS, 1, -1), w_out, b_out.reshape(N_EXPERTS, 1, -1))


def _moe(x1, logits_t, expert_w_in, expert_b_in, expert_w_out, expert_b_out, ln_g, ln_b):
    n_tok = x1.shape[0]
    _, _, n_blk, map_lanes = _moe_dims(n_tok)
    wts, slot, off, cnt, ends, blk_map = _route(logits_t)
    off_flat = off.T.reshape(-1)
    cnt_flat = cnt.T.reshape(-1)
    blk_e = blk_map[0, 0:n_blk]
    n_used = blk_map[0, map_lanes - 1:map_lanes]
    xs = _dispatch(off_flat, cnt_flat, ends.reshape(-1), slot, wts, x1)
    ys = _experts(blk_e, n_used, xs, expert_w_in, expert_b_in, expert_w_out, expert_b_out)
    return _combine(off_flat, cnt_flat, slot, x1, ln_g, ln_b, ys)


def kernel(x, ln1_g, ln1_b, ln2_g, ln2_b, w_in, shift_mix, decay_w0, decay_up, iclr_a0, iclr_up, gate_up, k_k, k_a, r_k, gn_g, gn_b, w_branch_rwkv, w_branch_attn, w_out, router_w, router_b, expert_w_in, expert_b_in, expert_w_out, expert_b_out):
    B, T, D = x.shape
    x2 = x.reshape(B * T, D)
    prw, q, k, vt, gates = _proj(x2, w_in[0].astype(BF16), T)
    y_rwkv = _rwkv(prw, B, T, shift_mix[0], decay_w0[0], decay_up[0], iclr_a0[0], iclr_up[0], gate_up[0],
                   k_k[0], k_a[0], r_k[0], gn_g[0], gn_b[0])
    y_attn = _moba(q, k, vt, B, T)
    x1, logits_t = _merge(x2, y_rwkv, y_attn, gates, w_branch_rwkv[0], w_branch_attn[0], w_out[0],
                          ln1_g[0], ln1_b[0], router_w[0], router_b[0])
    out = _moe(x1, logits_t, expert_w_in[0], expert_b_in[0], expert_w_out[0], expert_b_out[0], ln2_g[0], ln2_b[0])
    return out.reshape(B, T, D)
```

```python
import math

import jax
import jax.numpy as jnp
from jax import lax
from jax.experimental import pallas as pl
from jax.experimental.pallas import tpu as pltpu

F32 = jnp.float32
BF16 = jnp.bfloat16
I32 = jnp.int32
HI = lax.Precision.HIGHEST

D_MODEL = 1024
DEPTH = 1
RWKV_HEAD_DIM = 64
RWKV_DIM = 512
RWKV_HEADS = 8
DECAY_RANK = 64
ICLR_RANK = 64
GATE_RANK = 128
GN_EPS = 64e-5
ATTN_HEAD_DIM = 64
ATTN_DIM = 512
ATTN_HEADS = 8
MOBA_BLOCK = 256
MOBA_TOPK = 3
ROPE_THETA = 500000.0
ROPE_DIM = 16
NEG_INF = -1e30
N_EXPERTS = 32
TOP_K = 4
D_EXPERT = 1024
SWIGLU_LIMIT = 7.0
SWIGLU_ALPHA = 1.702
DEEPNORM_ALPHA = (2.0 * DEPTH) ** 0.25
LN_EPS = 1e-5
RWKV_COLS = 3 * RWKV_DIM + DECAY_RANK + ICLR_RANK + GATE_RANK
ATTN_COLS = 3 * ATTN_DIM
GATE_COLS = 2 * D_MODEL
IN_COLS = RWKV_COLS + ATTN_COLS + GATE_COLS

LANE = 128
SUBLANE = 8
VMEM_LIMIT_BYTES = 56 * 1024 * 1024

CHUNK = 64


def _cparams(sem):
    return pltpu.CompilerParams(dimension_semantics=sem, vmem_limit_bytes=VMEM_LIMIT_BYTES)


def _dot(a, b):
    return jnp.dot(a, b, preferred_element_type=F32)


def _dot_hi(a, b):
    return jnp.dot(a, b, preferred_element_type=F32, precision=HI)


def _dot_nt(a, b, precision=None):
    return lax.dot_general(a, b, (((1,), (1,)), ((), ())), preferred_element_type=F32, precision=precision)


def _dot_tn(a, b, precision=None):
    return lax.dot_general(a, b, (((0,), (0,)), ((), ())), preferred_element_type=F32, precision=precision)


def _bf16_pieces(x, n):
    pieces = []
    for _ in range(n):
        p = x.astype(BF16)
        pieces.append(p)
        x = x - p.astype(F32)
    return pieces


def _dot_exact_rhs(x, b_bf16, n):
    out = None
    for p in _bf16_pieces(x, n):
        d = _dot(p, b_bf16)
        out = d if out is None else out + d
    return out


PROJ_TM = 256
Q_SCALE = math.log2(math.e) * ATTN_HEAD_DIM ** -0.5


def _proj_kernel(x_ref, w_ref, cos_ref, sa_ref, sb_ref, prw_ref, q_ref, k_ref, v_ref, g_ref):
    xb = x_ref[...].astype(BF16)
    prw_ref[...] = _dot(xb, w_ref[:, 0:RWKV_COLS])
    c0 = RWKV_COLS
    cos = cos_ref[...]
    sa = sa_ref[...]
    sb = sb_ref[...]

    def rope(t):
        return t * cos + pltpu.roll(t, ATTN_DIM - ROPE_DIM // 2, 1) * sa + pltpu.roll(t, ROPE_DIM // 2, 1) * sb

    q_ref[...] = (rope(_dot(xb, w_ref[:, c0:c0 + ATTN_DIM])) * Q_SCALE).astype(BF16)
    k_ref[...] = rope(_dot(xb, w_ref[:, c0 + ATTN_DIM:c0 + 2 * ATTN_DIM])).astype(BF16)
    v_ref[...] = _dot(xb, w_ref[:, c0 + 2 * ATTN_DIM:c0 + 3 * ATTN_DIM]).astype(BF16)
    c1 = RWKV_COLS + ATTN_COLS
    g_ref[...] = jax.nn.sigmoid(_dot(xb, w_ref[:, c1:c1 + GATE_COLS])).astype(BF16)


def _rope_tables(T):
    half = ROPE_DIM // 2
    inv_freq = jnp.power(ROPE_THETA, -jnp.arange(0, ROPE_DIM, 2, dtype=F32) / ROPE_DIM)
    ang = jnp.arange(T).astype(F32)[:, None] * inv_freq[None, :]
    cos, sin = jnp.cos(ang), jnp.sin(ang)
    pad = jnp.zeros((T, ATTN_HEAD_DIM - ROPE_DIM), F32)
    cos_h = jnp.concatenate([cos, cos, pad + 1.0], axis=1)
    sa_h = jnp.concatenate([-sin, jnp.zeros((T, half), F32), pad], axis=1)
    sb_h = jnp.concatenate([jnp.zeros((T, half), F32), sin, pad], axis=1)
    tile = lambda t: jnp.tile(t, (1, ATTN_HEADS))
    return tile(cos_h), tile(sa_h), tile(sb_h)


def _proj(x2, w_in_bf, T):
    n_tok = x2.shape[0]
    tm = PROJ_TM
    t_tiles = T // tm
    cos, sa, sb = _rope_tables(T)
    row = lambda i: (i, 0)
    tab = lambda i: (i % t_tiles, 0)
    return pl.pallas_call(
        _proj_kernel,
        grid=(n_tok // tm,),
        in_specs=[
            pl.BlockSpec((tm, D_MODEL), row),
            pl.BlockSpec((D_MODEL, IN_COLS), lambda i: (0, 0), pipeline_mode=pl.Buffered(1)),
            pl.BlockSpec((tm, ATTN_DIM), tab),
            pl.BlockSpec((tm, ATTN_DIM), tab),
            pl.BlockSpec((tm, ATTN_DIM), tab),
        ],
        out_specs=[
            pl.BlockSpec((tm, RWKV_COLS), row),
            pl.BlockSpec((tm, ATTN_DIM), row),
            pl.BlockSpec((tm, ATTN_DIM), row),
            pl.BlockSpec((tm, ATTN_DIM), row),
            pl.BlockSpec((tm, GATE_COLS), row),
        ],
        out_shape=[
            jax.ShapeDtypeStruct((n_tok, RWKV_COLS), F32),
            jax.ShapeDtypeStruct((n_tok, ATTN_DIM), BF16),
            jax.ShapeDtypeStruct((n_tok, ATTN_DIM), BF16),
            jax.ShapeDtypeStruct((n_tok, ATTN_DIM), BF16),
            jax.ShapeDtypeStruct((n_tok, GATE_COLS), BF16),
        ],
        compiler_params=_cparams(("parallel",)),
        name="proj",
    )(x2, w_in_bf, cos, sa, sb)


RWKV_TT = 256


def _rwkv_kernel(p_ref, mix_ref, w0_ref, dup_ref, a0_ref, iup_ref, gup_ref, kk_ref, ka_ref, rk_ref,
                 gng_ref, gnb_ref, ones_ref, y_ref, s_scr, prev_scr):
    H, N, C = RWKV_HEADS, RWKV_HEAD_DIM, CHUNK

    @pl.when(pl.program_id(1) == 0)
    def _():
        s_scr[...] = jnp.zeros_like(s_scr)
        prev_scr[...] = jnp.zeros_like(prev_scr)

    TT = RWKV_TT
    n_chunks = TT // C
    bones = ones_ref[...]
    row = lax.broadcasted_iota(I32, (C, C), 0)
    col = lax.broadcasted_iota(I32, (C, C), 1)
    lower_incl = col <= row
    lower_strict = col < row
    eye = jnp.where(col == row, 1.0, 0.0)
    t_row = lax.broadcasted_iota(I32, (TT, TT), 0)
    t_col = lax.broadcasted_iota(I32, (TT, TT), 1)
    same_chunk = (t_row // C) == (t_col // C)
    chunk_ltri = jnp.where(same_chunk & (t_col <= t_row), 1.0, 0.0).astype(BF16)
    chunk_ones = jnp.where(same_chunk, 1.0, 0.0).astype(BF16)

    p = p_ref[...]
    first_row = lax.broadcasted_iota(I32, (TT, RWKV_COLS), 0) == 0
    prev = jnp.where(first_row, prev_scr[...], pltpu.roll(p, 1, 0))
    prev_scr[...] = p[TT - 1:TT, :]
    ps = p + (prev - p) * mix_ref[...]
    r = ps[:, 0:RWKV_DIM]
    k = ps[:, RWKV_DIM:2 * RWKV_DIM]
    v = ps[:, 2 * RWKV_DIM:3 * RWKV_DIM]
    o = 3 * RWKV_DIM
    xw = ps[:, o:o + DECAY_RANK]
    xa = ps[:, o + DECAY_RANK:o + DECAY_RANK + ICLR_RANK]
    xg = ps[:, o + DECAY_RANK + ICLR_RANK:RWKV_COLS]
    w_raw = w0_ref[...] + _dot(jnp.tanh(xw).astype(BF16), dup_ref[...])
    logw = -math.exp(-0.5) * jax.nn.sigmoid(w_raw)
    a = jax.nn.sigmoid(a0_ref[...] + _dot(xa.astype(BF16), iup_ref[...]))
    g = _dot(jax.nn.sigmoid(xg).astype(BF16), gup_ref[...])
    kk0 = k * kk_ref[...]
    kk = kk0 / jnp.maximum(jnp.sqrt(_dot_exact_rhs(kk0 * kk0, bones, 2)), 1e-12)
    kp = k * (1.0 + (a - 1.0) * ka_ref[...])
    kka = kk * a
    logw_pieces = _bf16_pieces(logw, 3)
    cum = sum(_dot(chunk_ltri, piece) for piece in logw_pieces)
    tot = sum(_dot(chunk_ones, piece) for piece in logw_pieces)
    e_neg = jnp.exp(-cum)
    at = (-kk * jnp.exp(cum - logw)).astype(BF16)
    rt = (r * jnp.exp(cum)).astype(BF16)
    bt = (kka * e_neg).astype(BF16)
    kt = (kp * e_neg).astype(BF16)
    e_end = jnp.exp(tot - cum)
    bh = (kka * e_end).astype(BF16)
    kh = (kp * e_end).astype(BF16)
    e_tot = jnp.exp(tot)
    vb = v.astype(BF16)

    units = [(c, h) for c in range(n_chunks) for h in range(H)]
    blk = lambda t, c, h: t[c * C:(c + 1) * C, h * N:(h + 1) * N]
    lhs = {u: jnp.concatenate([blk(at, *u), blk(rt, *u)], axis=0) for u in units}
    rhs = {u: jnp.concatenate([blk(bt, *u), blk(kt, *u)], axis=0) for u in units}
    aa = {u: _dot_nt(lhs[u], rhs[u]) for u in units}
    a_ab = {u: jnp.where(lower_strict, aa[u][0:C, 0:C], 0.0) for u in units}
    a_kv = {u: jnp.concatenate([jnp.where(lower_strict, aa[u][0:C, C:2 * C], 0.0),
                                jnp.where(lower_incl, aa[u][C:2 * C, C:2 * C], 0.0)], axis=0).astype(BF16)
            for u in units}
    a_rb = {u: jnp.where(lower_incl, aa[u][C:2 * C, 0:C], 0.0).astype(BF16) for u in units}
    akv = {u: _dot(a_kv[u], blk(vb, *u)) for u in units}
    tinv = {u: eye + a_ab[u] for u in units}
    npow = a_ab
    for _ in range(5):
        npb = {u: npow[u].astype(BF16) for u in units}
        npow = {u: _dot(npb[u], npb[u]) for u in units}
        tinv = {u: tinv[u] + _dot(tinv[u].astype(BF16), npow[u].astype(BF16)) for u in units}
    tinv_b = {u: tinv[u].astype(BF16) for u in units}

    state = [s_scr[h] for h in range(H)]
    y_rows = []
    for c in range(n_chunks):
        hs = [(c, h) for h in range(H)]
        ar_s = {u: _dot_nt(lhs[u], state[u[1]].astype(BF16)) for u in hs}
        ub = {u: _dot(tinv_b[u], (ar_s[u][0:C] + akv[u][0:C]).astype(BF16)).astype(BF16) for u in hs}
        ys = [ar_s[u][C:2 * C] + akv[u][C:2 * C] + _dot(a_rb[u], ub[u]) for u in hs]
        upd = {u: _dot_tn(jnp.concatenate([ub[u], blk(vb, *u)], axis=0),
                          jnp.concatenate([blk(bh, *u), blk(kh, *u)], axis=0)) for u in hs}
        state = [state[h] * e_tot[c * C:c * C + 1, h * N:(h + 1) * N] + upd[(c, h)] for h in range(H)]
        y_rows.append(jnp.concatenate(ys, axis=1))
    for h in range(H):
        s_scr[h] = state[h]

    y = jnp.concatenate(y_rows, axis=0)
    mu = _dot_exact_rhs(y, bones, 1) * (1.0 / N)
    yc = y - mu
    var = _dot_exact_rhs(yc * yc, bones, 1) * (1.0 / N)
    yn = yc * lax.rsqrt(var + GN_EPS) * gng_ref[...] + gnb_ref[...]
    bonus = _dot_exact_rhs(r * kp * rk_ref[...], bones, 1) * v
    y_ref[...] = ((yn + bonus) * g).astype(BF16)


def _rwkv(p_rwkv, B, T, shift_mix, decay_w0, decay_up, iclr_a0, iclr_up, gate_up, k_k, k_a, r_k, gn_g, gn_b):
    tt = RWKV_TT
    n_t = T // tt
    head = jnp.arange(RWKV_DIM) // RWKV_HEAD_DIM
    bones = (head[:, None] == head[None, :]).astype(BF16)
    vec = lambda a: a.reshape(1, -1)
    full = lambda shape: pl.BlockSpec(shape, lambda b, j: (0, 0))
    return pl.pallas_call(
        _rwkv_kernel,
        grid=(B, n_t),
        in_specs=[
            pl.BlockSpec((tt, RWKV_COLS), lambda b, j: (b * n_t + j, 0)),
            full((1, RWKV_COLS)), full((1, RWKV_DIM)), full((DECAY_RANK, RWKV_DIM)),
            full((1, RWKV_DIM)), full((ICLR_RANK, RWKV_DIM)), full((GATE_RANK, RWKV_DIM)),
            full((1, RWKV_DIM)), full((1, RWKV_DIM)), full((1, RWKV_DIM)),
            full((1, RWKV_DIM)), full((1, RWKV_DIM)), full((RWKV_DIM, RWKV_DIM)),
        ],
        out_specs=pl.BlockSpec((tt, RWKV_DIM), lambda b, j: (b * n_t + j, 0)),
        out_shape=jax.ShapeDtypeStruct((B * T, RWKV_DIM), BF16),
        scratch_shapes=[
            pltpu.VMEM((RWKV_HEADS, RWKV_HEAD_DIM, RWKV_HEAD_DIM), F32),
            pltpu.VMEM((1, RWKV_COLS), F32),
        ],
        compiler_params=_cparams(("parallel", "arbitrary")),
        name="rwkv",
    )(p_rwkv, vec(shift_mix), vec(decay_w0), decay_up.astype(BF16), vec(iclr_a0), iclr_up.astype(BF16),
      gate_up.astype(BF16), vec(k_k), vec(k_a), vec(r_k), vec(gn_g), vec(gn_b), bones)


MOBA_HP = 4


def _moba_kernel(q_ref, k_ref, v_ref, o_ref, kmean_scr, sel_scr, acc_scr, score_scr):
    blk_sz, dh = MOBA_BLOCK, ATTN_HEAD_DIM
    nb = k_ref.shape[0] // blk_sz
    i = pl.program_id(2)

    @pl.when(i == 0)
    def _():
        for n in range(nb):
            kb = k_ref[n * blk_sz:(n + 1) * blk_sz, :].astype(F32)
            kmean_scr[n:n + 1, :] = jnp.sum(kb, axis=0, keepdims=True) * (1.0 / blk_sz)

    blk = lax.broadcasted_iota(I32, (nb, blk_sz), 0)
    kpos = lax.broadcasted_iota(I32, (blk_sz, blk_sz), 0)
    qpos = lax.broadcasted_iota(I32, (blk_sz, blk_sz), 1)
    own_start = pl.multiple_of(i * blk_sz, blk_sz)
    heads = [slice(hh * dh, (hh + 1) * dh) for hh in range(MOBA_HP)]
    hds = range(MOBA_HP)
    qss = [q_ref[:, hs] for hs in heads]
    kmean_pieces = _bf16_pieces(kmean_scr[...], 3)
    gates = [sum(_dot_nt(piece[:, heads[hh]], qss[hh]) for piece in kmean_pieces) for hh in hds]
    own = [_dot_nt(k_ref[pl.ds(own_start, blk_sz), heads[hh]], qss[hh]) for hh in hds]
    for hh in hds:
        gate = jnp.where(blk < i, gates[hh], NEG_INF)
        rank = jnp.zeros((nb, blk_sz), I32)
        for m in range(nb):
            gm = gate[m:m + 1, :]
            beats = (gm > gate) | ((gm == gate) & (m < blk))
            rank = rank + beats.astype(I32)
        sel_scr[hh] = ((rank < MOBA_TOPK) & (blk < i)).astype(F32)
    stats, ps = [], []
    for hh in hds:
        s = jnp.where(kpos <= qpos, own[hh], NEG_INF)
        m0 = jnp.max(s, axis=0, keepdims=True)
        p = jnp.exp2(s - m0)
        stats += [m0, jnp.sum(p, axis=0, keepdims=True)]
        ps.append(p.astype(BF16))
    pvs = [_dot_tn(v_ref[pl.ds(own_start, blk_sz), heads[hh]], ps[hh]) for hh in hds]
    for hh in hds:
        acc_scr[hh] = pvs[hh]

    def scores(n):
        start = pl.multiple_of(n * blk_sz, blk_sz)
        return [_dot_nt(k_ref[pl.ds(start, blk_sz), heads[hh]], qss[hh]) for hh in hds]

    first = scores(0)
    for hh in hds:
        score_scr[hh] = first[hh]

    def body(n, carry):
        ahead = scores(jnp.minimum(n + 1, i - 1))
        start = pl.multiple_of(n * blk_sz, blk_sz)
        out, ps, alphas = [], [], []
        for hh in hds:
            m_run, l_run = carry[2 * hh], carry[2 * hh + 1]
            s = jnp.where(sel_scr[hh, pl.ds(n, 1), :] > 0.0, score_scr[hh], NEG_INF)
            m_new = jnp.maximum(m_run, jnp.max(s, axis=0, keepdims=True))
            alpha = jnp.exp2(m_run - m_new)
            p = jnp.exp2(s - m_new)
            out += [m_new, alpha * l_run + jnp.sum(p, axis=0, keepdims=True)]
            ps.append(p.astype(BF16))
            alphas.append(alpha)
        pvs = [_dot_tn(v_ref[pl.ds(start, blk_sz), heads[hh]], ps[hh]) for hh in hds]
        for hh in hds:
            acc_scr[hh] = alphas[hh] * acc_scr[hh] + pvs[hh]
            score_scr[hh] = ahead[hh]
        return tuple(out)

    stats = lax.fori_loop(0, i, body, tuple(stats))
    outs = [(acc_scr[hh] / stats[2 * hh + 1]).T for hh in range(MOBA_HP)]
    o_ref[...] = jnp.concatenate(outs, axis=1).astype(BF16)


def _moba(q, k, v, B, T):
    blk_sz = MOBA_BLOCK
    nq = T // blk_sz
    lanes = MOBA_HP * ATTN_HEAD_DIM
    kv_spec = pl.BlockSpec((T, lanes), lambda b, hp, i: (b, hp))
    q_spec = pl.BlockSpec((blk_sz, lanes), lambda b, hp, i: (b * nq + i, hp))
    return pl.pallas_call(
        _moba_kernel,
        grid=(B, ATTN_HEADS // MOBA_HP, nq),
        in_specs=[q_spec, kv_spec, kv_spec],
        out_specs=q_spec,
        out_shape=jax.ShapeDtypeStruct((B * T, ATTN_DIM), BF16),
        scratch_shapes=[
            pltpu.VMEM((T // blk_sz, lanes), F32),
            pltpu.VMEM((MOBA_HP, T // blk_sz, blk_sz), F32),
            pltpu.VMEM((MOBA_HP, ATTN_HEAD_DIM, blk_sz), F32),
            pltpu.VMEM((MOBA_HP, blk_sz, blk_sz), F32),
        ],
        compiler_params=_cparams(("parallel", "parallel", "arbitrary")),
        name="moba",
    )(q, k, v)


MERGE_TM = 512
MERGE_SUB = 128


def _layer_norm(h, g, b):
    mu = jnp.mean(h, axis=-1, keepdims=True)
    hc = h - mu
    var = jnp.mean(hc * hc, axis=-1, keepdims=True)
    return hc * lax.rsqrt(var + LN_EPS) * g + b


def _merge_kernel(x_ref, yr_ref, ya_ref, g_ref, wbr_ref, wba_ref, wo_ref, lng_ref, lnb_ref, rwt_ref, rb_ref,
                  x1_ref, lgt_ref):
    subs = [slice(s * MERGE_SUB, (s + 1) * MERGE_SUB) for s in range(MERGE_TM // MERGE_SUB)]
    yr = [_dot(yr_ref[sl, :], wbr_ref[...]) for sl in subs]
    ya = [_dot(ya_ref[sl, :], wba_ref[...]) for sl in subs]
    merged = [(g_ref[sl, 0:D_MODEL].astype(F32) * yr[s] + g_ref[sl, D_MODEL:2 * D_MODEL].astype(F32) * ya[s])
              .astype(BF16) for s, sl in enumerate(subs)]
    mix = [_dot(m, wo_ref[...]) for m in merged]
    x1 = [_layer_norm(DEEPNORM_ALPHA * x_ref[sl, :] + mix[s], lng_ref[...], lnb_ref[...])
          for s, sl in enumerate(subs)]
    for s, sl in enumerate(subs):
        x1_ref[sl, :] = x1[s]
    for s, sl in enumerate(subs):
        lgt_ref[:, sl] = _dot_nt(rwt_ref[...], x1[s], precision=HI) + rb_ref[...]


def _merge(x2, y_rwkv, y_attn, gates, w_br, w_ba, w_o, ln_g, ln_b, router_w, router_b):
    n_tok = x2.shape[0]
    tm = MERGE_TM
    row = lambda i: (i, 0)
    full = lambda shape: pl.BlockSpec(shape, lambda i: (0, 0))
    return pl.pallas_call(
        _merge_kernel,
        grid=(n_tok // tm,),
        in_specs=[
            pl.BlockSpec((tm, D_MODEL), row), pl.BlockSpec((tm, RWKV_DIM), row), pl.BlockSpec((tm, ATTN_DIM), row),
            pl.BlockSpec((tm, GATE_COLS), row),
            full((RWKV_DIM, D_MODEL)), full((ATTN_DIM, D_MODEL)), full((D_MODEL, D_MODEL)),
            full((1, D_MODEL)), full((1, D_MODEL)), full((N_EXPERTS, D_MODEL)), full((N_EXPERTS, 1)),
        ],
        out_specs=[pl.BlockSpec((tm, D_MODEL), row), pl.BlockSpec((N_EXPERTS, tm), lambda i: (0, i))],
        out_shape=[jax.ShapeDtypeStruct((n_tok, D_MODEL), F32), jax.ShapeDtypeStruct((N_EXPERTS, n_tok), F32)],
        compiler_params=_cparams(("parallel",)),
        name="merge",
    )(x2, y_rwkv, y_attn, gates, w_br.astype(BF16), w_ba.astype(BF16), w_o.astype(BF16),
      ln_g.reshape(1, -1), ln_b.reshape(1, -1), router_w.T, router_b.reshape(-1, 1))


MOE_TM = 512
ROW_BLOCK = 512
PIECE_ALIGN = 2 * SUBLANE
PIECE_SIZES = (512, 256, 128, 64, 32, 16)
TAIL_SIZES = (256, 128, 64, 32, 16)
STAGE_CHUNK = 256
ROW_W = D_MODEL + LANE


def _moe_dims(n_tok):
    n_tiles = n_tok // MOE_TM
    stage_rows = -(-(MOE_TM * TOP_K + N_EXPERTS * (PIECE_ALIGN - 1)) // STAGE_CHUNK) * STAGE_CHUNK
    max_rows = n_tok * TOP_K + n_tiles * N_EXPERTS * (PIECE_ALIGN - 1) + N_EXPERTS * (ROW_BLOCK - 1)
    n_blk = -(-max_rows // ROW_BLOCK)
    map_lanes = -(-(n_blk + 1) // LANE) * LANE
    return n_tiles, stage_rows, n_blk, map_lanes


def _round_up_f32(x, m):
    return jnp.floor((x + (m - 1)) * (1.0 / m)) * m


def _route_kernel(lg_ref, wt_ref, slot_ref, off_ref, cnt_ref, end_ref, map_ref, idx_scr, pos_scr, cnt_scr, off_scr):
    tm = MOE_TM
    n_tiles = cnt_scr.shape[1]
    phase = pl.program_id(0)
    i = pl.program_id(1)
    tok0 = pl.multiple_of(i * tm, tm)
    eio = lax.broadcasted_iota(I32, (N_EXPERTS, tm), 0)
    tile_lane = lax.broadcasted_iota(I32, (N_EXPERTS, n_tiles), 1)
    e_from = lax.broadcasted_iota(I32, (N_EXPERTS, N_EXPERTS), 1)
    e_to = lax.broadcasted_iota(I32, (N_EXPERTS, N_EXPERTS), 0)
    earlier_e = jnp.where(e_from < e_to, 1.0, 0.0)

    @pl.when((phase == 0) & (i == 0))
    def _():
        cnt_scr[...] = jnp.zeros_like(cnt_scr)

    @pl.when(phase == 0)
    def _():
        work = lg_ref[...]
        vals, hots = [], []
        for s in range(TOP_K):
            m = jnp.max(work, axis=0, keepdims=True)
            ix = jnp.min(jnp.where(work == m, eio, N_EXPERTS), axis=0, keepdims=True)
            hot = eio == ix
            idx_scr[s:s + 1, pl.ds(tok0, tm)] = ix
            vals.append(m)
            hots.append(hot)
            work = jnp.where(hot, -jnp.inf, work)
        es = [jnp.exp(v - vals[0]) for v in vals]
        denom = es[0] + es[1] + es[2] + es[3]
        for s in range(TOP_K):
            wt_ref[s:s + 1, :] = es[s] / denom
        multi_f = jnp.where(hots[0] | hots[1] | hots[2] | hots[3], 1.0, 0.0)
        t_from = lax.broadcasted_iota(I32, (tm, tm), 0)
        t_to = lax.broadcasted_iota(I32, (tm, tm), 1)
        before = jnp.where(t_from < t_to, 1.0, 0.0).astype(BF16)
        count = _dot(multi_f.astype(BF16), before)
        for s in range(TOP_K):
            pos_scr[s:s + 1, pl.ds(tok0, tm)] = jnp.sum(
                jnp.where(hots[s], count, 0.0), axis=0, keepdims=True).astype(I32)
        cnt8 = _round_up_f32(jnp.sum(multi_f, axis=1, keepdims=True), PIECE_ALIGN)
        cnt_scr[...] = cnt_scr[...] + jnp.where(tile_lane == i, cnt8, 0.0)

    @pl.when((phase == 1) & (i == 0))
    def _():
        cnt8 = cnt_scr[...]
        tot = _round_up_f32(jnp.sum(cnt8, axis=1, keepdims=True), ROW_BLOCK)
        p_start = _dot_hi(earlier_e, jnp.broadcast_to(tot, (N_EXPERTS, n_tiles)))
        i_from = lax.broadcasted_iota(I32, (n_tiles, n_tiles), 0)
        i_to = lax.broadcasted_iota(I32, (n_tiles, n_tiles), 1)
        earlier_tiles = _dot_hi(cnt8, jnp.where(i_from < i_to, 1.0, 0.0))
        off_scr[...] = p_start + earlier_tiles
        off_ref[...] = off_scr[...].astype(I32)
        cnt_ref[...] = cnt8.astype(I32)
        p_end = p_start[:, 0:1] + tot
        end_ref[...] = p_end.astype(I32)
        lanes = map_ref.shape[1]
        blk_start = lax.broadcasted_iota(I32, (N_EXPERTS, lanes), 1).astype(F32) * ROW_BLOCK
        blk_e = jnp.sum(jnp.where(blk_start >= p_end, 1.0, 0.0), axis=0, keepdims=True)
        blk_e = jnp.minimum(blk_e, N_EXPERTS - 1.0)
        n_used = jnp.max(p_end, axis=0, keepdims=True) * (1.0 / ROW_BLOCK)
        last = lax.broadcasted_iota(I32, (1, lanes), 1) == lanes - 1
        map_ref[...] = jnp.where(last, n_used, blk_e).astype(I32)

    @pl.when(phase == 1)
    def _():
        cnt_col = jnp.sum(jnp.where(tile_lane == i, cnt_scr[...], 0.0), axis=1, keepdims=True)
        local_off = _dot_hi(earlier_e, jnp.broadcast_to(cnt_col, (N_EXPERTS, tm)))
        for s in range(TOP_K):
            hot = eio == idx_scr[s:s + 1, pl.ds(tok0, tm)]
            base = jnp.sum(jnp.where(hot, local_off, 0.0), axis=0, keepdims=True).astype(I32)
            slot_ref[s:s + 1, :] = base + pos_scr[s:s + 1, pl.ds(tok0, tm)]


def _route(logits_t):
    n_tok = logits_t.shape[1]
    n_tiles, _, _, map_lanes = _moe_dims(n_tok)
    tm = MOE_TM
    last = n_tiles - 1
    const = lambda p, i: (0, 0)
    return pl.pallas_call(
        _route_kernel,
        grid=(2, n_tiles),
        in_specs=[pl.BlockSpec((N_EXPERTS, tm), lambda p, i: (0, i * (1 - p) + last * p))],
        out_specs=[
            pl.BlockSpec((TOP_K, tm), lambda p, i: (0, i * (1 - p) + last * p)),
            pl.BlockSpec((TOP_K, tm), lambda p, i: (0, i * p)),
            pl.BlockSpec((N_EXPERTS, n_tiles), const),
            pl.BlockSpec((N_EXPERTS, n_tiles), const),
            pl.BlockSpec((N_EXPERTS, 1), const),
            pl.BlockSpec((1, map_lanes), const),
        ],
        out_shape=[
            jax.ShapeDtypeStruct((TOP_K, n_tok), F32),
            jax.ShapeDtypeStruct((TOP_K, n_tok), I32),
            jax.ShapeDtypeStruct((N_EXPERTS, n_tiles), I32),
            jax.ShapeDtypeStruct((N_EXPERTS, n_tiles), I32),
            jax.ShapeDtypeStruct((N_EXPERTS, 1), I32),
            jax.ShapeDtypeStruct((1, map_lanes), I32),
        ],
        scratch_shapes=[
            pltpu.VMEM((TOP_K, n_tok), I32),
            pltpu.VMEM((TOP_K, n_tok), I32),
            pltpu.VMEM((N_EXPERTS, n_tiles), F32),
            pltpu.VMEM((N_EXPERTS, n_tiles), F32),
        ],
        compiler_params=_cparams(("arbitrary", "arbitrary")),
        name="route",
    )(logits_t)


def _piece_copies(off_ref, cnt_ref, tile, make_copy, action):
    def per_expert(e, stage_row):
        c8 = cnt_ref[tile * N_EXPERTS + e]
        hbm_row = off_ref[tile * N_EXPERTS + e]
        done = 0
        for size in PIECE_SIZES:
            bit = c8 & size

            @pl.when(bit != 0)
            def _(done=done, size=size):
                action(make_copy(pl.multiple_of(stage_row + done, PIECE_ALIGN),
                                 pl.multiple_of(hbm_row + done, PIECE_ALIGN), size))

            done = done + bit
        return stage_row + c8

    lax.fori_loop(0, N_EXPERTS, per_expert, 0)


def _slot_rows(slot_ref):
    return [slot_ref[s:s + 1, :] for s in range(TOP_K)]


def _one_hot_rows(slots, r0):
    rio = lax.broadcasted_iota(I32, (STAGE_CHUNK, MOE_TM), 0) + r0
    pick = jnp.zeros((STAGE_CHUNK, MOE_TM), F32)
    for s in range(TOP_K):
        pick = jnp.where(rio == slots[s], 1.0, pick)
    return pick.astype(BF16)


def _dispatch_kernel(off_ref, cnt_ref, end_ref, slot_ref, wt_ref, x1_ref, xs_hbm, stage, zeros, sem, zero_sem):
    i = pl.program_id(0)
    n_tiles = pl.num_programs(0)
    buf = i % 2

    @pl.when(i == 0)
    def _():
        zeros[...] = jnp.zeros_like(zeros)

        def tail_copies(action):
            def per_expert(e, carry):
                last_piece = (n_tiles - 1) * N_EXPERTS + e
                start = off_ref[last_piece] + cnt_ref[last_piece]
                tail = end_ref[e] - start
                done = 0
                for size in TAIL_SIZES:
                    bit = tail & size

                    @pl.when(bit != 0)
                    def _(done=done, size=size):
                        action(pltpu.make_async_copy(
                            zeros.at[pl.ds(0, size)],
                            xs_hbm.at[pl.ds(pl.multiple_of(start + done, PIECE_ALIGN), size)], zero_sem))

                    done = done + bit
                return carry

            lax.fori_loop(0, N_EXPERTS, per_expert, 0)

            def per_block(b, carry):
                action(pltpu.make_async_copy(
                    zeros, xs_hbm.at[pl.ds(pl.multiple_of(b * ROW_BLOCK, ROW_BLOCK), ROW_BLOCK)], zero_sem))
                return carry

            lax.fori_loop(end_ref[N_EXPERTS - 1] // ROW_BLOCK, xs_hbm.shape[0] // ROW_BLOCK, per_block, 0)

        tail_copies(lambda cp: cp.start())
        tail_copies(lambda cp: cp.wait())

    def out_copies(tile, b, action):
        def make_copy(stage_row, hbm_row, size):
            return pltpu.make_async_copy(stage.at[b, pl.ds(stage_row, size)], xs_hbm.at[pl.ds(hbm_row, size)],
                                         sem.at[b])
        _piece_copies(off_ref, cnt_ref, tile, make_copy, action)

    @pl.when(i >= 2)
    def _():
        out_copies(i - 2, buf, lambda cp: cp.wait())

    xb = x1_ref[...].astype(BF16)
    slots = _slot_rows(slot_ref)
    tail_lane = lax.broadcasted_iota(I32, (STAGE_CHUNK, LANE), 1)
    for r0 in range(0, stage.shape[1], STAGE_CHUNK):
        stage[buf, r0:r0 + STAGE_CHUNK, 0:D_MODEL] = _dot(_one_hot_rows(slots, r0), xb).astype(BF16)
        rio = lax.broadcasted_iota(I32, (STAGE_CHUNK, MOE_TM), 0) + r0
        w_sel = jnp.zeros((STAGE_CHUNK, MOE_TM), F32)
        for s in range(TOP_K):
            w_sel = jnp.where(rio == slots[s], wt_ref[s:s + 1, :], w_sel)
        w_row = jnp.sum(w_sel, axis=1, keepdims=True)
        w_a = w_row.astype(BF16).astype(F32)
        tail = jnp.where(tail_lane == 0, w_a, jnp.where(tail_lane == 1, w_row - w_a, 0.0))
        stage[buf, r0:r0 + STAGE_CHUNK, D_MODEL:ROW_W] = tail.astype(BF16)
    out_copies(i, buf, lambda cp: cp.start())

    @pl.when(i == n_tiles - 1)
    def _():
        @pl.when(i >= 1)
        def _():
            out_copies(i - 1, 1 - buf, lambda cp: cp.wait())

        out_copies(i, buf, lambda cp: cp.wait())


def _dispatch(off_flat, cnt_flat, end_flat, slot, wts, x1):
    n_tok = x1.shape[0]
    n_tiles, stage_rows, n_blk, _ = _moe_dims(n_tok)
    n_rows = n_blk * ROW_BLOCK
    return pl.pallas_call(
        _dispatch_kernel,
        grid_spec=pltpu.PrefetchScalarGridSpec(
            num_scalar_prefetch=3,
            grid=(n_tiles,),
            in_specs=[pl.BlockSpec((TOP_K, MOE_TM), lambda i, off, cnt, end: (0, i)),
                      pl.BlockSpec((TOP_K, MOE_TM), lambda i, off, cnt, end: (0, i)),
                      pl.BlockSpec((MOE_TM, D_MODEL), lambda i, off, cnt, end: (i, 0))],
            out_specs=pl.BlockSpec(memory_space=pl.ANY),
            scratch_shapes=[pltpu.VMEM((2, stage_rows, ROW_W), BF16),
                            pltpu.VMEM((ROW_BLOCK, ROW_W), BF16),
                            pltpu.SemaphoreType.DMA((2,)), pltpu.SemaphoreType.DMA(())],
        ),
        out_shape=jax.ShapeDtypeStruct((n_rows, ROW_W), BF16),
        compiler_params=_cparams(("arbitrary",)),
        name="dispatch",
    )(off_flat, cnt_flat, end_flat, slot, wts, x1)


def _combine_kernel(off_ref, cnt_ref, slot_ref, x1_ref, lng_ref, lnb_ref, ys_hbm, out_ref, stage, sem):
    i = pl.program_id(0)
    n_tiles = pl.num_programs(0)
    buf = i % 2

    def in_copies(tile, b, action):
        def make_copy(stage_row, hbm_row, size):
            return pltpu.make_async_copy(ys_hbm.at[pl.ds(hbm_row, size)], stage.at[b, pl.ds(stage_row, size)],
                                         sem.at[b])
        _piece_copies(off_ref, cnt_ref, tile, make_copy, action)

    @pl.when(i == 0)
    def _():
        stage[...] = jnp.zeros_like(stage)
        in_copies(0, 0, lambda cp: cp.start())

    @pl.when(i + 1 < n_tiles)
    def _():
        in_copies(i + 1, 1 - buf, lambda cp: cp.start())

    in_copies(i, buf, lambda cp: cp.wait())

    slots = _slot_rows(slot_ref)
    ffn = jnp.zeros((MOE_TM, D_MODEL), F32)
    for r0 in range(0, stage.shape[1], STAGE_CHUNK):
        ffn = ffn + _dot_tn(_one_hot_rows(slots, r0), stage[buf, r0:r0 + STAGE_CHUNK, :])
    out_ref[...] = _layer_norm(DEEPNORM_ALPHA * x1_ref[...] + ffn, lng_ref[...], lnb_ref[...])


def _combine(off_flat, cnt_flat, slot, x1, ln_g, ln_b, ys):
    n_tok = x1.shape[0]
    n_tiles, stage_rows, _, _ = _moe_dims(n_tok)
    tok = lambda i, off, cnt: (0, i)
    row = lambda i, off, cnt: (i, 0)
    full = lambda shape: pl.BlockSpec(shape, lambda i, off, cnt: (0, 0))
    return pl.pallas_call(
        _combine_kernel,
        grid_spec=pltpu.PrefetchScalarGridSpec(
            num_scalar_prefetch=2,
            grid=(n_tiles,),
            in_specs=[pl.BlockSpec((TOP_K, MOE_TM), tok),
                      pl.BlockSpec((MOE_TM, D_MODEL), row), full((1, D_MODEL)), full((1, D_MODEL)),
                      pl.BlockSpec(memory_space=pl.ANY)],
            out_specs=pl.BlockSpec((MOE_TM, D_MODEL), row),
            scratch_shapes=[pltpu.VMEM((2, stage_rows, D_MODEL), BF16), pltpu.SemaphoreType.DMA((2,))],
        ),
        out_shape=jax.ShapeDtypeStruct((n_tok, D_MODEL), F32),
        compiler_params=_cparams(("arbitrary",)),
        name="combine",
    )(off_flat, cnt_flat, slot, x1, ln_g.reshape(1, -1), ln_b.reshape(1, -1), ys)


CAST_ROWS = 128
FFN_CHUNKS = 4


def _expert_kernel(blk_e_ref, n_used_ref, xs_ref, win_ref, bin_ref, wout_ref, bout_ref, ys_ref, win_bf, wout_bf):
    rb = pl.program_id(0)
    new_expert = (rb == 0) | (blk_e_ref[rb] != blk_e_ref[jnp.maximum(rb - 1, 0)])

    @pl.when((rb < n_used_ref[0]) & new_expert)
    def _():
        for r in range(0, D_MODEL, CAST_ROWS):
            win_bf[r:r + CAST_ROWS, :] = win_ref[0, r:r + CAST_ROWS, :].astype(BF16)
        for r in range(0, D_EXPERT, CAST_ROWS):
            wout_bf[r:r + CAST_ROWS, :] = wout_ref[0, r:r + CAST_ROWS, :].astype(BF16)

    @pl.when(rb < n_used_ref[0])
    def _():
        x = xs_ref[:, 0:D_MODEL]
        w_tail = xs_ref[:, D_MODEL:ROW_W].astype(F32)
        w_row = w_tail[:, 0:1] + w_tail[:, 1:2]
        cw = D_EXPERT // FFN_CHUNKS

        def hidden(j):
            gs = slice(j * cw, (j + 1) * cw)
            ls = slice(D_EXPERT + j * cw, D_EXPERT + (j + 1) * cw)
            return _dot(x, win_bf[:, gs]) + bin_ref[0, :, gs], _dot(x, win_bf[:, ls]) + bin_ref[0, :, ls]

        ahead = hidden(0)
        y = bout_ref[0]
        for j in range(FFN_CHUNKS):
            g, lin = ahead
            if j + 1 < FFN_CHUNKS:
                ahead = hidden(j + 1)
            gate_h = jnp.minimum(g, SWIGLU_LIMIT)
            lin_h = jnp.clip(lin, -SWIGLU_LIMIT, SWIGLU_LIMIT)
            act = gate_h * jax.nn.sigmoid(SWIGLU_ALPHA * gate_h) * (lin_h + 1.0)
            y = y + _dot(act.astype(BF16), wout_bf[j * cw:(j + 1) * cw, :])
        ys_ref[...] = (y * w_row).astype(BF16)

    @pl.when(rb >= n_used_ref[0])
    def _():
        ys_ref[...] = jnp.zeros_like(ys_ref)


def _experts(blk_e, n_used, xs, w_in, b_in, w_out, b_out):
    n_rows = xs.shape[0]
    n_blk = n_rows // ROW_BLOCK
    used = lambda rb, n_used: jnp.maximum(jnp.minimum(rb, n_used[0] - 1), 0)
    rows = lambda rb, blk_e, n_used: (used(rb, n_used), 0)
    per_e = lambda rb, blk_e, n_used: (blk_e[used(rb, n_used)], 0, 0)
    return pl.pallas_call(
        _expert_kernel,
        grid_spec=pltpu.PrefetchScalarGridSpec(
            num_scalar_prefetch=2,
            grid=(n_blk,),
            in_specs=[
                pl.BlockSpec((ROW_BLOCK, ROW_W), rows),
                pl.BlockSpec((1, D_MODEL, 2 * D_EXPERT), per_e),
                pl.BlockSpec((1, 1, 2 * D_EXPERT), per_e),
                pl.BlockSpec((1, D_EXPERT, D_MODEL), per_e),
                pl.BlockSpec((1, 1, D_MODEL), per_e),
            ],
            out_specs=pl.BlockSpec((ROW_BLOCK, D_MODEL), lambda rb, blk_e, n_used: (rb, 0)),
            scratch_shapes=[pltpu.VMEM((D_MODEL, 2 * D_EXPERT), BF16), pltpu.VMEM((D_EXPERT, D_MODEL), BF16)],
        ),
        out_shape=jax.ShapeDtypeStruct((n_rows, D_MODEL), BF16),
        compiler_params=_cparams(("arbitrary",)),
        name="experts",
    )(blk_e, n_used, xs, w_in, b_in.reshape(N_EXPERTS, 1, -1), w_out, b_out.reshape(N_EXPERTS, 1, -1))


def _moe(x1, logits_t, expert_w_in, expert_b_in, expert_w_out, expert_b_out, ln_g, ln_b):
    n_tok = x1.shape[0]
    _, _, n_blk, map_lanes = _moe_dims(n_tok)
    wts, slot, off, cnt, ends, blk_map = _route(logits_t)
    off_flat = off.T.reshape(-1)
    cnt_flat = cnt.T.reshape(-1)
    blk_e = blk_map[0, 0:n_blk]
    n_used = blk_map[0, map_lanes - 1:map_lanes]
    xs = _dispatch(off_flat, cnt_flat, ends.reshape(-1), slot, wts, x1)
    ys = _experts(blk_e, n_used, xs, expert_w_in, expert_b_in, expert_w_out, expert_b_out)
    return _combine(off_flat, cnt_flat, slot, x1, ln_g, ln_b, ys)


def kernel(x, ln1_g, ln1_b, ln2_g, ln2_b, w_in, shift_mix, decay_w0, decay_up, iclr_a0, iclr_up, gate_up, k_k, k_a, r_k, gn_g, gn_b, w_branch_rwkv, w_branch_attn, w_out, router_w, router_b, expert_w_in, expert_b_in, expert_w_out, expert_b_out):
    B, T, D = x.shape
    x2 = x.reshape(B * T, D)
    prw, q, k, v, gates = _proj(x2, w_in[0].astype(BF16), T)
    y_rwkv = _rwkv(prw, B, T, shift_mix[0], decay_w0[0], decay_up[0], iclr_a0[0], iclr_up[0], gate_up[0],
                   k_k[0], k_a[0], r_k[0], gn_g[0], gn_b[0])
    y_attn = _moba(q, k, v, B, T)
    x1, logits_t = _merge(x2, y_rwkv, y_attn, gates, w_branch_rwkv[0], w_branch_attn[0], w_out[0],
                          ln1_g[0], ln1_b[0], router_w[0], router_b[0])
    out = _moe(x1, logits_t, expert_w_in[0], expert_b_in[0], expert_w_out[0], expert_b_out[0], ln2_g[0], ln2_b[0])
    return out.reshape(B, T, D)
```

```python
import math

import jax
import jax.numpy as jnp
from jax import lax
from jax.experimental import pallas as pl
from jax.experimental.pallas import tpu as pltpu

F32 = jnp.float32
BF16 = jnp.bfloat16
I32 = jnp.int32
HI = lax.Precision.HIGHEST

D_MODEL = 1024
DEPTH = 1
RWKV_HEAD_DIM = 64
RWKV_DIM = 512
RWKV_HEADS = 8
DECAY_RANK = 64
ICLR_RANK = 64
GATE_RANK = 128
GN_EPS = 64e-5
ATTN_HEAD_DIM = 64
ATTN_DIM = 512
ATTN_HEADS = 8
MOBA_BLOCK = 256
MOBA_TOPK = 3
ROPE_THETA = 500000.0
ROPE_DIM = 16
NEG_INF = -1e30
N_EXPERTS = 32
TOP_K = 4
D_EXPERT = 1024
SWIGLU_LIMIT = 7.0
SWIGLU_ALPHA = 1.702
DEEPNORM_ALPHA = (2.0 * DEPTH) ** 0.25
LN_EPS = 1e-5
RWKV_COLS = 3 * RWKV_DIM + DECAY_RANK + ICLR_RANK + GATE_RANK
ATTN_COLS = 3 * ATTN_DIM
GATE_COLS = 2 * D_MODEL
IN_COLS = RWKV_COLS + ATTN_COLS + GATE_COLS

LANE = 128
SUBLANE = 8
VMEM_LIMIT_BYTES = 56 * 1024 * 1024

CHUNK = 64


def _cparams(sem):
    return pltpu.CompilerParams(dimension_semantics=sem, vmem_limit_bytes=VMEM_LIMIT_BYTES)


def _dot(a, b):
    return jnp.dot(a, b, preferred_element_type=F32)


def _dot_hi(a, b):
    return jnp.dot(a, b, preferred_element_type=F32, precision=HI)


def _dot_nt(a, b, precision=None):
    return lax.dot_general(a, b, (((1,), (1,)), ((), ())), preferred_element_type=F32, precision=precision)


def _dot_tn(a, b, precision=None):
    return lax.dot_general(a, b, (((0,), (0,)), ((), ())), preferred_element_type=F32, precision=precision)


def _bf16_pieces(x, n):
    pieces = []
    for _ in range(n):
        p = x.astype(BF16)
        pieces.append(p)
        x = x - p.astype(F32)
    return pieces


def _dot_exact_rhs(x, b_bf16, n):
    out = None
    for p in _bf16_pieces(x, n):
        d = _dot(p, b_bf16)
        out = d if out is None else out + d
    return out


PROJ_TM = 256
Q_SCALE = math.log2(math.e) * ATTN_HEAD_DIM ** -0.5


def _proj_kernel(x_ref, w_ref, cos_ref, sa_ref, sb_ref, prw_ref, q_ref, k_ref, v_ref, g_ref):
    xb = x_ref[...].astype(BF16)
    prw_ref[...] = _dot(xb, w_ref[:, 0:RWKV_COLS])
    c0 = RWKV_COLS
    cos = cos_ref[...]
    sa = sa_ref[...]
    sb = sb_ref[...]

    def rope(t):
        return t * cos + pltpu.roll(t, ATTN_DIM - ROPE_DIM // 2, 1) * sa + pltpu.roll(t, ROPE_DIM // 2, 1) * sb

    q_ref[...] = (rope(_dot(xb, w_ref[:, c0:c0 + ATTN_DIM])) * Q_SCALE).astype(BF16)
    k_ref[...] = rope(_dot(xb, w_ref[:, c0 + ATTN_DIM:c0 + 2 * ATTN_DIM])).astype(BF16)
    v_ref[...] = _dot(xb, w_ref[:, c0 + 2 * ATTN_DIM:c0 + 3 * ATTN_DIM]).astype(BF16)
    c1 = RWKV_COLS + ATTN_COLS
    g_ref[...] = jax.nn.sigmoid(_dot(xb, w_ref[:, c1:c1 + GATE_COLS])).astype(BF16)


def _rope_tables(T):
    half = ROPE_DIM // 2
    inv_freq = jnp.power(ROPE_THETA, -jnp.arange(0, ROPE_DIM, 2, dtype=F32) / ROPE_DIM)
    ang = jnp.arange(T).astype(F32)[:, None] * inv_freq[None, :]
    cos, sin = jnp.cos(ang), jnp.sin(ang)
    pad = jnp.zeros((T, ATTN_HEAD_DIM - ROPE_DIM), F32)
    cos_h = jnp.concatenate([cos, cos, pad + 1.0], axis=1)
    sa_h = jnp.concatenate([-sin, jnp.zeros((T, half), F32), pad], axis=1)
    sb_h = jnp.concatenate([jnp.zeros((T, half), F32), sin, pad], axis=1)
    tile = lambda t: jnp.tile(t, (1, ATTN_HEADS))
    return tile(cos_h), tile(sa_h), tile(sb_h)


def _proj(x2, w_in_bf, T):
    n_tok = x2.shape[0]
    tm = PROJ_TM
    t_tiles = T // tm
    cos, sa, sb = _rope_tables(T)
    row = lambda i: (i, 0)
    tab = lambda i: (i % t_tiles, 0)
    return pl.pallas_call(
        _proj_kernel,
        grid=(n_tok // tm,),
        in_specs=[
            pl.BlockSpec((tm, D_MODEL), row),
            pl.BlockSpec((D_MODEL, IN_COLS), lambda i: (0, 0), pipeline_mode=pl.Buffered(1)),
            pl.BlockSpec((tm, ATTN_DIM), tab),
            pl.BlockSpec((tm, ATTN_DIM), tab),
            pl.BlockSpec((tm, ATTN_DIM), tab),
        ],
        out_specs=[
            pl.BlockSpec((tm, RWKV_COLS), row),
            pl.BlockSpec((tm, ATTN_DIM), row),
            pl.BlockSpec((tm, ATTN_DIM), row),
            pl.BlockSpec((tm, ATTN_DIM), row),
            pl.BlockSpec((tm, GATE_COLS), row),
        ],
        out_shape=[
            jax.ShapeDtypeStruct((n_tok, RWKV_COLS), F32),
            jax.ShapeDtypeStruct((n_tok, ATTN_DIM), BF16),
            jax.ShapeDtypeStruct((n_tok, ATTN_DIM), BF16),
            jax.ShapeDtypeStruct((n_tok, ATTN_DIM), BF16),
            jax.ShapeDtypeStruct((n_tok, GATE_COLS), BF16),
        ],
        compiler_params=_cparams(("parallel",)),
        name="proj",
    )(x2, w_in_bf, cos, sa, sb)


RWKV_TT = 256


def _rwkv_kernel(p_ref, mix_ref, w0_ref, dup_ref, a0_ref, iup_ref, gup_ref, kk_ref, ka_ref, rk_ref,
                 gng_ref, gnb_ref, ones_ref, y_ref, s_scr, prev_scr):
    H, N, C = RWKV_HEADS, RWKV_HEAD_DIM, CHUNK

    @pl.when(pl.program_id(1) == 0)
    def _():
        s_scr[...] = jnp.zeros_like(s_scr)
        prev_scr[...] = jnp.zeros_like(prev_scr)

    TT = RWKV_TT
    n_chunks = TT // C
    bones = ones_ref[...]
    row = lax.broadcasted_iota(I32, (C, C), 0)
    col = lax.broadcasted_iota(I32, (C, C), 1)
    lower_incl = col <= row
    lower_strict = col < row
    eye = jnp.where(col == row, 1.0, 0.0)
    t_row = lax.broadcasted_iota(I32, (TT, TT), 0)
    t_col = lax.broadcasted_iota(I32, (TT, TT), 1)
    same_chunk = (t_row // C) == (t_col // C)
    chunk_ltri = jnp.where(same_chunk & (t_col <= t_row), 1.0, 0.0).astype(BF16)

    p = p_ref[...]
    first_row = lax.broadcasted_iota(I32, (TT, RWKV_COLS), 0) == 0
    prev = jnp.where(first_row, prev_scr[...], pltpu.roll(p, 1, 0))
    prev_scr[...] = p[TT - 1:TT, :]
    ps = p + (prev - p) * mix_ref[...]
    r = ps[:, 0:RWKV_DIM]
    k = ps[:, RWKV_DIM:2 * RWKV_DIM]
    v = ps[:, 2 * RWKV_DIM:3 * RWKV_DIM]
    o = 3 * RWKV_DIM
    xw = ps[:, o:o + DECAY_RANK]
    xa = ps[:, o + DECAY_RANK:o + DECAY_RANK + ICLR_RANK]
    xg = ps[:, o + DECAY_RANK + ICLR_RANK:RWKV_COLS]
    w_raw = w0_ref[...] + _dot(jnp.tanh(xw).astype(BF16), dup_ref[...])
    logw = -math.exp(-0.5) * jax.nn.sigmoid(w_raw)
    a = jax.nn.sigmoid(a0_ref[...] + _dot(xa.astype(BF16), iup_ref[...]))
    g = _dot(jax.nn.sigmoid(xg).astype(BF16), gup_ref[...])
    kk0 = k * kk_ref[...]
    kk = kk0 / jnp.maximum(jnp.sqrt(_dot_exact_rhs(kk0 * kk0, bones, 1)), 1e-12)
    kp = k * (1.0 + (a - 1.0) * ka_ref[...])
    kka = kk * a
    logw_pieces = _bf16_pieces(logw, 3)
    cum = sum(_dot(chunk_ltri, piece) for piece in logw_pieces)
    tot = jnp.concatenate([jnp.broadcast_to(cum[(c + 1) * C - 1:(c + 1) * C, :], (C, RWKV_DIM))
                           for c in range(n_chunks)], axis=0)
    e_neg = jnp.exp(-cum)
    at = (-kk * jnp.exp(cum - logw)).astype(BF16)
    rt = (r * jnp.exp(cum)).astype(BF16)
    bt = (kka * e_neg).astype(BF16)
    kt = (kp * e_neg).astype(BF16)
    e_end = jnp.exp(tot - cum)
    bh = (kka * e_end).astype(BF16)
    kh = (kp * e_end).astype(BF16)
    e_tot = jnp.exp(tot)
    vb = v.astype(BF16)

    units = [(c, h) for c in range(n_chunks) for h in range(H)]
    blk = lambda t, c, h: t[c * C:(c + 1) * C, h * N:(h + 1) * N]
    lhs = {u: jnp.concatenate([blk(at, *u), blk(rt, *u)], axis=0) for u in units}
    rhs = {u: jnp.concatenate([blk(bt, *u), blk(kt, *u)], axis=0) for u in units}
    aa = {u: _dot_nt(lhs[u], rhs[u]) for u in units}
    a_ab = {u: jnp.where(lower_strict, aa[u][0:C, 0:C], 0.0) for u in units}
    a_kv = {u: jnp.concatenate([jnp.where(lower_strict, aa[u][0:C, C:2 * C], 0.0),
                                jnp.where(lower_incl, aa[u][C:2 * C, C:2 * C], 0.0)], axis=0).astype(BF16)
            for u in units}
    a_rb = {u: jnp.where(lower_incl, aa[u][C:2 * C, 0:C], 0.0).astype(BF16) for u in units}
    akv = {u: _dot(a_kv[u], blk(vb, *u)) for u in units}
    tinv = {u: eye + a_ab[u] for u in units}
    npow = a_ab
    for _ in range(5):
        npb = {u: npow[u].astype(BF16) for u in units}
        npow = {u: _dot(npb[u], npb[u]) for u in units}
        tinv = {u: tinv[u] + _dot(tinv[u].astype(BF16), npow[u].astype(BF16)) for u in units}
    tinv_b = {u: tinv[u].astype(BF16) for u in units}

    state = [s_scr[h] for h in range(H)]
    y_rows = []
    for c in range(n_chunks):
        hs = [(c, h) for h in range(H)]
        ar_s = {u: _dot_nt(lhs[u], state[u[1]].astype(BF16)) for u in hs}
        ub = {u: _dot(tinv_b[u], (ar_s[u][0:C] + akv[u][0:C]).astype(BF16)).astype(BF16) for u in hs}
        ys = [ar_s[u][C:2 * C] + akv[u][C:2 * C] + _dot(a_rb[u], ub[u]) for u in hs]
        upd = {u: _dot_tn(jnp.concatenate([ub[u], blk(vb, *u)], axis=0),
                          jnp.concatenate([blk(bh, *u), blk(kh, *u)], axis=0)) for u in hs}
        state = [state[h] * e_tot[c * C:c * C + 1, h * N:(h + 1) * N] + upd[(c, h)] for h in range(H)]
        y_rows.append(jnp.concatenate(ys, axis=1))
    for h in range(H):
        s_scr[h] = state[h]

    y = jnp.concatenate(y_rows, axis=0)
    mu = _dot_exact_rhs(y, bones, 1) * (1.0 / N)
    yc = y - mu
    var = _dot_exact_rhs(yc * yc, bones, 1) * (1.0 / N)
    yn = yc * lax.rsqrt(var + GN_EPS) * gng_ref[...] + gnb_ref[...]
    bonus = _dot_exact_rhs(r * kp * rk_ref[...], bones, 1) * v
    y_ref[...] = ((yn + bonus) * g).astype(BF16)


def _rwkv(p_rwkv, B, T, shift_mix, decay_w0, decay_up, iclr_a0, iclr_up, gate_up, k_k, k_a, r_k, gn_g, gn_b):
    tt = RWKV_TT
    n_t = T // tt
    head = jnp.arange(RWKV_DIM) // RWKV_HEAD_DIM
    bones = (head[:, None] == head[None, :]).astype(BF16)
    vec = lambda a: a.reshape(1, -1)
    full = lambda shape: pl.BlockSpec(shape, lambda b, j: (0, 0))
    return pl.pallas_call(
        _rwkv_kernel,
        grid=(B, n_t),
        in_specs=[
            pl.BlockSpec((tt, RWKV_COLS), lambda b, j: (b * n_t + j, 0)),
            full((1, RWKV_COLS)), full((1, RWKV_DIM)), full((DECAY_RANK, RWKV_DIM)),
            full((1, RWKV_DIM)), full((ICLR_RANK, RWKV_DIM)), full((GATE_RANK, RWKV_DIM)),
            full((1, RWKV_DIM)), full((1, RWKV_DIM)), full((1, RWKV_DIM)),
            full((1, RWKV_DIM)), full((1, RWKV_DIM)), full((RWKV_DIM, RWKV_DIM)),
        ],
        out_specs=pl.BlockSpec((tt, RWKV_DIM), lambda b, j: (b * n_t + j, 0)),
        out_shape=jax.ShapeDtypeStruct((B * T, RWKV_DIM), BF16),
        scratch_shapes=[
            pltpu.VMEM((RWKV_HEADS, RWKV_HEAD_DIM, RWKV_HEAD_DIM), F32),
            pltpu.VMEM((1, RWKV_COLS), F32),
        ],
        compiler_params=_cparams(("parallel", "arbitrary")),
        name="rwkv",
    )(p_rwkv, vec(shift_mix), vec(decay_w0), decay_up.astype(BF16), vec(iclr_a0), iclr_up.astype(BF16),
      gate_up.astype(BF16), vec(k_k), vec(k_a), vec(r_k), vec(gn_g), vec(gn_b), bones)


MOBA_HP = 8


def _moba_kernel(q_ref, k_ref, v_ref, o_ref, kmean_scr, sel_scr, acc_scr, score_scr):
    blk_sz, dh = MOBA_BLOCK, ATTN_HEAD_DIM
    nb = k_ref.shape[0] // blk_sz
    i = pl.program_id(2)

    @pl.when(i == 0)
    def _():
        for n in range(nb):
            kb = k_ref[n * blk_sz:(n + 1) * blk_sz, :].astype(F32)
            kmean_scr[n:n + 1, :] = jnp.sum(kb, axis=0, keepdims=True) * (1.0 / blk_sz)

    blk = lax.broadcasted_iota(I32, (nb, blk_sz), 0)
    kpos = lax.broadcasted_iota(I32, (blk_sz, blk_sz), 0)
    qpos = lax.broadcasted_iota(I32, (blk_sz, blk_sz), 1)
    own_start = pl.multiple_of(i * blk_sz, blk_sz)
    heads = [slice(hh * dh, (hh + 1) * dh) for hh in range(MOBA_HP)]
    hds = range(MOBA_HP)
    qss = [q_ref[:, hs] for hs in heads]
    kmean_pieces = _bf16_pieces(kmean_scr[...], 3)
    gates = [sum(_dot_nt(piece[:, heads[hh]], qss[hh]) for piece in kmean_pieces) for hh in hds]
    own = [_dot_nt(k_ref[pl.ds(own_start, blk_sz), heads[hh]], qss[hh]) for hh in hds]
    for hh in hds:
        gate = jnp.where(blk < i, gates[hh], NEG_INF)
        rank = jnp.zeros((nb, blk_sz), I32)
        for m in range(nb):
            gm = gate[m:m + 1, :]
            beats = (gm > gate) | ((gm == gate) & (m < blk))
            rank = rank + beats.astype(I32)
        sel_scr[hh] = ((rank < MOBA_TOPK) & (blk < i)).astype(F32)
    stats, ps = [], []
    for hh in hds:
        s = jnp.where(kpos <= qpos, own[hh], NEG_INF)
        m0 = jnp.max(s, axis=0, keepdims=True)
        p = jnp.exp2(s - m0)
        stats += [m0, jnp.sum(p, axis=0, keepdims=True)]
        ps.append(p.astype(BF16))
    pvs = [_dot_tn(v_ref[pl.ds(own_start, blk_sz), heads[hh]], ps[hh]) for hh in hds]
    for hh in hds:
        acc_scr[hh] = pvs[hh]

    def scores(n):
        start = pl.multiple_of(n * blk_sz, blk_sz)
        return [_dot_nt(k_ref[pl.ds(start, blk_sz), heads[hh]], qss[hh]) for hh in hds]

    first = scores(0)
    for hh in hds:
        score_scr[hh] = first[hh]

    def body(n, carry):
        ahead = scores(jnp.minimum(n + 1, i - 1))
        start = pl.multiple_of(n * blk_sz, blk_sz)
        out, ps, alphas = [], [], []
        for hh in hds:
            m_run, l_run = carry[2 * hh], carry[2 * hh + 1]
            s = jnp.where(sel_scr[hh, pl.ds(n, 1), :] > 0.0, score_scr[hh], NEG_INF)
            m_new = jnp.maximum(m_run, jnp.max(s, axis=0, keepdims=True))
            alpha = jnp.exp2(m_run - m_new)
            p = jnp.exp2(s - m_new)
            out += [m_new, alpha * l_run + jnp.sum(p, axis=0, keepdims=True)]
            ps.append(p.astype(BF16))
            alphas.append(alpha)
        pvs = [_dot_tn(v_ref[pl.ds(start, blk_sz), heads[hh]], ps[hh]) for hh in hds]
        for hh in hds:
            acc_scr[hh] = alphas[hh] * acc_scr[hh] + pvs[hh]
            score_scr[hh] = ahead[hh]
        return tuple(out)

    stats = lax.fori_loop(0, i, body, tuple(stats))
    outs = [(acc_scr[hh] / stats[2 * hh + 1]).T for hh in range(MOBA_HP)]
    o_ref[...] = jnp.concatenate(outs, axis=1).astype(BF16)


def _moba(q, k, v, B, T):
    blk_sz = MOBA_BLOCK
    nq = T // blk_sz
    lanes = MOBA_HP * ATTN_HEAD_DIM
    kv_spec = pl.BlockSpec((T, lanes), lambda b, hp, i: (b, hp))
    q_spec = pl.BlockSpec((blk_sz, lanes), lambda b, hp, i: (b * nq + i, hp))
    return pl.pallas_call(
        _moba_kernel,
        grid=(B, ATTN_HEADS // MOBA_HP, nq),
        in_specs=[q_spec, kv_spec, kv_spec],
        out_specs=q_spec,
        out_shape=jax.ShapeDtypeStruct((B * T, ATTN_DIM), BF16),
        scratch_shapes=[
            pltpu.VMEM((T // blk_sz, lanes), F32),
            pltpu.VMEM((MOBA_HP, T // blk_sz, blk_sz), F32),
            pltpu.VMEM((MOBA_HP, ATTN_HEAD_DIM, blk_sz), F32),
            pltpu.VMEM((MOBA_HP, blk_sz, blk_sz), F32),
        ],
        compiler_params=_cparams(("parallel", "parallel", "arbitrary")),
        name="moba",
    )(q, k, v)


MERGE_TM = 512
MERGE_SUB = 128


def _layer_norm(h, g, b):
    mu = jnp.mean(h, axis=-1, keepdims=True)
    hc = h - mu
    var = jnp.mean(hc * hc, axis=-1, keepdims=True)
    return hc * lax.rsqrt(var + LN_EPS) * g + b


def _merge_kernel(x_ref, yr_ref, ya_ref, g_ref, wbr_ref, wba_ref, wo_ref, lng_ref, lnb_ref, rwt_ref, rb_ref,
                  x1_ref, lgt_ref):
    subs = [slice(s * MERGE_SUB, (s + 1) * MERGE_SUB) for s in range(MERGE_TM // MERGE_SUB)]
    yr = [_dot(yr_ref[sl, :], wbr_ref[...]) for sl in subs]
    ya = [_dot(ya_ref[sl, :], wba_ref[...]) for sl in subs]
    merged = [(g_ref[sl, 0:D_MODEL].astype(F32) * yr[s] + g_ref[sl, D_MODEL:2 * D_MODEL].astype(F32) * ya[s])
              .astype(BF16) for s, sl in enumerate(subs)]
    mix = [_dot(m, wo_ref[...]) for m in merged]
    x1 = [_layer_norm(DEEPNORM_ALPHA * x_ref[sl, :] + mix[s], lng_ref[...], lnb_ref[...])
          for s, sl in enumerate(subs)]
    for s, sl in enumerate(subs):
        x1_ref[sl, :] = x1[s]
    for s, sl in enumerate(subs):
        lgt_ref[:, sl] = _dot_nt(rwt_ref[...], x1[s], precision=HI) + rb_ref[...]


def _merge(x2, y_rwkv, y_attn, gates, w_br, w_ba, w_o, ln_g, ln_b, router_w, router_b):
    n_tok = x2.shape[0]
    tm = MERGE_TM
    row = lambda i: (i, 0)
    full = lambda shape: pl.BlockSpec(shape, lambda i: (0, 0))
    return pl.pallas_call(
        _merge_kernel,
        grid=(n_tok // tm,),
        in_specs=[
            pl.BlockSpec((tm, D_MODEL), row), pl.BlockSpec((tm, RWKV_DIM), row), pl.BlockSpec((tm, ATTN_DIM), row),
            pl.BlockSpec((tm, GATE_COLS), row),
            full((RWKV_DIM, D_MODEL)), full((ATTN_DIM, D_MODEL)), full((D_MODEL, D_MODEL)),
            full((1, D_MODEL)), full((1, D_MODEL)), full((N_EXPERTS, D_MODEL)), full((N_EXPERTS, 1)),
        ],
        out_specs=[pl.BlockSpec((tm, D_MODEL), row), pl.BlockSpec((N_EXPERTS, tm), lambda i: (0, i))],
        out_shape=[jax.ShapeDtypeStruct((n_tok, D_MODEL), F32), jax.ShapeDtypeStruct((N_EXPERTS, n_tok), F32)],
        compiler_params=_cparams(("parallel",)),
        name="merge",
    )(x2, y_rwkv, y_attn, gates, w_br.astype(BF16), w_ba.astype(BF16), w_o.astype(BF16),
      ln_g.reshape(1, -1), ln_b.reshape(1, -1), router_w.T, router_b.reshape(-1, 1))


MOE_TM = 512
ROW_BLOCK = 512
PIECE_ALIGN = 2 * SUBLANE
PIECE_SIZES = (512, 256, 128, 64, 32, 16)
TAIL_SIZES = (256, 128, 64, 32, 16)
STAGE_CHUNK = 256
ROW_W = D_MODEL + LANE


def _moe_dims(n_tok):
    n_tiles = n_tok // MOE_TM
    stage_rows = -(-(MOE_TM * TOP_K + N_EXPERTS * (PIECE_ALIGN - 1)) // STAGE_CHUNK) * STAGE_CHUNK
    max_rows = n_tok * TOP_K + n_tiles * N_EXPERTS * (PIECE_ALIGN - 1) + N_EXPERTS * (ROW_BLOCK - 1)
    n_blk = -(-max_rows // ROW_BLOCK)
    map_lanes = -(-(n_blk + 1) // LANE) * LANE
    return n_tiles, stage_rows, n_blk, map_lanes


def _round_up_f32(x, m):
    return jnp.floor((x + (m - 1)) * (1.0 / m)) * m


def _route_kernel(lg_ref, wt_ref, slot_ref, off_ref, cnt_ref, end_ref, map_ref, idx_scr, pos_scr, cnt_scr, off_scr):
    tm = MOE_TM
    n_tiles = cnt_scr.shape[1]
    phase = pl.program_id(0)
    i = pl.program_id(1)
    tok0 = pl.multiple_of(i * tm, tm)
    eio = lax.broadcasted_iota(I32, (N_EXPERTS, tm), 0)
    tile_lane = lax.broadcasted_iota(I32, (N_EXPERTS, n_tiles), 1)
    e_from = lax.broadcasted_iota(I32, (N_EXPERTS, N_EXPERTS), 1)
    e_to = lax.broadcasted_iota(I32, (N_EXPERTS, N_EXPERTS), 0)
    earlier_e = jnp.where(e_from < e_to, 1.0, 0.0)

    @pl.when((phase == 0) & (i == 0))
    def _():
        cnt_scr[...] = jnp.zeros_like(cnt_scr)

    @pl.when(phase == 0)
    def _():
        work = lg_ref[...]
        vals, hots = [], []
        for s in range(TOP_K):
            m = jnp.max(work, axis=0, keepdims=True)
            ix = jnp.min(jnp.where(work == m, eio, N_EXPERTS), axis=0, keepdims=True)
            hot = eio == ix
            idx_scr[s:s + 1, pl.ds(tok0, tm)] = ix
            vals.append(m)
            hots.append(hot)
            work = jnp.where(hot, -jnp.inf, work)
        es = [jnp.exp(v - vals[0]) for v in vals]
        denom = es[0] + es[1] + es[2] + es[3]
        for s in range(TOP_K):
            wt_ref[s:s + 1, :] = es[s] / denom
        multi_f = jnp.where(hots[0] | hots[1] | hots[2] | hots[3], 1.0, 0.0)
        t_from = lax.broadcasted_iota(I32, (tm, tm), 0)
        t_to = lax.broadcasted_iota(I32, (tm, tm), 1)
        before = jnp.where(t_from < t_to, 1.0, 0.0).astype(BF16)
        count = _dot(multi_f.astype(BF16), before)
        for s in range(TOP_K):
            pos_scr[s:s + 1, pl.ds(tok0, tm)] = jnp.sum(
                jnp.where(hots[s], count, 0.0), axis=0, keepdims=True).astype(I32)
        cnt8 = _round_up_f32(jnp.sum(multi_f, axis=1, keepdims=True), PIECE_ALIGN)
        cnt_scr[...] = cnt_scr[...] + jnp.where(tile_lane == i, cnt8, 0.0)

    @pl.when((phase == 1) & (i == 0))
    def _():
        cnt8 = cnt_scr[...]
        tot = _round_up_f32(jnp.sum(cnt8, axis=1, keepdims=True), ROW_BLOCK)
        p_start = _dot_hi(earlier_e, jnp.broadcast_to(tot, (N_EXPERTS, n_tiles)))
        i_from = lax.broadcasted_iota(I32, (n_tiles, n_tiles), 0)
        i_to = lax.broadcasted_iota(I32, (n_tiles, n_tiles), 1)
        earlier_tiles = _dot_hi(cnt8, jnp.where(i_from < i_to, 1.0, 0.0))
        off_scr[...] = p_start + earlier_tiles
        off_ref[...] = off_scr[...].astype(I32)
        cnt_ref[...] = cnt8.astype(I32)
        p_end = p_start[:, 0:1] + tot
        end_ref[...] = p_end.astype(I32)
        lanes = map_ref.shape[1]
        blk_start = lax.broadcasted_iota(I32, (N_EXPERTS, lanes), 1).astype(F32) * ROW_BLOCK
        blk_e = jnp.sum(jnp.where(blk_start >= p_end, 1.0, 0.0), axis=0, keepdims=True)
        blk_e = jnp.minimum(blk_e, N_EXPERTS - 1.0)
        n_used = jnp.max(p_end, axis=0, keepdims=True) * (1.0 / ROW_BLOCK)
        last = lax.broadcasted_iota(I32, (1, lanes), 1) == lanes - 1
        map_ref[...] = jnp.where(last, n_used, blk_e).astype(I32)

    @pl.when(phase == 1)
    def _():
        cnt_col = jnp.sum(jnp.where(tile_lane == i, cnt_scr[...], 0.0), axis=1, keepdims=True)
        local_off = _dot_hi(earlier_e, jnp.broadcast_to(cnt_col, (N_EXPERTS, tm)))
        for s in range(TOP_K):
            hot = eio == idx_scr[s:s + 1, pl.ds(tok0, tm)]
            base = jnp.sum(jnp.where(hot, local_off, 0.0), axis=0, keepdims=True).astype(I32)
            slot_ref[s:s + 1, :] = base + pos_scr[s:s + 1, pl.ds(tok0, tm)]


def _route(logits_t):
    n_tok = logits_t.shape[1]
    n_tiles, _, _, map_lanes = _moe_dims(n_tok)
    tm = MOE_TM
    last = n_tiles - 1
    const = lambda p, i: (0, 0)
    return pl.pallas_call(
        _route_kernel,
        grid=(2, n_tiles),
        in_specs=[pl.BlockSpec((N_EXPERTS, tm), lambda p, i: (0, i * (1 - p) + last * p))],
        out_specs=[
            pl.BlockSpec((TOP_K, tm), lambda p, i: (0, i * (1 - p) + last * p)),
            pl.BlockSpec((TOP_K, tm), lambda p, i: (0, i * p)),
            pl.BlockSpec((N_EXPERTS, n_tiles), const),
            pl.BlockSpec((N_EXPERTS, n_tiles), const),
            pl.BlockSpec((N_EXPERTS, 1), const),
            pl.BlockSpec((1, map_lanes), const),
        ],
        out_shape=[
            jax.ShapeDtypeStruct((TOP_K, n_tok), F32),
            jax.ShapeDtypeStruct((TOP_K, n_tok), I32),
            jax.ShapeDtypeStruct((N_EXPERTS, n_tiles), I32),
            jax.ShapeDtypeStruct((N_EXPERTS, n_tiles), I32),
            jax.ShapeDtypeStruct((N_EXPERTS, 1), I32),
            jax.ShapeDtypeStruct((1, map_lanes), I32),
        ],
        scratch_shapes=[
            pltpu.VMEM((TOP_K, n_tok), I32),
            pltpu.VMEM((TOP_K, n_tok), I32),
            pltpu.VMEM((N_EXPERTS, n_tiles), F32),
            pltpu.VMEM((N_EXPERTS, n_tiles), F32),
        ],
        compiler_params=_cparams(("arbitrary", "arbitrary")),
        name="route",
    )(logits_t)


def _piece_copies(off_ref, cnt_ref, tile, make_copy, action):
    def per_expert(e, stage_row):
        c8 = cnt_ref[tile * N_EXPERTS + e]
        hbm_row = off_ref[tile * N_EXPERTS + e]
        done = 0
        for size in PIECE_SIZES:
            bit = c8 & size

            @pl.when(bit != 0)
            def _(done=done, size=size):
                action(make_copy(pl.multiple_of(stage_row + done, PIECE_ALIGN),
                                 pl.multiple_of(hbm_row + done, PIECE_ALIGN), size))

            done = done + bit
        return stage_row + c8

    lax.fori_loop(0, N_EXPERTS, per_expert, 0)


def _slot_rows(slot_ref):
    return [slot_ref[s:s + 1, :] for s in range(TOP_K)]


def _one_hot_rows(slots, r0):
    rio = lax.broadcasted_iota(I32, (STAGE_CHUNK, MOE_TM), 0) + r0
    pick = jnp.zeros((STAGE_CHUNK, MOE_TM), F32)
    for s in range(TOP_K):
        pick = jnp.where(rio == slots[s], 1.0, pick)
    return pick.astype(BF16)


def _dispatch_kernel(off_ref, cnt_ref, end_ref, slot_ref, wt_ref, x1_ref, xs_hbm, stage, zeros, sem, zero_sem):
    i = pl.program_id(0)
    n_tiles = pl.num_programs(0)
    buf = i % 2

    @pl.when(i == 0)
    def _():
        zeros[...] = jnp.zeros_like(zeros)

        def tail_copies(action):
            def per_expert(e, carry):
                last_piece = (n_tiles - 1) * N_EXPERTS + e
                start = off_ref[last_piece] + cnt_ref[last_piece]
                tail = end_ref[e] - start
                done = 0
                for size in TAIL_SIZES:
                    bit = tail & size

                    @pl.when(bit != 0)
                    def _(done=done, size=size):
                        action(pltpu.make_async_copy(
                            zeros.at[pl.ds(0, size)],
                            xs_hbm.at[pl.ds(pl.multiple_of(start + done, PIECE_ALIGN), size)], zero_sem))

                    done = done + bit
                return carry

            lax.fori_loop(0, N_EXPERTS, per_expert, 0)

            def per_block(b, carry):
                action(pltpu.make_async_copy(
                    zeros, xs_hbm.at[pl.ds(pl.multiple_of(b * ROW_BLOCK, ROW_BLOCK), ROW_BLOCK)], zero_sem))
                return carry

            lax.fori_loop(end_ref[N_EXPERTS - 1] // ROW_BLOCK, xs_hbm.shape[0] // ROW_BLOCK, per_block, 0)

        tail_copies(lambda cp: cp.start())
        tail_copies(lambda cp: cp.wait())

    def out_copies(tile, b, action):
        def make_copy(stage_row, hbm_row, size):
            return pltpu.make_async_copy(stage.at[b, pl.ds(stage_row, size)], xs_hbm.at[pl.ds(hbm_row, size)],
                                         sem.at[b])
        _piece_copies(off_ref, cnt_ref, tile, make_copy, action)

    @pl.when(i >= 2)
    def _():
        out_copies(i - 2, buf, lambda cp: cp.wait())

    xb = x1_ref[...].astype(BF16)
    slots = _slot_rows(slot_ref)
    tail_lane = lax.broadcasted_iota(I32, (STAGE_CHUNK, LANE), 1)
    for r0 in range(0, stage.shape[1], STAGE_CHUNK):
        stage[buf, r0:r0 + STAGE_CHUNK, 0:D_MODEL] = _dot(_one_hot_rows(slots, r0), xb).astype(BF16)
        rio = lax.broadcasted_iota(I32, (STAGE_CHUNK, MOE_TM), 0) + r0
        w_sel = jnp.zeros((STAGE_CHUNK, MOE_TM), F32)
        for s in range(TOP_K):
            w_sel = jnp.where(rio == slots[s], wt_ref[s:s + 1, :], w_sel)
        w_row = jnp.sum(w_sel, axis=1, keepdims=True)
        w_a = w_row.astype(BF16).astype(F32)
        tail = jnp.where(tail_lane == 0, w_a, jnp.where(tail_lane == 1, w_row - w_a, 0.0))
        stage[buf, r0:r0 + STAGE_CHUNK, D_MODEL:ROW_W] = tail.astype(BF16)
    out_copies(i, buf, lambda cp: cp.start())

    @pl.when(i == n_tiles - 1)
    def _():
        @pl.when(i >= 1)
        def _():
            out_copies(i - 1, 1 - buf, lambda cp: cp.wait())

        out_copies(i, buf, lambda cp: cp.wait())


def _dispatch(off_flat, cnt_flat, end_flat, slot, wts, x1):
    n_tok = x1.shape[0]
    n_tiles, stage_rows, n_blk, _ = _moe_dims(n_tok)
    n_rows = n_blk * ROW_BLOCK
    return pl.pallas_call(
        _dispatch_kernel,
        grid_spec=pltpu.PrefetchScalarGridSpec(
            num_scalar_prefetch=3,
            grid=(n_tiles,),
            in_specs=[pl.BlockSpec((TOP_K, MOE_TM), lambda i, off, cnt, end: (0, i)),
                      pl.BlockSpec((TOP_K, MOE_TM), lambda i, off, cnt, end: (0, i)),
                      pl.BlockSpec((MOE_TM, D_MODEL), lambda i, off, cnt, end: (i, 0))],
            out_specs=pl.BlockSpec(memory_space=pl.ANY),
            scratch_shapes=[pltpu.VMEM((2, stage_rows, ROW_W), BF16),
                            pltpu.VMEM((ROW_BLOCK, ROW_W), BF16),
                            pltpu.SemaphoreType.DMA((2,)), pltpu.SemaphoreType.DMA(())],
        ),
        out_shape=jax.ShapeDtypeStruct((n_rows, ROW_W), BF16),
        compiler_params=_cparams(("arbitrary",)),
        name="dispatch",
    )(off_flat, cnt_flat, end_flat, slot, wts, x1)


def _combine_kernel(off_ref, cnt_ref, slot_ref, x1_ref, lng_ref, lnb_ref, ys_hbm, out_ref, stage, sem):
    i = pl.program_id(0)
    n_tiles = pl.num_programs(0)
    buf = i % 2

    def in_copies(tile, b, action):
        def make_copy(stage_row, hbm_row, size):
            return pltpu.make_async_copy(ys_hbm.at[pl.ds(hbm_row, size)], stage.at[b, pl.ds(stage_row, size)],
                                         sem.at[b])
        _piece_copies(off_ref, cnt_ref, tile, make_copy, action)

    @pl.when(i == 0)
    def _():
        stage[...] = jnp.zeros_like(stage)
        in_copies(0, 0, lambda cp: cp.start())

    @pl.when(i + 1 < n_tiles)
    def _():
        in_copies(i + 1, 1 - buf, lambda cp: cp.start())

    in_copies(i, buf, lambda cp: cp.wait())

    slots = _slot_rows(slot_ref)
    ffn = jnp.zeros((MOE_TM, D_MODEL), F32)
    for r0 in range(0, stage.shape[1], STAGE_CHUNK):
        ffn = ffn + _dot_tn(_one_hot_rows(slots, r0), stage[buf, r0:r0 + STAGE_CHUNK, :])
    out_ref[...] = _layer_norm(DEEPNORM_ALPHA * x1_ref[...] + ffn, lng_ref[...], lnb_ref[...])


def _combine(off_flat, cnt_flat, slot, x1, ln_g, ln_b, ys):
    n_tok = x1.shape[0]
    n_tiles, stage_rows, _, _ = _moe_dims(n_tok)
    tok = lambda i, off, cnt: (0, i)
    row = lambda i, off, cnt: (i, 0)
    full = lambda shape: pl.BlockSpec(shape, lambda i, off, cnt: (0, 0))
    return pl.pallas_call(
        _combine_kernel,
        grid_spec=pltpu.PrefetchScalarGridSpec(
            num_scalar_prefetch=2,
            grid=(n_tiles,),
            in_specs=[pl.BlockSpec((TOP_K, MOE_TM), tok),
                      pl.BlockSpec((MOE_TM, D_MODEL), row), full((1, D_MODEL)), full((1, D_MODEL)),
                      pl.BlockSpec(memory_space=pl.ANY)],
            out_specs=pl.BlockSpec((MOE_TM, D_MODEL), row),
            scratch_shapes=[pltpu.VMEM((2, stage_rows, D_MODEL), BF16), pltpu.SemaphoreType.DMA((2,))],
        ),
        out_shape=jax.ShapeDtypeStruct((n_tok, D_MODEL), F32),
        compiler_params=_cparams(("arbitrary",)),
        name="combine",
    )(off_flat, cnt_flat, slot, x1, ln_g.reshape(1, -1), ln_b.reshape(1, -1), ys)


CAST_ROWS = 128
FFN_CHUNKS = 4


def _expert_kernel(blk_e_ref, n_used_ref, xs_ref, win_ref, bin_ref, wout_ref, bout_ref, ys_ref, win_bf, wout_bf):
    rb = pl.program_id(0)
    new_expert = (rb == 0) | (blk_e_ref[rb] != blk_e_ref[jnp.maximum(rb - 1, 0)])

    @pl.when((rb < n_used_ref[0]) & new_expert)
    def _():
        for r in range(0, D_MODEL, CAST_ROWS):
            win_bf[r:r + CAST_ROWS, :] = win_ref[0, r:r + CAST_ROWS, :].astype(BF16)
        for r in range(0, D_EXPERT, CAST_ROWS):
            wout_bf[r:r + CAST_ROWS, :] = wout_ref[0, r:r + CAST_ROWS, :].astype(BF16)

    @pl.when(rb < n_used_ref[0])
    def _():
        x = xs_ref[:, 0:D_MODEL]
        w_tail = xs_ref[:, D_MODEL:ROW_W].astype(F32)
        w_row = w_tail[:, 0:1] + w_tail[:, 1:2]
        cw = D_EXPERT // FFN_CHUNKS

        def hidden(j):
            gs = slice(j * cw, (j + 1) * cw)
            ls = slice(D_EXPERT + j * cw, D_EXPERT + (j + 1) * cw)
            return _dot(x, win_bf[:, gs]) + bin_ref[0, :, gs], _dot(x, win_bf[:, ls]) + bin_ref[0, :, ls]

        ahead = hidden(0)
        y = bout_ref[0]
        for j in range(FFN_CHUNKS):
            g, lin = ahead
            if j + 1 < FFN_CHUNKS:
                ahead = hidden(j + 1)
            gate_h = jnp.minimum(g, SWIGLU_LIMIT)
            lin_h = jnp.clip(lin, -SWIGLU_LIMIT, SWIGLU_LIMIT)
            act = gate_h * jax.nn.sigmoid(SWIGLU_ALPHA * gate_h) * (lin_h + 1.0)
            y = y + _dot(act.astype(BF16), wout_bf[j * cw:(j + 1) * cw, :])
        ys_ref[...] = (y * w_row).astype(BF16)

    @pl.when(rb >= n_used_ref[0])
    def _():
        ys_ref[...] = jnp.zeros_like(ys_ref)


def _experts(blk_e, n_used, xs, w_in, b_in, w_out, b_out):
    n_rows = xs.shape[0]
    n_blk = n_rows // ROW_BLOCK
    used = lambda rb, n_used: jnp.maximum(jnp.minimum(rb, n_used[0] - 1), 0)
    rows = lambda rb, blk_e, n_used: (used(rb, n_used), 0)
    per_e = lambda rb, blk_e, n_used: (blk_e[used(rb, n_used)], 0, 0)
    return pl.pallas_call(
        _expert_kernel,
        grid_spec=pltpu.PrefetchScalarGridSpec(
            num_scalar_prefetch=2,
            grid=(n_blk,),
            in_specs=[
                pl.BlockSpec((ROW_BLOCK, ROW_W), rows),
                pl.BlockSpec((1, D_MODEL, 2 * D_EXPERT), per_e),
                pl.BlockSpec((1, 1, 2 * D_EXPERT), per_e),
                pl.BlockSpec((1, D_EXPERT, D_MODEL), per_e),
                pl.BlockSpec((1, 1, D_MODEL), per_e),
            ],
            out_specs=pl.BlockSpec((ROW_BLOCK, D_MODEL), lambda rb, blk_e, n_used: (rb, 0)),
            scratch_shapes=[pltpu.VMEM((D_MODEL, 2 * D_EXPERT), BF16), pltpu.VMEM((D_EXPERT, D_MODEL), BF16)],
        ),
        out_shape=jax.ShapeDtypeStruct((n_rows, D_MODEL), BF16),
        compiler_params=_cparams(("arbitrary",)),
        name="experts",
    )(blk_e, n_used, xs, w_in, b_in.reshape(N_EXPERTS, 1, -1), w_out, b_out.reshape(N_EXPERTS, 1, -1))


def _moe(x1, logits_t, expert_w_in, expert_b_in, expert_w_out, expert_b_out, ln_g, ln_b):
    n_tok = x1.shape[0]
    _, _, n_blk, map_lanes = _moe_dims(n_tok)
    wts, slot, off, cnt, ends, blk_map = _route(logits_t)
    off_flat = off.T.reshape(-1)
    cnt_flat = cnt.T.reshape(-1)
    blk_e = blk_map[0, 0:n_blk]
    n_used = blk_map[0, map_lanes - 1:map_lanes]
    xs = _dispatch(off_flat, cnt_flat, ends.reshape(-1), slot, wts, x1)
    ys = _experts(blk_e, n_used, xs, expert_w_in, expert_b_in, expert_w_out, expert_b_out)
    return _combine(off_flat, cnt_flat, slot, x1, ln_g, ln_b, ys)


def kernel(x, ln1_g, ln1_b, ln2_g, ln2_b, w_in, shift_mix, decay_w0, decay_up, iclr_a0, iclr_up, gate_up, k_k, k_a, r_k, gn_g, gn_b, w_branch_rwkv, w_branch_attn, w_out, router_w, router_b, expert_w_in, expert_b_in, expert_w_out, expert_b_out):
    B, T, D = x.shape
    x2 = x.reshape(B * T, D)
    prw, q, k, v, gates = _proj(x2, w_in[0].astype(BF16), T)
    y_rwkv = _rwkv(prw, B, T, shift_mix[0], decay_w0[0], decay_up[0], iclr_a0[0], iclr_up[0], gate_up[0],
                   k_k[0], k_a[0], r_k[0], gn_g[0], gn_b[0])
    y_attn = _moba(q, k, v, B, T)
    x1, logits_t = _merge(x2, y_rwkv, y_attn, gates, w_branch_rwkv[0], w_branch_attn[0], w_out[0],
                          ln1_g[0], ln1_b[0], router_w[0], router_b[0])
    out = _moe(x1, logits_t, expert_w_in[0], expert_b_in[0], expert_w_out[0], expert_b_out[0], ln2_g[0], ln2_b[0])
    return out.reshape(B, T, D)
```

```python
import math

import jax
import jax.numpy as jnp
from jax import lax
from jax.experimental import pallas as pl
from jax.experimental.pallas import tpu as pltpu

F32 = jnp.float32
BF16 = jnp.bfloat16
I32 = jnp.int32
HI = lax.Precision.HIGHEST

D_MODEL = 1024
DEPTH = 1
RWKV_HEAD_DIM = 64
RWKV_DIM = 512
RWKV_HEADS = 8
DECAY_RANK = 64
ICLR_RANK = 64
GATE_RANK = 128
GN_EPS = 64e-5
ATTN_HEAD_DIM = 64
ATTN_DIM = 512
ATTN_HEADS = 8
MOBA_BLOCK = 256
MOBA_TOPK = 3
ROPE_THETA = 500000.0
ROPE_DIM = 16
NEG_INF = -1e30
N_EXPERTS = 32
TOP_K = 4
D_EXPERT = 1024
SWIGLU_LIMIT = 7.0
SWIGLU_ALPHA = 1.702
DEEPNORM_ALPHA = (2.0 * DEPTH) ** 0.25
LN_EPS = 1e-5
RWKV_COLS = 3 * RWKV_DIM + DECAY_RANK + ICLR_RANK + GATE_RANK
ATTN_COLS = 3 * ATTN_DIM
GATE_COLS = 2 * D_MODEL
IN_COLS = RWKV_COLS + ATTN_COLS + GATE_COLS

LANE = 128
SUBLANE = 8
VMEM_LIMIT_BYTES = 56 * 1024 * 1024

CHUNK = 64


def _cparams(sem):
    return pltpu.CompilerParams(dimension_semantics=sem, vmem_limit_bytes=VMEM_LIMIT_BYTES)


def _dot(a, b):
    return jnp.dot(a, b, preferred_element_type=F32)


def _dot_hi(a, b):
    return jnp.dot(a, b, preferred_element_type=F32, precision=HI)


def _dot_nt(a, b, precision=None):
    return lax.dot_general(a, b, (((1,), (1,)), ((), ())), preferred_element_type=F32, precision=precision)


def _dot_tn(a, b, precision=None):
    return lax.dot_general(a, b, (((0,), (0,)), ((), ())), preferred_element_type=F32, precision=precision)


def _bf16_pieces(x, n):
    pieces = []
    for _ in range(n):
        p = x.astype(BF16)
        pieces.append(p)
        x = x - p.astype(F32)
    return pieces


def _dot_exact_rhs(x, b_bf16, n):
    out = None
    for p in _bf16_pieces(x, n):
        d = _dot(p, b_bf16)
        out = d if out is None else out + d
    return out


PROJ_TM = 512
Q_SCALE = math.log2(math.e) * ATTN_HEAD_DIM ** -0.5


def _proj_kernel(x_ref, w_ref, cos_ref, sa_ref, sb_ref, prw_ref, q_ref, k_ref, v_ref, g_ref):
    xb = x_ref[...].astype(BF16)
    prw_ref[...] = _dot(xb, w_ref[:, 0:RWKV_COLS])
    c0 = RWKV_COLS
    cos = cos_ref[...]
    sa = sa_ref[...]
    sb = sb_ref[...]

    def rope(t):
        return t * cos + pltpu.roll(t, ATTN_DIM - ROPE_DIM // 2, 1) * sa + pltpu.roll(t, ROPE_DIM // 2, 1) * sb

    q_ref[...] = (rope(_dot(xb, w_ref[:, c0:c0 + ATTN_DIM])) * Q_SCALE).astype(BF16)
    k_ref[...] = rope(_dot(xb, w_ref[:, c0 + ATTN_DIM:c0 + 2 * ATTN_DIM])).astype(BF16)
    v_ref[...] = _dot(xb, w_ref[:, c0 + 2 * ATTN_DIM:c0 + 3 * ATTN_DIM]).astype(BF16)
    c1 = RWKV_COLS + ATTN_COLS
    g_ref[...] = jax.nn.sigmoid(_dot(xb, w_ref[:, c1:c1 + GATE_COLS])).astype(BF16)


def _rope_tables(T):
    half = ROPE_DIM // 2
    inv_freq = jnp.power(ROPE_THETA, -jnp.arange(0, ROPE_DIM, 2, dtype=F32) / ROPE_DIM)
    ang = jnp.arange(T).astype(F32)[:, None] * inv_freq[None, :]
    cos, sin = jnp.cos(ang), jnp.sin(ang)
    pad = jnp.zeros((T, ATTN_HEAD_DIM - ROPE_DIM), F32)
    cos_h = jnp.concatenate([cos, cos, pad + 1.0], axis=1)
    sa_h = jnp.concatenate([-sin, jnp.zeros((T, half), F32), pad], axis=1)
    sb_h = jnp.concatenate([jnp.zeros((T, half), F32), sin, pad], axis=1)
    tile = lambda t: jnp.tile(t, (1, ATTN_HEADS))
    return tile(cos_h), tile(sa_h), tile(sb_h)


def _proj(x2, w_in_bf, T):
    n_tok = x2.shape[0]
    tm = PROJ_TM
    t_tiles = T // tm
    cos, sa, sb = _rope_tables(T)
    row = lambda i: (i, 0)
    tab = lambda i: (i % t_tiles, 0)
    return pl.pallas_call(
        _proj_kernel,
        grid=(n_tok // tm,),
        in_specs=[
            pl.BlockSpec((tm, D_MODEL), row),
            pl.BlockSpec((D_MODEL, IN_COLS), lambda i: (0, 0), pipeline_mode=pl.Buffered(1)),
            pl.BlockSpec((tm, ATTN_DIM), tab),
            pl.BlockSpec((tm, ATTN_DIM), tab),
            pl.BlockSpec((tm, ATTN_DIM), tab),
        ],
        out_specs=[
            pl.BlockSpec((tm, RWKV_COLS), row),
            pl.BlockSpec((tm, ATTN_DIM), row),
            pl.BlockSpec((tm, ATTN_DIM), row),
            pl.BlockSpec((tm, ATTN_DIM), row),
            pl.BlockSpec((tm, GATE_COLS), row),
        ],
        out_shape=[
            jax.ShapeDtypeStruct((n_tok, RWKV_COLS), F32),
            jax.ShapeDtypeStruct((n_tok, ATTN_DIM), BF16),
            jax.ShapeDtypeStruct((n_tok, ATTN_DIM), BF16),
            jax.ShapeDtypeStruct((n_tok, ATTN_DIM), BF16),
            jax.ShapeDtypeStruct((n_tok, GATE_COLS), BF16),
        ],
        compiler_params=_cparams(("parallel",)),
        name="proj",
    )(x2, w_in_bf, cos, sa, sb)


RWKV_TT = 256


def _rwkv_kernel(p_ref, mix_ref, w0_ref, dup_ref, a0_ref, iup_ref, gup_ref, kk_ref, ka_ref, rk_ref,
                 gng_ref, gnb_ref, ones_ref, y_ref, s_scr, prev_scr):
    H, N, C = RWKV_HEADS, RWKV_HEAD_DIM, CHUNK

    @pl.when(pl.program_id(1) == 0)
    def _():
        s_scr[...] = jnp.zeros_like(s_scr)
        prev_scr[...] = jnp.zeros_like(prev_scr)

    TT = RWKV_TT
    n_chunks = TT // C
    bones = ones_ref[...]
    row = lax.broadcasted_iota(I32, (C, C), 0)
    col = lax.broadcasted_iota(I32, (C, C), 1)
    lower_incl = col <= row
    lower_strict = col < row
    eye = jnp.where(col == row, 1.0, 0.0)
    t_row = lax.broadcasted_iota(I32, (TT, TT), 0)
    t_col = lax.broadcasted_iota(I32, (TT, TT), 1)
    same_chunk = (t_row // C) == (t_col // C)
    chunk_ltri = jnp.where(same_chunk & (t_col <= t_row), 1.0, 0.0).astype(BF16)

    p = p_ref[...]
    first_row = lax.broadcasted_iota(I32, (TT, RWKV_COLS), 0) == 0
    prev = jnp.where(first_row, prev_scr[...], pltpu.roll(p, 1, 0))
    prev_scr[...] = p[TT - 1:TT, :]
    ps = p + (prev - p) * mix_ref[...]
    r = ps[:, 0:RWKV_DIM]
    k = ps[:, RWKV_DIM:2 * RWKV_DIM]
    v = ps[:, 2 * RWKV_DIM:3 * RWKV_DIM]
    o = 3 * RWKV_DIM
    xw = ps[:, o:o + DECAY_RANK]
    xa = ps[:, o + DECAY_RANK:o + DECAY_RANK + ICLR_RANK]
    xg = ps[:, o + DECAY_RANK + ICLR_RANK:RWKV_COLS]
    w_raw = w0_ref[...] + _dot(jnp.tanh(xw).astype(BF16), dup_ref[...])
    logw = -math.exp(-0.5) * jax.nn.sigmoid(w_raw)
    a = jax.nn.sigmoid(a0_ref[...] + _dot(xa.astype(BF16), iup_ref[...]))
    g = _dot(jax.nn.sigmoid(xg).astype(BF16), gup_ref[...])
    kk0 = k * kk_ref[...]
    kk = kk0 / jnp.maximum(jnp.sqrt(_dot_exact_rhs(kk0 * kk0, bones, 1)), 1e-12)
    kp = k * (1.0 + (a - 1.0) * ka_ref[...])
    kka = kk * a
    logw_pieces = _bf16_pieces(logw, 3)
    cum = sum(_dot(chunk_ltri, piece) for piece in logw_pieces)
    tot = jnp.concatenate([jnp.broadcast_to(cum[(c + 1) * C - 1:(c + 1) * C, :], (C, RWKV_DIM))
                           for c in range(n_chunks)], axis=0)
    e_neg = jnp.exp(-cum)
    at = (-kk * jnp.exp(cum - logw)).astype(BF16)
    rt = (r * jnp.exp(cum)).astype(BF16)
    bt = (kka * e_neg).astype(BF16)
    kt = (kp * e_neg).astype(BF16)
    e_end = jnp.exp(tot - cum)
    bh = (kka * e_end).astype(BF16)
    kh = (kp * e_end).astype(BF16)
    e_tot = jnp.exp(tot)
    vb = v.astype(BF16)

    units = [(c, h) for c in range(n_chunks) for h in range(H)]
    blk = lambda t, c, h: t[c * C:(c + 1) * C, h * N:(h + 1) * N]
    lhs = {u: jnp.concatenate([blk(at, *u), blk(rt, *u)], axis=0) for u in units}
    rhs = {u: jnp.concatenate([blk(bt, *u), blk(kt, *u)], axis=0) for u in units}
    aa = {u: _dot_nt(lhs[u], rhs[u]) for u in units}
    a_ab = {u: jnp.where(lower_strict, aa[u][0:C, 0:C], 0.0) for u in units}
    a_kv = {u: jnp.concatenate([jnp.where(lower_strict, aa[u][0:C, C:2 * C], 0.0),
                                jnp.where(lower_incl, aa[u][C:2 * C, C:2 * C], 0.0)], axis=0).astype(BF16)
            for u in units}
    a_rb = {u: jnp.where(lower_incl, aa[u][C:2 * C, 0:C], 0.0).astype(BF16) for u in units}
    akv = {u: _dot(a_kv[u], blk(vb, *u)) for u in units}
    tinv = {u: eye + a_ab[u] for u in units}
    npow = a_ab
    for _ in range(5):
        npb = {u: npow[u].astype(BF16) for u in units}
        npow = {u: _dot(npb[u], npb[u]) for u in units}
        tinv = {u: tinv[u] + _dot(tinv[u].astype(BF16), npow[u].astype(BF16)) for u in units}
    tinv_b = {u: tinv[u].astype(BF16) for u in units}

    state = [s_scr[h] for h in range(H)]
    y_rows = []
    for c in range(n_chunks):
        hs = [(c, h) for h in range(H)]
        ar_s = {u: _dot_nt(lhs[u], state[u[1]].astype(BF16)) for u in hs}
        ub = {u: _dot(tinv_b[u], (ar_s[u][0:C] + akv[u][0:C]).astype(BF16)).astype(BF16) for u in hs}
        ys = [ar_s[u][C:2 * C] + akv[u][C:2 * C] + _dot(a_rb[u], ub[u]) for u in hs]
        upd = {u: _dot_tn(jnp.concatenate([ub[u], blk(vb, *u)], axis=0),
                          jnp.concatenate([blk(bh, *u), blk(kh, *u)], axis=0)) for u in hs}
        state = [state[h] * e_tot[c * C:c * C + 1, h * N:(h + 1) * N] + upd[(c, h)] for h in range(H)]
        y_rows.append(jnp.concatenate(ys, axis=1))
    for h in range(H):
        s_scr[h] = state[h]

    y = jnp.concatenate(y_rows, axis=0)
    mu = _dot_exact_rhs(y, bones, 1) * (1.0 / N)
    yc = y - mu
    var = _dot_exact_rhs(yc * yc, bones, 1) * (1.0 / N)
    yn = yc * lax.rsqrt(var + GN_EPS) * gng_ref[...] + gnb_ref[...]
    bonus = _dot_exact_rhs(r * kp * rk_ref[...], bones, 1) * v
    y_ref[...] = ((yn + bonus) * g).astype(BF16)


def _rwkv(p_rwkv, B, T, shift_mix, decay_w0, decay_up, iclr_a0, iclr_up, gate_up, k_k, k_a, r_k, gn_g, gn_b):
    tt = RWKV_TT
    n_t = T // tt
    head = jnp.arange(RWKV_DIM) // RWKV_HEAD_DIM
    bones = (head[:, None] == head[None, :]).astype(BF16)
    vec = lambda a: a.reshape(1, -1)
    full = lambda shape: pl.BlockSpec(shape, lambda b, j: (0, 0))
    return pl.pallas_call(
        _rwkv_kernel,
        grid=(B, n_t),
        in_specs=[
            pl.BlockSpec((tt, RWKV_COLS), lambda b, j: (b * n_t + j, 0)),
            full((1, RWKV_COLS)), full((1, RWKV_DIM)), full((DECAY_RANK, RWKV_DIM)),
            full((1, RWKV_DIM)), full((ICLR_RANK, RWKV_DIM)), full((GATE_RANK, RWKV_DIM)),
            full((1, RWKV_DIM)), full((1, RWKV_DIM)), full((1, RWKV_DIM)),
            full((1, RWKV_DIM)), full((1, RWKV_DIM)), full((RWKV_DIM, RWKV_DIM)),
        ],
        out_specs=pl.BlockSpec((tt, RWKV_DIM), lambda b, j: (b * n_t + j, 0)),
        out_shape=jax.ShapeDtypeStruct((B * T, RWKV_DIM), BF16),
        scratch_shapes=[
            pltpu.VMEM((RWKV_HEADS, RWKV_HEAD_DIM, RWKV_HEAD_DIM), F32),
            pltpu.VMEM((1, RWKV_COLS), F32),
        ],
        compiler_params=_cparams(("parallel", "arbitrary")),
        name="rwkv",
    )(p_rwkv, vec(shift_mix), vec(decay_w0), decay_up.astype(BF16), vec(iclr_a0), iclr_up.astype(BF16),
      gate_up.astype(BF16), vec(k_k), vec(k_a), vec(r_k), vec(gn_g), vec(gn_b), bones)


MOBA_HP = 8


def _moba_kernel(q_ref, k_ref, v_ref, o_ref, kmean_scr, sel_scr, acc_scr, score_scr):
    blk_sz, dh = MOBA_BLOCK, ATTN_HEAD_DIM
    nb = k_ref.shape[0] // blk_sz
    i = pl.program_id(2)

    @pl.when(i == 0)
    def _():
        for n in range(nb):
            kb = k_ref[n * blk_sz:(n + 1) * blk_sz, :].astype(F32)
            kmean_scr[n:n + 1, :] = jnp.sum(kb, axis=0, keepdims=True) * (1.0 / blk_sz)

    blk = lax.broadcasted_iota(I32, (nb, blk_sz), 0)
    kpos = lax.broadcasted_iota(I32, (blk_sz, blk_sz), 0)
    qpos = lax.broadcasted_iota(I32, (blk_sz, blk_sz), 1)
    own_start = pl.multiple_of(i * blk_sz, blk_sz)
    heads = [slice(hh * dh, (hh + 1) * dh) for hh in range(MOBA_HP)]
    hds = range(MOBA_HP)
    qss = [q_ref[:, hs] for hs in heads]
    kmean_pieces = _bf16_pieces(kmean_scr[...], 3)
    gates = [sum(_dot_nt(piece[:, heads[hh]], qss[hh]) for piece in kmean_pieces) for hh in hds]
    own = [_dot_nt(k_ref[pl.ds(own_start, blk_sz), heads[hh]], qss[hh]) for hh in hds]
    for hh in hds:
        gate = jnp.where(blk < i, gates[hh], NEG_INF)
        rank = jnp.zeros((nb, blk_sz), I32)
        for m in range(nb):
            gm = gate[m:m + 1, :]
            beats = (gm > gate) | ((gm == gate) & (m < blk))
            rank = rank + beats.astype(I32)
        sel_scr[hh] = ((rank < MOBA_TOPK) & (blk < i)).astype(F32)
    stats, ps = [], []
    for hh in hds:
        s = jnp.where(kpos <= qpos, own[hh], NEG_INF)
        m0 = jnp.max(s, axis=0, keepdims=True)
        p = jnp.exp2(s - m0)
        stats += [m0, jnp.sum(p, axis=0, keepdims=True)]
        ps.append(p.astype(BF16))
    pvs = [_dot_tn(v_ref[pl.ds(own_start, blk_sz), heads[hh]], ps[hh]) for hh in hds]
    for hh in hds:
        acc_scr[hh] = pvs[hh]

    def scores(n):
        start = pl.multiple_of(n * blk_sz, blk_sz)
        return [_dot_nt(k_ref[pl.ds(start, blk_sz), heads[hh]], qss[hh]) for hh in hds]

    first = scores(0)
    for hh in hds:
        score_scr[hh] = first[hh]

    def body(n, carry):
        ahead = scores(jnp.minimum(n + 1, i - 1))
        start = pl.multiple_of(n * blk_sz, blk_sz)
        out, ps, alphas = [], [], []
        for hh in hds:
            m_run, l_run = carry[2 * hh], carry[2 * hh + 1]
            s = jnp.where(sel_scr[hh, pl.ds(n, 1), :] > 0.0, score_scr[hh], NEG_INF)
            m_new = jnp.maximum(m_run, jnp.max(s, axis=0, keepdims=True))
            alpha = jnp.exp2(m_run - m_new)
            p = jnp.exp2(s - m_new)
            out += [m_new, alpha * l_run + jnp.sum(p, axis=0, keepdims=True)]
            ps.append(p.astype(BF16))
            alphas.append(alpha)
        pvs = [_dot_tn(v_ref[pl.ds(start, blk_sz), heads[hh]], ps[hh]) for hh in hds]
        for hh in hds:
            acc_scr[hh] = alphas[hh] * acc_scr[hh] + pvs[hh]
            score_scr[hh] = ahead[hh]
        return tuple(out)

    stats = lax.fori_loop(0, i, body, tuple(stats))
    outs = [(acc_scr[hh] / stats[2 * hh + 1]).T for hh in range(MOBA_HP)]
    o_ref[...] = jnp.concatenate(outs, axis=1).astype(BF16)


def _moba(q, k, v, B, T):
    blk_sz = MOBA_BLOCK
    nq = T // blk_sz
    lanes = MOBA_HP * ATTN_HEAD_DIM
    kv_spec = pl.BlockSpec((T, lanes), lambda b, hp, i: (b, hp))
    q_spec = pl.BlockSpec((blk_sz, lanes), lambda b, hp, i: (b * nq + i, hp))
    return pl.pallas_call(
        _moba_kernel,
        grid=(B, ATTN_HEADS // MOBA_HP, nq),
        in_specs=[q_spec, kv_spec, kv_spec],
        out_specs=q_spec,
        out_shape=jax.ShapeDtypeStruct((B * T, ATTN_DIM), BF16),
        scratch_shapes=[
            pltpu.VMEM((T // blk_sz, lanes), F32),
            pltpu.VMEM((MOBA_HP, T // blk_sz, blk_sz), F32),
            pltpu.VMEM((MOBA_HP, ATTN_HEAD_DIM, blk_sz), F32),
            pltpu.VMEM((MOBA_HP, blk_sz, blk_sz), F32),
        ],
        compiler_params=_cparams(("parallel", "parallel", "arbitrary")),
        name="moba",
    )(q, k, v)


MERGE_TM = 512
MERGE_SUB = 128


def _layer_norm(h, g, b):
    mu = jnp.mean(h, axis=-1, keepdims=True)
    hc = h - mu
    var = jnp.mean(hc * hc, axis=-1, keepdims=True)
    return hc * lax.rsqrt(var + LN_EPS) * g + b


def _merge_kernel(x_ref, yr_ref, ya_ref, g_ref, wbr_ref, wba_ref, wo_ref, lng_ref, lnb_ref, rwt_ref, rb_ref,
                  x1_ref, lgt_ref):
    subs = [slice(s * MERGE_SUB, (s + 1) * MERGE_SUB) for s in range(MERGE_TM // MERGE_SUB)]
    yr = [_dot(yr_ref[sl, :], wbr_ref[...]) for sl in subs]
    ya = [_dot(ya_ref[sl, :], wba_ref[...]) for sl in subs]
    merged = [(g_ref[sl, 0:D_MODEL].astype(F32) * yr[s] + g_ref[sl, D_MODEL:2 * D_MODEL].astype(F32) * ya[s])
              .astype(BF16) for s, sl in enumerate(subs)]
    mix = [_dot(m, wo_ref[...]) for m in merged]
    x1 = [_layer_norm(DEEPNORM_ALPHA * x_ref[sl, :] + mix[s], lng_ref[...], lnb_ref[...])
          for s, sl in enumerate(subs)]
    for s, sl in enumerate(subs):
        x1_ref[sl, :] = x1[s]
    for s, sl in enumerate(subs):
        lgt_ref[:, sl] = _dot_nt(rwt_ref[...], x1[s], precision=HI) + rb_ref[...]


def _merge(x2, y_rwkv, y_attn, gates, w_br, w_ba, w_o, ln_g, ln_b, router_w, router_b):
    n_tok = x2.shape[0]
    tm = MERGE_TM
    row = lambda i: (i, 0)
    full = lambda shape: pl.BlockSpec(shape, lambda i: (0, 0))
    return pl.pallas_call(
        _merge_kernel,
        grid=(n_tok // tm,),
        in_specs=[
            pl.BlockSpec((tm, D_MODEL), row), pl.BlockSpec((tm, RWKV_DIM), row), pl.BlockSpec((tm, ATTN_DIM), row),
            pl.BlockSpec((tm, GATE_COLS), row),
            full((RWKV_DIM, D_MODEL)), full((ATTN_DIM, D_MODEL)), full((D_MODEL, D_MODEL)),
            full((1, D_MODEL)), full((1, D_MODEL)), full((N_EXPERTS, D_MODEL)), full((N_EXPERTS, 1)),
        ],
        out_specs=[pl.BlockSpec((tm, D_MODEL), row), pl.BlockSpec((N_EXPERTS, tm), lambda i: (0, i))],
        out_shape=[jax.ShapeDtypeStruct((n_tok, D_MODEL), F32), jax.ShapeDtypeStruct((N_EXPERTS, n_tok), F32)],
        compiler_params=_cparams(("parallel",)),
        name="merge",
    )(x2, y_rwkv, y_attn, gates, w_br.astype(BF16), w_ba.astype(BF16), w_o.astype(BF16),
      ln_g.reshape(1, -1), ln_b.reshape(1, -1), router_w.T, router_b.reshape(-1, 1))


MOE_TM = 512
ROW_BLOCK = 512
PIECE_ALIGN = 2 * SUBLANE
PIECE_SIZES = (512, 256, 128, 64, 32, 16)
TAIL_SIZES = (256, 128, 64, 32, 16)
STAGE_CHUNK = 256
ROW_W = D_MODEL + LANE


def _moe_dims(n_tok):
    n_tiles = n_tok // MOE_TM
    stage_rows = -(-(MOE_TM * TOP_K + N_EXPERTS * (PIECE_ALIGN - 1)) // STAGE_CHUNK) * STAGE_CHUNK
    max_rows = n_tok * TOP_K + n_tiles * N_EXPERTS * (PIECE_ALIGN - 1) + N_EXPERTS * (ROW_BLOCK - 1)
    n_blk = -(-max_rows // ROW_BLOCK)
    map_lanes = -(-(n_blk + 1) // LANE) * LANE
    return n_tiles, stage_rows, n_blk, map_lanes


def _round_up_f32(x, m):
    return jnp.floor((x + (m - 1)) * (1.0 / m)) * m


def _route_kernel(lg_ref, wt_ref, idx_ref, slot_ref, off_ref, cnt_ref, end_ref, map_ref,
                  idx_scr, pos_scr, cnt_scr, off_scr):
    tm = MOE_TM
    n_tiles = cnt_scr.shape[1]
    phase = pl.program_id(0)
    i = pl.program_id(1)
    tok0 = pl.multiple_of(i * tm, tm)
    eio = lax.broadcasted_iota(I32, (N_EXPERTS, tm), 0)
    tile_lane = lax.broadcasted_iota(I32, (N_EXPERTS, n_tiles), 1)
    e_from = lax.broadcasted_iota(I32, (N_EXPERTS, N_EXPERTS), 1)
    e_to = lax.broadcasted_iota(I32, (N_EXPERTS, N_EXPERTS), 0)
    earlier_e = jnp.where(e_from < e_to, 1.0, 0.0)

    @pl.when((phase == 0) & (i == 0))
    def _():
        cnt_scr[...] = jnp.zeros_like(cnt_scr)

    @pl.when(phase == 0)
    def _():
        work = lg_ref[...]
        vals, hots = [], []
        for s in range(TOP_K):
            m = jnp.max(work, axis=0, keepdims=True)
            ix = jnp.min(jnp.where(work == m, eio, N_EXPERTS), axis=0, keepdims=True)
            hot = eio == ix
            idx_scr[s:s + 1, pl.ds(tok0, tm)] = ix
            idx_ref[s:s + 1, :] = ix
            vals.append(m)
            hots.append(hot)
            work = jnp.where(hot, -jnp.inf, work)
        es = [jnp.exp(v - vals[0]) for v in vals]
        denom = es[0] + es[1] + es[2] + es[3]
        for s in range(TOP_K):
            wt_ref[s:s + 1, :] = es[s] / denom
        multi_f = jnp.where(hots[0] | hots[1] | hots[2] | hots[3], 1.0, 0.0)
        t_from = lax.broadcasted_iota(I32, (tm, tm), 0)
        t_to = lax.broadcasted_iota(I32, (tm, tm), 1)
        before = jnp.where(t_from < t_to, 1.0, 0.0).astype(BF16)
        count = _dot(multi_f.astype(BF16), before)
        for s in range(TOP_K):
            pos_scr[s:s + 1, pl.ds(tok0, tm)] = jnp.sum(
                jnp.where(hots[s], count, 0.0), axis=0, keepdims=True).astype(I32)
        cnt8 = _round_up_f32(jnp.sum(multi_f, axis=1, keepdims=True), PIECE_ALIGN)
        cnt_scr[...] = cnt_scr[...] + jnp.where(tile_lane == i, cnt8, 0.0)

    @pl.when((phase == 1) & (i == 0))
    def _():
        cnt8 = cnt_scr[...]
        tot = _round_up_f32(jnp.sum(cnt8, axis=1, keepdims=True), ROW_BLOCK)
        p_start = _dot_hi(earlier_e, jnp.broadcast_to(tot, (N_EXPERTS, n_tiles)))
        i_from = lax.broadcasted_iota(I32, (n_tiles, n_tiles), 0)
        i_to = lax.broadcasted_iota(I32, (n_tiles, n_tiles), 1)
        earlier_tiles = _dot_hi(cnt8, jnp.where(i_from < i_to, 1.0, 0.0))
        off_scr[...] = p_start + earlier_tiles
        off_ref[...] = off_scr[...].astype(I32)
        cnt_ref[...] = cnt8.astype(I32)
        p_end = p_start[:, 0:1] + tot
        end_ref[...] = p_end.astype(I32)
        lanes = map_ref.shape[1]
        blk_start = lax.broadcasted_iota(I32, (N_EXPERTS, lanes), 1).astype(F32) * ROW_BLOCK
        blk_e = jnp.sum(jnp.where(blk_start >= p_end, 1.0, 0.0), axis=0, keepdims=True)
        blk_e = jnp.minimum(blk_e, N_EXPERTS - 1.0)
        n_used = jnp.max(p_end, axis=0, keepdims=True) * (1.0 / ROW_BLOCK)
        last = lax.broadcasted_iota(I32, (1, lanes), 1) == lanes - 1
        map_ref[...] = jnp.where(last, n_used, blk_e).astype(I32)

    @pl.when(phase == 1)
    def _():
        cnt_col = jnp.sum(jnp.where(tile_lane == i, cnt_scr[...], 0.0), axis=1, keepdims=True)
        local_off = _dot_hi(earlier_e, jnp.broadcast_to(cnt_col, (N_EXPERTS, tm)))
        for s in range(TOP_K):
            hot = eio == idx_scr[s:s + 1, pl.ds(tok0, tm)]
            base = jnp.sum(jnp.where(hot, local_off, 0.0), axis=0, keepdims=True).astype(I32)
            slot_ref[s:s + 1, :] = base + pos_scr[s:s + 1, pl.ds(tok0, tm)]


def _route(logits_t):
    n_tok = logits_t.shape[1]
    n_tiles, _, _, map_lanes = _moe_dims(n_tok)
    tm = MOE_TM
    last = n_tiles - 1
    const = lambda p, i: (0, 0)
    return pl.pallas_call(
        _route_kernel,
        grid=(2, n_tiles),
        in_specs=[pl.BlockSpec((N_EXPERTS, tm), lambda p, i: (0, i * (1 - p) + last * p))],
        out_specs=[
            pl.BlockSpec((TOP_K, tm), lambda p, i: (0, i * (1 - p) + last * p)),
            pl.BlockSpec((TOP_K, tm), lambda p, i: (0, i * (1 - p) + last * p)),
            pl.BlockSpec((TOP_K, tm), lambda p, i: (0, i * p)),
            pl.BlockSpec((N_EXPERTS, n_tiles), const),
            pl.BlockSpec((N_EXPERTS, n_tiles), const),
            pl.BlockSpec((N_EXPERTS, 1), const),
            pl.BlockSpec((1, map_lanes), const),
        ],
        out_shape=[
            jax.ShapeDtypeStruct((TOP_K, n_tok), F32),
            jax.ShapeDtypeStruct((TOP_K, n_tok), I32),
            jax.ShapeDtypeStruct((TOP_K, n_tok), I32),
            jax.ShapeDtypeStruct((N_EXPERTS, n_tiles), I32),
            jax.ShapeDtypeStruct((N_EXPERTS, n_tiles), I32),
            jax.ShapeDtypeStruct((N_EXPERTS, 1), I32),
            jax.ShapeDtypeStruct((1, map_lanes), I32),
        ],
        scratch_shapes=[
            pltpu.VMEM((TOP_K, n_tok), I32),
            pltpu.VMEM((TOP_K, n_tok), I32),
            pltpu.VMEM((N_EXPERTS, n_tiles), F32),
            pltpu.VMEM((N_EXPERTS, n_tiles), F32),
        ],
        compiler_params=_cparams(("arbitrary", "arbitrary")),
        name="route",
    )(logits_t)


def _piece_copies(off_ref, cnt_ref, tile, make_copy, action):
    def per_expert(e, stage_row):
        c8 = cnt_ref[tile * N_EXPERTS + e]
        hbm_row = off_ref[tile * N_EXPERTS + e]
        done = 0
        for size in PIECE_SIZES:
            bit = c8 & size

            @pl.when(bit != 0)
            def _(done=done, size=size):
                action(make_copy(pl.multiple_of(stage_row + done, PIECE_ALIGN),
                                 pl.multiple_of(hbm_row + done, PIECE_ALIGN), size))

            done = done + bit
        return stage_row + c8

    lax.fori_loop(0, N_EXPERTS, per_expert, 0)


def _slot_rows(slot_ref):
    return [slot_ref[s:s + 1, :] for s in range(TOP_K)]


def _one_hot_rows(slots, r0):
    rio = lax.broadcasted_iota(I32, (STAGE_CHUNK, MOE_TM), 0) + r0
    pick = jnp.zeros((STAGE_CHUNK, MOE_TM), F32)
    for s in range(TOP_K):
        pick = jnp.where(rio == slots[s], 1.0, pick)
    return pick.astype(BF16)


def _dispatch_kernel(off_ref, cnt_ref, end_ref, slot_ref, wt_ref, idx_ref, x1_ref, xs_hbm, stage, zeros, sem,
                     zero_sem):
    i = pl.program_id(0)
    n_tiles = pl.num_programs(0)
    buf = i % 2

    @pl.when(i == 0)
    def _():
        zeros[...] = jnp.zeros_like(zeros)

        def tail_copies(action):
            def per_expert(e, carry):
                last_piece = (n_tiles - 1) * N_EXPERTS + e
                start = off_ref[last_piece] + cnt_ref[last_piece]
                tail = end_ref[e] - start
                done = 0
                for size in TAIL_SIZES:
                    bit = tail & size

                    @pl.when(bit != 0)
                    def _(done=done, size=size):
                        action(pltpu.make_async_copy(
                            zeros.at[pl.ds(0, size)],
                            xs_hbm.at[pl.ds(pl.multiple_of(start + done, PIECE_ALIGN), size)], zero_sem))

                    done = done + bit
                return carry

            lax.fori_loop(0, N_EXPERTS, per_expert, 0)

            def per_block(b, carry):
                action(pltpu.make_async_copy(
                    zeros, xs_hbm.at[pl.ds(pl.multiple_of(b * ROW_BLOCK, ROW_BLOCK), ROW_BLOCK)], zero_sem))
                return carry

            lax.fori_loop(end_ref[N_EXPERTS - 1] // ROW_BLOCK, xs_hbm.shape[0] // ROW_BLOCK, per_block, 0)

        tail_copies(lambda cp: cp.start())
        tail_copies(lambda cp: cp.wait())

    def out_copies(tile, b, action):
        def make_copy(stage_row, hbm_row, size):
            return pltpu.make_async_copy(stage.at[b, pl.ds(stage_row, size)], xs_hbm.at[pl.ds(hbm_row, size)],
                                         sem.at[b])
        _piece_copies(off_ref, cnt_ref, tile, make_copy, action)

    @pl.when(i >= 2)
    def _():
        out_copies(i - 2, buf, lambda cp: cp.wait())

    w = wt_ref[...]
    w_a = w.astype(BF16).astype(F32)
    tail_t = jnp.concatenate(
        [w_a, w - w_a, idx_ref[...].astype(F32), jnp.zeros((LANE - 3 * TOP_K, MOE_TM), F32)], axis=0)
    xb = jnp.concatenate([x1_ref[...].astype(BF16), tail_t.T.astype(BF16)], axis=1)
    slots = _slot_rows(slot_ref)
    for r0 in range(0, stage.shape[1], STAGE_CHUNK):
        stage[buf, r0:r0 + STAGE_CHUNK, :] = _dot(_one_hot_rows(slots, r0), xb).astype(BF16)
    out_copies(i, buf, lambda cp: cp.start())

    @pl.when(i == n_tiles - 1)
    def _():
        @pl.when(i >= 1)
        def _():
            out_copies(i - 1, 1 - buf, lambda cp: cp.wait())

        out_copies(i, buf, lambda cp: cp.wait())


def _dispatch(off_flat, cnt_flat, end_flat, slot, wts, idx, x1):
    n_tok = x1.shape[0]
    n_tiles, stage_rows, n_blk, _ = _moe_dims(n_tok)
    n_rows = n_blk * ROW_BLOCK
    return pl.pallas_call(
        _dispatch_kernel,
        grid_spec=pltpu.PrefetchScalarGridSpec(
            num_scalar_prefetch=3,
            grid=(n_tiles,),
            in_specs=[pl.BlockSpec((TOP_K, MOE_TM), lambda i, off, cnt, end: (0, i)),
                      pl.BlockSpec((TOP_K, MOE_TM), lambda i, off, cnt, end: (0, i)),
                      pl.BlockSpec((TOP_K, MOE_TM), lambda i, off, cnt, end: (0, i)),
                      pl.BlockSpec((MOE_TM, D_MODEL), lambda i, off, cnt, end: (i, 0))],
            out_specs=pl.BlockSpec(memory_space=pl.ANY),
            scratch_shapes=[pltpu.VMEM((2, stage_rows, ROW_W), BF16),
                            pltpu.VMEM((ROW_BLOCK, ROW_W), BF16),
                            pltpu.SemaphoreType.DMA((2,)), pltpu.SemaphoreType.DMA(())],
        ),
        out_shape=jax.ShapeDtypeStruct((n_rows, ROW_W), BF16),
        compiler_params=_cparams(("arbitrary",)),
        name="dispatch",
    )(off_flat, cnt_flat, end_flat, slot, wts, idx, x1)


def _combine_kernel(off_ref, cnt_ref, slot_ref, x1_ref, lng_ref, lnb_ref, ys_hbm, out_ref, stage, sem):
    i = pl.program_id(0)
    n_tiles = pl.num_programs(0)
    buf = i % 2

    def in_copies(tile, b, action):
        def make_copy(stage_row, hbm_row, size):
            return pltpu.make_async_copy(ys_hbm.at[pl.ds(hbm_row, size)], stage.at[b, pl.ds(stage_row, size)],
                                         sem.at[b])
        _piece_copies(off_ref, cnt_ref, tile, make_copy, action)

    @pl.when(i == 0)
    def _():
        stage[...] = jnp.zeros_like(stage)
        in_copies(0, 0, lambda cp: cp.start())

    @pl.when(i + 1 < n_tiles)
    def _():
        in_copies(i + 1, 1 - buf, lambda cp: cp.start())

    in_copies(i, buf, lambda cp: cp.wait())

    slots = _slot_rows(slot_ref)
    ffn = jnp.zeros((MOE_TM, D_MODEL), F32)
    for r0 in range(0, stage.shape[1], STAGE_CHUNK):
        ffn = ffn + _dot_tn(_one_hot_rows(slots, r0), stage[buf, r0:r0 + STAGE_CHUNK, :])
    out_ref[...] = _layer_norm(DEEPNORM_ALPHA * x1_ref[...] + ffn, lng_ref[...], lnb_ref[...])


def _combine(off_flat, cnt_flat, slot, x1, ln_g, ln_b, ys):
    n_tok = x1.shape[0]
    n_tiles, stage_rows, _, _ = _moe_dims(n_tok)
    tok = lambda i, off, cnt: (0, i)
    row = lambda i, off, cnt: (i, 0)
    full = lambda shape: pl.BlockSpec(shape, lambda i, off, cnt: (0, 0))
    return pl.pallas_call(
        _combine_kernel,
        grid_spec=pltpu.PrefetchScalarGridSpec(
            num_scalar_prefetch=2,
            grid=(n_tiles,),
            in_specs=[pl.BlockSpec((TOP_K, MOE_TM), tok),
                      pl.BlockSpec((MOE_TM, D_MODEL), row), full((1, D_MODEL)), full((1, D_MODEL)),
                      pl.BlockSpec(memory_space=pl.ANY)],
            out_specs=pl.BlockSpec((MOE_TM, D_MODEL), row),
            scratch_shapes=[pltpu.VMEM((2, stage_rows, D_MODEL), BF16), pltpu.SemaphoreType.DMA((2,))],
        ),
        out_shape=jax.ShapeDtypeStruct((n_tok, D_MODEL), F32),
        compiler_params=_cparams(("arbitrary",)),
        name="combine",
    )(off_flat, cnt_flat, slot, x1, ln_g.reshape(1, -1), ln_b.reshape(1, -1), ys)


CAST_ROWS = 128
FFN_CHUNKS = 4


def _expert_kernel(blk_e_ref, n_used_ref, xs_ref, win_ref, bin_ref, wout_ref, bout_ref, ys_ref, win_bf, wout_bf):
    rb = pl.program_id(0)
    new_expert = (rb == 0) | (blk_e_ref[rb] != blk_e_ref[jnp.maximum(rb - 1, 0)])

    @pl.when((rb < n_used_ref[0]) & new_expert)
    def _():
        for r in range(0, D_MODEL, CAST_ROWS):
            win_bf[r:r + CAST_ROWS, :] = win_ref[0, r:r + CAST_ROWS, :].astype(BF16)
        for r in range(0, D_EXPERT, CAST_ROWS):
            wout_bf[r:r + CAST_ROWS, :] = wout_ref[0, r:r + CAST_ROWS, :].astype(BF16)

    @pl.when(rb < n_used_ref[0])
    def _():
        x = xs_ref[:, 0:D_MODEL]
        tail = xs_ref[:, D_MODEL:ROW_W].astype(F32)
        mine = blk_e_ref[rb].astype(F32)
        w_row = jnp.zeros((tail.shape[0], 1), F32)
        for s in range(TOP_K):
            w_s = tail[:, s:s + 1] + tail[:, TOP_K + s:TOP_K + s + 1]
            w_row = jnp.where(tail[:, 2 * TOP_K + s:2 * TOP_K + s + 1] == mine, w_s, w_row)
        cw = D_EXPERT // FFN_CHUNKS

        def hidden(j):
            gs = slice(j * cw, (j + 1) * cw)
            ls = slice(D_EXPERT + j * cw, D_EXPERT + (j + 1) * cw)
            return _dot(x, win_bf[:, gs]) + bin_ref[0, :, gs], _dot(x, win_bf[:, ls]) + bin_ref[0, :, ls]

        ahead = hidden(0)
        y = bout_ref[0]
        for j in range(FFN_CHUNKS):
            g, lin = ahead
            if j + 1 < FFN_CHUNKS:
                ahead = hidden(j + 1)
            gate_h = jnp.minimum(g, SWIGLU_LIMIT)
            lin_h = jnp.clip(lin, -SWIGLU_LIMIT, SWIGLU_LIMIT)
            act = gate_h * jax.nn.sigmoid(SWIGLU_ALPHA * gate_h) * (lin_h + 1.0)
            y = y + _dot(act.astype(BF16), wout_bf[j * cw:(j + 1) * cw, :])
        ys_ref[...] = (y * w_row).astype(BF16)

    @pl.when(rb >= n_used_ref[0])
    def _():
        ys_ref[...] = jnp.zeros_like(ys_ref)


def _experts(blk_e, n_used, xs, w_in, b_in, w_out, b_out):
    n_rows = xs.shape[0]
    n_blk = n_rows // ROW_BLOCK
    used = lambda rb, n_used: jnp.maximum(jnp.minimum(rb, n_used[0] - 1), 0)
    rows = lambda rb, blk_e, n_used: (used(rb, n_used), 0)
    per_e = lambda rb, blk_e, n_used: (blk_e[used(rb, n_used)], 0, 0)
    return pl.pallas_call(
        _expert_kernel,
        grid_spec=pltpu.PrefetchScalarGridSpec(
            num_scalar_prefetch=2,
            grid=(n_blk,),
            in_specs=[
                pl.BlockSpec((ROW_BLOCK, ROW_W), rows),
                pl.BlockSpec((1, D_MODEL, 2 * D_EXPERT), per_e),
                pl.BlockSpec((1, 1, 2 * D_EXPERT), per_e),
                pl.BlockSpec((1, D_EXPERT, D_MODEL), per_e),
                pl.BlockSpec((1, 1, D_MODEL), per_e),
            ],
            out_specs=pl.BlockSpec((ROW_BLOCK, D_MODEL), lambda rb, blk_e, n_used: (rb, 0)),
            scratch_shapes=[pltpu.VMEM((D_MODEL, 2 * D_EXPERT), BF16), pltpu.VMEM((D_EXPERT, D_MODEL), BF16)],
        ),
        out_shape=jax.ShapeDtypeStruct((n_rows, D_MODEL), BF16),
        compiler_params=_cparams(("arbitrary",)),
        name="experts",
    )(blk_e, n_used, xs, w_in, b_in.reshape(N_EXPERTS, 1, -1), w_out, b_out.reshape(N_EXPERTS, 1, -1))


def _moe(x1, logits_t, expert_w_in, expert_b_in, expert_w_out, expert_b_out, ln_g, ln_b):
    n_tok = x1.shape[0]
    _, _, n_blk, map_lanes = _moe_dims(n_tok)
    wts, idx, slot, off, cnt, ends, blk_map = _route(logits_t)
    off_flat = off.T.reshape(-1)
    cnt_flat = cnt.T.reshape(-1)
    blk_e = blk_map[0, 0:n_blk]
    n_used = blk_map[0, map_lanes - 1:map_lanes]
    xs = _dispatch(off_flat, cnt_flat, ends.reshape(-1), slot, wts, idx, x1)
    ys = _experts(blk_e, n_used, xs, expert_w_in, expert_b_in, expert_w_out, expert_b_out)
    return _combine(off_flat, cnt_flat, slot, x1, ln_g, ln_b, ys)


def kernel(x, ln1_g, ln1_b, ln2_g, ln2_b, w_in, shift_mix, decay_w0, decay_up, iclr_a0, iclr_up, gate_up, k_k, k_a, r_k, gn_g, gn_b, w_branch_rwkv, w_branch_attn, w_out, router_w, router_b, expert_w_in, expert_b_in, expert_w_out, expert_b_out):
    B, T, D = x.shape
    x2 = x.reshape(B * T, D)
    prw, q, k, v, gates = _proj(x2, w_in[0].astype(BF16), T)
    y_rwkv = _rwkv(prw, B, T, shift_mix[0], decay_w0[0], decay_up[0], iclr_a0[0], iclr_up[0], gate_up[0],
                   k_k[0], k_a[0], r_k[0], gn_g[0], gn_b[0])
    y_attn = _moba(q, k, v, B, T)
    x1, logits_t = _merge(x2, y_rwkv, y_attn, gates, w_branch_rwkv[0], w_branch_attn[0], w_out[0],
                          ln1_g[0], ln1_b[0], router_w[0], router_b[0])
    out = _moe(x1, logits_t, expert_w_in[0], expert_b_in[0], expert_w_out[0], expert_b_out[0], ln2_g[0], ln2_b[0])
    return out.reshape(B, T, D)
```

```python
import math

import jax
import jax.numpy as jnp
from jax import lax
from jax.experimental import pallas as pl
from jax.experimental.pallas import tpu as pltpu

F32 = jnp.float32
BF16 = jnp.bfloat16
I32 = jnp.int32
HI = lax.Precision.HIGHEST

D_MODEL = 1024
DEPTH = 1
RWKV_HEAD_DIM = 64
RWKV_DIM = 512
RWKV_HEADS = 8
DECAY_RANK = 64
ICLR_RANK = 64
GATE_RANK = 128
GN_EPS = 64e-5
ATTN_HEAD_DIM = 64
ATTN_DIM = 512
ATTN_HEADS = 8
MOBA_BLOCK = 256
MOBA_TOPK = 3
ROPE_THETA = 500000.0
ROPE_DIM = 16
NEG_INF = -1e30
N_EXPERTS = 32
TOP_K = 4
D_EXPERT = 1024
SWIGLU_LIMIT = 7.0
SWIGLU_ALPHA = 1.702
DEEPNORM_ALPHA = (2.0 * DEPTH) ** 0.25
LN_EPS = 1e-5
RWKV_COLS = 3 * RWKV_DIM + DECAY_RANK + ICLR_RANK + GATE_RANK
ATTN_COLS = 3 * ATTN_DIM
GATE_COLS = 2 * D_MODEL
IN_COLS = RWKV_COLS + ATTN_COLS + GATE_COLS

LANE = 128
SUBLANE = 8
VMEM_LIMIT_BYTES = 56 * 1024 * 1024

CHUNK = 64


def _cparams(sem):
    return pltpu.CompilerParams(dimension_semantics=sem, vmem_limit_bytes=VMEM_LIMIT_BYTES)


def _dot(a, b):
    return jnp.dot(a, b, preferred_element_type=F32)


def _dot_hi(a, b):
    return jnp.dot(a, b, preferred_element_type=F32, precision=HI)


def _dot_nt(a, b, precision=None):
    return lax.dot_general(a, b, (((1,), (1,)), ((), ())), preferred_element_type=F32, precision=precision)


def _dot_tn(a, b, precision=None):
    return lax.dot_general(a, b, (((0,), (0,)), ((), ())), preferred_element_type=F32, precision=precision)


def _bf16_pieces(x, n):
    pieces = []
    for _ in range(n):
        p = x.astype(BF16)
        pieces.append(p)
        x = x - p.astype(F32)
    return pieces


def _dot_exact_rhs(x, b_bf16, n):
    out = None
    for p in _bf16_pieces(x, n):
        d = _dot(p, b_bf16)
        out = d if out is None else out + d
    return out


PROJ_TM = 512
Q_SCALE = math.log2(math.e) * ATTN_HEAD_DIM ** -0.5


def _proj_kernel(x_ref, w_ref, cos_ref, sa_ref, sb_ref, prw_ref, q_ref, k_ref, v_ref, g_ref):
    xb = x_ref[...].astype(BF16)
    prw_ref[...] = _dot(xb, w_ref[:, 0:RWKV_COLS])
    c0 = RWKV_COLS
    cos = cos_ref[...]
    sa = sa_ref[...]
    sb = sb_ref[...]

    def rope(t):
        return t * cos + pltpu.roll(t, ATTN_DIM - ROPE_DIM // 2, 1) * sa + pltpu.roll(t, ROPE_DIM // 2, 1) * sb

    q_ref[...] = (rope(_dot(xb, w_ref[:, c0:c0 + ATTN_DIM])) * Q_SCALE).astype(BF16)
    k_ref[...] = rope(_dot(xb, w_ref[:, c0 + ATTN_DIM:c0 + 2 * ATTN_DIM])).astype(BF16)
    v_ref[...] = _dot(xb, w_ref[:, c0 + 2 * ATTN_DIM:c0 + 3 * ATTN_DIM]).astype(BF16)
    c1 = RWKV_COLS + ATTN_COLS
    g_ref[...] = jax.nn.sigmoid(_dot(xb, w_ref[:, c1:c1 + GATE_COLS])).astype(BF16)


def _rope_tables(T):
    half = ROPE_DIM // 2
    inv_freq = jnp.power(ROPE_THETA, -jnp.arange(0, ROPE_DIM, 2, dtype=F32) / ROPE_DIM)
    ang = jnp.arange(T).astype(F32)[:, None] * inv_freq[None, :]
    cos, sin = jnp.cos(ang), jnp.sin(ang)
    pad = jnp.zeros((T, ATTN_HEAD_DIM - ROPE_DIM), F32)
    cos_h = jnp.concatenate([cos, cos, pad + 1.0], axis=1)
    sa_h = jnp.concatenate([-sin, jnp.zeros((T, half), F32), pad], axis=1)
    sb_h = jnp.concatenate([jnp.zeros((T, half), F32), sin, pad], axis=1)
    tile = lambda t: jnp.tile(t, (1, ATTN_HEADS))
    return tile(cos_h), tile(sa_h), tile(sb_h)


def _proj(x2, w_in_bf, T):
    n_tok = x2.shape[0]
    tm = PROJ_TM
    t_tiles = T // tm
    cos, sa, sb = _rope_tables(T)
    row = lambda i: (i, 0)
    tab = lambda i: (i % t_tiles, 0)
    return pl.pallas_call(
        _proj_kernel,
        grid=(n_tok // tm,),
        in_specs=[
            pl.BlockSpec((tm, D_MODEL), row),
            pl.BlockSpec((D_MODEL, IN_COLS), lambda i: (0, 0), pipeline_mode=pl.Buffered(1)),
            pl.BlockSpec((tm, ATTN_DIM), tab),
            pl.BlockSpec((tm, ATTN_DIM), tab),
            pl.BlockSpec((tm, ATTN_DIM), tab),
        ],
        out_specs=[
            pl.BlockSpec((tm, RWKV_COLS), row),
            pl.BlockSpec((tm, ATTN_DIM), row),
            pl.BlockSpec((tm, ATTN_DIM), row),
            pl.BlockSpec((tm, ATTN_DIM), row),
            pl.BlockSpec((tm, GATE_COLS), row),
        ],
        out_shape=[
            jax.ShapeDtypeStruct((n_tok, RWKV_COLS), F32),
            jax.ShapeDtypeStruct((n_tok, ATTN_DIM), BF16),
            jax.ShapeDtypeStruct((n_tok, ATTN_DIM), BF16),
            jax.ShapeDtypeStruct((n_tok, ATTN_DIM), BF16),
            jax.ShapeDtypeStruct((n_tok, GATE_COLS), BF16),
        ],
        compiler_params=_cparams(("parallel",)),
        name="proj",
    )(x2, w_in_bf, cos, sa, sb)


RWKV_TT = 256


def _rwkv_kernel(p_ref, mix_ref, w0_ref, dup_ref, a0_ref, iup_ref, gup_ref, kk_ref, ka_ref, rk_ref,
                 gng_ref, gnb_ref, ones_ref, y_ref, s_scr, prev_scr):
    H, N, C = RWKV_HEADS, RWKV_HEAD_DIM, CHUNK

    @pl.when(pl.program_id(1) == 0)
    def _():
        s_scr[...] = jnp.zeros_like(s_scr)
        prev_scr[...] = jnp.zeros_like(prev_scr)

    TT = RWKV_TT
    n_chunks = TT // C
    bones = ones_ref[...]
    row = lax.broadcasted_iota(I32, (C, C), 0)
    col = lax.broadcasted_iota(I32, (C, C), 1)
    lower_incl = col <= row
    lower_strict = col < row
    eye = jnp.where(col == row, 1.0, 0.0)
    t_row = lax.broadcasted_iota(I32, (TT, TT), 0)
    t_col = lax.broadcasted_iota(I32, (TT, TT), 1)
    same_chunk = (t_row // C) == (t_col // C)
    chunk_ltri = jnp.where(same_chunk & (t_col <= t_row), 1.0, 0.0).astype(BF16)

    p = p_ref[...]
    first_row = lax.broadcasted_iota(I32, (TT, RWKV_COLS), 0) == 0
    prev = jnp.where(first_row, prev_scr[...], pltpu.roll(p, 1, 0))
    prev_scr[...] = p[TT - 1:TT, :]
    ps = p + (prev - p) * mix_ref[...]
    r = ps[:, 0:RWKV_DIM]
    k = ps[:, RWKV_DIM:2 * RWKV_DIM]
    v = ps[:, 2 * RWKV_DIM:3 * RWKV_DIM]
    o = 3 * RWKV_DIM
    xw = ps[:, o:o + DECAY_RANK]
    xa = ps[:, o + DECAY_RANK:o + DECAY_RANK + ICLR_RANK]
    xg = ps[:, o + DECAY_RANK + ICLR_RANK:RWKV_COLS]
    w_raw = w0_ref[...] + _dot(jnp.tanh(xw).astype(BF16), dup_ref[...])
    logw = -math.exp(-0.5) * jax.nn.sigmoid(w_raw)
    a = jax.nn.sigmoid(a0_ref[...] + _dot(xa.astype(BF16), iup_ref[...]))
    g = _dot(jax.nn.sigmoid(xg).astype(BF16), gup_ref[...])
    kk0 = k * kk_ref[...]
    kk = kk0 / jnp.maximum(jnp.sqrt(_dot_exact_rhs(kk0 * kk0, bones, 1)), 1e-12)
    kp = k * (1.0 + (a - 1.0) * ka_ref[...])
    kka = kk * a
    logw_pieces = _bf16_pieces(logw, 2)
    cum = sum(_dot(chunk_ltri, piece) for piece in logw_pieces)
    tot = jnp.concatenate([jnp.broadcast_to(cum[(c + 1) * C - 1:(c + 1) * C, :], (C, RWKV_DIM))
                           for c in range(n_chunks)], axis=0)
    e_neg = jnp.exp(-cum)
    at = (-kk * jnp.exp(cum - logw)).astype(BF16)
    rt = (r * jnp.exp(cum)).astype(BF16)
    bt = (kka * e_neg).astype(BF16)
    kt = (kp * e_neg).astype(BF16)
    e_end = jnp.exp(tot - cum)
    bh = (kka * e_end).astype(BF16)
    kh = (kp * e_end).astype(BF16)
    e_tot = jnp.exp(tot)
    vb = v.astype(BF16)

    units = [(c, h) for c in range(n_chunks) for h in range(H)]
    blk = lambda t, c, h: t[c * C:(c + 1) * C, h * N:(h + 1) * N]
    lhs = {u: jnp.concatenate([blk(at, *u), blk(rt, *u)], axis=0) for u in units}
    rhs = {u: jnp.concatenate([blk(bt, *u), blk(kt, *u)], axis=0) for u in units}
    aa = {u: _dot_nt(lhs[u], rhs[u]) for u in units}
    a_ab = {u: jnp.where(lower_strict, aa[u][0:C, 0:C], 0.0) for u in units}
    a_kv = {u: jnp.concatenate([jnp.where(lower_strict, aa[u][0:C, C:2 * C], 0.0),
                                jnp.where(lower_incl, aa[u][C:2 * C, C:2 * C], 0.0)], axis=0).astype(BF16)
            for u in units}
    a_rb = {u: jnp.where(lower_incl, aa[u][C:2 * C, 0:C], 0.0).astype(BF16) for u in units}
    akv = {u: _dot(a_kv[u], blk(vb, *u)) for u in units}
    tinv = {u: eye + a_ab[u] for u in units}
    npow = a_ab
    for _ in range(5):
        npb = {u: npow[u].astype(BF16) for u in units}
        npow = {u: _dot(npb[u], npb[u]) for u in units}
        tinv = {u: tinv[u] + _dot(tinv[u].astype(BF16), npow[u].astype(BF16)) for u in units}
    tinv_b = {u: tinv[u].astype(BF16) for u in units}

    state = [s_scr[h] for h in range(H)]
    y_rows = []
    for c in range(n_chunks):
        hs = [(c, h) for h in range(H)]
        ar_s = {u: _dot_nt(lhs[u], state[u[1]].astype(BF16)) for u in hs}
        ub = {u: _dot(tinv_b[u], (ar_s[u][0:C] + akv[u][0:C]).astype(BF16)).astype(BF16) for u in hs}
        ys = [ar_s[u][C:2 * C] + akv[u][C:2 * C] + _dot(a_rb[u], ub[u]) for u in hs]
        upd = {u: _dot_tn(jnp.concatenate([ub[u], blk(vb, *u)], axis=0),
                          jnp.concatenate([blk(bh, *u), blk(kh, *u)], axis=0)) for u in hs}
        state = [state[h] * e_tot[c * C:c * C + 1, h * N:(h + 1) * N] + upd[(c, h)] for h in range(H)]
        y_rows.append(jnp.concatenate(ys, axis=1))
    for h in range(H):
        s_scr[h] = state[h]

    y = jnp.concatenate(y_rows, axis=0)
    mu = _dot_exact_rhs(y, bones, 1) * (1.0 / N)
    yc = y - mu
    var = _dot_exact_rhs(yc * yc, bones, 1) * (1.0 / N)
    yn = yc * lax.rsqrt(var + GN_EPS) * gng_ref[...] + gnb_ref[...]
    bonus = _dot_exact_rhs(r * kp * rk_ref[...], bones, 1) * v
    y_ref[...] = ((yn + bonus) * g).astype(BF16)


def _rwkv(p_rwkv, B, T, shift_mix, decay_w0, decay_up, iclr_a0, iclr_up, gate_up, k_k, k_a, r_k, gn_g, gn_b):
    tt = RWKV_TT
    n_t = T // tt
    head = jnp.arange(RWKV_DIM) // RWKV_HEAD_DIM
    bones = (head[:, None] == head[None, :]).astype(BF16)
    vec = lambda a: a.reshape(1, -1)
    full = lambda shape: pl.BlockSpec(shape, lambda b, j: (0, 0))
    return pl.pallas_call(
        _rwkv_kernel,
        grid=(B, n_t),
        in_specs=[
            pl.BlockSpec((tt, RWKV_COLS), lambda b, j: (b * n_t + j, 0)),
            full((1, RWKV_COLS)), full((1, RWKV_DIM)), full((DECAY_RANK, RWKV_DIM)),
            full((1, RWKV_DIM)), full((ICLR_RANK, RWKV_DIM)), full((GATE_RANK, RWKV_DIM)),
            full((1, RWKV_DIM)), full((1, RWKV_DIM)), full((1, RWKV_DIM)),
            full((1, RWKV_DIM)), full((1, RWKV_DIM)), full((RWKV_DIM, RWKV_DIM)),
        ],
        out_specs=pl.BlockSpec((tt, RWKV_DIM), lambda b, j: (b * n_t + j, 0)),
        out_shape=jax.ShapeDtypeStruct((B * T, RWKV_DIM), BF16),
        scratch_shapes=[
            pltpu.VMEM((RWKV_HEADS, RWKV_HEAD_DIM, RWKV_HEAD_DIM), F32),
            pltpu.VMEM((1, RWKV_COLS), F32),
        ],
        compiler_params=_cparams(("parallel", "arbitrary")),
        name="rwkv",
    )(p_rwkv, vec(shift_mix), vec(decay_w0), decay_up.astype(BF16), vec(iclr_a0), iclr_up.astype(BF16),
      gate_up.astype(BF16), vec(k_k), vec(k_a), vec(r_k), vec(gn_g), vec(gn_b), bones)


MOBA_HP = 8


def _moba_kernel(q_ref, k_ref, v_ref, o_ref, kmean_scr, sel_scr, acc_scr, score_scr):
    blk_sz, dh = MOBA_BLOCK, ATTN_HEAD_DIM
    nb = k_ref.shape[0] // blk_sz
    i = pl.program_id(2)

    @pl.when(i == 0)
    def _():
        for n in range(nb):
            kb = k_ref[n * blk_sz:(n + 1) * blk_sz, :].astype(F32)
            kmean_scr[n:n + 1, :] = jnp.sum(kb, axis=0, keepdims=True) * (1.0 / blk_sz)

    blk = lax.broadcasted_iota(I32, (nb, blk_sz), 0)
    kpos = lax.broadcasted_iota(I32, (blk_sz, blk_sz), 0)
    qpos = lax.broadcasted_iota(I32, (blk_sz, blk_sz), 1)
    own_start = pl.multiple_of(i * blk_sz, blk_sz)
    heads = [slice(hh * dh, (hh + 1) * dh) for hh in range(MOBA_HP)]
    hds = range(MOBA_HP)
    qss = [q_ref[:, hs] for hs in heads]
    kmean_pieces = _bf16_pieces(kmean_scr[...], 3)
    gates = [sum(_dot_nt(piece[:, heads[hh]], qss[hh]) for piece in kmean_pieces) for hh in hds]
    own = [_dot_nt(k_ref[pl.ds(own_start, blk_sz), heads[hh]], qss[hh]) for hh in hds]
    for hh in hds:
        work = jnp.where(blk < i, gates[hh], NEG_INF)
        chosen = blk < 0
        for _ in range(MOBA_TOPK):
            top = jnp.max(work, axis=0, keepdims=True)
            first = jnp.min(jnp.where(work == top, blk, nb), axis=0, keepdims=True)
            hit = blk == first
            chosen = chosen | hit
            work = jnp.where(hit, -jnp.inf, work)
        sel_scr[hh] = jnp.where(chosen & (blk < i), 1.0, 0.0)
    stats, ps = [], []
    for hh in hds:
        s = jnp.where(kpos <= qpos, own[hh], NEG_INF)
        m0 = jnp.max(s, axis=0, keepdims=True)
        p = jnp.exp2(s - m0)
        stats += [m0, jnp.sum(p, axis=0, keepdims=True)]
        ps.append(p.astype(BF16))
    pvs = [_dot_tn(v_ref[pl.ds(own_start, blk_sz), heads[hh]], ps[hh]) for hh in hds]
    for hh in hds:
        acc_scr[hh] = pvs[hh]

    def scores(n):
        start = pl.multiple_of(n * blk_sz, blk_sz)
        return [_dot_nt(k_ref[pl.ds(start, blk_sz), heads[hh]], qss[hh]) for hh in hds]

    first = scores(0)
    for hh in hds:
        score_scr[hh] = first[hh]

    def body(n, carry):
        ahead = scores(jnp.minimum(n + 1, i - 1))
        start = pl.multiple_of(n * blk_sz, blk_sz)
        out, ps, alphas = [], [], []
        for hh in hds:
            m_run, l_run = carry[2 * hh], carry[2 * hh + 1]
            s = jnp.where(sel_scr[hh, pl.ds(n, 1), :] > 0.0, score_scr[hh], NEG_INF)
            m_new = jnp.maximum(m_run, jnp.max(s, axis=0, keepdims=True))
            alpha = jnp.exp2(m_run - m_new)
            p = jnp.exp2(s - m_new)
            out += [m_new, alpha * l_run + jnp.sum(p, axis=0, keepdims=True)]
            ps.append(p.astype(BF16))
            alphas.append(alpha)
        pvs = [_dot_tn(v_ref[pl.ds(start, blk_sz), heads[hh]], ps[hh]) for hh in hds]
        for hh in hds:
            acc_scr[hh] = alphas[hh] * acc_scr[hh] + pvs[hh]
            score_scr[hh] = ahead[hh]
        return tuple(out)

    stats = lax.fori_loop(0, i, body, tuple(stats))
    outs = [(acc_scr[hh] / stats[2 * hh + 1]).T for hh in range(MOBA_HP)]
    o_ref[...] = jnp.concatenate(outs, axis=1).astype(BF16)


def _moba(q, k, v, B, T):
    blk_sz = MOBA_BLOCK
    nq = T // blk_sz
    lanes = MOBA_HP * ATTN_HEAD_DIM
    kv_spec = pl.BlockSpec((T, lanes), lambda b, hp, i: (b, hp))
    q_spec = pl.BlockSpec((blk_sz, lanes), lambda b, hp, i: (b * nq + i, hp))
    return pl.pallas_call(
        _moba_kernel,
        grid=(B, ATTN_HEADS // MOBA_HP, nq),
        in_specs=[q_spec, kv_spec, kv_spec],
        out_specs=q_spec,
        out_shape=jax.ShapeDtypeStruct((B * T, ATTN_DIM), BF16),
        scratch_shapes=[
            pltpu.VMEM((T // blk_sz, lanes), F32),
            pltpu.VMEM((MOBA_HP, T // blk_sz, blk_sz), F32),
            pltpu.VMEM((MOBA_HP, ATTN_HEAD_DIM, blk_sz), F32),
            pltpu.VMEM((MOBA_HP, blk_sz, blk_sz), F32),
        ],
        compiler_params=_cparams(("parallel", "parallel", "arbitrary")),
        name="moba",
    )(q, k, v)


MERGE_TM = 512
MERGE_SUB = 128


def _layer_norm(h, g, b):
    mu = jnp.mean(h, axis=-1, keepdims=True)
    hc = h - mu
    var = jnp.mean(hc * hc, axis=-1, keepdims=True)
    return hc * lax.rsqrt(var + LN_EPS) * g + b


def _merge_kernel(x_ref, yr_ref, ya_ref, g_ref, wbr_ref, wba_ref, wo_ref, lng_ref, lnb_ref, rwt_ref, rb_ref,
                  x1_ref, lgt_ref):
    subs = [slice(s * MERGE_SUB, (s + 1) * MERGE_SUB) for s in range(MERGE_TM // MERGE_SUB)]
    yr = [_dot(yr_ref[sl, :], wbr_ref[...]) for sl in subs]
    ya = [_dot(ya_ref[sl, :], wba_ref[...]) for sl in subs]
    merged = [(g_ref[sl, 0:D_MODEL].astype(F32) * yr[s] + g_ref[sl, D_MODEL:2 * D_MODEL].astype(F32) * ya[s])
              .astype(BF16) for s, sl in enumerate(subs)]
    mix = [_dot(m, wo_ref[...]) for m in merged]
    x1 = [_layer_norm(DEEPNORM_ALPHA * x_ref[sl, :] + mix[s], lng_ref[...], lnb_ref[...])
          for s, sl in enumerate(subs)]
    for s, sl in enumerate(subs):
        x1_ref[sl, :] = x1[s]
    rw_a, rw_b = _bf16_pieces(rwt_ref[...], 2)
    for s, sl in enumerate(subs):
        x_a, x_b = _bf16_pieces(x1[s], 2)
        lgt_ref[:, sl] = _dot_nt(rw_a, x_a) + _dot_nt(rw_a, x_b) + _dot_nt(rw_b, x_a) + rb_ref[...]


def _merge(x2, y_rwkv, y_attn, gates, w_br, w_ba, w_o, ln_g, ln_b, router_w, router_b):
    n_tok = x2.shape[0]
    tm = MERGE_TM
    row = lambda i: (i, 0)
    full = lambda shape: pl.BlockSpec(shape, lambda i: (0, 0))
    return pl.pallas_call(
        _merge_kernel,
        grid=(n_tok // tm,),
        in_specs=[
            pl.BlockSpec((tm, D_MODEL), row), pl.BlockSpec((tm, RWKV_DIM), row), pl.BlockSpec((tm, ATTN_DIM), row),
            pl.BlockSpec((tm, GATE_COLS), row),
            full((RWKV_DIM, D_MODEL)), full((ATTN_DIM, D_MODEL)), full((D_MODEL, D_MODEL)),
            full((1, D_MODEL)), full((1, D_MODEL)), full((N_EXPERTS, D_MODEL)), full((N_EXPERTS, 1)),
        ],
        out_specs=[pl.BlockSpec((tm, D_MODEL), row), pl.BlockSpec((N_EXPERTS, tm), lambda i: (0, i))],
        out_shape=[jax.ShapeDtypeStruct((n_tok, D_MODEL), F32), jax.ShapeDtypeStruct((N_EXPERTS, n_tok), F32)],
        compiler_params=_cparams(("parallel",)),
        name="merge",
    )(x2, y_rwkv, y_attn, gates, w_br.astype(BF16), w_ba.astype(BF16), w_o.astype(BF16),
      ln_g.reshape(1, -1), ln_b.reshape(1, -1), router_w.T, router_b.reshape(-1, 1))


MOE_TM = 512
ROW_BLOCK = 512
PIECE_ALIGN = 2 * SUBLANE
PIECE_SIZES = (512, 256, 128, 64, 32, 16)
TAIL_SIZES = (256, 128, 64, 32, 16)
STAGE_CHUNK = 256
ROW_W = D_MODEL + LANE


def _moe_dims(n_tok):
    n_tiles = n_tok // MOE_TM
    stage_rows = -(-(MOE_TM * TOP_K + N_EXPERTS * (PIECE_ALIGN - 1)) // STAGE_CHUNK) * STAGE_CHUNK
    max_rows = n_tok * TOP_K + n_tiles * N_EXPERTS * (PIECE_ALIGN - 1) + N_EXPERTS * (ROW_BLOCK - 1)
    n_blk = -(-max_rows // ROW_BLOCK)
    map_lanes = -(-(n_blk + 1) // LANE) * LANE
    return n_tiles, stage_rows, n_blk, map_lanes


def _round_up_f32(x, m):
    return jnp.floor((x + (m - 1)) * (1.0 / m)) * m


def _route_kernel(lg_ref, wt_ref, idx_ref, slot_ref, off_ref, cnt_ref, end_ref, map_ref,
                  idx_scr, pos_scr, cnt_scr, off_scr):
    tm = MOE_TM
    n_tiles = cnt_scr.shape[1]
    phase = pl.program_id(0)
    i = pl.program_id(1)
    tok0 = pl.multiple_of(i * tm, tm)
    eio = lax.broadcasted_iota(I32, (N_EXPERTS, tm), 0)
    tile_lane = lax.broadcasted_iota(I32, (N_EXPERTS, n_tiles), 1)
    e_from = lax.broadcasted_iota(I32, (N_EXPERTS, N_EXPERTS), 1)
    e_to = lax.broadcasted_iota(I32, (N_EXPERTS, N_EXPERTS), 0)
    earlier_e = jnp.where(e_from < e_to, 1.0, 0.0)

    @pl.when((phase == 0) & (i == 0))
    def _():
        cnt_scr[...] = jnp.zeros_like(cnt_scr)

    @pl.when(phase == 0)
    def _():
        work = lg_ref[...]
        vals, hots = [], []
        for s in range(TOP_K):
            m = jnp.max(work, axis=0, keepdims=True)
            ix = jnp.min(jnp.where(work == m, eio, N_EXPERTS), axis=0, keepdims=True)
            hot = eio == ix
            idx_scr[s:s + 1, pl.ds(tok0, tm)] = ix
            idx_ref[s:s + 1, :] = ix
            vals.append(m)
            hots.append(hot)
            work = jnp.where(hot, -jnp.inf, work)
        es = [jnp.exp(v - vals[0]) for v in vals]
        denom = es[0] + es[1] + es[2] + es[3]
        for s in range(TOP_K):
            wt_ref[s:s + 1, :] = es[s] / denom
        multi_f = jnp.where(hots[0] | hots[1] | hots[2] | hots[3], 1.0, 0.0)
        t_from = lax.broadcasted_iota(I32, (tm, tm), 0)
        t_to = lax.broadcasted_iota(I32, (tm, tm), 1)
        before = jnp.where(t_from < t_to, 1.0, 0.0).astype(BF16)
        count = _dot(multi_f.astype(BF16), before)
        for s in range(TOP_K):
            pos_scr[s:s + 1, pl.ds(tok0, tm)] = jnp.sum(
                jnp.where(hots[s], count, 0.0), axis=0, keepdims=True).astype(I32)
        cnt8 = _round_up_f32(jnp.sum(multi_f, axis=1, keepdims=True), PIECE_ALIGN)
        cnt_scr[...] = cnt_scr[...] + jnp.where(tile_lane == i, cnt8, 0.0)

    @pl.when((phase == 1) & (i == 0))
    def _():
        cnt8 = cnt_scr[...]
        tot = _round_up_f32(jnp.sum(cnt8, axis=1, keepdims=True), ROW_BLOCK)
        p_start = _dot_hi(earlier_e, jnp.broadcast_to(tot, (N_EXPERTS, n_tiles)))
        i_from = lax.broadcasted_iota(I32, (n_tiles, n_tiles), 0)
        i_to = lax.broadcasted_iota(I32, (n_tiles, n_tiles), 1)
        earlier_tiles = _dot_hi(cnt8, jnp.where(i_from < i_to, 1.0, 0.0))
        off_scr[...] = p_start + earlier_tiles
        off_ref[...] = off_scr[...].astype(I32)
        cnt_ref[...] = cnt8.astype(I32)
        p_end = p_start[:, 0:1] + tot
        end_ref[...] = p_end.astype(I32)
        lanes = map_ref.shape[1]
        blk_start = lax.broadcasted_iota(I32, (N_EXPERTS, lanes), 1).astype(F32) * ROW_BLOCK
        blk_e = jnp.sum(jnp.where(blk_start >= p_end, 1.0, 0.0), axis=0, keepdims=True)
        blk_e = jnp.minimum(blk_e, N_EXPERTS - 1.0)
        n_used = jnp.max(p_end, axis=0, keepdims=True) * (1.0 / ROW_BLOCK)
        last = lax.broadcasted_iota(I32, (1, lanes), 1) == lanes - 1
        map_ref[...] = jnp.where(last, n_used, blk_e).astype(I32)

    @pl.when(phase == 1)
    def _():
        cnt_col = jnp.sum(jnp.where(tile_lane == i, cnt_scr[...], 0.0), axis=1, keepdims=True)
        local_off = _dot_hi(earlier_e, jnp.broadcast_to(cnt_col, (N_EXPERTS, tm)))
        for s in range(TOP_K):
            hot = eio == idx_scr[s:s + 1, pl.ds(tok0, tm)]
            base = jnp.sum(jnp.where(hot, local_off, 0.0), axis=0, keepdims=True).astype(I32)
            slot_ref[s:s + 1, :] = base + pos_scr[s:s + 1, pl.ds(tok0, tm)]


def _route(logits_t):
    n_tok = logits_t.shape[1]
    n_tiles, _, _, map_lanes = _moe_dims(n_tok)
    tm = MOE_TM
    last = n_tiles - 1
    const = lambda p, i: (0, 0)
    return pl.pallas_call(
        _route_kernel,
        grid=(2, n_tiles),
        in_specs=[pl.BlockSpec((N_EXPERTS, tm), lambda p, i: (0, i * (1 - p) + last * p))],
        out_specs=[
            pl.BlockSpec((TOP_K, tm), lambda p, i: (0, i * (1 - p) + last * p)),
            pl.BlockSpec((TOP_K, tm), lambda p, i: (0, i * (1 - p) + last * p)),
            pl.BlockSpec((TOP_K, tm), lambda p, i: (0, i * p)),
            pl.BlockSpec((N_EXPERTS, n_tiles), const),
            pl.BlockSpec((N_EXPERTS, n_tiles), const),
            pl.BlockSpec((N_EXPERTS, 1), const),
            pl.BlockSpec((1, map_lanes), const),
        ],
        out_shape=[
            jax.ShapeDtypeStruct((TOP_K, n_tok), F32),
            jax.ShapeDtypeStruct((TOP_K, n_tok), I32),
            jax.ShapeDtypeStruct((TOP_K, n_tok), I32),
            jax.ShapeDtypeStruct((N_EXPERTS, n_tiles), I32),
            jax.ShapeDtypeStruct((N_EXPERTS, n_tiles), I32),
            jax.ShapeDtypeStruct((N_EXPERTS, 1), I32),
            jax.ShapeDtypeStruct((1, map_lanes), I32),
        ],
        scratch_shapes=[
            pltpu.VMEM((TOP_K, n_tok), I32),
            pltpu.VMEM((TOP_K, n_tok), I32),
            pltpu.VMEM((N_EXPERTS, n_tiles), F32),
            pltpu.VMEM((N_EXPERTS, n_tiles), F32),
        ],
        compiler_params=_cparams(("arbitrary", "arbitrary")),
        name="route",
    )(logits_t)


def _piece_copies(off_ref, cnt_ref, tile, make_copy, action):
    def per_expert(e, stage_row):
        c8 = cnt_ref[tile * N_EXPERTS + e]
        hbm_row = off_ref[tile * N_EXPERTS + e]
        done = 0
        for size in PIECE_SIZES:
            bit = c8 & size

            @pl.when(bit != 0)
            def _(done=done, size=size):
                action(make_copy(pl.multiple_of(stage_row + done, PIECE_ALIGN),
                                 pl.multiple_of(hbm_row + done, PIECE_ALIGN), size))

            done = done + bit
        return stage_row + c8

    lax.fori_loop(0, N_EXPERTS, per_expert, 0)


def _slot_rows(slot_ref):
    return [slot_ref[s:s + 1, :] for s in range(TOP_K)]


def _one_hot_rows(slots, r0):
    rio = lax.broadcasted_iota(I32, (STAGE_CHUNK, MOE_TM), 0) + r0
    pick = jnp.zeros((STAGE_CHUNK, MOE_TM), F32)
    for s in range(TOP_K):
        pick = jnp.where(rio == slots[s], 1.0, pick)
    return pick.astype(BF16)


def _dispatch_kernel(off_ref, cnt_ref, end_ref, slot_ref, wt_ref, idx_ref, x1_ref, xs_hbm, stage, zeros, sem,
                     zero_sem):
    i = pl.program_id(0)
    n_tiles = pl.num_programs(0)
    buf = i % 2

    @pl.when(i == 0)
    def _():
        zeros[...] = jnp.zeros_like(zeros)

        def tail_copies(action):
            def per_expert(e, carry):
                last_piece = (n_tiles - 1) * N_EXPERTS + e
                start = off_ref[last_piece] + cnt_ref[last_piece]
                tail = end_ref[e] - start
                done = 0
                for size in TAIL_SIZES:
                    bit = tail & size

                    @pl.when(bit != 0)
                    def _(done=done, size=size):
                        action(pltpu.make_async_copy(
                            zeros.at[pl.ds(0, size)],
                            xs_hbm.at[pl.ds(pl.multiple_of(start + done, PIECE_ALIGN), size)], zero_sem))

                    done = done + bit
                return carry

            lax.fori_loop(0, N_EXPERTS, per_expert, 0)

            def per_block(b, carry):
                action(pltpu.make_async_copy(
                    zeros, xs_hbm.at[pl.ds(pl.multiple_of(b * ROW_BLOCK, ROW_BLOCK), ROW_BLOCK)], zero_sem))
                return carry

            lax.fori_loop(end_ref[N_EXPERTS - 1] // ROW_BLOCK, xs_hbm.shape[0] // ROW_BLOCK, per_block, 0)

        tail_copies(lambda cp: cp.start())
        tail_copies(lambda cp: cp.wait())

    def out_copies(tile, b, action):
        def make_copy(stage_row, hbm_row, size):
            return pltpu.make_async_copy(stage.at[b, pl.ds(stage_row, size)], xs_hbm.at[pl.ds(hbm_row, size)],
                                         sem.at[b])
        _piece_copies(off_ref, cnt_ref, tile, make_copy, action)

    @pl.when(i >= 2)
    def _():
        out_copies(i - 2, buf, lambda cp: cp.wait())

    w = wt_ref[...]
    w_a = w.astype(BF16).astype(F32)
    tail_t = jnp.concatenate(
        [w_a, w - w_a, idx_ref[...].astype(F32), jnp.zeros((LANE - 3 * TOP_K, MOE_TM), F32)], axis=0)
    xb = jnp.concatenate([x1_ref[...].astype(BF16), tail_t.T.astype(BF16)], axis=1)
    slots = _slot_rows(slot_ref)
    for r0 in range(0, stage.shape[1], STAGE_CHUNK):
        stage[buf, r0:r0 + STAGE_CHUNK, :] = _dot(_one_hot_rows(slots, r0), xb).astype(BF16)
    out_copies(i, buf, lambda cp: cp.start())

    @pl.when(i == n_tiles - 1)
    def _():
        @pl.when(i >= 1)
        def _():
            out_copies(i - 1, 1 - buf, lambda cp: cp.wait())

        out_copies(i, buf, lambda cp: cp.wait())


def _dispatch(off_flat, cnt_flat, end_flat, slot, wts, idx, x1):
    n_tok = x1.shape[0]
    n_tiles, stage_rows, n_blk, _ = _moe_dims(n_tok)
    n_rows = n_blk * ROW_BLOCK
    return pl.pallas_call(
        _dispatch_kernel,
        grid_spec=pltpu.PrefetchScalarGridSpec(
            num_scalar_prefetch=3,
            grid=(n_tiles,),
            in_specs=[pl.BlockSpec((TOP_K, MOE_TM), lambda i, off, cnt, end: (0, i)),
                      pl.BlockSpec((TOP_K, MOE_TM), lambda i, off, cnt, end: (0, i)),
                      pl.BlockSpec((TOP_K, MOE_TM), lambda i, off, cnt, end: (0, i)),
                      pl.BlockSpec((MOE_TM, D_MODEL), lambda i, off, cnt, end: (i, 0))],
            out_specs=pl.BlockSpec(memory_space=pl.ANY),
            scratch_shapes=[pltpu.VMEM((2, stage_rows, ROW_W), BF16),
                            pltpu.VMEM((ROW_BLOCK, ROW_W), BF16),
                            pltpu.SemaphoreType.DMA((2,)), pltpu.SemaphoreType.DMA(())],
        ),
        out_shape=jax.ShapeDtypeStruct((n_rows, ROW_W), BF16),
        compiler_params=_cparams(("arbitrary",)),
        name="dispatch",
    )(off_flat, cnt_flat, end_flat, slot, wts, idx, x1)


def _combine_kernel(off_ref, cnt_ref, slot_ref, x1_ref, lng_ref, lnb_ref, ys_hbm, out_ref, stage, sem):
    i = pl.program_id(0)
    n_tiles = pl.num_programs(0)
    buf = i % 2

    def in_copies(tile, b, action):
        def make_copy(stage_row, hbm_row, size):
            return pltpu.make_async_copy(ys_hbm.at[pl.ds(hbm_row, size)], stage.at[b, pl.ds(stage_row, size)],
                                         sem.at[b])
        _piece_copies(off_ref, cnt_ref, tile, make_copy, action)

    @pl.when(i == 0)
    def _():
        stage[...] = jnp.zeros_like(stage)
        in_copies(0, 0, lambda cp: cp.start())

    @pl.when(i + 1 < n_tiles)
    def _():
        in_copies(i + 1, 1 - buf, lambda cp: cp.start())

    in_copies(i, buf, lambda cp: cp.wait())

    slots = _slot_rows(slot_ref)
    ffn = jnp.zeros((MOE_TM, D_MODEL), F32)
    for r0 in range(0, stage.shape[1], STAGE_CHUNK):
        ffn = ffn + _dot_tn(_one_hot_rows(slots, r0), stage[buf, r0:r0 + STAGE_CHUNK, :])
    out_ref[...] = _layer_norm(DEEPNORM_ALPHA * x1_ref[...] + ffn, lng_ref[...], lnb_ref[...])


def _combine(off_flat, cnt_flat, slot, x1, ln_g, ln_b, ys):
    n_tok = x1.shape[0]
    n_tiles, stage_rows, _, _ = _moe_dims(n_tok)
    tok = lambda i, off, cnt: (0, i)
    row = lambda i, off, cnt: (i, 0)
    full = lambda shape: pl.BlockSpec(shape, lambda i, off, cnt: (0, 0))
    return pl.pallas_call(
        _combine_kernel,
        grid_spec=pltpu.PrefetchScalarGridSpec(
            num_scalar_prefetch=2,
            grid=(n_tiles,),
            in_specs=[pl.BlockSpec((TOP_K, MOE_TM), tok),
                      pl.BlockSpec((MOE_TM, D_MODEL), row), full((1, D_MODEL)), full((1, D_MODEL)),
                      pl.BlockSpec(memory_space=pl.ANY)],
            out_specs=pl.BlockSpec((MOE_TM, D_MODEL), row),
            scratch_shapes=[pltpu.VMEM((2, stage_rows, D_MODEL), BF16), pltpu.SemaphoreType.DMA((2,))],
        ),
        out_shape=jax.ShapeDtypeStruct((n_tok, D_MODEL), F32),
        compiler_params=_cparams(("arbitrary",)),
        name="combine",
    )(off_flat, cnt_flat, slot, x1, ln_g.reshape(1, -1), ln_b.reshape(1, -1), ys)


CAST_ROWS = 128
FFN_CHUNKS = 4


def _expert_kernel(blk_e_ref, n_used_ref, xs_ref, win_ref, bin_ref, wout_ref, bout_ref, ys_ref, win_bf, wout_bf):
    rb = pl.program_id(0)
    new_expert = (rb == 0) | (blk_e_ref[rb] != blk_e_ref[jnp.maximum(rb - 1, 0)])

    @pl.when((rb < n_used_ref[0]) & new_expert)
    def _():
        for r in range(0, D_MODEL, CAST_ROWS):
            win_bf[r:r + CAST_ROWS, :] = win_ref[0, r:r + CAST_ROWS, :].astype(BF16)
        for r in range(0, D_EXPERT, CAST_ROWS):
            wout_bf[r:r + CAST_ROWS, :] = wout_ref[0, r:r + CAST_ROWS, :].astype(BF16)

    @pl.when(rb < n_used_ref[0])
    def _():
        x = xs_ref[:, 0:D_MODEL]
        tail = xs_ref[:, D_MODEL:ROW_W].astype(F32)
        mine = blk_e_ref[rb].astype(F32)
        w_row = jnp.zeros((tail.shape[0], 1), F32)
        for s in range(TOP_K):
            w_s = tail[:, s:s + 1] + tail[:, TOP_K + s:TOP_K + s + 1]
            w_row = jnp.where(tail[:, 2 * TOP_K + s:2 * TOP_K + s + 1] == mine, w_s, w_row)
        cw = D_EXPERT // FFN_CHUNKS

        def hidden(j):
            gs = slice(j * cw, (j + 1) * cw)
            ls = slice(D_EXPERT + j * cw, D_EXPERT + (j + 1) * cw)
            return _dot(x, win_bf[:, gs]) + bin_ref[0, :, gs], _dot(x, win_bf[:, ls]) + bin_ref[0, :, ls]

        ahead = hidden(0)
        y = bout_ref[0]
        for j in range(FFN_CHUNKS):
            g, lin = ahead
            if j + 1 < FFN_CHUNKS:
                ahead = hidden(j + 1)
            gate_h = jnp.minimum(g, SWIGLU_LIMIT)
            lin_h = jnp.clip(lin, -SWIGLU_LIMIT, SWIGLU_LIMIT)
            act = gate_h * jax.nn.sigmoid(SWIGLU_ALPHA * gate_h) * (lin_h + 1.0)
            y = y + _dot(act.astype(BF16), wout_bf[j * cw:(j + 1) * cw, :])
        ys_ref[...] = (y * w_row).astype(BF16)

    @pl.when(rb >= n_used_ref[0])
    def _():
        ys_ref[...] = jnp.zeros_like(ys_ref)


def _experts(blk_e, n_used, xs, w_in, b_in, w_out, b_out):
    n_rows = xs.shape[0]
    n_blk = n_rows // ROW_BLOCK
    used = lambda rb, n_used: jnp.maximum(jnp.minimum(rb, n_used[0] - 1), 0)
    rows = lambda rb, blk_e, n_used: (used(rb, n_used), 0)
    per_e = lambda rb, blk_e, n_used: (blk_e[used(rb, n_used)], 0, 0)
    return pl.pallas_call(
        _expert_kernel,
        grid_spec=pltpu.PrefetchScalarGridSpec(
            num_scalar_prefetch=2,
            grid=(n_blk,),
            in_specs=[
                pl.BlockSpec((ROW_BLOCK, ROW_W), rows),
                pl.BlockSpec((1, D_MODEL, 2 * D_EXPERT), per_e),
                pl.BlockSpec((1, 1, 2 * D_EXPERT), per_e),
                pl.BlockSpec((1, D_EXPERT, D_MODEL), per_e),
                pl.BlockSpec((1, 1, D_MODEL), per_e),
            ],
            out_specs=pl.BlockSpec((ROW_BLOCK, D_MODEL), lambda rb, blk_e, n_used: (rb, 0)),
            scratch_shapes=[pltpu.VMEM((D_MODEL, 2 * D_EXPERT), BF16), pltpu.VMEM((D_EXPERT, D_MODEL), BF16)],
        ),
        out_shape=jax.ShapeDtypeStruct((n_rows, D_MODEL), BF16),
        compiler_params=_cparams(("arbitrary",)),
        name="experts",
    )(blk_e, n_used, xs, w_in, b_in.reshape(N_EXPERTS, 1, -1), w_out, b_out.reshape(N_EXPERTS, 1, -1))


def _moe(x1, logits_t, expert_w_in, expert_b_in, expert_w_out, expert_b_out, ln_g, ln_b):
    n_tok = x1.shape[0]
    _, _, n_blk, map_lanes = _moe_dims(n_tok)
    wts, idx, slot, off, cnt, ends, blk_map = _route(logits_t)
    off_flat = off.T.reshape(-1)
    cnt_flat = cnt.T.reshape(-1)
    blk_e = blk_map[0, 0:n_blk]
    n_used = blk_map[0, map_lanes - 1:map_lanes]
    xs = _dispatch(off_flat, cnt_flat, ends.reshape(-1), slot, wts, idx, x1)
    ys = _experts(blk_e, n_used, xs, expert_w_in, expert_b_in, expert_w_out, expert_b_out)
    return _combine(off_flat, cnt_flat, slot, x1, ln_g, ln_b, ys)


def kernel(x, ln1_g, ln1_b, ln2_g, ln2_b, w_in, shift_mix, decay_w0, decay_up, iclr_a0, iclr_up, gate_up, k_k, k_a, r_k, gn_g, gn_b, w_branch_rwkv, w_branch_attn, w_out, router_w, router_b, expert_w_in, expert_b_in, expert_w_out, expert_b_out):
    B, T, D = x.shape
    x2 = x.reshape(B * T, D)
    prw, q, k, v, gates = _proj(x2, w_in[0].astype(BF16), T)
    y_rwkv = _rwkv(prw, B, T, shift_mix[0], decay_w0[0], decay_up[0], iclr_a0[0], iclr_up[0], gate_up[0],
                   k_k[0], k_a[0], r_k[0], gn_g[0], gn_b[0])
    y_attn = _moba(q, k, v, B, T)
    x1, logits_t = _merge(x2, y_rwkv, y_attn, gates, w_branch_rwkv[0], w_branch_attn[0], w_out[0],
                          ln1_g[0], ln1_b[0], router_w[0], router_b[0])
    out = _moe(x1, logits_t, expert_w_in[0], expert_b_in[0], expert_w_out[0], expert_b_out[0], ln2_g[0], ln2_b[0])
    return out.reshape(B, T, D)
```

```python
import math

import jax
import jax.numpy as jnp
from jax import lax
from jax.experimental import pallas as pl
from jax.experimental.pallas import tpu as pltpu

F32 = jnp.float32
BF16 = jnp.bfloat16
I32 = jnp.int32
HI = lax.Precision.HIGHEST

D_MODEL = 1024
DEPTH = 1
RWKV_HEAD_DIM = 64
RWKV_DIM = 512
RWKV_HEADS = 8
DECAY_RANK = 64
ICLR_RANK = 64
GATE_RANK = 128
GN_EPS = 64e-5
ATTN_HEAD_DIM = 64
ATTN_DIM = 512
ATTN_HEADS = 8
MOBA_BLOCK = 256
MOBA_TOPK = 3
ROPE_THETA = 500000.0
ROPE_DIM = 16
NEG_INF = -1e30
N_EXPERTS = 32
TOP_K = 4
D_EXPERT = 1024
SWIGLU_LIMIT = 7.0
SWIGLU_ALPHA = 1.702
DEEPNORM_ALPHA = (2.0 * DEPTH) ** 0.25
LN_EPS = 1e-5
RWKV_COLS = 3 * RWKV_DIM + DECAY_RANK + ICLR_RANK + GATE_RANK
ATTN_COLS = 3 * ATTN_DIM
GATE_COLS = 2 * D_MODEL
IN_COLS = RWKV_COLS + ATTN_COLS + GATE_COLS

LANE = 128
SUBLANE = 8
VMEM_LIMIT_BYTES = 56 * 1024 * 1024

CHUNK = 64


def _cparams(sem):
    return pltpu.CompilerParams(dimension_semantics=sem, vmem_limit_bytes=VMEM_LIMIT_BYTES)


def _dot(a, b):
    return jnp.dot(a, b, preferred_element_type=F32)


def _dot_hi(a, b):
    return jnp.dot(a, b, preferred_element_type=F32, precision=HI)


def _dot_nt(a, b, precision=None):
    return lax.dot_general(a, b, (((1,), (1,)), ((), ())), preferred_element_type=F32, precision=precision)


def _dot_tn(a, b, precision=None):
    return lax.dot_general(a, b, (((0,), (0,)), ((), ())), preferred_element_type=F32, precision=precision)


def _bf16_pieces(x, n):
    pieces = []
    for _ in range(n):
        p = x.astype(BF16)
        pieces.append(p)
        x = x - p.astype(F32)
    return pieces


def _dot_exact_rhs(x, b_bf16, n):
    out = None
    for p in _bf16_pieces(x, n):
        d = _dot(p, b_bf16)
        out = d if out is None else out + d
    return out


PROJ_TM = 512
Q_SCALE = math.log2(math.e) * ATTN_HEAD_DIM ** -0.5


def _proj_kernel(x_ref, w_ref, cos_ref, sa_ref, sb_ref, prw_ref, q_ref, k_ref, v_ref, g_ref):
    xb = x_ref[...].astype(BF16)
    prw_ref[...] = _dot(xb, w_ref[:, 0:RWKV_COLS])
    c0 = RWKV_COLS
    cos = cos_ref[...]
    sa = sa_ref[...]
    sb = sb_ref[...]

    def rope(t):
        return t * cos + pltpu.roll(t, ATTN_DIM - ROPE_DIM // 2, 1) * sa + pltpu.roll(t, ROPE_DIM // 2, 1) * sb

    q_ref[...] = (rope(_dot(xb, w_ref[:, c0:c0 + ATTN_DIM])) * Q_SCALE).astype(BF16)
    k_ref[...] = rope(_dot(xb, w_ref[:, c0 + ATTN_DIM:c0 + 2 * ATTN_DIM])).astype(BF16)
    v_ref[...] = _dot(xb, w_ref[:, c0 + 2 * ATTN_DIM:c0 + 3 * ATTN_DIM]).astype(BF16)
    c1 = RWKV_COLS + ATTN_COLS
    g_ref[...] = jax.nn.sigmoid(_dot(xb, w_ref[:, c1:c1 + GATE_COLS])).astype(BF16)


def _rope_tables(T):
    half = ROPE_DIM // 2
    inv_freq = jnp.power(ROPE_THETA, -jnp.arange(0, ROPE_DIM, 2, dtype=F32) / ROPE_DIM)
    ang = jnp.arange(T).astype(F32)[:, None] * inv_freq[None, :]
    cos, sin = jnp.cos(ang), jnp.sin(ang)
    pad = jnp.zeros((T, ATTN_HEAD_DIM - ROPE_DIM), F32)
    cos_h = jnp.concatenate([cos, cos, pad + 1.0], axis=1)
    sa_h = jnp.concatenate([-sin, jnp.zeros((T, half), F32), pad], axis=1)
    sb_h = jnp.concatenate([jnp.zeros((T, half), F32), sin, pad], axis=1)
    tile = lambda t: jnp.tile(t, (1, ATTN_HEADS))
    return tile(cos_h), tile(sa_h), tile(sb_h)


def _proj(x2, w_in_bf, T):
    n_tok = x2.shape[0]
    tm = PROJ_TM
    t_tiles = T // tm
    cos, sa, sb = _rope_tables(T)
    row = lambda i: (i, 0)
    tab = lambda i: (i % t_tiles, 0)
    return pl.pallas_call(
        _proj_kernel,
        grid=(n_tok // tm,),
        in_specs=[
            pl.BlockSpec((tm, D_MODEL), row),
            pl.BlockSpec((D_MODEL, IN_COLS), lambda i: (0, 0), pipeline_mode=pl.Buffered(1)),
            pl.BlockSpec((tm, ATTN_DIM), tab),
            pl.BlockSpec((tm, ATTN_DIM), tab),
            pl.BlockSpec((tm, ATTN_DIM), tab),
        ],
        out_specs=[
            pl.BlockSpec((tm, RWKV_COLS), row),
            pl.BlockSpec((tm, ATTN_DIM), row),
            pl.BlockSpec((tm, ATTN_DIM), row),
            pl.BlockSpec((tm, ATTN_DIM), row),
            pl.BlockSpec((tm, GATE_COLS), row),
        ],
        out_shape=[
            jax.ShapeDtypeStruct((n_tok, RWKV_COLS), F32),
            jax.ShapeDtypeStruct((n_tok, ATTN_DIM), BF16),
            jax.ShapeDtypeStruct((n_tok, ATTN_DIM), BF16),
            jax.ShapeDtypeStruct((n_tok, ATTN_DIM), BF16),
            jax.ShapeDtypeStruct((n_tok, GATE_COLS), BF16),
        ],
        compiler_params=_cparams(("parallel",)),
        name="proj",
    )(x2, w_in_bf, cos, sa, sb)


RWKV_TT = 256


def _rwkv_kernel(p_ref, mix_ref, w0_ref, dup_ref, a0_ref, iup_ref, gup_ref, kk_ref, ka_ref, rk_ref,
                 gng_ref, gnb_ref, ones_ref, y_ref, s_scr, prev_scr):
    H, N, C = RWKV_HEADS, RWKV_HEAD_DIM, CHUNK

    @pl.when(pl.program_id(1) == 0)
    def _():
        s_scr[...] = jnp.zeros_like(s_scr)
        prev_scr[...] = jnp.zeros_like(prev_scr)

    TT = RWKV_TT
    n_chunks = TT // C
    bones = ones_ref[...]
    row = lax.broadcasted_iota(I32, (C, C), 0)
    col = lax.broadcasted_iota(I32, (C, C), 1)
    lower_incl = col <= row
    lower_strict = col < row
    eye = jnp.where(col == row, 1.0, 0.0)
    t_row = lax.broadcasted_iota(I32, (TT, TT), 0)
    t_col = lax.broadcasted_iota(I32, (TT, TT), 1)
    same_chunk = (t_row // C) == (t_col // C)
    chunk_ltri = jnp.where(same_chunk & (t_col <= t_row), 1.0, 0.0).astype(BF16)

    p = p_ref[...]
    first_row = lax.broadcasted_iota(I32, (TT, RWKV_COLS), 0) == 0
    prev = jnp.where(first_row, prev_scr[...], pltpu.roll(p, 1, 0))
    prev_scr[...] = p[TT - 1:TT, :]
    ps = p + (prev - p) * mix_ref[...]
    r = ps[:, 0:RWKV_DIM]
    k = ps[:, RWKV_DIM:2 * RWKV_DIM]
    v = ps[:, 2 * RWKV_DIM:3 * RWKV_DIM]
    o = 3 * RWKV_DIM
    xw = ps[:, o:o + DECAY_RANK]
    xa = ps[:, o + DECAY_RANK:o + DECAY_RANK + ICLR_RANK]
    xg = ps[:, o + DECAY_RANK + ICLR_RANK:RWKV_COLS]
    w_raw = w0_ref[...] + _dot(jnp.tanh(xw).astype(BF16), dup_ref[...])
    logw = -math.exp(-0.5) * jax.nn.sigmoid(w_raw)
    a = jax.nn.sigmoid(a0_ref[...] + _dot(xa.astype(BF16), iup_ref[...]))
    g = _dot(jax.nn.sigmoid(xg).astype(BF16), gup_ref[...])
    kk0 = k * kk_ref[...]
    kk = kk0 / jnp.maximum(jnp.sqrt(_dot_exact_rhs(kk0 * kk0, bones, 1)), 1e-12)
    kp = k * (1.0 + (a - 1.0) * ka_ref[...])
    kka = kk * a
    logw_pieces = _bf16_pieces(logw, 2)
    cum = sum(_dot(chunk_ltri, piece) for piece in logw_pieces)
    tot = jnp.concatenate([jnp.broadcast_to(cum[(c + 1) * C - 1:(c + 1) * C, :], (C, RWKV_DIM))
                           for c in range(n_chunks)], axis=0)
    e_neg = jnp.exp(-cum)
    at = (-kk * jnp.exp(cum - logw)).astype(BF16)
    rt = (r * jnp.exp(cum)).astype(BF16)
    bt = (kka * e_neg).astype(BF16)
    kt = (kp * e_neg).astype(BF16)
    e_end = jnp.exp(tot - cum)
    bh = (kka * e_end).astype(BF16)
    kh = (kp * e_end).astype(BF16)
    e_tot = jnp.exp(tot)
    vb = v.astype(BF16)

    units = [(c, h) for c in range(n_chunks) for h in range(H)]
    blk = lambda t, c, h: t[c * C:(c + 1) * C, h * N:(h + 1) * N]
    lhs = {u: jnp.concatenate([blk(at, *u), blk(rt, *u)], axis=0) for u in units}
    rhs = {u: jnp.concatenate([blk(bt, *u), blk(kt, *u)], axis=0) for u in units}
    aa = {u: _dot_nt(lhs[u], rhs[u]) for u in units}
    a_ab = {u: jnp.where(lower_strict, aa[u][0:C, 0:C], 0.0) for u in units}
    a_kv = {u: jnp.concatenate([jnp.where(lower_strict, aa[u][0:C, C:2 * C], 0.0),
                                jnp.where(lower_incl, aa[u][C:2 * C, C:2 * C], 0.0)], axis=0).astype(BF16)
            for u in units}
    a_rb = {u: jnp.where(lower_incl, aa[u][C:2 * C, 0:C], 0.0).astype(BF16) for u in units}
    akv = {u: _dot(a_kv[u], blk(vb, *u)) for u in units}
    tinv = {u: eye + a_ab[u] for u in units}
    npow = a_ab
    for _ in range(5):
        npb = {u: npow[u].astype(BF16) for u in units}
        npow = {u: _dot(npb[u], npb[u]) for u in units}
        tinv = {u: tinv[u] + _dot(tinv[u].astype(BF16), npow[u].astype(BF16)) for u in units}
    tinv_b = {u: tinv[u].astype(BF16) for u in units}

    state = [s_scr[h] for h in range(H)]
    y_rows = []
    for c in range(n_chunks):
        hs = [(c, h) for h in range(H)]
        ar_s = {u: _dot_nt(lhs[u], state[u[1]].astype(BF16)) for u in hs}
        ub = {u: _dot(tinv_b[u], (ar_s[u][0:C] + akv[u][0:C]).astype(BF16)).astype(BF16) for u in hs}
        ys = [ar_s[u][C:2 * C] + akv[u][C:2 * C] + _dot(a_rb[u], ub[u]) for u in hs]
        upd = {u: _dot_tn(jnp.concatenate([ub[u], blk(vb, *u)], axis=0),
                          jnp.concatenate([blk(bh, *u), blk(kh, *u)], axis=0)) for u in hs}
        state = [state[h] * e_tot[c * C:c * C + 1, h * N:(h + 1) * N] + upd[(c, h)] for h in range(H)]
        y_rows.append(jnp.concatenate(ys, axis=1))
    for h in range(H):
        s_scr[h] = state[h]

    y = jnp.concatenate(y_rows, axis=0)
    mu = _dot_exact_rhs(y, bones, 1) * (1.0 / N)
    yc = y - mu
    var = _dot_exact_rhs(yc * yc, bones, 1) * (1.0 / N)
    yn = yc * lax.rsqrt(var + GN_EPS) * gng_ref[...] + gnb_ref[...]
    bonus = _dot_exact_rhs(r * kp * rk_ref[...], bones, 1) * v
    y_ref[...] = ((yn + bonus) * g).astype(BF16)


def _rwkv(p_rwkv, B, T, shift_mix, decay_w0, decay_up, iclr_a0, iclr_up, gate_up, k_k, k_a, r_k, gn_g, gn_b):
    tt = RWKV_TT
    n_t = T // tt
    head = jnp.arange(RWKV_DIM) // RWKV_HEAD_DIM
    bones = (head[:, None] == head[None, :]).astype(BF16)
    vec = lambda a: a.reshape(1, -1)
    full = lambda shape: pl.BlockSpec(shape, lambda b, j: (0, 0))
    return pl.pallas_call(
        _rwkv_kernel,
        grid=(B, n_t),
        in_specs=[
            pl.BlockSpec((tt, RWKV_COLS), lambda b, j: (b * n_t + j, 0)),
            full((1, RWKV_COLS)), full((1, RWKV_DIM)), full((DECAY_RANK, RWKV_DIM)),
            full((1, RWKV_DIM)), full((ICLR_RANK, RWKV_DIM)), full((GATE_RANK, RWKV_DIM)),
            full((1, RWKV_DIM)), full((1, RWKV_DIM)), full((1, RWKV_DIM)),
            full((1, RWKV_DIM)), full((1, RWKV_DIM)), full((RWKV_DIM, RWKV_DIM)),
        ],
        out_specs=pl.BlockSpec((tt, RWKV_DIM), lambda b, j: (b * n_t + j, 0)),
        out_shape=jax.ShapeDtypeStruct((B * T, RWKV_DIM), BF16),
        scratch_shapes=[
            pltpu.VMEM((RWKV_HEADS, RWKV_HEAD_DIM, RWKV_HEAD_DIM), F32),
            pltpu.VMEM((1, RWKV_COLS), F32),
        ],
        compiler_params=_cparams(("parallel", "arbitrary")),
        name="rwkv",
    )(p_rwkv, vec(shift_mix), vec(decay_w0), decay_up.astype(BF16), vec(iclr_a0), iclr_up.astype(BF16),
      gate_up.astype(BF16), vec(k_k), vec(k_a), vec(r_k), vec(gn_g), vec(gn_b), bones)


MOBA_HP = 8


def _moba_kernel(q_ref, k_ref, v_ref, o_ref, kmean_scr, sel_scr, acc_scr, score_even, score_odd):
    blk_sz, dh = MOBA_BLOCK, ATTN_HEAD_DIM
    nb = k_ref.shape[0] // blk_sz
    i = pl.program_id(2)

    @pl.when(i == 0)
    def _():
        for n in range(nb):
            kb = k_ref[n * blk_sz:(n + 1) * blk_sz, :].astype(F32)
            kmean_scr[n:n + 1, :] = jnp.sum(kb, axis=0, keepdims=True) * (1.0 / blk_sz)

    blk = lax.broadcasted_iota(I32, (nb, blk_sz), 0)
    kpos = lax.broadcasted_iota(I32, (blk_sz, blk_sz), 0)
    qpos = lax.broadcasted_iota(I32, (blk_sz, blk_sz), 1)
    own_start = pl.multiple_of(i * blk_sz, blk_sz)
    heads = [slice(hh * dh, (hh + 1) * dh) for hh in range(MOBA_HP)]
    hds = range(MOBA_HP)
    qss = [q_ref[:, hs] for hs in heads]
    kmean_pieces = _bf16_pieces(kmean_scr[...], 3)
    gates = [sum(_dot_nt(piece[:, heads[hh]], qss[hh]) for piece in kmean_pieces) for hh in hds]
    own = [_dot_nt(k_ref[pl.ds(own_start, blk_sz), heads[hh]], qss[hh]) for hh in hds]
    for hh in hds:
        work = jnp.where(blk < i, gates[hh], NEG_INF)
        chosen = blk < 0
        for _ in range(MOBA_TOPK):
            top = jnp.max(work, axis=0, keepdims=True)
            first = jnp.min(jnp.where(work == top, blk, nb), axis=0, keepdims=True)
            hit = blk == first
            chosen = chosen | hit
            work = jnp.where(hit, -jnp.inf, work)
        sel_scr[hh] = jnp.where(chosen & (blk < i), 1.0, 0.0)
    stats, ps = [], []
    for hh in hds:
        s = jnp.where(kpos <= qpos, own[hh], NEG_INF)
        m0 = jnp.max(s, axis=0, keepdims=True)
        p = jnp.exp2(s - m0)
        stats += [m0, jnp.sum(p, axis=0, keepdims=True)]
        ps.append(p.astype(BF16))
    pvs = [_dot_tn(v_ref[pl.ds(own_start, blk_sz), heads[hh]], ps[hh]) for hh in hds]
    for hh in hds:
        acc_scr[hh] = pvs[hh]

    def scores(n):
        start = pl.multiple_of(n * blk_sz, blk_sz)
        return [_dot_nt(k_ref[pl.ds(start, blk_sz), heads[hh]], qss[hh]) for hh in hds]

    first = scores(0)
    for hh in hds:
        score_even[hh] = first[hh]

    def step(n, carry, here, there):
        ahead = scores(jnp.minimum(n + 1, i - 1))
        for hh in hds:
            there[hh] = ahead[hh]
        start = pl.multiple_of(n * blk_sz, blk_sz)
        out, ps, alphas = [], [], []
        for hh in hds:
            m_run, l_run = carry[2 * hh], carry[2 * hh + 1]
            s = jnp.where(sel_scr[hh, pl.ds(n, 1), :] > 0.0, here[hh], NEG_INF)
            m_new = jnp.maximum(m_run, jnp.max(s, axis=0, keepdims=True))
            alpha = jnp.exp2(m_run - m_new)
            p = jnp.exp2(s - m_new)
            out += [m_new, alpha * l_run + jnp.sum(p, axis=0, keepdims=True)]
            ps.append(p.astype(BF16))
            alphas.append(alpha)
        pvs = [_dot_tn(v_ref[pl.ds(start, blk_sz), heads[hh]], ps[hh]) for hh in hds]
        for hh in hds:
            acc_scr[hh] = alphas[hh] * acc_scr[hh] + pvs[hh]
        return tuple(out)

    def two_blocks(t, carry):
        carry = step(2 * t, carry, score_even, score_odd)
        return step(2 * t + 1, carry, score_odd, score_even)

    stats = lax.fori_loop(0, i // 2, two_blocks, tuple(stats))
    stats = lax.cond(i % 2 == 1, lambda c: step(i - 1, c, score_even, score_odd), lambda c: c, stats)
    outs = [(acc_scr[hh] / stats[2 * hh + 1]).T for hh in range(MOBA_HP)]
    o_ref[...] = jnp.concatenate(outs, axis=1).astype(BF16)


def _moba(q, k, v, B, T):
    blk_sz = MOBA_BLOCK
    nq = T // blk_sz
    lanes = MOBA_HP * ATTN_HEAD_DIM
    kv_spec = pl.BlockSpec((T, lanes), lambda b, hp, i: (b, hp))
    q_spec = pl.BlockSpec((blk_sz, lanes), lambda b, hp, i: (b * nq + i, hp))
    return pl.pallas_call(
        _moba_kernel,
        grid=(B, ATTN_HEADS // MOBA_HP, nq),
        in_specs=[q_spec, kv_spec, kv_spec],
        out_specs=q_spec,
        out_shape=jax.ShapeDtypeStruct((B * T, ATTN_DIM), BF16),
        scratch_shapes=[
            pltpu.VMEM((T // blk_sz, lanes), F32),
            pltpu.VMEM((MOBA_HP, T // blk_sz, blk_sz), F32),
            pltpu.VMEM((MOBA_HP, ATTN_HEAD_DIM, blk_sz), F32),
            pltpu.VMEM((MOBA_HP, blk_sz, blk_sz), F32),
            pltpu.VMEM((MOBA_HP, blk_sz, blk_sz), F32),
        ],
        compiler_params=_cparams(("parallel", "parallel", "arbitrary")),
        name="moba",
    )(q, k, v)


MERGE_TM = 512
MERGE_SUB = 128


def _layer_norm(h, g, b):
    mu = jnp.mean(h, axis=-1, keepdims=True)
    hc = h - mu
    var = jnp.mean(hc * hc, axis=-1, keepdims=True)
    return hc * lax.rsqrt(var + LN_EPS) * g + b


def _merge_kernel(x_ref, yr_ref, ya_ref, g_ref, wbr_ref, wba_ref, wo_ref, lng_ref, lnb_ref, rwt_ref, rb_ref,
                  x1_ref, lgt_ref):
    subs = [slice(s * MERGE_SUB, (s + 1) * MERGE_SUB) for s in range(MERGE_TM // MERGE_SUB)]
    yr = [_dot(yr_ref[sl, :], wbr_ref[...]) for sl in subs]
    ya = [_dot(ya_ref[sl, :], wba_ref[...]) for sl in subs]
    merged = [(g_ref[sl, 0:D_MODEL].astype(F32) * yr[s] + g_ref[sl, D_MODEL:2 * D_MODEL].astype(F32) * ya[s])
              .astype(BF16) for s, sl in enumerate(subs)]
    mix = [_dot(m, wo_ref[...]) for m in merged]
    x1 = [_layer_norm(DEEPNORM_ALPHA * x_ref[sl, :] + mix[s], lng_ref[...], lnb_ref[...])
          for s, sl in enumerate(subs)]
    for s, sl in enumerate(subs):
        x1_ref[sl, :] = x1[s]
    rw_a, rw_b = _bf16_pieces(rwt_ref[...], 2)
    for s, sl in enumerate(subs):
        x_a, x_b = _bf16_pieces(x1[s], 2)
        lgt_ref[:, sl] = _dot_nt(rw_a, x_a) + _dot_nt(rw_a, x_b) + _dot_nt(rw_b, x_a) + rb_ref[...]


def _merge(x2, y_rwkv, y_attn, gates, w_br, w_ba, w_o, ln_g, ln_b, router_w, router_b):
    n_tok = x2.shape[0]
    tm = MERGE_TM
    row = lambda i: (i, 0)
    full = lambda shape: pl.BlockSpec(shape, lambda i: (0, 0))
    return pl.pallas_call(
        _merge_kernel,
        grid=(n_tok // tm,),
        in_specs=[
            pl.BlockSpec((tm, D_MODEL), row), pl.BlockSpec((tm, RWKV_DIM), row), pl.BlockSpec((tm, ATTN_DIM), row),
            pl.BlockSpec((tm, GATE_COLS), row),
            full((RWKV_DIM, D_MODEL)), full((ATTN_DIM, D_MODEL)), full((D_MODEL, D_MODEL)),
            full((1, D_MODEL)), full((1, D_MODEL)), full((N_EXPERTS, D_MODEL)), full((N_EXPERTS, 1)),
        ],
        out_specs=[pl.BlockSpec((tm, D_MODEL), row), pl.BlockSpec((N_EXPERTS, tm), lambda i: (0, i))],
        out_shape=[jax.ShapeDtypeStruct((n_tok, D_MODEL), F32), jax.ShapeDtypeStruct((N_EXPERTS, n_tok), F32)],
        compiler_params=_cparams(("parallel",)),
        name="merge",
    )(x2, y_rwkv, y_attn, gates, w_br.astype(BF16), w_ba.astype(BF16), w_o.astype(BF16),
      ln_g.reshape(1, -1), ln_b.reshape(1, -1), router_w.T, router_b.reshape(-1, 1))


MOE_TM = 512
ROW_BLOCK = 512
PIECE_ALIGN = 2 * SUBLANE
PIECE_SIZES = (512, 256, 128, 64, 32, 16)
TAIL_SIZES = (256, 128, 64, 32, 16)
STAGE_CHUNK = 256
ROW_W = D_MODEL + LANE


def _moe_dims(n_tok):
    n_tiles = n_tok // MOE_TM
    stage_rows = -(-(MOE_TM * TOP_K + N_EXPERTS * (PIECE_ALIGN - 1)) // STAGE_CHUNK) * STAGE_CHUNK
    max_rows = n_tok * TOP_K + n_tiles * N_EXPERTS * (PIECE_ALIGN - 1) + N_EXPERTS * (ROW_BLOCK - 1)
    n_blk = -(-max_rows // ROW_BLOCK)
    map_lanes = -(-(n_blk + 1) // LANE) * LANE
    return n_tiles, stage_rows, n_blk, map_lanes


def _round_up_f32(x, m):
    return jnp.floor((x + (m - 1)) * (1.0 / m)) * m


def _route_kernel(lg_ref, wt_ref, idx_ref, slot_ref, off_ref, cnt_ref, end_ref, map_ref,
                  idx_scr, pos_scr, cnt_scr, off_scr):
    tm = MOE_TM
    n_tiles = cnt_scr.shape[1]
    phase = pl.program_id(0)
    i = pl.program_id(1)
    tok0 = pl.multiple_of(i * tm, tm)
    eio = lax.broadcasted_iota(I32, (N_EXPERTS, tm), 0)
    tile_lane = lax.broadcasted_iota(I32, (N_EXPERTS, n_tiles), 1)
    e_from = lax.broadcasted_iota(I32, (N_EXPERTS, N_EXPERTS), 1)
    e_to = lax.broadcasted_iota(I32, (N_EXPERTS, N_EXPERTS), 0)
    earlier_e = jnp.where(e_from < e_to, 1.0, 0.0)

    @pl.when((phase == 0) & (i == 0))
    def _():
        cnt_scr[...] = jnp.zeros_like(cnt_scr)

    @pl.when(phase == 0)
    def _():
        work = lg_ref[...]
        vals, hots = [], []
        for s in range(TOP_K):
            m = jnp.max(work, axis=0, keepdims=True)
            ix = jnp.min(jnp.where(work == m, eio, N_EXPERTS), axis=0, keepdims=True)
            hot = eio == ix
            idx_scr[s:s + 1, pl.ds(tok0, tm)] = ix
            idx_ref[s:s + 1, :] = ix
            vals.append(m)
            hots.append(hot)
            work = jnp.where(hot, -jnp.inf, work)
        es = [jnp.exp(v - vals[0]) for v in vals]
        denom = es[0] + es[1] + es[2] + es[3]
        for s in range(TOP_K):
            wt_ref[s:s + 1, :] = es[s] / denom
        multi_f = jnp.where(hots[0] | hots[1] | hots[2] | hots[3], 1.0, 0.0)
        t_from = lax.broadcasted_iota(I32, (tm, tm), 0)
        t_to = lax.broadcasted_iota(I32, (tm, tm), 1)
        before = jnp.where(t_from < t_to, 1.0, 0.0).astype(BF16)
        count = _dot(multi_f.astype(BF16), before)
        for s in range(TOP_K):
            pos_scr[s:s + 1, pl.ds(tok0, tm)] = jnp.sum(
                jnp.where(hots[s], count, 0.0), axis=0, keepdims=True).astype(I32)
        cnt8 = _round_up_f32(jnp.sum(multi_f, axis=1, keepdims=True), PIECE_ALIGN)
        cnt_scr[...] = cnt_scr[...] + jnp.where(tile_lane == i, cnt8, 0.0)

    @pl.when((phase == 1) & (i == 0))
    def _():
        cnt8 = cnt_scr[...]
        tot = _round_up_f32(jnp.sum(cnt8, axis=1, keepdims=True), ROW_BLOCK)
        p_start = _dot_hi(earlier_e, jnp.broadcast_to(tot, (N_EXPERTS, n_tiles)))
        i_from = lax.broadcasted_iota(I32, (n_tiles, n_tiles), 0)
        i_to = lax.broadcasted_iota(I32, (n_tiles, n_tiles), 1)
        earlier_tiles = _dot_hi(cnt8, jnp.where(i_from < i_to, 1.0, 0.0))
        off_scr[...] = p_start + earlier_tiles
        off_ref[...] = off_scr[...].astype(I32)
        cnt_ref[...] = cnt8.astype(I32)
        p_end = p_start[:, 0:1] + tot
        end_ref[...] = p_end.astype(I32)
        lanes = map_ref.shape[1]
        blk_start = lax.broadcasted_iota(I32, (N_EXPERTS, lanes), 1).astype(F32) * ROW_BLOCK
        blk_e = jnp.sum(jnp.where(blk_start >= p_end, 1.0, 0.0), axis=0, keepdims=True)
        blk_e = jnp.minimum(blk_e, N_EXPERTS - 1.0)
        n_used = jnp.max(p_end, axis=0, keepdims=True) * (1.0 / ROW_BLOCK)
        last = lax.broadcasted_iota(I32, (1, lanes), 1) == lanes - 1
        map_ref[...] = jnp.where(last, n_used, blk_e).astype(I32)

    @pl.when(phase == 1)
    def _():
        cnt_col = jnp.sum(jnp.where(tile_lane == i, cnt_scr[...], 0.0), axis=1, keepdims=True)
        local_off = _dot_hi(earlier_e, jnp.broadcast_to(cnt_col, (N_EXPERTS, tm)))
        for s in range(TOP_K):
            hot = eio == idx_scr[s:s + 1, pl.ds(tok0, tm)]
            base = jnp.sum(jnp.where(hot, local_off, 0.0), axis=0, keepdims=True).astype(I32)
            slot_ref[s:s + 1, :] = base + pos_scr[s:s + 1, pl.ds(tok0, tm)]


def _route(logits_t):
    n_tok = logits_t.shape[1]
    n_tiles, _, _, map_lanes = _moe_dims(n_tok)
    tm = MOE_TM
    last = n_tiles - 1
    const = lambda p, i: (0, 0)
    return pl.pallas_call(
        _route_kernel,
        grid=(2, n_tiles),
        in_specs=[pl.BlockSpec((N_EXPERTS, tm), lambda p, i: (0, i * (1 - p) + last * p))],
        out_specs=[
            pl.BlockSpec((TOP_K, tm), lambda p, i: (0, i * (1 - p) + last * p)),
            pl.BlockSpec((TOP_K, tm), lambda p, i: (0, i * (1 - p) + last * p)),
            pl.BlockSpec((TOP_K, tm), lambda p, i: (0, i * p)),
            pl.BlockSpec((N_EXPERTS, n_tiles), const),
            pl.BlockSpec((N_EXPERTS, n_tiles), const),
            pl.BlockSpec((N_EXPERTS, 1), const),
            pl.BlockSpec((1, map_lanes), const),
        ],
        out_shape=[
            jax.ShapeDtypeStruct((TOP_K, n_tok), F32),
            jax.ShapeDtypeStruct((TOP_K, n_tok), I32),
            jax.ShapeDtypeStruct((TOP_K, n_tok), I32),
            jax.ShapeDtypeStruct((N_EXPERTS, n_tiles), I32),
            jax.ShapeDtypeStruct((N_EXPERTS, n_tiles), I32),
            jax.ShapeDtypeStruct((N_EXPERTS, 1), I32),
            jax.ShapeDtypeStruct((1, map_lanes), I32),
        ],
        scratch_shapes=[
            pltpu.VMEM((TOP_K, n_tok), I32),
            pltpu.VMEM((TOP_K, n_tok), I32),
            pltpu.VMEM((N_EXPERTS, n_tiles), F32),
            pltpu.VMEM((N_EXPERTS, n_tiles), F32),
        ],
        compiler_params=_cparams(("arbitrary", "arbitrary")),
        name="route",
    )(logits_t)


def _piece_copies(off_ref, cnt_ref, tile, make_copy, action):
    def per_expert(e, stage_row):
        c8 = cnt_ref[tile * N_EXPERTS + e]
        hbm_row = off_ref[tile * N_EXPERTS + e]
        done = 0
        for size in PIECE_SIZES:
            bit = c8 & size

            @pl.when(bit != 0)
            def _(done=done, size=size):
                action(make_copy(pl.multiple_of(stage_row + done, PIECE_ALIGN),
                                 pl.multiple_of(hbm_row + done, PIECE_ALIGN), size))

            done = done + bit
        return stage_row + c8

    lax.fori_loop(0, N_EXPERTS, per_expert, 0)


def _slot_rows(slot_ref):
    return [slot_ref[s:s + 1, :] for s in range(TOP_K)]


def _one_hot_rows(slots, r0):
    rio = lax.broadcasted_iota(I32, (STAGE_CHUNK, MOE_TM), 0) + r0
    pick = jnp.zeros((STAGE_CHUNK, MOE_TM), F32)
    for s in range(TOP_K):
        pick = jnp.where(rio == slots[s], 1.0, pick)
    return pick.astype(BF16)


def _dispatch_kernel(off_ref, cnt_ref, end_ref, slot_ref, wt_ref, idx_ref, x1_ref, xs_hbm, stage, zeros, sem,
                     zero_sem):
    i = pl.program_id(0)
    n_tiles = pl.num_programs(0)
    buf = i % 2

    @pl.when(i == 0)
    def _():
        zeros[...] = jnp.zeros_like(zeros)

        def tail_copies(action):
            def per_expert(e, carry):
                last_piece = (n_tiles - 1) * N_EXPERTS + e
                start = off_ref[last_piece] + cnt_ref[last_piece]
                tail = end_ref[e] - start
                done = 0
                for size in TAIL_SIZES:
                    bit = tail & size

                    @pl.when(bit != 0)
                    def _(done=done, size=size):
                        action(pltpu.make_async_copy(
                            zeros.at[pl.ds(0, size)],
                            xs_hbm.at[pl.ds(pl.multiple_of(start + done, PIECE_ALIGN), size)], zero_sem))

                    done = done + bit
                return carry

            lax.fori_loop(0, N_EXPERTS, per_expert, 0)

            def per_block(b, carry):
                action(pltpu.make_async_copy(
                    zeros, xs_hbm.at[pl.ds(pl.multiple_of(b * ROW_BLOCK, ROW_BLOCK), ROW_BLOCK)], zero_sem))
                return carry

            lax.fori_loop(end_ref[N_EXPERTS - 1] // ROW_BLOCK, xs_hbm.shape[0] // ROW_BLOCK, per_block, 0)

        tail_copies(lambda cp: cp.start())
        tail_copies(lambda cp: cp.wait())

    def out_copies(tile, b, action):
        def make_copy(stage_row, hbm_row, size):
            return pltpu.make_async_copy(stage.at[b, pl.ds(stage_row, size)], xs_hbm.at[pl.ds(hbm_row, size)],
                                         sem.at[b])
        _piece_copies(off_ref, cnt_ref, tile, make_copy, action)

    @pl.when(i >= 2)
    def _():
        out_copies(i - 2, buf, lambda cp: cp.wait())

    w = wt_ref[...]
    w_a = w.astype(BF16).astype(F32)
    tail_t = jnp.concatenate(
        [w_a, w - w_a, idx_ref[...].astype(F32), jnp.zeros((LANE - 3 * TOP_K, MOE_TM), F32)], axis=0)
    xb = jnp.concatenate([x1_ref[...].astype(BF16), tail_t.T.astype(BF16)], axis=1)
    slots = _slot_rows(slot_ref)
    for r0 in range(0, stage.shape[1], STAGE_CHUNK):
        stage[buf, r0:r0 + STAGE_CHUNK, :] = _dot(_one_hot_rows(slots, r0), xb).astype(BF16)
    out_copies(i, buf, lambda cp: cp.start())

    @pl.when(i == n_tiles - 1)
    def _():
        @pl.when(i >= 1)
        def _():
            out_copies(i - 1, 1 - buf, lambda cp: cp.wait())

        out_copies(i, buf, lambda cp: cp.wait())


def _dispatch(off_flat, cnt_flat, end_flat, slot, wts, idx, x1):
    n_tok = x1.shape[0]
    n_tiles, stage_rows, n_blk, _ = _moe_dims(n_tok)
    n_rows = n_blk * ROW_BLOCK
    return pl.pallas_call(
        _dispatch_kernel,
        grid_spec=pltpu.PrefetchScalarGridSpec(
            num_scalar_prefetch=3,
            grid=(n_tiles,),
            in_specs=[pl.BlockSpec((TOP_K, MOE_TM), lambda i, off, cnt, end: (0, i)),
                      pl.BlockSpec((TOP_K, MOE_TM), lambda i, off, cnt, end: (0, i)),
                      pl.BlockSpec((TOP_K, MOE_TM), lambda i, off, cnt, end: (0, i)),
                      pl.BlockSpec((MOE_TM, D_MODEL), lambda i, off, cnt, end: (i, 0))],
            out_specs=pl.BlockSpec(memory_space=pl.ANY),
            scratch_shapes=[pltpu.VMEM((2, stage_rows, ROW_W), BF16),
                            pltpu.VMEM((ROW_BLOCK, ROW_W), BF16),
                            pltpu.SemaphoreType.DMA((2,)), pltpu.SemaphoreType.DMA(())],
        ),
        out_shape=jax.ShapeDtypeStruct((n_rows, ROW_W), BF16),
        compiler_params=_cparams(("arbitrary",)),
        name="dispatch",
    )(off_flat, cnt_flat, end_flat, slot, wts, idx, x1)


def _combine_kernel(off_ref, cnt_ref, slot_ref, x1_ref, lng_ref, lnb_ref, ys_hbm, out_ref, stage, sem):
    i = pl.program_id(0)
    n_tiles = pl.num_programs(0)
    buf = i % 2

    def in_copies(tile, b, action):
        def make_copy(stage_row, hbm_row, size):
            return pltpu.make_async_copy(ys_hbm.at[pl.ds(hbm_row, size)], stage.at[b, pl.ds(stage_row, size)],
                                         sem.at[b])
        _piece_copies(off_ref, cnt_ref, tile, make_copy, action)

    @pl.when(i == 0)
    def _():
        stage[...] = jnp.zeros_like(stage)
        in_copies(0, 0, lambda cp: cp.start())

    @pl.when(i + 1 < n_tiles)
    def _():
        in_copies(i + 1, 1 - buf, lambda cp: cp.start())

    in_copies(i, buf, lambda cp: cp.wait())

    slots = _slot_rows(slot_ref)
    ffn = jnp.zeros((MOE_TM, D_MODEL), F32)
    for r0 in range(0, stage.shape[1], STAGE_CHUNK):
        ffn = ffn + _dot_tn(_one_hot_rows(slots, r0), stage[buf, r0:r0 + STAGE_CHUNK, :])
    out_ref[...] = _layer_norm(DEEPNORM_ALPHA * x1_ref[...] + ffn, lng_ref[...], lnb_ref[...])


def _combine(off_flat, cnt_flat, slot, x1, ln_g, ln_b, ys):
    n_tok = x1.shape[0]
    n_tiles, stage_rows, _, _ = _moe_dims(n_tok)
    tok = lambda i, off, cnt: (0, i)
    row = lambda i, off, cnt: (i, 0)
    full = lambda shape: pl.BlockSpec(shape, lambda i, off, cnt: (0, 0))
    return pl.pallas_call(
        _combine_kernel,
        grid_spec=pltpu.PrefetchScalarGridSpec(
            num_scalar_prefetch=2,
            grid=(n_tiles,),
            in_specs=[pl.BlockSpec((TOP_K, MOE_TM), tok),
                      pl.BlockSpec((MOE_TM, D_MODEL), row), full((1, D_MODEL)), full((1, D_MODEL)),
                      pl.BlockSpec(memory_space=pl.ANY)],
            out_specs=pl.BlockSpec((MOE_TM, D_MODEL), row),
            scratch_shapes=[pltpu.VMEM((2, stage_rows, D_MODEL), BF16), pltpu.SemaphoreType.DMA((2,))],
        ),
        out_shape=jax.ShapeDtypeStruct((n_tok, D_MODEL), F32),
        compiler_params=_cparams(("arbitrary",)),
        name="combine",
    )(off_flat, cnt_flat, slot, x1, ln_g.reshape(1, -1), ln_b.reshape(1, -1), ys)


CAST_ROWS = 128
FFN_CHUNKS = 4


def _expert_kernel(blk_e_ref, n_used_ref, xs_ref, win_ref, bin_ref, wout_ref, bout_ref, ys_ref, win_bf, wout_bf):
    rb = pl.program_id(0)
    new_expert = (rb == 0) | (blk_e_ref[rb] != blk_e_ref[jnp.maximum(rb - 1, 0)])

    @pl.when((rb < n_used_ref[0]) & new_expert)
    def _():
        for r in range(0, D_MODEL, CAST_ROWS):
            win_bf[r:r + CAST_ROWS, :] = win_ref[0, r:r + CAST_ROWS, :].astype(BF16)
        for r in range(0, D_EXPERT, CAST_ROWS):
            wout_bf[r:r + CAST_ROWS, :] = wout_ref[0, r:r + CAST_ROWS, :].astype(BF16)

    @pl.when(rb < n_used_ref[0])
    def _():
        x = xs_ref[:, 0:D_MODEL]
        tail = xs_ref[:, D_MODEL:ROW_W].astype(F32)
        mine = blk_e_ref[rb].astype(F32)
        w_row = jnp.zeros((tail.shape[0], 1), F32)
        for s in range(TOP_K):
            w_s = tail[:, s:s + 1] + tail[:, TOP_K + s:TOP_K + s + 1]
            w_row = jnp.where(tail[:, 2 * TOP_K + s:2 * TOP_K + s + 1] == mine, w_s, w_row)
        cw = D_EXPERT // FFN_CHUNKS

        def hidden(j):
            gs = slice(j * cw, (j + 1) * cw)
            ls = slice(D_EXPERT + j * cw, D_EXPERT + (j + 1) * cw)
            return _dot(x, win_bf[:, gs]) + bin_ref[0, :, gs], _dot(x, win_bf[:, ls]) + bin_ref[0, :, ls]

        ahead = hidden(0)
        y = bout_ref[0]
        for j in range(FFN_CHUNKS):
            g, lin = ahead
            if j + 1 < FFN_CHUNKS:
                ahead = hidden(j + 1)
            gate_h = jnp.minimum(g, SWIGLU_LIMIT)
            lin_h = jnp.clip(lin, -SWIGLU_LIMIT, SWIGLU_LIMIT)
            act = gate_h * jax.nn.sigmoid(SWIGLU_ALPHA * gate_h) * (lin_h + 1.0)
            y = y + _dot(act.astype(BF16), wout_bf[j * cw:(j + 1) * cw, :])
        ys_ref[...] = (y * w_row).astype(BF16)

    @pl.when(rb >= n_used_ref[0])
    def _():
        ys_ref[...] = jnp.zeros_like(ys_ref)


def _experts(blk_e, n_used, xs, w_in, b_in, w_out, b_out):
    n_rows = xs.shape[0]
    n_blk = n_rows // ROW_BLOCK
    used = lambda rb, n_used: jnp.maximum(jnp.minimum(rb, n_used[0] - 1), 0)
    rows = lambda rb, blk_e, n_used: (used(rb, n_used), 0)
    per_e = lambda rb, blk_e, n_used: (blk_e[used(rb, n_used)], 0, 0)
    return pl.pallas_call(
        _expert_kernel,
        grid_spec=pltpu.PrefetchScalarGridSpec(
            num_scalar_prefetch=2,
            grid=(n_blk,),
            in_specs=[
                pl.BlockSpec((ROW_BLOCK, ROW_W), rows),
                pl.BlockSpec((1, D_MODEL, 2 * D_EXPERT), per_e),
                pl.BlockSpec((1, 1, 2 * D_EXPERT), per_e),
                pl.BlockSpec((1, D_EXPERT, D_MODEL), per_e),
                pl.BlockSpec((1, 1, D_MODEL), per_e),
            ],
            out_specs=pl.BlockSpec((ROW_BLOCK, D_MODEL), lambda rb, blk_e, n_used: (rb, 0)),
            scratch_shapes=[pltpu.VMEM((D_MODEL, 2 * D_EXPERT), BF16), pltpu.VMEM((D_EXPERT, D_MODEL), BF16)],
        ),
        out_shape=jax.ShapeDtypeStruct((n_rows, D_MODEL), BF16),
        compiler_params=_cparams(("arbitrary",)),
        name="experts",
    )(blk_e, n_used, xs, w_in, b_in.reshape(N_EXPERTS, 1, -1), w_out, b_out.reshape(N_EXPERTS, 1, -1))


def _moe(x1, logits_t, expert_w_in, expert_b_in, expert_w_out, expert_b_out, ln_g, ln_b):
    n_tok = x1.shape[0]
    _, _, n_blk, map_lanes = _moe_dims(n_tok)
    wts, idx, slot, off, cnt, ends, blk_map = _route(logits_t)
    off_flat = off.T.reshape(-1)
    cnt_flat = cnt.T.reshape(-1)
    blk_e = blk_map[0, 0:n_blk]
    n_used = blk_map[0, map_lanes - 1:map_lanes]
    xs = _dispatch(off_flat, cnt_flat, ends.reshape(-1), slot, wts, idx, x1)
    ys = _experts(blk_e, n_used, xs, expert_w_in, expert_b_in, expert_w_out, expert_b_out)
    return _combine(off_flat, cnt_flat, slot, x1, ln_g, ln_b, ys)


def kernel(x, ln1_g, ln1_b, ln2_g, ln2_b, w_in, shift_mix, decay_w0, decay_up, iclr_a0, iclr_up, gate_up, k_k, k_a, r_k, gn_g, gn_b, w_branch_rwkv, w_branch_attn, w_out, router_w, router_b, expert_w_in, expert_b_in, expert_w_out, expert_b_out):
    B, T, D = x.shape
    x2 = x.reshape(B * T, D)
    prw, q, k, v, gates = _proj(x2, w_in[0].astype(BF16), T)
    y_rwkv = _rwkv(prw, B, T, shift_mix[0], decay_w0[0], decay_up[0], iclr_a0[0], iclr_up[0], gate_up[0],
                   k_k[0], k_a[0], r_k[0], gn_g[0], gn_b[0])
    y_attn = _moba(q, k, v, B, T)
    x1, logits_t = _merge(x2, y_rwkv, y_attn, gates, w_branch_rwkv[0], w_branch_attn[0], w_out[0],
                          ln1_g[0], ln1_b[0], router_w[0], router_b[0])
    out = _moe(x1, logits_t, expert_w_in[0], expert_b_in[0], expert_w_out[0], expert_b_out[0], ln2_g[0], ln2_b[0])
    return out.reshape(B, T, D)
```

```python
import math

import jax
import jax.numpy as jnp
from jax import lax
from jax.experimental import pallas as pl
from jax.experimental.pallas import tpu as pltpu

F32 = jnp.float32
BF16 = jnp.bfloat16
I32 = jnp.int32
HI = lax.Precision.HIGHEST

D_MODEL = 1024
DEPTH = 1
RWKV_HEAD_DIM = 64
RWKV_DIM = 512
RWKV_HEADS = 8
DECAY_RANK = 64
ICLR_RANK = 64
GATE_RANK = 128
GN_EPS = 64e-5
ATTN_HEAD_DIM = 64
ATTN_DIM = 512
ATTN_HEADS = 8
MOBA_BLOCK = 256
MOBA_TOPK = 3
ROPE_THETA = 500000.0
ROPE_DIM = 16
NEG_INF = -1e30
N_EXPERTS = 32
TOP_K = 4
D_EXPERT = 1024
SWIGLU_LIMIT = 7.0
SWIGLU_ALPHA = 1.702
DEEPNORM_ALPHA = (2.0 * DEPTH) ** 0.25
LN_EPS = 1e-5
RWKV_COLS = 3 * RWKV_DIM + DECAY_RANK + ICLR_RANK + GATE_RANK
ATTN_COLS = 3 * ATTN_DIM
GATE_COLS = 2 * D_MODEL
IN_COLS = RWKV_COLS + ATTN_COLS + GATE_COLS

LANE = 128
SUBLANE = 8
VMEM_LIMIT_BYTES = 56 * 1024 * 1024

CHUNK = 64


def _cparams(sem):
    return pltpu.CompilerParams(dimension_semantics=sem, vmem_limit_bytes=VMEM_LIMIT_BYTES)


def _dot(a, b):
    return jnp.dot(a, b, preferred_element_type=F32)


def _dot_hi(a, b):
    return jnp.dot(a, b, preferred_element_type=F32, precision=HI)


def _dot_nt(a, b, precision=None):
    return lax.dot_general(a, b, (((1,), (1,)), ((), ())), preferred_element_type=F32, precision=precision)


def _dot_tn(a, b, precision=None):
    return lax.dot_general(a, b, (((0,), (0,)), ((), ())), preferred_element_type=F32, precision=precision)


def _bf16_pieces(x, n):
    pieces = []
    for _ in range(n):
        p = x.astype(BF16)
        pieces.append(p)
        x = x - p.astype(F32)
    return pieces


def _dot_exact_rhs(x, b_bf16, n):
    out = None
    for p in _bf16_pieces(x, n):
        d = _dot(p, b_bf16)
        out = d if out is None else out + d
    return out


PROJ_TM = 512
Q_SCALE = math.log2(math.e) * ATTN_HEAD_DIM ** -0.5


def _proj_kernel(x_ref, w_ref, cos_ref, sa_ref, sb_ref, prw_ref, q_ref, k_ref, v_ref, g_ref):
    xb = x_ref[...].astype(BF16)
    prw_ref[...] = _dot(xb, w_ref[:, 0:RWKV_COLS])
    c0 = RWKV_COLS
    widen = lambda t: jnp.concatenate([t] * (ATTN_DIM // LANE), axis=1)
    cos = widen(cos_ref[...])
    sa = widen(sa_ref[...])
    sb = widen(sb_ref[...])

    def rope(t):
        return t * cos + pltpu.roll(t, ATTN_DIM - ROPE_DIM // 2, 1) * sa + pltpu.roll(t, ROPE_DIM // 2, 1) * sb

    q_ref[...] = (rope(_dot(xb, w_ref[:, c0:c0 + ATTN_DIM])) * Q_SCALE).astype(BF16)
    k_ref[...] = rope(_dot(xb, w_ref[:, c0 + ATTN_DIM:c0 + 2 * ATTN_DIM])).astype(BF16)
    v_ref[...] = _dot(xb, w_ref[:, c0 + 2 * ATTN_DIM:c0 + 3 * ATTN_DIM]).astype(BF16)
    c1 = RWKV_COLS + ATTN_COLS
    g_ref[...] = jax.nn.sigmoid(_dot(xb, w_ref[:, c1:c1 + GATE_COLS])).astype(BF16)


def _rope_tables(T):
    half = ROPE_DIM // 2
    inv_freq = jnp.power(ROPE_THETA, -jnp.arange(0, ROPE_DIM, 2, dtype=F32) / ROPE_DIM)
    ang = jnp.arange(T).astype(F32)[:, None] * inv_freq[None, :]
    cos, sin = jnp.cos(ang), jnp.sin(ang)
    pad = jnp.zeros((T, ATTN_HEAD_DIM - ROPE_DIM), F32)
    cos_h = jnp.concatenate([cos, cos, pad + 1.0], axis=1)
    sa_h = jnp.concatenate([-sin, jnp.zeros((T, half), F32), pad], axis=1)
    sb_h = jnp.concatenate([jnp.zeros((T, half), F32), sin, pad], axis=1)
    tile = lambda t: jnp.tile(t, (1, LANE // ATTN_HEAD_DIM))
    return tile(cos_h), tile(sa_h), tile(sb_h)


def _proj(x2, w_in_bf, T):
    n_tok = x2.shape[0]
    tm = PROJ_TM
    t_tiles = T // tm
    cos, sa, sb = _rope_tables(T)
    row = lambda i: (i, 0)
    tab = lambda i: (i % t_tiles, 0)
    return pl.pallas_call(
        _proj_kernel,
        grid=(n_tok // tm,),
        in_specs=[
            pl.BlockSpec((tm, D_MODEL), row),
            pl.BlockSpec((D_MODEL, IN_COLS), lambda i: (0, 0), pipeline_mode=pl.Buffered(1)),
            pl.BlockSpec((tm, LANE), tab),
            pl.BlockSpec((tm, LANE), tab),
            pl.BlockSpec((tm, LANE), tab),
        ],
        out_specs=[
            pl.BlockSpec((tm, RWKV_COLS), row),
            pl.BlockSpec((tm, ATTN_DIM), row),
            pl.BlockSpec((tm, ATTN_DIM), row),
            pl.BlockSpec((tm, ATTN_DIM), row),
            pl.BlockSpec((tm, GATE_COLS), row),
        ],
        out_shape=[
            jax.ShapeDtypeStruct((n_tok, RWKV_COLS), F32),
            jax.ShapeDtypeStruct((n_tok, ATTN_DIM), BF16),
            jax.ShapeDtypeStruct((n_tok, ATTN_DIM), BF16),
            jax.ShapeDtypeStruct((n_tok, ATTN_DIM), BF16),
            jax.ShapeDtypeStruct((n_tok, GATE_COLS), BF16),
        ],
        compiler_params=_cparams(("parallel",)),
        name="proj",
    )(x2, w_in_bf, cos, sa, sb)


RWKV_TT = 256


def _rwkv_kernel(p_ref, mix_ref, w0_ref, dup_ref, a0_ref, iup_ref, gup_ref, kk_ref, ka_ref, rk_ref,
                 gng_ref, gnb_ref, ones_ref, y_ref, s_scr, prev_scr):
    H, N, C = RWKV_HEADS, RWKV_HEAD_DIM, CHUNK

    @pl.when(pl.program_id(1) == 0)
    def _():
        s_scr[...] = jnp.zeros_like(s_scr)
        prev_scr[...] = jnp.zeros_like(prev_scr)

    TT = RWKV_TT
    n_chunks = TT // C
    bones = ones_ref[...]
    row = lax.broadcasted_iota(I32, (C, C), 0)
    col = lax.broadcasted_iota(I32, (C, C), 1)
    lower_incl = col <= row
    lower_strict = col < row
    eye = jnp.where(col == row, 1.0, 0.0)
    t_row = lax.broadcasted_iota(I32, (TT, TT), 0)
    t_col = lax.broadcasted_iota(I32, (TT, TT), 1)
    same_chunk = (t_row // C) == (t_col // C)
    chunk_ltri = jnp.where(same_chunk & (t_col <= t_row), 1.0, 0.0).astype(BF16)

    p = p_ref[...]
    first_row = lax.broadcasted_iota(I32, (TT, RWKV_COLS), 0) == 0
    prev = jnp.where(first_row, prev_scr[...], pltpu.roll(p, 1, 0))
    prev_scr[...] = p[TT - 1:TT, :]
    ps = p + (prev - p) * mix_ref[...]
    r = ps[:, 0:RWKV_DIM]
    k = ps[:, RWKV_DIM:2 * RWKV_DIM]
    v = ps[:, 2 * RWKV_DIM:3 * RWKV_DIM]
    o = 3 * RWKV_DIM
    xw = ps[:, o:o + DECAY_RANK]
    xa = ps[:, o + DECAY_RANK:o + DECAY_RANK + ICLR_RANK]
    xg = ps[:, o + DECAY_RANK + ICLR_RANK:RWKV_COLS]
    w_raw = w0_ref[...] + _dot(jnp.tanh(xw).astype(BF16), dup_ref[...])
    logw = -math.exp(-0.5) * jax.nn.sigmoid(w_raw)
    a = jax.nn.sigmoid(a0_ref[...] + _dot(xa.astype(BF16), iup_ref[...]))
    g = _dot(jax.nn.sigmoid(xg).astype(BF16), gup_ref[...])
    kk0 = k * kk_ref[...]
    kk = kk0 / jnp.maximum(jnp.sqrt(_dot_exact_rhs(kk0 * kk0, bones, 1)), 1e-12)
    kp = k * (1.0 + (a - 1.0) * ka_ref[...])
    kka = kk * a
    logw_pieces = _bf16_pieces(logw, 2)
    cum = sum(_dot(chunk_ltri, piece) for piece in logw_pieces)
    tot = jnp.concatenate([jnp.broadcast_to(cum[(c + 1) * C - 1:(c + 1) * C, :], (C, RWKV_DIM))
                           for c in range(n_chunks)], axis=0)
    e_neg = jnp.exp(-cum)
    at = (-kk * jnp.exp(cum - logw)).astype(BF16)
    rt = (r * jnp.exp(cum)).astype(BF16)
    bt = (kka * e_neg).astype(BF16)
    kt = (kp * e_neg).astype(BF16)
    e_end = jnp.exp(tot - cum)
    bh = (kka * e_end).astype(BF16)
    kh = (kp * e_end).astype(BF16)
    e_tot = jnp.exp(tot)
    vb = v.astype(BF16)

    units = [(c, h) for c in range(n_chunks) for h in range(H)]
    blk = lambda t, c, h: t[c * C:(c + 1) * C, h * N:(h + 1) * N]
    lhs = {u: jnp.concatenate([blk(at, *u), blk(rt, *u)], axis=0) for u in units}
    rhs = {u: jnp.concatenate([blk(bt, *u), blk(kt, *u)], axis=0) for u in units}
    aa = {u: _dot_nt(lhs[u], rhs[u]) for u in units}
    a_ab = {u: jnp.where(lower_strict, aa[u][0:C, 0:C], 0.0) for u in units}
    a_kv = {u: jnp.concatenate([jnp.where(lower_strict, aa[u][0:C, C:2 * C], 0.0),
                                jnp.where(lower_incl, aa[u][C:2 * C, C:2 * C], 0.0)], axis=0).astype(BF16)
            for u in units}
    a_rb = {u: jnp.where(lower_incl, aa[u][C:2 * C, 0:C], 0.0).astype(BF16) for u in units}
    akv = {u: _dot(a_kv[u], blk(vb, *u)) for u in units}
    tinv = {u: eye + a_ab[u] for u in units}
    npow = a_ab
    for _ in range(5):
        npb = {u: npow[u].astype(BF16) for u in units}
        npow = {u: _dot(npb[u], npb[u]) for u in units}
        tinv = {u: tinv[u] + _dot(tinv[u].astype(BF16), npow[u].astype(BF16)) for u in units}
    tinv_b = {u: tinv[u].astype(BF16) for u in units}

    state = [s_scr[h] for h in range(H)]
    y_rows = []
    for c in range(n_chunks):
        hs = [(c, h) for h in range(H)]
        ar_s = {u: _dot_nt(lhs[u], state[u[1]].astype(BF16)) for u in hs}
        ub = {u: _dot(tinv_b[u], (ar_s[u][0:C] + akv[u][0:C]).astype(BF16)).astype(BF16) for u in hs}
        ys = [ar_s[u][C:2 * C] + akv[u][C:2 * C] + _dot(a_rb[u], ub[u]) for u in hs]
        upd = {u: _dot_tn(jnp.concatenate([ub[u], blk(vb, *u)], axis=0),
                          jnp.concatenate([blk(bh, *u), blk(kh, *u)], axis=0)) for u in hs}
        state = [state[h] * e_tot[c * C:c * C + 1, h * N:(h + 1) * N] + upd[(c, h)] for h in range(H)]
        y_rows.append(jnp.concatenate(ys, axis=1))
    for h in range(H):
        s_scr[h] = state[h]

    y = jnp.concatenate(y_rows, axis=0)
    mu = _dot_exact_rhs(y, bones, 1) * (1.0 / N)
    yc = y - mu
    var = _dot_exact_rhs(yc * yc, bones, 1) * (1.0 / N)
    yn = yc * lax.rsqrt(var + GN_EPS) * gng_ref[...] + gnb_ref[...]
    bonus = _dot_exact_rhs(r * kp * rk_ref[...], bones, 1) * v
    y_ref[...] = ((yn + bonus) * g).astype(BF16)


def _rwkv(p_rwkv, B, T, shift_mix, decay_w0, decay_up, iclr_a0, iclr_up, gate_up, k_k, k_a, r_k, gn_g, gn_b):
    tt = RWKV_TT
    n_t = T // tt
    head = jnp.arange(RWKV_DIM) // RWKV_HEAD_DIM
    bones = (head[:, None] == head[None, :]).astype(BF16)
    vec = lambda a: a.reshape(1, -1)
    full = lambda shape: pl.BlockSpec(shape, lambda b, j: (0, 0))
    return pl.pallas_call(
        _rwkv_kernel,
        grid=(B, n_t),
        in_specs=[
            pl.BlockSpec((tt, RWKV_COLS), lambda b, j: (b * n_t + j, 0)),
            full((1, RWKV_COLS)), full((1, RWKV_DIM)), full((DECAY_RANK, RWKV_DIM)),
            full((1, RWKV_DIM)), full((ICLR_RANK, RWKV_DIM)), full((GATE_RANK, RWKV_DIM)),
            full((1, RWKV_DIM)), full((1, RWKV_DIM)), full((1, RWKV_DIM)),
            full((1, RWKV_DIM)), full((1, RWKV_DIM)), full((RWKV_DIM, RWKV_DIM)),
        ],
        out_specs=pl.BlockSpec((tt, RWKV_DIM), lambda b, j: (b * n_t + j, 0)),
        out_shape=jax.ShapeDtypeStruct((B * T, RWKV_DIM), BF16),
        scratch_shapes=[
            pltpu.VMEM((RWKV_HEADS, RWKV_HEAD_DIM, RWKV_HEAD_DIM), F32),
            pltpu.VMEM((1, RWKV_COLS), F32),
        ],
        compiler_params=_cparams(("parallel", "arbitrary")),
        name="rwkv",
    )(p_rwkv, vec(shift_mix), vec(decay_w0), decay_up.astype(BF16), vec(iclr_a0), iclr_up.astype(BF16),
      gate_up.astype(BF16), vec(k_k), vec(k_a), vec(r_k), vec(gn_g), vec(gn_b), bones)


MOBA_HP = 8


def _moba_kernel(q_ref, k_ref, v_ref, o_ref, kmean_scr, sel_scr, acc_scr, score_even, score_odd):
    blk_sz, dh = MOBA_BLOCK, ATTN_HEAD_DIM
    nb = k_ref.shape[0] // blk_sz
    i = pl.program_id(2)

    @pl.when(i == 0)
    def _():
        for n in range(nb):
            kb = k_ref[n * blk_sz:(n + 1) * blk_sz, :].astype(F32)
            kmean_scr[n:n + 1, :] = jnp.sum(kb, axis=0, keepdims=True) * (1.0 / blk_sz)

    blk = lax.broadcasted_iota(I32, (nb, blk_sz), 0)
    kpos = lax.broadcasted_iota(I32, (blk_sz, blk_sz), 0)
    qpos = lax.broadcasted_iota(I32, (blk_sz, blk_sz), 1)
    own_start = pl.multiple_of(i * blk_sz, blk_sz)
    heads = [slice(hh * dh, (hh + 1) * dh) for hh in range(MOBA_HP)]
    hds = range(MOBA_HP)
    qss = [q_ref[:, hs] for hs in heads]
    kmean_pieces = _bf16_pieces(kmean_scr[...], 3)
    gates = [sum(_dot_nt(piece[:, heads[hh]], qss[hh]) for piece in kmean_pieces) for hh in hds]
    def scores(n):
        start = pl.multiple_of(n * blk_sz, blk_sz)
        return [_dot_nt(k_ref[pl.ds(start, blk_sz), heads[hh]], qss[hh]) for hh in hds]

    own = scores(i)
    first = scores(0)
    for hh in hds:
        score_even[hh] = first[hh]
    for hh in hds:
        work = jnp.where(blk < i, gates[hh], NEG_INF)
        chosen = blk < 0
        for _ in range(MOBA_TOPK):
            top = jnp.max(work, axis=0, keepdims=True)
            lowest = jnp.min(jnp.where(work == top, blk, nb), axis=0, keepdims=True)
            hit = blk == lowest
            chosen = chosen | hit
            work = jnp.where(hit, -jnp.inf, work)
        sel_scr[hh] = jnp.where(chosen & (blk < i), 1.0, 0.0)
    stats, ps = [], []
    for hh in hds:
        s = jnp.where(kpos <= qpos, own[hh], NEG_INF)
        m0 = jnp.max(s, axis=0, keepdims=True)
        p = jnp.exp2(s - m0)
        stats += [m0, jnp.sum(p, axis=0, keepdims=True)]
        ps.append(p.astype(BF16))
    pvs = [_dot_tn(v_ref[pl.ds(own_start, blk_sz), heads[hh]], ps[hh]) for hh in hds]
    for hh in hds:
        acc_scr[hh] = pvs[hh]

    def step(n, carry, here, there):
        ahead = scores(jnp.minimum(n + 1, i - 1))
        for hh in hds:
            there[hh] = ahead[hh]
        start = pl.multiple_of(n * blk_sz, blk_sz)
        out, ps, alphas = [], [], []
        for hh in hds:
            m_run, l_run = carry[2 * hh], carry[2 * hh + 1]
            s = jnp.where(sel_scr[hh, pl.ds(n, 1), :] > 0.0, here[hh], NEG_INF)
            m_new = jnp.maximum(m_run, jnp.max(s, axis=0, keepdims=True))
            alpha = jnp.exp2(m_run - m_new)
            p = jnp.exp2(s - m_new)
            out += [m_new, alpha * l_run + jnp.sum(p, axis=0, keepdims=True)]
            ps.append(p.astype(BF16))
            alphas.append(alpha)
        pvs = [_dot_tn(v_ref[pl.ds(start, blk_sz), heads[hh]], ps[hh]) for hh in hds]
        for hh in hds:
            acc_scr[hh] = alphas[hh] * acc_scr[hh] + pvs[hh]
        return tuple(out)

    def two_blocks(t, carry):
        carry = step(2 * t, carry, score_even, score_odd)
        return step(2 * t + 1, carry, score_odd, score_even)

    stats = lax.fori_loop(0, i // 2, two_blocks, tuple(stats))
    stats = lax.cond(i % 2 == 1, lambda c: step(i - 1, c, score_even, score_odd), lambda c: c, stats)
    outs = [(acc_scr[hh] / stats[2 * hh + 1]).T for hh in range(MOBA_HP)]
    o_ref[...] = jnp.concatenate(outs, axis=1).astype(BF16)


def _moba(q, k, v, B, T):
    blk_sz = MOBA_BLOCK
    nq = T // blk_sz
    lanes = MOBA_HP * ATTN_HEAD_DIM
    kv_spec = pl.BlockSpec((T, lanes), lambda b, hp, i: (b, hp))
    q_spec = pl.BlockSpec((blk_sz, lanes), lambda b, hp, i: (b * nq + i, hp))
    return pl.pallas_call(
        _moba_kernel,
        grid=(B, ATTN_HEADS // MOBA_HP, nq),
        in_specs=[q_spec, kv_spec, kv_spec],
        out_specs=q_spec,
        out_shape=jax.ShapeDtypeStruct((B * T, ATTN_DIM), BF16),
        scratch_shapes=[
            pltpu.VMEM((T // blk_sz, lanes), F32),
            pltpu.VMEM((MOBA_HP, T // blk_sz, blk_sz), F32),
            pltpu.VMEM((MOBA_HP, ATTN_HEAD_DIM, blk_sz), F32),
            pltpu.VMEM((MOBA_HP, blk_sz, blk_sz), F32),
            pltpu.VMEM((MOBA_HP, blk_sz, blk_sz), F32),
        ],
        compiler_params=_cparams(("parallel", "parallel", "arbitrary")),
        name="moba",
    )(q, k, v)


MERGE_TM = 512
MERGE_SUB = 128


def _layer_norm(h, g, b):
    mu = jnp.mean(h, axis=-1, keepdims=True)
    hc = h - mu
    var = jnp.mean(hc * hc, axis=-1, keepdims=True)
    return hc * lax.rsqrt(var + LN_EPS) * g + b


def _merge_kernel(x_ref, yr_ref, ya_ref, g_ref, wbr_ref, wba_ref, wo_ref, lng_ref, lnb_ref, rwt_ref, rb_ref,
                  x1_ref, lgt_ref):
    subs = [slice(s * MERGE_SUB, (s + 1) * MERGE_SUB) for s in range(MERGE_TM // MERGE_SUB)]
    yr = [_dot(yr_ref[sl, :], wbr_ref[...]) for sl in subs]
    ya = [_dot(ya_ref[sl, :], wba_ref[...]) for sl in subs]
    merged = [(g_ref[sl, 0:D_MODEL].astype(F32) * yr[s] + g_ref[sl, D_MODEL:2 * D_MODEL].astype(F32) * ya[s])
              .astype(BF16) for s, sl in enumerate(subs)]
    mix = [_dot(m, wo_ref[...]) for m in merged]
    x1 = [_layer_norm(DEEPNORM_ALPHA * x_ref[sl, :] + mix[s], lng_ref[...], lnb_ref[...])
          for s, sl in enumerate(subs)]
    for s, sl in enumerate(subs):
        x1_ref[sl, :] = x1[s]
    rw_a, rw_b = _bf16_pieces(rwt_ref[...], 2)
    for s, sl in enumerate(subs):
        x_a, x_b = _bf16_pieces(x1[s], 2)
        lgt_ref[:, sl] = _dot_nt(rw_a, x_a) + _dot_nt(rw_a, x_b) + _dot_nt(rw_b, x_a) + rb_ref[...]


def _merge(x2, y_rwkv, y_attn, gates, w_br, w_ba, w_o, ln_g, ln_b, router_w, router_b):
    n_tok = x2.shape[0]
    tm = MERGE_TM
    row = lambda i: (i, 0)
    full = lambda shape: pl.BlockSpec(shape, lambda i: (0, 0))
    return pl.pallas_call(
        _merge_kernel,
        grid=(n_tok // tm,),
        in_specs=[
            pl.BlockSpec((tm, D_MODEL), row), pl.BlockSpec((tm, RWKV_DIM), row), pl.BlockSpec((tm, ATTN_DIM), row),
            pl.BlockSpec((tm, GATE_COLS), row),
            full((RWKV_DIM, D_MODEL)), full((ATTN_DIM, D_MODEL)), full((D_MODEL, D_MODEL)),
            full((1, D_MODEL)), full((1, D_MODEL)), full((N_EXPERTS, D_MODEL)), full((N_EXPERTS, 1)),
        ],
        out_specs=[pl.BlockSpec((tm, D_MODEL), row), pl.BlockSpec((N_EXPERTS, tm), lambda i: (0, i))],
        out_shape=[jax.ShapeDtypeStruct((n_tok, D_MODEL), F32), jax.ShapeDtypeStruct((N_EXPERTS, n_tok), F32)],
        compiler_params=_cparams(("parallel",)),
        name="merge",
    )(x2, y_rwkv, y_attn, gates, w_br.astype(BF16), w_ba.astype(BF16), w_o.astype(BF16),
      ln_g.reshape(1, -1), ln_b.reshape(1, -1), router_w.T, router_b.reshape(-1, 1))


MOE_TM = 512
ROW_BLOCK = 512
PIECE_ALIGN = 2 * SUBLANE
PIECE_SIZES = (512, 256, 128, 64, 32, 16)
TAIL_SIZES = (256, 128, 64, 32, 16)
STAGE_CHUNK = 256
ROW_W = D_MODEL + LANE


def _moe_dims(n_tok):
    n_tiles = n_tok // MOE_TM
    stage_rows = -(-(MOE_TM * TOP_K + N_EXPERTS * (PIECE_ALIGN - 1)) // STAGE_CHUNK) * STAGE_CHUNK
    max_rows = n_tok * TOP_K + n_tiles * N_EXPERTS * (PIECE_ALIGN - 1) + N_EXPERTS * (ROW_BLOCK - 1)
    n_blk = -(-max_rows // ROW_BLOCK)
    map_lanes = -(-(n_blk + 1) // LANE) * LANE
    return n_tiles, stage_rows, n_blk, map_lanes


def _round_up_f32(x, m):
    return jnp.floor((x + (m - 1)) * (1.0 / m)) * m


def _route_kernel(lg_ref, wt_ref, idx_ref, slot_ref, off_ref, cnt_ref, end_ref, map_ref,
                  idx_scr, pos_scr, cnt_scr, off_scr):
    tm = MOE_TM
    n_tiles = cnt_scr.shape[1]
    phase = pl.program_id(0)
    i = pl.program_id(1)
    tok0 = pl.multiple_of(i * tm, tm)
    eio = lax.broadcasted_iota(I32, (N_EXPERTS, tm), 0)
    tile_lane = lax.broadcasted_iota(I32, (N_EXPERTS, n_tiles), 1)
    e_from = lax.broadcasted_iota(I32, (N_EXPERTS, N_EXPERTS), 1)
    e_to = lax.broadcasted_iota(I32, (N_EXPERTS, N_EXPERTS), 0)
    earlier_e = jnp.where(e_from < e_to, 1.0, 0.0)

    @pl.when((phase == 0) & (i == 0))
    def _():
        cnt_scr[...] = jnp.zeros_like(cnt_scr)

    @pl.when(phase == 0)
    def _():
        work = lg_ref[...]
        vals, hots = [], []
        for s in range(TOP_K):
            m = jnp.max(work, axis=0, keepdims=True)
            ix = jnp.min(jnp.where(work == m, eio, N_EXPERTS), axis=0, keepdims=True)
            hot = eio == ix
            idx_scr[s:s + 1, pl.ds(tok0, tm)] = ix
            idx_ref[s:s + 1, :] = ix
            vals.append(m)
            hots.append(hot)
            work = jnp.where(hot, -jnp.inf, work)
        es = [jnp.exp(v - vals[0]) for v in vals]
        denom = es[0] + es[1] + es[2] + es[3]
        for s in range(TOP_K):
            wt_ref[s:s + 1, :] = es[s] / denom
        multi_f = jnp.where(hots[0] | hots[1] | hots[2] | hots[3], 1.0, 0.0)
        t_from = lax.broadcasted_iota(I32, (tm, tm), 0)
        t_to = lax.broadcasted_iota(I32, (tm, tm), 1)
        before = jnp.where(t_from < t_to, 1.0, 0.0).astype(BF16)
        count = _dot(multi_f.astype(BF16), before)
        for s in range(TOP_K):
            pos_scr[s:s + 1, pl.ds(tok0, tm)] = jnp.sum(
                jnp.where(hots[s], count, 0.0), axis=0, keepdims=True).astype(I32)
        cnt8 = _round_up_f32(jnp.sum(multi_f, axis=1, keepdims=True), PIECE_ALIGN)
        cnt_scr[...] = cnt_scr[...] + jnp.where(tile_lane == i, cnt8, 0.0)

    @pl.when((phase == 1) & (i == 0))
    def _():
        cnt8 = cnt_scr[...]
        tot = _round_up_f32(jnp.sum(cnt8, axis=1, keepdims=True), ROW_BLOCK)
        p_start = _dot_hi(earlier_e, jnp.broadcast_to(tot, (N_EXPERTS, n_tiles)))
        i_from = lax.broadcasted_iota(I32, (n_tiles, n_tiles), 0)
        i_to = lax.broadcasted_iota(I32, (n_tiles, n_tiles), 1)
        earlier_tiles = _dot_hi(cnt8, jnp.where(i_from < i_to, 1.0, 0.0))
        off_scr[...] = p_start + earlier_tiles
        off_ref[...] = off_scr[...].astype(I32)
        cnt_ref[...] = cnt8.astype(I32)
        p_end = p_start[:, 0:1] + tot
        end_ref[...] = p_end.astype(I32)
        lanes = map_ref.shape[1]
        blk_start = lax.broadcasted_iota(I32, (N_EXPERTS, lanes), 1).astype(F32) * ROW_BLOCK
        blk_e = jnp.sum(jnp.where(blk_start >= p_end, 1.0, 0.0), axis=0, keepdims=True)
        blk_e = jnp.minimum(blk_e, N_EXPERTS - 1.0)
        n_used = jnp.max(p_end, axis=0, keepdims=True) * (1.0 / ROW_BLOCK)
        last = lax.broadcasted_iota(I32, (1, lanes), 1) == lanes - 1
        map_ref[...] = jnp.where(last, n_used, blk_e).astype(I32)

    @pl.when(phase == 1)
    def _():
        cnt_col = jnp.sum(jnp.where(tile_lane == i, cnt_scr[...], 0.0), axis=1, keepdims=True)
        local_off = _dot_hi(earlier_e, jnp.broadcast_to(cnt_col, (N_EXPERTS, tm)))
        for s in range(TOP_K):
            hot = eio == idx_scr[s:s + 1, pl.ds(tok0, tm)]
            base = jnp.sum(jnp.where(hot, local_off, 0.0), axis=0, keepdims=True).astype(I32)
            slot_ref[s:s + 1, :] = base + pos_scr[s:s + 1, pl.ds(tok0, tm)]


def _route(logits_t):
    n_tok = logits_t.shape[1]
    n_tiles, _, _, map_lanes = _moe_dims(n_tok)
    tm = MOE_TM
    last = n_tiles - 1
    const = lambda p, i: (0, 0)
    return pl.pallas_call(
        _route_kernel,
        grid=(2, n_tiles),
        in_specs=[pl.BlockSpec((N_EXPERTS, tm), lambda p, i: (0, i * (1 - p) + last * p))],
        out_specs=[
            pl.BlockSpec((TOP_K, tm), lambda p, i: (0, i * (1 - p) + last * p)),
            pl.BlockSpec((TOP_K, tm), lambda p, i: (0, i * (1 - p) + last * p)),
            pl.BlockSpec((TOP_K, tm), lambda p, i: (0, i * p)),
            pl.BlockSpec((N_EXPERTS, n_tiles), const),
            pl.BlockSpec((N_EXPERTS, n_tiles), const),
            pl.BlockSpec((N_EXPERTS, 1), const),
            pl.BlockSpec((1, map_lanes), const),
        ],
        out_shape=[
            jax.ShapeDtypeStruct((TOP_K, n_tok), F32),
            jax.ShapeDtypeStruct((TOP_K, n_tok), I32),
            jax.ShapeDtypeStruct((TOP_K, n_tok), I32),
            jax.ShapeDtypeStruct((N_EXPERTS, n_tiles), I32),
            jax.ShapeDtypeStruct((N_EXPERTS, n_tiles), I32),
            jax.ShapeDtypeStruct((N_EXPERTS, 1), I32),
            jax.ShapeDtypeStruct((1, map_lanes), I32),
        ],
        scratch_shapes=[
            pltpu.VMEM((TOP_K, n_tok), I32),
            pltpu.VMEM((TOP_K, n_tok), I32),
            pltpu.VMEM((N_EXPERTS, n_tiles), F32),
            pltpu.VMEM((N_EXPERTS, n_tiles), F32),
        ],
        compiler_params=_cparams(("arbitrary", "arbitrary")),
        name="route",
    )(logits_t)


def _piece_copies(off_ref, cnt_ref, tile, make_copy, action):
    def per_expert(e, stage_row):
        c8 = cnt_ref[tile * N_EXPERTS + e]
        hbm_row = off_ref[tile * N_EXPERTS + e]
        done = 0
        for size in PIECE_SIZES:
            bit = c8 & size

            @pl.when(bit != 0)
            def _(done=done, size=size):
                action(make_copy(pl.multiple_of(stage_row + done, PIECE_ALIGN),
                                 pl.multiple_of(hbm_row + done, PIECE_ALIGN), size))

            done = done + bit
        return stage_row + c8

    lax.fori_loop(0, N_EXPERTS, per_expert, 0)


def _slot_rows(slot_ref):
    return [slot_ref[s:s + 1, :] for s in range(TOP_K)]


def _one_hot_rows(slots, r0):
    rio = lax.broadcasted_iota(I32, (STAGE_CHUNK, MOE_TM), 0) + r0
    pick = jnp.zeros((STAGE_CHUNK, MOE_TM), F32)
    for s in range(TOP_K):
        pick = jnp.where(rio == slots[s], 1.0, pick)
    return pick.astype(BF16)


def _dispatch_kernel(off_ref, cnt_ref, end_ref, slot_ref, wt_ref, idx_ref, x1_ref, xs_hbm, stage, zeros, sem,
                     zero_sem):
    i = pl.program_id(0)
    n_tiles = pl.num_programs(0)
    buf = i % 2

    @pl.when(i == 0)
    def _():
        zeros[...] = jnp.zeros_like(zeros)

        def tail_copies(action):
            def per_expert(e, carry):
                last_piece = (n_tiles - 1) * N_EXPERTS + e
                start = off_ref[last_piece] + cnt_ref[last_piece]
                tail = end_ref[e] - start
                done = 0
                for size in TAIL_SIZES:
                    bit = tail & size

                    @pl.when(bit != 0)
                    def _(done=done, size=size):
                        action(pltpu.make_async_copy(
                            zeros.at[pl.ds(0, size)],
                            xs_hbm.at[pl.ds(pl.multiple_of(start + done, PIECE_ALIGN), size)], zero_sem))

                    done = done + bit
                return carry

            lax.fori_loop(0, N_EXPERTS, per_expert, 0)

            def per_block(b, carry):
                action(pltpu.make_async_copy(
                    zeros, xs_hbm.at[pl.ds(pl.multiple_of(b * ROW_BLOCK, ROW_BLOCK), ROW_BLOCK)], zero_sem))
                return carry

            lax.fori_loop(end_ref[N_EXPERTS - 1] // ROW_BLOCK, xs_hbm.shape[0] // ROW_BLOCK, per_block, 0)

        tail_copies(lambda cp: cp.start())
        tail_copies(lambda cp: cp.wait())

    def out_copies(tile, b, action):
        def make_copy(stage_row, hbm_row, size):
            return pltpu.make_async_copy(stage.at[b, pl.ds(stage_row, size)], xs_hbm.at[pl.ds(hbm_row, size)],
                                         sem.at[b])
        _piece_copies(off_ref, cnt_ref, tile, make_copy, action)

    @pl.when(i >= 2)
    def _():
        out_copies(i - 2, buf, lambda cp: cp.wait())

    w = wt_ref[...]
    w_a = w.astype(BF16).astype(F32)
    tail_t = jnp.concatenate(
        [w_a, w - w_a, idx_ref[...].astype(F32), jnp.zeros((LANE - 3 * TOP_K, MOE_TM), F32)], axis=0)
    xb = jnp.concatenate([x1_ref[...].astype(BF16), tail_t.T.astype(BF16)], axis=1)
    slots = _slot_rows(slot_ref)
    for r0 in range(0, stage.shape[1], STAGE_CHUNK):
        stage[buf, r0:r0 + STAGE_CHUNK, :] = _dot(_one_hot_rows(slots, r0), xb).astype(BF16)
    out_copies(i, buf, lambda cp: cp.start())

    @pl.when(i == n_tiles - 1)
    def _():
        @pl.when(i >= 1)
        def _():
            out_copies(i - 1, 1 - buf, lambda cp: cp.wait())

        out_copies(i, buf, lambda cp: cp.wait())


def _dispatch(off_flat, cnt_flat, end_flat, slot, wts, idx, x1):
    n_tok = x1.shape[0]
    n_tiles, stage_rows, n_blk, _ = _moe_dims(n_tok)
    n_rows = n_blk * ROW_BLOCK
    return pl.pallas_call(
        _dispatch_kernel,
        grid_spec=pltpu.PrefetchScalarGridSpec(
            num_scalar_prefetch=3,
            grid=(n_tiles,),
            in_specs=[pl.BlockSpec((TOP_K, MOE_TM), lambda i, off, cnt, end: (0, i)),
                      pl.BlockSpec((TOP_K, MOE_TM), lambda i, off, cnt, end: (0, i)),
                      pl.BlockSpec((TOP_K, MOE_TM), lambda i, off, cnt, end: (0, i)),
                      pl.BlockSpec((MOE_TM, D_MODEL), lambda i, off, cnt, end: (i, 0))],
            out_specs=pl.BlockSpec(memory_space=pl.ANY),
            scratch_shapes=[pltpu.VMEM((2, stage_rows, ROW_W), BF16),
                            pltpu.VMEM((ROW_BLOCK, ROW_W), BF16),
                            pltpu.SemaphoreType.DMA((2,)), pltpu.SemaphoreType.DMA(())],
        ),
        out_shape=jax.ShapeDtypeStruct((n_rows, ROW_W), BF16),
        compiler_params=_cparams(("arbitrary",)),
        name="dispatch",
    )(off_flat, cnt_flat, end_flat, slot, wts, idx, x1)


def _combine_kernel(off_ref, cnt_ref, slot_ref, x1_ref, lng_ref, lnb_ref, ys_hbm, out_ref, stage, sem):
    i = pl.program_id(0)
    n_tiles = pl.num_programs(0)
    buf = i % 2

    def in_copies(tile, b, action):
        def make_copy(stage_row, hbm_row, size):
            return pltpu.make_async_copy(ys_hbm.at[pl.ds(hbm_row, size)], stage.at[b, pl.ds(stage_row, size)],
                                         sem.at[b])
        _piece_copies(off_ref, cnt_ref, tile, make_copy, action)

    @pl.when(i == 0)
    def _():
        stage[...] = jnp.zeros_like(stage)
        in_copies(0, 0, lambda cp: cp.start())

    @pl.when(i + 1 < n_tiles)
    def _():
        in_copies(i + 1, 1 - buf, lambda cp: cp.start())

    in_copies(i, buf, lambda cp: cp.wait())

    slots = _slot_rows(slot_ref)
    ffn = jnp.zeros((MOE_TM, D_MODEL), F32)
    for r0 in range(0, stage.shape[1], STAGE_CHUNK):
        ffn = ffn + _dot_tn(_one_hot_rows(slots, r0), stage[buf, r0:r0 + STAGE_CHUNK, :])
    out_ref[...] = _layer_norm(DEEPNORM_ALPHA * x1_ref[...] + ffn, lng_ref[...], lnb_ref[...])


def _combine(off_flat, cnt_flat, slot, x1, ln_g, ln_b, ys):
    n_tok = x1.shape[0]
    n_tiles, stage_rows, _, _ = _moe_dims(n_tok)
    tok = lambda i, off, cnt: (0, i)
    row = lambda i, off, cnt: (i, 0)
    full = lambda shape: pl.BlockSpec(shape, lambda i, off, cnt: (0, 0))
    return pl.pallas_call(
        _combine_kernel,
        grid_spec=pltpu.PrefetchScalarGridSpec(
            num_scalar_prefetch=2,
            grid=(n_tiles,),
            in_specs=[pl.BlockSpec((TOP_K, MOE_TM), tok),
                      pl.BlockSpec((MOE_TM, D_MODEL), row), full((1, D_MODEL)), full((1, D_MODEL)),
                      pl.BlockSpec(memory_space=pl.ANY)],
            out_specs=pl.BlockSpec((MOE_TM, D_MODEL), row),
            scratch_shapes=[pltpu.VMEM((2, stage_rows, D_MODEL), BF16), pltpu.SemaphoreType.DMA((2,))],
        ),
        out_shape=jax.ShapeDtypeStruct((n_tok, D_MODEL), F32),
        compiler_params=_cparams(("arbitrary",)),
        name="combine",
    )(off_flat, cnt_flat, slot, x1, ln_g.reshape(1, -1), ln_b.reshape(1, -1), ys)


CAST_ROWS = 128
FFN_CHUNKS = 4


def _expert_kernel(blk_e_ref, n_used_ref, xs_ref, win_ref, bin_ref, wout_ref, bout_ref, ys_ref, win_bf, wout_bf):
    rb = pl.program_id(0)
    new_expert = (rb == 0) | (blk_e_ref[rb] != blk_e_ref[jnp.maximum(rb - 1, 0)])

    @pl.when((rb < n_used_ref[0]) & new_expert)
    def _():
        for r in range(0, D_MODEL, CAST_ROWS):
            win_bf[r:r + CAST_ROWS, :] = win_ref[0, r:r + CAST_ROWS, :].astype(BF16)
        for r in range(0, D_EXPERT, CAST_ROWS):
            wout_bf[r:r + CAST_ROWS, :] = wout_ref[0, r:r + CAST_ROWS, :].astype(BF16)

    @pl.when(rb < n_used_ref[0])
    def _():
        x = xs_ref[:, 0:D_MODEL]
        tail = xs_ref[:, D_MODEL:ROW_W].astype(F32)
        mine = blk_e_ref[rb].astype(F32)
        w_row = jnp.zeros((tail.shape[0], 1), F32)
        for s in range(TOP_K):
            w_s = tail[:, s:s + 1] + tail[:, TOP_K + s:TOP_K + s + 1]
            w_row = jnp.where(tail[:, 2 * TOP_K + s:2 * TOP_K + s + 1] == mine, w_s, w_row)
        cw = D_EXPERT // FFN_CHUNKS

        def hidden(j):
            gs = slice(j * cw, (j + 1) * cw)
            ls = slice(D_EXPERT + j * cw, D_EXPERT + (j + 1) * cw)
            return _dot(x, win_bf[:, gs]) + bin_ref[0, :, gs], _dot(x, win_bf[:, ls]) + bin_ref[0, :, ls]

        ahead = hidden(0)
        y = bout_ref[0]
        for j in range(FFN_CHUNKS):
            g, lin = ahead
            if j + 1 < FFN_CHUNKS:
                ahead = hidden(j + 1)
            gate_h = jnp.minimum(g, SWIGLU_LIMIT)
            lin_h = jnp.clip(lin, -SWIGLU_LIMIT, SWIGLU_LIMIT)
            act = gate_h * jax.nn.sigmoid(SWIGLU_ALPHA * gate_h) * (lin_h + 1.0)
            y = y + _dot(act.astype(BF16), wout_bf[j * cw:(j + 1) * cw, :])
        ys_ref[...] = (y * w_row).astype(BF16)

    @pl.when(rb >= n_used_ref[0])
    def _():
        ys_ref[...] = jnp.zeros_like(ys_ref)


def _experts(blk_e, n_used, xs, w_in, b_in, w_out, b_out):
    n_rows = xs.shape[0]
    n_blk = n_rows // ROW_BLOCK
    used = lambda rb, n_used: jnp.maximum(jnp.minimum(rb, n_used[0] - 1), 0)
    rows = lambda rb, blk_e, n_used: (used(rb, n_used), 0)
    per_e = lambda rb, blk_e, n_used: (blk_e[used(rb, n_used)], 0, 0)
    return pl.pallas_call(
        _expert_kernel,
        grid_spec=pltpu.PrefetchScalarGridSpec(
            num_scalar_prefetch=2,
            grid=(n_blk,),
            in_specs=[
                pl.BlockSpec((ROW_BLOCK, ROW_W), rows),
                pl.BlockSpec((1, D_MODEL, 2 * D_EXPERT), per_e),
                pl.BlockSpec((1, 1, 2 * D_EXPERT), per_e),
                pl.BlockSpec((1, D_EXPERT, D_MODEL), per_e),
                pl.BlockSpec((1, 1, D_MODEL), per_e),
            ],
            out_specs=pl.BlockSpec((ROW_BLOCK, D_MODEL), lambda rb, blk_e, n_used: (rb, 0)),
            scratch_shapes=[pltpu.VMEM((D_MODEL, 2 * D_EXPERT), BF16), pltpu.VMEM((D_EXPERT, D_MODEL), BF16)],
        ),
        out_shape=jax.ShapeDtypeStruct((n_rows, D_MODEL), BF16),
        compiler_params=_cparams(("arbitrary",)),
        name="experts",
    )(blk_e, n_used, xs, w_in, b_in.reshape(N_EXPERTS, 1, -1), w_out, b_out.reshape(N_EXPERTS, 1, -1))


def _moe(x1, logits_t, expert_w_in, expert_b_in, expert_w_out, expert_b_out, ln_g, ln_b):
    n_tok = x1.shape[0]
    _, _, n_blk, map_lanes = _moe_dims(n_tok)
    wts, idx, slot, off, cnt, ends, blk_map = _route(logits_t)
    off_flat = off.T.reshape(-1)
    cnt_flat = cnt.T.reshape(-1)
    blk_e = blk_map[0, 0:n_blk]
    n_used = blk_map[0, map_lanes - 1:map_lanes]
    xs = _dispatch(off_flat, cnt_flat, ends.reshape(-1), slot, wts, idx, x1)
    ys = _experts(blk_e, n_used, xs, expert_w_in, expert_b_in, expert_w_out, expert_b_out)
    return _combine(off_flat, cnt_flat, slot, x1, ln_g, ln_b, ys)


def kernel(x, ln1_g, ln1_b, ln2_g, ln2_b, w_in, shift_mix, decay_w0, decay_up, iclr_a0, iclr_up, gate_up, k_k, k_a, r_k, gn_g, gn_b, w_branch_rwkv, w_branch_attn, w_out, router_w, router_b, expert_w_in, expert_b_in, expert_w_out, expert_b_out):
    B, T, D = x.shape
    x2 = x.reshape(B * T, D)
    prw, q, k, v, gates = _proj(x2, w_in[0].astype(BF16), T)
    y_rwkv = _rwkv(prw, B, T, shift_mix[0], decay_w0[0], decay_up[0], iclr_a0[0], iclr_up[0], gate_up[0],
                   k_k[0], k_a[0], r_k[0], gn_g[0], gn_b[0])
    y_attn = _moba(q, k, v, B, T)
    x1, logits_t = _merge(x2, y_rwkv, y_attn, gates, w_branch_rwkv[0], w_branch_attn[0], w_out[0],
                          ln1_g[0], ln1_b[0], router_w[0], router_b[0])
    out = _moe(x1, logits_t, expert_w_in[0], expert_b_in[0], expert_w_out[0], expert_b_out[0], ln2_g[0], ln2_b[0])
    return out.reshape(B, T, D)
```

```python
import math

import jax
import jax.numpy as jnp
from jax import lax
from jax.experimental import pallas as pl
from jax.experimental.pallas import tpu as pltpu

F32 = jnp.float32
BF16 = jnp.bfloat16
I32 = jnp.int32
HI = lax.Precision.HIGHEST

D_MODEL = 1024
DEPTH = 1
RWKV_HEAD_DIM = 64
RWKV_DIM = 512
RWKV_HEADS = 8
DECAY_RANK = 64
ICLR_RANK = 64
GATE_RANK = 128
GN_EPS = 64e-5
ATTN_HEAD_DIM = 64
ATTN_DIM = 512
ATTN_HEADS = 8
MOBA_BLOCK = 256
MOBA_TOPK = 3
ROPE_THETA = 500000.0
ROPE_DIM = 16
NEG_INF = -1e30
N_EXPERTS = 32
TOP_K = 4
D_EXPERT = 1024
SWIGLU_LIMIT = 7.0
SWIGLU_ALPHA = 1.702
DEEPNORM_ALPHA = (2.0 * DEPTH) ** 0.25
LN_EPS = 1e-5
RWKV_COLS = 3 * RWKV_DIM + DECAY_RANK + ICLR_RANK + GATE_RANK
ATTN_COLS = 3 * ATTN_DIM
GATE_COLS = 2 * D_MODEL
IN_COLS = RWKV_COLS + ATTN_COLS + GATE_COLS

LANE = 128
SUBLANE = 8
VMEM_LIMIT_BYTES = 56 * 1024 * 1024

CHUNK = 64


def _cparams(sem):
    return pltpu.CompilerParams(dimension_semantics=sem, vmem_limit_bytes=VMEM_LIMIT_BYTES)


def _dot(a, b):
    return jnp.dot(a, b, preferred_element_type=F32)


def _dot_hi(a, b):
    return jnp.dot(a, b, preferred_element_type=F32, precision=HI)


def _dot_nt(a, b, precision=None):
    return lax.dot_general(a, b, (((1,), (1,)), ((), ())), preferred_element_type=F32, precision=precision)


def _dot_tn(a, b, precision=None):
    return lax.dot_general(a, b, (((0,), (0,)), ((), ())), preferred_element_type=F32, precision=precision)


def _bf16_pieces(x, n):
    pieces = []
    for _ in range(n):
        p = x.astype(BF16)
        pieces.append(p)
        x = x - p.astype(F32)
    return pieces


def _dot_exact_rhs(x, b_bf16, n):
    out = None
    for p in _bf16_pieces(x, n):
        d = _dot(p, b_bf16)
        out = d if out is None else out + d
    return out


PROJ_TM = 512
Q_SCALE = math.log2(math.e) * ATTN_HEAD_DIM ** -0.5


def _proj_kernel(x_ref, w_ref, cos_ref, sa_ref, sb_ref, prw_ref, q_ref, k_ref, v_ref, g_ref):
    xb = x_ref[...].astype(BF16)
    prw_ref[...] = _dot(xb, w_ref[:, 0:RWKV_COLS])
    c0 = RWKV_COLS
    widen = lambda t: jnp.concatenate([t] * (ATTN_DIM // LANE), axis=1)
    cos = widen(cos_ref[...])
    sa = widen(sa_ref[...])
    sb = widen(sb_ref[...])

    def rope(t):
        return t * cos + pltpu.roll(t, ATTN_DIM - ROPE_DIM // 2, 1) * sa + pltpu.roll(t, ROPE_DIM // 2, 1) * sb

    q_ref[...] = (rope(_dot(xb, w_ref[:, c0:c0 + ATTN_DIM])) * Q_SCALE).astype(BF16)
    k_ref[...] = rope(_dot(xb, w_ref[:, c0 + ATTN_DIM:c0 + 2 * ATTN_DIM])).astype(BF16)
    v_ref[...] = _dot(xb, w_ref[:, c0 + 2 * ATTN_DIM:c0 + 3 * ATTN_DIM]).astype(BF16)
    c1 = RWKV_COLS + ATTN_COLS
    g_ref[...] = jax.nn.sigmoid(_dot(xb, w_ref[:, c1:c1 + GATE_COLS])).astype(BF16)


def _rope_tables(T):
    half = ROPE_DIM // 2
    inv_freq = jnp.power(ROPE_THETA, -jnp.arange(0, ROPE_DIM, 2, dtype=F32) / ROPE_DIM)
    ang = jnp.arange(T).astype(F32)[:, None] * inv_freq[None, :]
    cos, sin = jnp.cos(ang), jnp.sin(ang)
    pad = jnp.zeros((T, ATTN_HEAD_DIM - ROPE_DIM), F32)
    cos_h = jnp.concatenate([cos, cos, pad + 1.0], axis=1)
    sa_h = jnp.concatenate([-sin, jnp.zeros((T, half), F32), pad], axis=1)
    sb_h = jnp.concatenate([jnp.zeros((T, half), F32), sin, pad], axis=1)
    tile = lambda t: jnp.tile(t, (1, LANE // ATTN_HEAD_DIM))
    return tile(cos_h), tile(sa_h), tile(sb_h)


def _proj(x2, w_in_bf, T):
    n_tok = x2.shape[0]
    tm = PROJ_TM
    t_tiles = T // tm
    cos, sa, sb = _rope_tables(T)
    row = lambda i: (i, 0)
    tab = lambda i: (i % t_tiles, 0)
    return pl.pallas_call(
        _proj_kernel,
        grid=(n_tok // tm,),
        in_specs=[
            pl.BlockSpec((tm, D_MODEL), row),
            pl.BlockSpec((D_MODEL, IN_COLS), lambda i: (0, 0), pipeline_mode=pl.Buffered(1)),
            pl.BlockSpec((tm, LANE), tab),
            pl.BlockSpec((tm, LANE), tab),
            pl.BlockSpec((tm, LANE), tab),
        ],
        out_specs=[
            pl.BlockSpec((tm, RWKV_COLS), row),
            pl.BlockSpec((tm, ATTN_DIM), row),
            pl.BlockSpec((tm, ATTN_DIM), row),
            pl.BlockSpec((tm, ATTN_DIM), row),
            pl.BlockSpec((tm, GATE_COLS), row),
        ],
        out_shape=[
            jax.ShapeDtypeStruct((n_tok, RWKV_COLS), F32),
            jax.ShapeDtypeStruct((n_tok, ATTN_DIM), BF16),
            jax.ShapeDtypeStruct((n_tok, ATTN_DIM), BF16),
            jax.ShapeDtypeStruct((n_tok, ATTN_DIM), BF16),
            jax.ShapeDtypeStruct((n_tok, GATE_COLS), BF16),
        ],
        compiler_params=_cparams(("parallel",)),
        name="proj",
    )(x2, w_in_bf, cos, sa, sb)


RWKV_TT = 256


def _rwkv_kernel(p_ref, mix_ref, w0_ref, dup_ref, a0_ref, iup_ref, gup_ref, kk_ref, ka_ref, rk_ref,
                 gng_ref, gnb_ref, ones_ref, y_ref, s_scr, prev_scr):
    H, N, C = RWKV_HEADS, RWKV_HEAD_DIM, CHUNK

    @pl.when(pl.program_id(1) == 0)
    def _():
        s_scr[...] = jnp.zeros_like(s_scr)
        prev_scr[...] = jnp.zeros_like(prev_scr)

    TT = RWKV_TT
    n_chunks = TT // C
    bones = ones_ref[...]
    row = lax.broadcasted_iota(I32, (C, C), 0)
    col = lax.broadcasted_iota(I32, (C, C), 1)
    lower_incl = col <= row
    lower_strict = col < row
    eye = jnp.where(col == row, 1.0, 0.0)
    t_row = lax.broadcasted_iota(I32, (TT, TT), 0)
    t_col = lax.broadcasted_iota(I32, (TT, TT), 1)
    same_chunk = (t_row // C) == (t_col // C)
    chunk_ltri = jnp.where(same_chunk & (t_col <= t_row), 1.0, 0.0).astype(BF16)

    p = p_ref[...]
    first_row = lax.broadcasted_iota(I32, (TT, RWKV_COLS), 0) == 0
    prev = jnp.where(first_row, prev_scr[...], pltpu.roll(p, 1, 0))
    prev_scr[...] = p[TT - 1:TT, :]
    ps = p + (prev - p) * mix_ref[...]
    r = ps[:, 0:RWKV_DIM]
    k = ps[:, RWKV_DIM:2 * RWKV_DIM]
    v = ps[:, 2 * RWKV_DIM:3 * RWKV_DIM]
    o = 3 * RWKV_DIM
    xw = ps[:, o:o + DECAY_RANK]
    xa = ps[:, o + DECAY_RANK:o + DECAY_RANK + ICLR_RANK]
    xg = ps[:, o + DECAY_RANK + ICLR_RANK:RWKV_COLS]
    w_raw = w0_ref[...] + _dot(jnp.tanh(xw).astype(BF16), dup_ref[...])
    logw = -math.exp(-0.5) * jax.nn.sigmoid(w_raw)
    a = jax.nn.sigmoid(a0_ref[...] + _dot(xa.astype(BF16), iup_ref[...]))
    g = _dot(jax.nn.sigmoid(xg).astype(BF16), gup_ref[...])
    kk0 = k * kk_ref[...]
    kk = kk0 / jnp.maximum(jnp.sqrt(_dot_exact_rhs(kk0 * kk0, bones, 1)), 1e-12)
    kp = k * (1.0 + (a - 1.0) * ka_ref[...])
    kka = kk * a
    logw_pieces = _bf16_pieces(logw, 2)
    cum = sum(_dot(chunk_ltri, piece) for piece in logw_pieces)
    tot = jnp.concatenate([jnp.broadcast_to(cum[(c + 1) * C - 1:(c + 1) * C, :], (C, RWKV_DIM))
                           for c in range(n_chunks)], axis=0)
    e_neg = jnp.exp(-cum)
    at = (-kk * jnp.exp(cum - logw)).astype(BF16)
    rt = (r * jnp.exp(cum)).astype(BF16)
    bt = (kka * e_neg).astype(BF16)
    kt = (kp * e_neg).astype(BF16)
    e_end = jnp.exp(tot - cum)
    bh = (kka * e_end).astype(BF16)
    kh = (kp * e_end).astype(BF16)
    e_tot = jnp.exp(tot)
    vb = v.astype(BF16)

    units = [(c, h) for c in range(n_chunks) for h in range(H)]
    blk = lambda t, c, h: t[c * C:(c + 1) * C, h * N:(h + 1) * N]
    lhs = {u: jnp.concatenate([blk(at, *u), blk(rt, *u)], axis=0) for u in units}
    rhs = {u: jnp.concatenate([blk(bt, *u), blk(kt, *u)], axis=0) for u in units}
    aa = {u: _dot_nt(lhs[u], rhs[u]) for u in units}
    a_ab = {u: jnp.where(lower_strict, aa[u][0:C, 0:C], 0.0) for u in units}
    a_kv = {u: jnp.concatenate([jnp.where(lower_strict, aa[u][0:C, C:2 * C], 0.0),
                                jnp.where(lower_incl, aa[u][C:2 * C, C:2 * C], 0.0)], axis=0).astype(BF16)
            for u in units}
    a_rb = {u: jnp.where(lower_incl, aa[u][C:2 * C, 0:C], 0.0).astype(BF16) for u in units}
    akv = {u: _dot(a_kv[u], blk(vb, *u)) for u in units}
    tinv = {u: eye + a_ab[u] for u in units}
    npow = a_ab
    for _ in range(5):
        npb = {u: npow[u].astype(BF16) for u in units}
        npow = {u: _dot(npb[u], npb[u]) for u in units}
        tinv = {u: tinv[u] + _dot(tinv[u].astype(BF16), npow[u].astype(BF16)) for u in units}
    tinv_b = {u: tinv[u].astype(BF16) for u in units}

    state = [s_scr[h] for h in range(H)]
    y_rows = []
    for c in range(n_chunks):
        hs = [(c, h) for h in range(H)]
        ar_s = {u: _dot_nt(lhs[u], state[u[1]].astype(BF16)) for u in hs}
        ub = {u: _dot(tinv_b[u], (ar_s[u][0:C] + akv[u][0:C]).astype(BF16)).astype(BF16) for u in hs}
        ys = [ar_s[u][C:2 * C] + akv[u][C:2 * C] + _dot(a_rb[u], ub[u]) for u in hs]
        upd = {u: _dot_tn(jnp.concatenate([ub[u], blk(vb, *u)], axis=0),
                          jnp.concatenate([blk(bh, *u), blk(kh, *u)], axis=0)) for u in hs}
        state = [state[h] * e_tot[c * C:c * C + 1, h * N:(h + 1) * N] + upd[(c, h)] for h in range(H)]
        y_rows.append(jnp.concatenate(ys, axis=1))
    for h in range(H):
        s_scr[h] = state[h]

    y = jnp.concatenate(y_rows, axis=0)
    mu = _dot_exact_rhs(y, bones, 1) * (1.0 / N)
    yc = y - mu
    var = _dot_exact_rhs(yc * yc, bones, 1) * (1.0 / N)
    yn = yc * lax.rsqrt(var + GN_EPS) * gng_ref[...] + gnb_ref[...]
    bonus = _dot_exact_rhs(r * kp * rk_ref[...], bones, 1) * v
    y_ref[...] = ((yn + bonus) * g).astype(BF16)


def _rwkv(p_rwkv, B, T, shift_mix, decay_w0, decay_up, iclr_a0, iclr_up, gate_up, k_k, k_a, r_k, gn_g, gn_b):
    tt = RWKV_TT
    n_t = T // tt
    head = jnp.arange(RWKV_DIM) // RWKV_HEAD_DIM
    bones = (head[:, None] == head[None, :]).astype(BF16)
    vec = lambda a: a.reshape(1, -1)
    full = lambda shape: pl.BlockSpec(shape, lambda b, j: (0, 0))
    return pl.pallas_call(
        _rwkv_kernel,
        grid=(B, n_t),
        in_specs=[
            pl.BlockSpec((tt, RWKV_COLS), lambda b, j: (b * n_t + j, 0)),
            full((1, RWKV_COLS)), full((1, RWKV_DIM)), full((DECAY_RANK, RWKV_DIM)),
            full((1, RWKV_DIM)), full((ICLR_RANK, RWKV_DIM)), full((GATE_RANK, RWKV_DIM)),
            full((1, RWKV_DIM)), full((1, RWKV_DIM)), full((1, RWKV_DIM)),
            full((1, RWKV_DIM)), full((1, RWKV_DIM)), full((RWKV_DIM, RWKV_DIM)),
        ],
        out_specs=pl.BlockSpec((tt, RWKV_DIM), lambda b, j: (b * n_t + j, 0)),
        out_shape=jax.ShapeDtypeStruct((B * T, RWKV_DIM), BF16),
        scratch_shapes=[
            pltpu.VMEM((RWKV_HEADS, RWKV_HEAD_DIM, RWKV_HEAD_DIM), F32),
            pltpu.VMEM((1, RWKV_COLS), F32),
        ],
        compiler_params=_cparams(("parallel", "arbitrary")),
        name="rwkv",
    )(p_rwkv, vec(shift_mix), vec(decay_w0), decay_up.astype(BF16), vec(iclr_a0), iclr_up.astype(BF16),
      gate_up.astype(BF16), vec(k_k), vec(k_a), vec(r_k), vec(gn_g), vec(gn_b), bones)


MOBA_HP = 8


def _moba_kernel(q_ref, k_ref, v_ref, o_ref, kmean_scr, sel_scr, acc_scr, score_even, score_odd):
    blk_sz, dh = MOBA_BLOCK, ATTN_HEAD_DIM
    nb = k_ref.shape[0] // blk_sz
    i = pl.program_id(2)

    @pl.when(i == 0)
    def _():
        for n in range(nb):
            kb = k_ref[n * blk_sz:(n + 1) * blk_sz, :].astype(F32)
            kmean_scr[n:n + 1, :] = jnp.sum(kb, axis=0, keepdims=True) * (1.0 / blk_sz)

    blk = lax.broadcasted_iota(I32, (nb, blk_sz), 0)
    kpos = lax.broadcasted_iota(I32, (blk_sz, blk_sz), 0)
    qpos = lax.broadcasted_iota(I32, (blk_sz, blk_sz), 1)
    own_start = pl.multiple_of(i * blk_sz, blk_sz)
    heads = [slice(hh * dh, (hh + 1) * dh) for hh in range(MOBA_HP)]
    hds = range(MOBA_HP)
    qss = [q_ref[:, hs] for hs in heads]
    kmean_pieces = _bf16_pieces(kmean_scr[...], 3)
    gates = [sum(_dot_nt(piece[:, heads[hh]], qss[hh]) for piece in kmean_pieces) for hh in hds]
    def scores(n):
        start = pl.multiple_of(n * blk_sz, blk_sz)
        return [_dot_nt(k_ref[pl.ds(start, blk_sz), heads[hh]], qss[hh]) for hh in hds]

    own = scores(i)
    first = scores(0)
    for hh in hds:
        score_even[hh] = first[hh]
    for hh in hds:
        work = jnp.where(blk < i, gates[hh], NEG_INF)
        chosen = blk < 0
        for _ in range(MOBA_TOPK):
            top = jnp.max(work, axis=0, keepdims=True)
            lowest = jnp.min(jnp.where(work == top, blk, nb), axis=0, keepdims=True)
            hit = blk == lowest
            chosen = chosen | hit
            work = jnp.where(hit, -jnp.inf, work)
        sel_scr[hh] = jnp.where(chosen & (blk < i), 1.0, 0.0)
    stats, ps = [], []
    for hh in hds:
        s = jnp.where(kpos <= qpos, own[hh], NEG_INF)
        m0 = jnp.max(s, axis=0, keepdims=True)
        p = jnp.exp2(s - m0)
        stats += [m0, jnp.sum(p, axis=0, keepdims=True)]
        ps.append(p.astype(BF16))
    pvs = [_dot_tn(v_ref[pl.ds(own_start, blk_sz), heads[hh]], ps[hh]) for hh in hds]
    for hh in hds:
        acc_scr[hh] = pvs[hh]

    def step(n, carry, here, there):
        ahead = scores(jnp.minimum(n + 1, i - 1))
        for hh in hds:
            there[hh] = ahead[hh]
        start = pl.multiple_of(n * blk_sz, blk_sz)
        out, ps, alphas = [], [], []
        for hh in hds:
            m_run, l_run = carry[2 * hh], carry[2 * hh + 1]
            s = jnp.where(sel_scr[hh, pl.ds(n, 1), :] > 0.0, here[hh], NEG_INF)
            m_new = jnp.maximum(m_run, jnp.max(s, axis=0, keepdims=True))
            alpha = jnp.exp2(m_run - m_new)
            p = jnp.exp2(s - m_new)
            out += [m_new, alpha * l_run + jnp.sum(p, axis=0, keepdims=True)]
            ps.append(p.astype(BF16))
            alphas.append(alpha)
        pvs = [_dot_tn(v_ref[pl.ds(start, blk_sz), heads[hh]], ps[hh]) for hh in hds]
        for hh in hds:
            acc_scr[hh] = alphas[hh] * acc_scr[hh] + pvs[hh]
        return tuple(out)

    def two_blocks(t, carry):
        carry = step(2 * t, carry, score_even, score_odd)
        return step(2 * t + 1, carry, score_odd, score_even)

    stats = lax.fori_loop(0, i // 2, two_blocks, tuple(stats))
    stats = lax.cond(i % 2 == 1, lambda c: step(i - 1, c, score_even, score_odd), lambda c: c, stats)
    outs = [(acc_scr[hh] / stats[2 * hh + 1]).T for hh in range(MOBA_HP)]
    o_ref[...] = jnp.concatenate(outs, axis=1).astype(BF16)


def _moba(q, k, v, B, T):
    blk_sz = MOBA_BLOCK
    nq = T // blk_sz
    lanes = MOBA_HP * ATTN_HEAD_DIM
    kv_spec = pl.BlockSpec((T, lanes), lambda b, hp, i: (b, hp))
    q_spec = pl.BlockSpec((blk_sz, lanes), lambda b, hp, i: (b * nq + i, hp))
    return pl.pallas_call(
        _moba_kernel,
        grid=(B, ATTN_HEADS // MOBA_HP, nq),
        in_specs=[q_spec, kv_spec, kv_spec],
        out_specs=q_spec,
        out_shape=jax.ShapeDtypeStruct((B * T, ATTN_DIM), BF16),
        scratch_shapes=[
            pltpu.VMEM((T // blk_sz, lanes), F32),
            pltpu.VMEM((MOBA_HP, T // blk_sz, blk_sz), F32),
            pltpu.VMEM((MOBA_HP, ATTN_HEAD_DIM, blk_sz), F32),
            pltpu.VMEM((MOBA_HP, blk_sz, blk_sz), F32),
            pltpu.VMEM((MOBA_HP, blk_sz, blk_sz), F32),
        ],
        compiler_params=_cparams(("parallel", "parallel", "arbitrary")),
        name="moba",
    )(q, k, v)


MERGE_TM = 512
MERGE_SUB = 128


def _layer_norm(h, g, b):
    mu = jnp.mean(h, axis=-1, keepdims=True)
    hc = h - mu
    var = jnp.mean(hc * hc, axis=-1, keepdims=True)
    return hc * lax.rsqrt(var + LN_EPS) * g + b


def _merge_kernel(x_ref, yr_ref, ya_ref, g_ref, wbr_ref, wba_ref, wo_ref, lng_ref, lnb_ref, rwt_ref, rb_ref,
                  x1_ref, lgt_ref):
    subs = [slice(s * MERGE_SUB, (s + 1) * MERGE_SUB) for s in range(MERGE_TM // MERGE_SUB)]
    yr = [_dot(yr_ref[sl, :], wbr_ref[...]) for sl in subs]
    ya = [_dot(ya_ref[sl, :], wba_ref[...]) for sl in subs]
    merged = [(g_ref[sl, 0:D_MODEL].astype(F32) * yr[s] + g_ref[sl, D_MODEL:2 * D_MODEL].astype(F32) * ya[s])
              .astype(BF16) for s, sl in enumerate(subs)]
    mix = [_dot(m, wo_ref[...]) for m in merged]
    x1 = [_layer_norm(DEEPNORM_ALPHA * x_ref[sl, :] + mix[s], lng_ref[...], lnb_ref[...])
          for s, sl in enumerate(subs)]
    for s, sl in enumerate(subs):
        x1_ref[sl, :] = x1[s]
    rw_a, rw_b = _bf16_pieces(rwt_ref[...], 2)
    for s, sl in enumerate(subs):
        x_a, x_b = _bf16_pieces(x1[s], 2)
        lgt_ref[:, sl] = _dot_nt(rw_a, x_a) + _dot_nt(rw_a, x_b) + _dot_nt(rw_b, x_a) + rb_ref[...]


def _merge(x2, y_rwkv, y_attn, gates, w_br, w_ba, w_o, ln_g, ln_b, router_w, router_b):
    n_tok = x2.shape[0]
    tm = MERGE_TM
    row = lambda i: (i, 0)
    full = lambda shape: pl.BlockSpec(shape, lambda i: (0, 0))
    return pl.pallas_call(
        _merge_kernel,
        grid=(n_tok // tm,),
        in_specs=[
            pl.BlockSpec((tm, D_MODEL), row), pl.BlockSpec((tm, RWKV_DIM), row), pl.BlockSpec((tm, ATTN_DIM), row),
            pl.BlockSpec((tm, GATE_COLS), row),
            full((RWKV_DIM, D_MODEL)), full((ATTN_DIM, D_MODEL)), full((D_MODEL, D_MODEL)),
            full((1, D_MODEL)), full((1, D_MODEL)), full((N_EXPERTS, D_MODEL)), full((N_EXPERTS, 1)),
        ],
        out_specs=[pl.BlockSpec((tm, D_MODEL), row), pl.BlockSpec((N_EXPERTS, tm), lambda i: (0, i))],
        out_shape=[jax.ShapeDtypeStruct((n_tok, D_MODEL), F32), jax.ShapeDtypeStruct((N_EXPERTS, n_tok), F32)],
        compiler_params=_cparams(("parallel",)),
        name="merge",
    )(x2, y_rwkv, y_attn, gates, w_br.astype(BF16), w_ba.astype(BF16), w_o.astype(BF16),
      ln_g.reshape(1, -1), ln_b.reshape(1, -1), router_w.T, router_b.reshape(-1, 1))


MOE_TM = 512
ROW_BLOCK = 512
PIECE_ALIGN = 2 * SUBLANE
PIECE_SIZES = (512, 256, 128, 64, 32, 16)
PIECE_SPLIT = 128
TAIL_SIZES = (256, 128, 64, 32, 16)
STAGE_CHUNK = 256
ROW_W = D_MODEL + LANE


def _moe_dims(n_tok):
    n_tiles = n_tok // MOE_TM
    stage_rows = -(-(MOE_TM * TOP_K + N_EXPERTS * (PIECE_ALIGN - 1)) // STAGE_CHUNK) * STAGE_CHUNK
    max_rows = n_tok * TOP_K + n_tiles * N_EXPERTS * (PIECE_ALIGN - 1) + N_EXPERTS * (ROW_BLOCK - 1)
    n_blk = -(-max_rows // ROW_BLOCK)
    map_lanes = -(-(n_blk + 1) // LANE) * LANE
    return n_tiles, stage_rows, n_blk, map_lanes


def _round_up_f32(x, m):
    return jnp.floor((x + (m - 1)) * (1.0 / m)) * m


def _route_kernel(lg_ref, wt_ref, idx_ref, slot_ref, off_ref, cnt_ref, end_ref, map_ref,
                  idx_scr, pos_scr, cnt_scr, off_scr):
    tm = MOE_TM
    n_tiles = cnt_scr.shape[1]
    phase = pl.program_id(0)
    i = pl.program_id(1)
    tok0 = pl.multiple_of(i * tm, tm)
    eio = lax.broadcasted_iota(I32, (N_EXPERTS, tm), 0)
    tile_lane = lax.broadcasted_iota(I32, (N_EXPERTS, n_tiles), 1)
    e_from = lax.broadcasted_iota(I32, (N_EXPERTS, N_EXPERTS), 1)
    e_to = lax.broadcasted_iota(I32, (N_EXPERTS, N_EXPERTS), 0)
    earlier_e = jnp.where(e_from < e_to, 1.0, 0.0)

    @pl.when((phase == 0) & (i == 0))
    def _():
        cnt_scr[...] = jnp.zeros_like(cnt_scr)

    @pl.when(phase == 0)
    def _():
        work = lg_ref[...]
        vals, hots = [], []
        for s in range(TOP_K):
            m = jnp.max(work, axis=0, keepdims=True)
            ix = jnp.min(jnp.where(work == m, eio, N_EXPERTS), axis=0, keepdims=True)
            hot = eio == ix
            idx_scr[s:s + 1, pl.ds(tok0, tm)] = ix
            idx_ref[s:s + 1, :] = ix
            vals.append(m)
            hots.append(hot)
            work = jnp.where(hot, -jnp.inf, work)
        es = [jnp.exp(v - vals[0]) for v in vals]
        denom = es[0] + es[1] + es[2] + es[3]
        for s in range(TOP_K):
            wt_ref[s:s + 1, :] = es[s] / denom
        multi_f = jnp.where(hots[0] | hots[1] | hots[2] | hots[3], 1.0, 0.0)
        t_from = lax.broadcasted_iota(I32, (tm, tm), 0)
        t_to = lax.broadcasted_iota(I32, (tm, tm), 1)
        before = jnp.where(t_from < t_to, 1.0, 0.0).astype(BF16)
        count = _dot(multi_f.astype(BF16), before)
        for s in range(TOP_K):
            pos_scr[s:s + 1, pl.ds(tok0, tm)] = jnp.sum(
                jnp.where(hots[s], count, 0.0), axis=0, keepdims=True).astype(I32)
        cnt8 = _round_up_f32(jnp.sum(multi_f, axis=1, keepdims=True), PIECE_ALIGN)
        cnt_scr[...] = cnt_scr[...] + jnp.where(tile_lane == i, cnt8, 0.0)

    @pl.when((phase == 1) & (i == 0))
    def _():
        cnt8 = cnt_scr[...]
        tot = _round_up_f32(jnp.sum(cnt8, axis=1, keepdims=True), ROW_BLOCK)
        p_start = _dot_hi(earlier_e, jnp.broadcast_to(tot, (N_EXPERTS, n_tiles)))
        i_from = lax.broadcasted_iota(I32, (n_tiles, n_tiles), 0)
        i_to = lax.broadcasted_iota(I32, (n_tiles, n_tiles), 1)
        earlier_tiles = _dot_hi(cnt8, jnp.where(i_from < i_to, 1.0, 0.0))
        off_scr[...] = p_start + earlier_tiles
        off_ref[...] = off_scr[...].astype(I32)
        cnt_ref[...] = cnt8.astype(I32)
        p_end = p_start[:, 0:1] + tot
        end_ref[...] = p_end.astype(I32)
        lanes = map_ref.shape[1]
        blk_start = lax.broadcasted_iota(I32, (N_EXPERTS, lanes), 1).astype(F32) * ROW_BLOCK
        blk_e = jnp.sum(jnp.where(blk_start >= p_end, 1.0, 0.0), axis=0, keepdims=True)
        blk_e = jnp.minimum(blk_e, N_EXPERTS - 1.0)
        n_used = jnp.max(p_end, axis=0, keepdims=True) * (1.0 / ROW_BLOCK)
        last = lax.broadcasted_iota(I32, (1, lanes), 1) == lanes - 1
        map_ref[...] = jnp.where(last, n_used, blk_e).astype(I32)

    @pl.when(phase == 1)
    def _():
        cnt_col = jnp.sum(jnp.where(tile_lane == i, cnt_scr[...], 0.0), axis=1, keepdims=True)
        local_off = _dot_hi(earlier_e, jnp.broadcast_to(cnt_col, (N_EXPERTS, tm)))
        for s in range(TOP_K):
            hot = eio == idx_scr[s:s + 1, pl.ds(tok0, tm)]
            base = jnp.sum(jnp.where(hot, local_off, 0.0), axis=0, keepdims=True).astype(I32)
            slot_ref[s:s + 1, :] = base + pos_scr[s:s + 1, pl.ds(tok0, tm)]


def _route(logits_t):
    n_tok = logits_t.shape[1]
    n_tiles, _, _, map_lanes = _moe_dims(n_tok)
    tm = MOE_TM
    last = n_tiles - 1
    const = lambda p, i: (0, 0)
    return pl.pallas_call(
        _route_kernel,
        grid=(2, n_tiles),
        in_specs=[pl.BlockSpec((N_EXPERTS, tm), lambda p, i: (0, i * (1 - p) + last * p))],
        out_specs=[
            pl.BlockSpec((TOP_K, tm), lambda p, i: (0, i * (1 - p) + last * p)),
            pl.BlockSpec((TOP_K, tm), lambda p, i: (0, i * (1 - p) + last * p)),
            pl.BlockSpec((TOP_K, tm), lambda p, i: (0, i * p)),
            pl.BlockSpec((N_EXPERTS, n_tiles), const),
            pl.BlockSpec((N_EXPERTS, n_tiles), const),
            pl.BlockSpec((N_EXPERTS, 1), const),
            pl.BlockSpec((1, map_lanes), const),
        ],
        out_shape=[
            jax.ShapeDtypeStruct((TOP_K, n_tok), F32),
            jax.ShapeDtypeStruct((TOP_K, n_tok), I32),
            jax.ShapeDtypeStruct((TOP_K, n_tok), I32),
            jax.ShapeDtypeStruct((N_EXPERTS, n_tiles), I32),
            jax.ShapeDtypeStruct((N_EXPERTS, n_tiles), I32),
            jax.ShapeDtypeStruct((N_EXPERTS, 1), I32),
            jax.ShapeDtypeStruct((1, map_lanes), I32),
        ],
        scratch_shapes=[
            pltpu.VMEM((TOP_K, n_tok), I32),
            pltpu.VMEM((TOP_K, n_tok), I32),
            pltpu.VMEM((N_EXPERTS, n_tiles), F32),
            pltpu.VMEM((N_EXPERTS, n_tiles), F32),
        ],
        compiler_params=_cparams(("arbitrary", "arbitrary")),
        name="route",
    )(logits_t)


def _piece_copies(off_ref, cnt_ref, tile, make_copy, action):
    def per_expert(e, stage_row):
        c8 = cnt_ref[tile * N_EXPERTS + e]
        hbm_row = off_ref[tile * N_EXPERTS + e]

        def copies(sizes, done):
            for size in sizes:
                bit = c8 & size

                @pl.when(bit != 0)
                def _(done=done, size=size):
                    action(make_copy(pl.multiple_of(stage_row + done, PIECE_ALIGN),
                                     pl.multiple_of(hbm_row + done, PIECE_ALIGN), size))

                done = done + bit

        large = [s for s in PIECE_SIZES if s >= PIECE_SPLIT]
        small = [s for s in PIECE_SIZES if s < PIECE_SPLIT]
        pl.when(c8 >= PIECE_SPLIT)(lambda: copies(large, 0))
        copies(small, c8 - (c8 & (PIECE_SPLIT - 1)))
        return stage_row + c8

    lax.fori_loop(0, N_EXPERTS, per_expert, 0)


def _slot_rows(slot_ref):
    return [slot_ref[s:s + 1, :] for s in range(TOP_K)]


def _one_hot_rows(slots, r0):
    rio = lax.broadcasted_iota(I32, (STAGE_CHUNK, MOE_TM), 0) + r0
    pick = jnp.zeros((STAGE_CHUNK, MOE_TM), F32)
    for s in range(TOP_K):
        pick = jnp.where(rio == slots[s], 1.0, pick)
    return pick.astype(BF16)


def _dispatch_kernel(off_ref, cnt_ref, end_ref, slot_ref, wt_ref, idx_ref, x1_ref, xs_hbm, stage, zeros, sem,
                     zero_sem):
    i = pl.program_id(0)
    n_tiles = pl.num_programs(0)
    buf = i % 2

    @pl.when(i == 0)
    def _():
        zeros[...] = jnp.zeros_like(zeros)

        def tail_copies(action):
            def per_expert(e, carry):
                last_piece = (n_tiles - 1) * N_EXPERTS + e
                start = off_ref[last_piece] + cnt_ref[last_piece]
                tail = end_ref[e] - start
                done = 0
                for size in TAIL_SIZES:
                    bit = tail & size

                    @pl.when(bit != 0)
                    def _(done=done, size=size):
                        action(pltpu.make_async_copy(
                            zeros.at[pl.ds(0, size)],
                            xs_hbm.at[pl.ds(pl.multiple_of(start + done, PIECE_ALIGN), size)], zero_sem))

                    done = done + bit
                return carry

            lax.fori_loop(0, N_EXPERTS, per_expert, 0)

            def per_block(b, carry):
                action(pltpu.make_async_copy(
                    zeros, xs_hbm.at[pl.ds(pl.multiple_of(b * ROW_BLOCK, ROW_BLOCK), ROW_BLOCK)], zero_sem))
                return carry

            lax.fori_loop(end_ref[N_EXPERTS - 1] // ROW_BLOCK, xs_hbm.shape[0] // ROW_BLOCK, per_block, 0)

        tail_copies(lambda cp: cp.start())
        tail_copies(lambda cp: cp.wait())

    def out_copies(tile, b, action):
        def make_copy(stage_row, hbm_row, size):
            return pltpu.make_async_copy(stage.at[b, pl.ds(stage_row, size)], xs_hbm.at[pl.ds(hbm_row, size)],
                                         sem.at[b])
        _piece_copies(off_ref, cnt_ref, tile, make_copy, action)

    @pl.when(i >= 2)
    def _():
        out_copies(i - 2, buf, lambda cp: cp.wait())

    w = wt_ref[...]
    w_a = w.astype(BF16).astype(F32)
    tail_t = jnp.concatenate(
        [w_a, w - w_a, idx_ref[...].astype(F32), jnp.zeros((LANE - 3 * TOP_K, MOE_TM), F32)], axis=0)
    xb = jnp.concatenate([x1_ref[...].astype(BF16), tail_t.T.astype(BF16)], axis=1)
    slots = _slot_rows(slot_ref)
    for r0 in range(0, stage.shape[1], STAGE_CHUNK):
        stage[buf, r0:r0 + STAGE_CHUNK, :] = _dot(_one_hot_rows(slots, r0), xb).astype(BF16)
    out_copies(i, buf, lambda cp: cp.start())

    @pl.when(i == n_tiles - 1)
    def _():
        @pl.when(i >= 1)
        def _():
            out_copies(i - 1, 1 - buf, lambda cp: cp.wait())

        out_copies(i, buf, lambda cp: cp.wait())


def _dispatch(off_flat, cnt_flat, end_flat, slot, wts, idx, x1):
    n_tok = x1.shape[0]
    n_tiles, stage_rows, n_blk, _ = _moe_dims(n_tok)
    n_rows = n_blk * ROW_BLOCK
    return pl.pallas_call(
        _dispatch_kernel,
        grid_spec=pltpu.PrefetchScalarGridSpec(
            num_scalar_prefetch=3,
            grid=(n_tiles,),
            in_specs=[pl.BlockSpec((TOP_K, MOE_TM), lambda i, off, cnt, end: (0, i)),
                      pl.BlockSpec((TOP_K, MOE_TM), lambda i, off, cnt, end: (0, i)),
                      pl.BlockSpec((TOP_K, MOE_TM), lambda i, off, cnt, end: (0, i)),
                      pl.BlockSpec((MOE_TM, D_MODEL), lambda i, off, cnt, end: (i, 0))],
            out_specs=pl.BlockSpec(memory_space=pl.ANY),
            scratch_shapes=[pltpu.VMEM((2, stage_rows, ROW_W), BF16),
                            pltpu.VMEM((ROW_BLOCK, ROW_W), BF16),
                            pltpu.SemaphoreType.DMA((2,)), pltpu.SemaphoreType.DMA(())],
        ),
        out_shape=jax.ShapeDtypeStruct((n_rows, ROW_W), BF16),
        compiler_params=_cparams(("arbitrary",)),
        name="dispatch",
    )(off_flat, cnt_flat, end_flat, slot, wts, idx, x1)


def _combine_kernel(off_ref, cnt_ref, slot_ref, x1_ref, lng_ref, lnb_ref, ys_hbm, out_ref, stage, sem):
    i = pl.program_id(0)
    n_tiles = pl.num_programs(0)
    buf = i % 2

    def in_copies(tile, b, action):
        def make_copy(stage_row, hbm_row, size):
            return pltpu.make_async_copy(ys_hbm.at[pl.ds(hbm_row, size)], stage.at[b, pl.ds(stage_row, size)],
                                         sem.at[b])
        _piece_copies(off_ref, cnt_ref, tile, make_copy, action)

    @pl.when(i == 0)
    def _():
        stage[...] = jnp.zeros_like(stage)
        in_copies(0, 0, lambda cp: cp.start())

    @pl.when(i + 1 < n_tiles)
    def _():
        in_copies(i + 1, 1 - buf, lambda cp: cp.start())

    in_copies(i, buf, lambda cp: cp.wait())

    slots = _slot_rows(slot_ref)
    ffn = jnp.zeros((MOE_TM, D_MODEL), F32)
    for r0 in range(0, stage.shape[1], STAGE_CHUNK):
        ffn = ffn + _dot_tn(_one_hot_rows(slots, r0), stage[buf, r0:r0 + STAGE_CHUNK, :])
    out_ref[...] = _layer_norm(DEEPNORM_ALPHA * x1_ref[...] + ffn, lng_ref[...], lnb_ref[...])


def _combine(off_flat, cnt_flat, slot, x1, ln_g, ln_b, ys):
    n_tok = x1.shape[0]
    n_tiles, stage_rows, _, _ = _moe_dims(n_tok)
    tok = lambda i, off, cnt: (0, i)
    row = lambda i, off, cnt: (i, 0)
    full = lambda shape: pl.BlockSpec(shape, lambda i, off, cnt: (0, 0))
    return pl.pallas_call(
        _combine_kernel,
        grid_spec=pltpu.PrefetchScalarGridSpec(
            num_scalar_prefetch=2,
            grid=(n_tiles,),
            in_specs=[pl.BlockSpec((TOP_K, MOE_TM), tok),
                      pl.BlockSpec((MOE_TM, D_MODEL), row), full((1, D_MODEL)), full((1, D_MODEL)),
                      pl.BlockSpec(memory_space=pl.ANY)],
            out_specs=pl.BlockSpec((MOE_TM, D_MODEL), row),
            scratch_shapes=[pltpu.VMEM((2, stage_rows, D_MODEL), BF16), pltpu.SemaphoreType.DMA((2,))],
        ),
        out_shape=jax.ShapeDtypeStruct((n_tok, D_MODEL), F32),
        compiler_params=_cparams(("arbitrary",)),
        name="combine",
    )(off_flat, cnt_flat, slot, x1, ln_g.reshape(1, -1), ln_b.reshape(1, -1), ys)


CAST_ROWS = 128
FFN_CHUNKS = 4


def _expert_kernel(blk_e_ref, n_used_ref, xs_ref, win_ref, bin_ref, wout_ref, bout_ref, ys_ref, win_bf, wout_bf):
    rb = pl.program_id(0)
    new_expert = (rb == 0) | (blk_e_ref[rb] != blk_e_ref[jnp.maximum(rb - 1, 0)])

    @pl.when((rb < n_used_ref[0]) & new_expert)
    def _():
        for r in range(0, D_MODEL, CAST_ROWS):
            win_bf[r:r + CAST_ROWS, :] = win_ref[0, r:r + CAST_ROWS, :].astype(BF16)
        for r in range(0, D_EXPERT, CAST_ROWS):
            wout_bf[r:r + CAST_ROWS, :] = wout_ref[0, r:r + CAST_ROWS, :].astype(BF16)

    @pl.when(rb < n_used_ref[0])
    def _():
        x = xs_ref[:, 0:D_MODEL]
        tail = xs_ref[:, D_MODEL:ROW_W].astype(F32)
        mine = blk_e_ref[rb].astype(F32)
        w_row = jnp.zeros((tail.shape[0], 1), F32)
        for s in range(TOP_K):
            w_s = tail[:, s:s + 1] + tail[:, TOP_K + s:TOP_K + s + 1]
            w_row = jnp.where(tail[:, 2 * TOP_K + s:2 * TOP_K + s + 1] == mine, w_s, w_row)
        cw = D_EXPERT // FFN_CHUNKS

        def hidden(j):
            gs = slice(j * cw, (j + 1) * cw)
            ls = slice(D_EXPERT + j * cw, D_EXPERT + (j + 1) * cw)
            return _dot(x, win_bf[:, gs]) + bin_ref[0, :, gs], _dot(x, win_bf[:, ls]) + bin_ref[0, :, ls]

        ahead = hidden(0)
        y = bout_ref[0]
        for j in range(FFN_CHUNKS):
            g, lin = ahead
            if j + 1 < FFN_CHUNKS:
                ahead = hidden(j + 1)
            gate_h = jnp.minimum(g, SWIGLU_LIMIT)
            lin_h = jnp.clip(lin, -SWIGLU_LIMIT, SWIGLU_LIMIT)
            act = gate_h * jax.nn.sigmoid(SWIGLU_ALPHA * gate_h) * (lin_h + 1.0)
            y = y + _dot(act.astype(BF16), wout_bf[j * cw:(j + 1) * cw, :])
        ys_ref[...] = (y * w_row).astype(BF16)

    @pl.when(rb >= n_used_ref[0])
    def _():
        ys_ref[...] = jnp.zeros_like(ys_ref)


def _experts(blk_e, n_used, xs, w_in, b_in, w_out, b_out):
    n_rows = xs.shape[0]
    n_blk = n_rows // ROW_BLOCK
    used = lambda rb, n_used: jnp.maximum(jnp.minimum(rb, n_used[0] - 1), 0)
    rows = lambda rb, blk_e, n_used: (used(rb, n_used), 0)
    per_e = lambda rb, blk_e, n_used: (blk_e[used(rb, n_used)], 0, 0)
    return pl.pallas_call(
        _expert_kernel,
        grid_spec=pltpu.PrefetchScalarGridSpec(
            num_scalar_prefetch=2,
            grid=(n_blk,),
            in_specs=[
                pl.BlockSpec((ROW_BLOCK, ROW_W), rows),
                pl.BlockSpec((1, D_MODEL, 2 * D_EXPERT), per_e),
                pl.BlockSpec((1, 1, 2 * D_EXPERT), per_e),
                pl.BlockSpec((1, D_EXPERT, D_MODEL), per_e),
                pl.BlockSpec((1, 1, D_MODEL), per_e),
            ],
            out_specs=pl.BlockSpec((ROW_BLOCK, D_MODEL), lambda rb, blk_e, n_used: (rb, 0)),
            scratch_shapes=[pltpu.VMEM((D_MODEL, 2 * D_EXPERT), BF16), pltpu.VMEM((D_EXPERT, D_MODEL), BF16)],
        ),
        out_shape=jax.ShapeDtypeStruct((n_rows, D_MODEL), BF16),
        compiler_params=_cparams(("arbitrary",)),
        name="experts",
    )(blk_e, n_used, xs, w_in, b_in.reshape(N_EXPERTS, 1, -1), w_out, b_out.reshape(N_EXPERTS, 1, -1))


def _moe(x1, logits_t, expert_w_in, expert_b_in, expert_w_out, expert_b_out, ln_g, ln_b):
    n_tok = x1.shape[0]
    _, _, n_blk, map_lanes = _moe_dims(n_tok)
    wts, idx, slot, off, cnt, ends, blk_map = _route(logits_t)
    off_flat = off.T.reshape(-1)
    cnt_flat = cnt.T.reshape(-1)
    blk_e = blk_map[0, 0:n_blk]
    n_used = blk_map[0, map_lanes - 1:map_lanes]
    xs = _dispatch(off_flat, cnt_flat, ends.reshape(-1), slot, wts, idx, x1)
    ys = _experts(blk_e, n_used, xs, expert_w_in, expert_b_in, expert_w_out, expert_b_out)
    return _combine(off_flat, cnt_flat, slot, x1, ln_g, ln_b, ys)


def kernel(x, ln1_g, ln1_b, ln2_g, ln2_b, w_in, shift_mix, decay_w0, decay_up, iclr_a0, iclr_up, gate_up, k_k, k_a, r_k, gn_g, gn_b, w_branch_rwkv, w_branch_attn, w_out, router_w, router_b, expert_w_in, expert_b_in, expert_w_out, expert_b_out):
    B, T, D = x.shape
    x2 = x.reshape(B * T, D)
    prw, q, k, v, gates = _proj(x2, w_in[0].astype(BF16), T)
    y_rwkv = _rwkv(prw, B, T, shift_mix[0], decay_w0[0], decay_up[0], iclr_a0[0], iclr_up[0], gate_up[0],
                   k_k[0], k_a[0], r_k[0], gn_g[0], gn_b[0])
    y_attn = _moba(q, k, v, B, T)
    x1, logits_t = _merge(x2, y_rwkv, y_attn, gates, w_branch_rwkv[0], w_branch_attn[0], w_out[0],
                          ln1_g[0], ln1_b[0], router_w[0], router_b[0])
    out = _moe(x1, logits_t, expert_w_in[0], expert_b_in[0], expert_w_out[0], expert_b_out[0], ln2_g[0], ln2_b[0])
    return out.reshape(B, T, D)
```
